```python
import math, functools
import jax, jax.numpy as jnp
from jax import lax
import numpy as np

D_MODEL = 2048
BATCH = 32
SEQ = 256
DEPTH = 4
DEC_BATCH = 4
DEC_SEQ = 4096
PAST_LEN = 512

GRID_W = 64
MIX_W = D_MODEL // 4
N_BRANCH = 4
GLA_HEADS = 4
GLA_DV = MIX_W // GLA_HEADS
GLA_DK = GLA_DV // 2
GLA_RANK = 16
GLA_NORMALIZER = 16.0
HGRN_DIM = 128
HGRN_HEADS = MIX_W // HGRN_DIM
SSM_HEADDIM = 64
SSM_HEADS = MIX_W // SSM_HEADDIM
SSM_GROUPS = 2
SSM_STATE = 128
SSM_CONV = 3
SSM_XBC = MIX_W + 2 * SSM_GROUPS * SSM_STATE
S5_CH = 16
S5_STATE = 64
S5_GROUPS = MIX_W // S5_CH
VEC_CHUNK = 16
SSD_CHUNK = 64
D_FF = ((8 * D_MODEL // 3 + 255) // 256) * 256
N_EXPERTS = 8
TOP_K = 2
D_FF_EXPERT = D_MODEL // 2
N_DENSE = (DEPTH + 1) // 2
N_MOE = DEPTH // 2
EPS = 1e-6
IN_SIZES = (GLA_HEADS * GLA_DK, GLA_HEADS * GLA_DK, MIX_W, MIX_W, 2 * GLA_RANK,
            MIX_W, 2 * MIX_W, MIX_W, MIX_W,
            MIX_W, SSM_XBC, 2 * SSM_HEADS,
            MIX_W,
            N_BRANCH * D_MODEL)
N_IN = sum(IN_SIZES)

kernel_name = 'gated_hybrid_diffusion_step'


def rmsnorm(x, g):
    xf = x.astype(jnp.float32)
    y = xf * lax.rsqrt(jnp.mean(xf * xf, axis=-1, keepdims=True) + EPS)
    return (y * g.astype(jnp.float32)).astype(x.dtype)


def flip(t):
    return jnp.flip(t, axis=1)


def grid_pos_embed(n_tokens, dim):
    rows = n_tokens // GRID_W
    row = jnp.repeat(jnp.arange(rows, dtype=jnp.float32), GRID_W)
    col = jnp.tile(jnp.arange(GRID_W, dtype=jnp.float32), rows)

    def sincos(pos, d):
        half = d // 2
        omega = 1.0 / (10000.0 ** (jnp.arange(half, dtype=jnp.float32) / half))
        ang = pos[:, None] * omega[None, :]
        return jnp.concatenate([jnp.sin(ang), jnp.cos(ang)], axis=-1)

    return jnp.concatenate([sincos(row, dim // 2), sincos(col, dim // 2)], axis=-1)


def gated_linear_scan(q, k, v, log_g, s0):
    Bz, L, H, K = q.shape
    V = v.shape[-1]
    n = L // VEC_CHUNK

    def blocks(t):
        return t.reshape(Bz, n, VEC_CHUNK, H, t.shape[-1]).transpose(1, 0, 3, 2, 4)

    mask = jnp.tril(jnp.ones((VEC_CHUNK, VEC_CHUNK), dtype=bool))

    def step(S, blk):
        qb, kb, vb, gb = blk
        b = jnp.cumsum(gb, axis=2)
        qg = qb * jnp.exp(b)
        kg = kb * jnp.exp(-b)
        att = jnp.where(mask, jnp.einsum('bhik,bhjk->bhij', qg, kg), 0.0)
        o = jnp.einsum('bhij,bhjv->bhiv', att, vb) + jnp.einsum('bhik,bhkv->bhiv', qg, S)
        b_last = b[:, :, -1:]
        S = (jnp.exp(b_last[:, :, 0])[..., None] * S
             + jnp.einsum('bhjk,bhjv->bhkv', kb * jnp.exp(b_last - b), vb))
        return S, o

    S, o = lax.scan(step, s0, (blocks(q), blocks(k), blocks(v), blocks(log_g)))
    return o.transpose(1, 0, 3, 2, 4).reshape(Bz, L, H, V), S


def bidir_gated_linear(q, k_f, k_b, v, g_f, g_b, s0):
    o_f, s_f = gated_linear_scan(q, k_f, v, g_f, s0[:, 0])
    o_b, s_b = gated_linear_scan(flip(q), flip(k_b), flip(v), flip(g_b), s0[:, 1])
    return o_f + flip(o_b), jnp.stack([s_f, s_b], axis=1)


def segsum(t):
    T = t.shape[-1]
    tt = jnp.broadcast_to(t[..., :, None], t.shape + (T,))
    strict = jnp.tril(jnp.ones((T, T), dtype=bool), -1)
    cs = jnp.cumsum(jnp.where(strict, tt, 0.0), axis=-2)
    return jnp.where(jnp.tril(jnp.ones((T, T), dtype=bool)), cs, -jnp.inf)


def ssd_scan(x, a, bm, cm, s0):
    Bz, L, H, P = x.shape
    G, N = bm.shape[2], bm.shape[3]
    R = H // G
    n = L // SSD_CHUNK
    x = x.reshape(Bz, n, SSD_CHUNK, G, R, P)
    a = a.reshape(Bz, n, SSD_CHUNK, G, R).transpose(0, 1, 3, 4, 2)
    bm = bm.reshape(Bz, n, SSD_CHUNK, G, N)
    cm = cm.reshape(Bz, n, SSD_CHUNK, G, N)
    a_cs = jnp.cumsum(a, axis=-1)
    lmat = jnp.exp(segsum(a))
    scores = jnp.einsum('bclgn,bcsgn->bcgls', cm, bm)
    y_diag = jnp.einsum('bcgrls,bcsgrp->bclgrp', scores[:, :, :, None] * lmat, x)
    decay_states = jnp.exp(a_cs[..., -1:] - a_cs).transpose(0, 1, 4, 2, 3)[..., None]
    states = jnp.einsum('bcsgn,bcsgrp->bcgrpn', bm, x * decay_states)
    states = jnp.concatenate([s0.reshape(Bz, 1, G, R, P, N), states], axis=1)
    chunk_a = jnp.pad(a_cs[..., -1], ((0, 0), (1, 0), (0, 0), (0, 0)))
    decay_chunk = jnp.exp(segsum(chunk_a.transpose(0, 2, 3, 1)))
    new_states = jnp.einsum('bgrzc,bcgrpn->bzgrpn', decay_chunk, states)
    prev, final = new_states[:, :-1], new_states[:, -1]
    y_off = (jnp.einsum('bclgn,bcgrpn->bclgrp', cm, prev)
             * jnp.exp(a_cs).transpose(0, 1, 4, 2, 3)[..., None])
    return (y_diag + y_off).reshape(Bz, L, H, P), final.reshape(Bz, H, P, N)


def complex_affine(e1, e2):
    a1r, a1i, b1r, b1i = e1
    a2r, a2i, b2r, b2i = e2
    return (a2r * a1r - a2i * a1i, a2r * a1i + a2i * a1r,
            a2r * b1r - a2i * b1i + b2r, a2r * b1i + a2i * b1r + b2i)


def s5_scan(u, a_re, a_im, log_dt, b_re, b_im, c_re, c_im, h0_re, h0_im):
    f32 = jnp.float32
    a_re = a_re.astype(f32)
    a_im = a_im.astype(f32)
    dt = jnp.exp(log_dt.astype(f32))[:, None]
    mag = jnp.exp(a_re * dt)
    ang = a_im * dt
    lam_r, lam_i = mag * jnp.cos(ang), mag * jnp.sin(ang)
    den = a_re * a_re + a_im * a_im
    zr = ((lam_r - 1.0) * a_re + lam_i * a_im) / den
    zi = (lam_i * a_re - (lam_r - 1.0) * a_im) / den
    b_re = b_re.astype(f32)
    b_im = b_im.astype(f32)
    bb_r = zr[..., None] * b_re - zi[..., None] * b_im
    bb_i = zr[..., None] * b_im + zi[..., None] * b_re
    bu_r = jnp.einsum('gpq,blgq->blgp', bb_r, u)
    bu_i = jnp.einsum('gpq,blgq->blgp', bb_i, u)
    h0_re = h0_re.astype(f32)
    h0_im = h0_im.astype(f32)
    bu_r = bu_r.at[:, 0].add(lam_r * h0_re - lam_i * h0_im)
    bu_i = bu_i.at[:, 0].add(lam_r * h0_im + lam_i * h0_re)
    ar = jnp.broadcast_to(lam_r, bu_r.shape)
    ai = jnp.broadcast_to(lam_i, bu_i.shape)
    _, _, h_r, h_i = lax.associative_scan(complex_affine, (ar, ai, bu_r, bu_i), axis=1)
    y = (jnp.einsum('gqp,blgp->blgq', c_re.astype(f32), h_r)
         - jnp.einsum('gqp,blgp->blgq', c_im.astype(f32), h_i))
    return y, h_r[:, -1], h_i[:, -1]


def centred_depthwise_conv(x, w, b):
    pad = (SSM_CONV - 1) // 2
    y = lax.conv_general_dilated(x, w[:, None, :].astype(x.dtype), (1,), [(pad, pad)],
                                 dimension_numbers=('NWC', 'WIO', 'NWC'),
                                 feature_group_count=x.shape[-1])
    return y + b.astype(x.dtype)


def gla_branch(q_in, k_in, v_in, r_in, lr_in, wa2, ba, norm_g, s0):
    f32 = jnp.float32
    Bz, L, _ = q_in.shape
    q = q_in.reshape(Bz, L, GLA_HEADS, GLA_DK).astype(f32) * GLA_DK ** -0.5
    k = k_in.reshape(Bz, L, GLA_HEADS, GLA_DK).astype(f32)
    v = v_in.reshape(Bz, L, GLA_HEADS, GLA_DV).astype(f32)
    lr = lr_in.reshape(Bz, L, 2, GLA_RANK).astype(f32)
    log_a = jax.nn.log_sigmoid(jnp.einsum('bldr,drk->bldk', lr, wa2.astype(f32)) + ba.astype(f32)) / GLA_NORMALIZER
    log_a = log_a.reshape(Bz, L, 2, GLA_HEADS, GLA_DK)
    o, st = bidir_gated_linear(q, k, k, v, log_a[:, :, 0], log_a[:, :, 1], s0.astype(f32))
    o = rmsnorm(o, norm_g) * jax.nn.silu(r_in.reshape(Bz, L, GLA_HEADS, GLA_DV).astype(f32))
    return o.reshape(Bz, L, MIX_W), st


def hgrn2_branch(q_in, f_in, i_in, g_in, lb, norm_g, s0):
    f32 = jnp.float32
    Bz, L, _ = q_in.shape
    lb = lb.reshape(2, HGRN_HEADS, HGRN_DIM)
    f_pre = f_in.reshape(Bz, L, 2, HGRN_HEADS, HGRN_DIM).astype(f32)
    log_f = jnp.logaddexp(jnp.log(lb), jnp.log1p(-lb) + jax.nn.log_sigmoid(f_pre))
    k = -jnp.expm1(log_f)
    q = q_in.reshape(Bz, L, HGRN_HEADS, HGRN_DIM).astype(f32)
    v = i_in.reshape(Bz, L, HGRN_HEADS, HGRN_DIM).astype(f32)
    o, st = bidir_gated_linear(q, k[:, :, 0], k[:, :, 1], v, log_f[:, :, 0], log_f[:, :, 1], s0.astype(f32))
    o = rmsnorm(o, norm_g) * jax.nn.silu(g_in.reshape(Bz, L, HGRN_HEADS, HGRN_DIM).astype(f32))
    return o.reshape(Bz, L, MIX_W), st


def mamba2_branch(z_in, xbc_in, dt_in, conv_w, conv_b, a_log, dt_bias, d_skip, norm_g, s0):
    f32 = jnp.float32
    Bz, L, _ = z_in.shape
    xbc = jax.nn.silu(centred_depthwise_conv(xbc_in, conv_w, conv_b)).astype(f32)
    xs, bm, cm = jnp.split(xbc, [MIX_W, MIX_W + SSM_GROUPS * SSM_STATE], axis=-1)
    xs = xs.reshape(Bz, L, SSM_HEADS, SSM_HEADDIM)
    bm = bm.reshape(Bz, L, SSM_GROUPS, SSM_STATE)
    cm = cm.reshape(Bz, L, SSM_GROUPS, SSM_STATE)
    dt = jax.nn.softplus(dt_in.reshape(Bz, L, 2, SSM_HEADS).astype(f32) + dt_bias.astype(f32))
    a = -jnp.exp(a_log.astype(f32))
    s0 = s0.astype(f32)
    y_f, s_f = ssd_scan(xs * dt[:, :, 0, :, None], dt[:, :, 0] * a[0], bm, cm, s0[:, 0])
    y_b, s_b = ssd_scan(flip(xs * dt[:, :, 1, :, None]), flip(dt[:, :, 1] * a[1]), flip(bm), flip(cm), s0[:, 1])
    y = y_f + flip(y_b) + d_skip.astype(f32)[:, None] * xs
    y = y.reshape(Bz, L, MIX_W) * jax.nn.silu(z_in.astype(f32))
    y = rmsnorm(y.reshape(Bz, L, SSM_GROUPS, MIX_W // SSM_GROUPS),
                norm_g.reshape(SSM_GROUPS, MIX_W // SSM_GROUPS)).reshape(Bz, L, MIX_W)
    return y, jnp.stack([s_f, s_b], axis=1)


def s5_branch(u_in, a_re, a_im, log_dt, b_re, b_im, c_re, c_im, d_skip, glu_w, glu_b, s0_re, s0_im):
    f32 = jnp.float32
    Bz, L, _ = u_in.shape
    u = u_in.reshape(Bz, L, S5_GROUPS, S5_CH).astype(f32)
    y_f, hfr, hfi = s5_scan(u, a_re[0], a_im[0], log_dt[0], b_re, b_im, c_re, c_im, s0_re[:, 0], s0_im[:, 0])
    y_b, hbr, hbi = s5_scan(flip(u), a_re[1], a_im[1], log_dt[1], b_re, b_im, c_re, c_im, s0_re[:, 1], s0_im[:, 1])
    y = (y_f + flip(y_b) + d_skip.astype(f32).reshape(S5_GROUPS, S5_CH) * u).reshape(Bz, L, MIX_W)
    y = jax.nn.gelu(y)
    y = y * jax.nn.sigmoid(y @ glu_w.astype(f32) + glu_b.astype(f32))
    return y, jnp.stack([hfr, hbr], axis=1), jnp.stack([hfi, hbi], axis=1)


def mixer(h, lp, lb, init):
    s_gla, s_hg, s_ssm, s_s5r, s_s5i = init
    Bz, L, _ = h.shape
    proj = h @ lp['w_in']
    points = np.cumsum(IN_SIZES)[:-1].tolist()
    (gla_q, gla_k, gla_v, gla_r, gla_lr, hg_q, hg_f, hg_i, hg_g,
     ssm_z, ssm_xbc, ssm_dt, s5_u, gate_pre) = jnp.split(proj, points, axis=-1)
    o_gla, n_gla = gla_branch(gla_q, gla_k, gla_v, gla_r, gla_lr, lp['gla_wa2'], lp['gla_ba'], lp['gla_norm_g'], s_gla)
    o_hg, n_hg = hgrn2_branch(hg_q, hg_f, hg_i, hg_g, lb, lp['hgrn_norm_g'], s_hg)
    o_ssm, n_ssm = mamba2_branch(ssm_z, ssm_xbc, ssm_dt, lp['ssm_conv_w'], lp['ssm_conv_b'], lp['ssm_a_log'],
                                 lp['ssm_dt_bias'], lp['ssm_d'], lp['ssm_norm_g'], s_ssm)
    o_s5, n_s5r, n_s5i = s5_branch(s5_u, lp['s5_a_re'], lp['s5_a_im'], lp['s5_log_dt'], lp['s5_b_re'], lp['s5_b_im'],
                                   lp['s5_c_re'], lp['s5_c_im'], lp['s5_d'], lp['s5_glu_w'], lp['s5_glu_b'],
                                   s_s5r, s_s5i)
    branches = jnp.stack([o_gla, o_hg, o_ssm, o_s5], axis=2).astype(h.dtype)
    up = jnp.einsum('blnm,nmd->blnd', branches, lp['w_branch'])
    gates = jax.nn.sigmoid(gate_pre.reshape(Bz, L, N_BRANCH, D_MODEL))
    merged = jnp.sum(gates * up, axis=2)
    return merged @ lp['w_out'], (n_gla, n_hg, n_ssm, n_s5r, n_s5i)


def swiglu(h, w1, w3, w2):
    return (jax.nn.silu(h @ w1) * (h @ w3)) @ w2


def moe_swiglu(h, router_w, router_b, w1, w3, w2):
    logits = (h @ router_w).astype(jnp.float32) + router_b.astype(jnp.float32)
    top_val, top_idx = lax.top_k(logits, TOP_K)
    probs = jax.nn.softmax(top_val, axis=-1)
    gate = jnp.sum(jax.nn.one_hot(top_idx, N_EXPERTS, dtype=jnp.float32) * probs[..., None], axis=-2)
    a = jnp.einsum('bld,edf->blef', h, w1)
    b = jnp.einsum('bld,edf->blef', h, w3)
    hid = jax.nn.silu(a) * b * gate[..., None].astype(h.dtype)
    return jnp.einsum('blef,efd->bld', hid, w2)


def trunk_layer(x, mod, lp, lb, ffn, init):
    shift1, scale1, gate1, shift2, scale2, gate2 = jnp.split(mod, 6, axis=-1)
    h = rmsnorm(x, lp['norm1_g']) * (1 + scale1) + shift1
    mix, states = mixer(h, lp, lb, init)
    x = x + gate1 * mix
    h = rmsnorm(x, lp['norm2_g']) * (1 + scale2) + shift2
    x = x + gate2 * ffn(h)
    return x, states


def setup_inputs(seed: int = 0) -> dict:
    key = jax.random.key(seed)
    keys = jax.random.split(key, 64)
    it = iter(range(64))
    f32 = jnp.float32

    def nrm(shape, scale=1.0):
        return jax.random.normal(keys[next(it)], shape, f32) * scale

    def gain(shape):
        return 1.0 + nrm(shape, 0.02)

    def unif(shape, lo, hi):
        return jax.random.uniform(keys[next(it)], shape, f32, lo, hi)

    dt0 = jnp.exp(unif((DEPTH, 2, SSM_HEADS), math.log(1e-3), math.log(1e-1)))
    s5_im_base = jnp.pi * jnp.arange(S5_STATE, dtype=f32)
    return {
        'x_prompt': nrm((BATCH, SEQ, D_MODEL)),
        'x_sample': nrm((DEC_BATCH, DEC_SEQ, D_MODEL)),
        'c': nrm((DEC_BATCH, D_MODEL)),
        'c_ctx': nrm((D_MODEL,)),
        'state_gla': nrm((DEC_BATCH, DEPTH, 2, GLA_HEADS, GLA_DK, GLA_DV), 0.5),
        'state_hgrn': nrm((DEC_BATCH, DEPTH, 2, HGRN_HEADS, HGRN_DIM, HGRN_DIM), 0.5),
        'state_ssm': nrm((DEC_BATCH, DEPTH, 2, SSM_HEADS, SSM_HEADDIM, SSM_STATE), 0.5),
        'state_s5_re': nrm((DEC_BATCH, DEPTH, 2, S5_GROUPS, S5_STATE), 0.1),
        'state_s5_im': nrm((DEC_BATCH, DEPTH, 2, S5_GROUPS, S5_STATE), 0.1),
        'norm1_g': gain((DEPTH, D_MODEL)),
        'norm2_g': gain((DEPTH, D_MODEL)),
        'ada_w': nrm((DEPTH, D_MODEL, 6 * D_MODEL), 0.5 * D_MODEL ** -0.5),
        'ada_b': nrm((DEPTH, 6 * D_MODEL), 0.02),
        'w_in': nrm((DEPTH, D_MODEL, N_IN), D_MODEL ** -0.5),
        'gla_wa2': nrm((DEPTH, 2, GLA_RANK, GLA_HEADS * GLA_DK), GLA_RANK ** -0.5),
        'gla_ba': nrm((DEPTH, 2, GLA_HEADS * GLA_DK), 0.1),
        'gla_norm_g': gain((DEPTH, GLA_DV)),
        'hgrn_lb_logits': nrm((2, DEPTH, MIX_W), 0.1),
        'hgrn_norm_g': gain((DEPTH, HGRN_DIM)),
        'ssm_conv_w': nrm((DEPTH, SSM_CONV, SSM_XBC), SSM_CONV ** -0.5),
        'ssm_conv_b': nrm((DEPTH, SSM_XBC), 0.02),
        'ssm_a_log': jnp.log(unif((DEPTH, 2, SSM_HEADS), 1.0, 16.0)),
        'ssm_dt_bias': dt0 + jnp.log(-jnp.expm1(-dt0)),
        'ssm_d': 1.0 + nrm((DEPTH, SSM_HEADS), 0.1),
        'ssm_norm_g': gain((DEPTH, MIX_W)),
        's5_a_re': -0.5 + nrm((DEPTH, 2, S5_GROUPS, S5_STATE), 0.01),
        's5_a_im': s5_im_base + nrm((DEPTH, 2, S5_GROUPS, S5_STATE), 0.01),
        's5_log_dt': unif((DEPTH, 2, S5_GROUPS), math.log(1e-3), math.log(1e-1)),
        's5_b_re': nrm((DEPTH, S5_GROUPS, S5_STATE, S5_CH), (2 * S5_CH) ** -0.5),
        's5_b_im': nrm((DEPTH, S5_GROUPS, S5_STATE, S5_CH), (2 * S5_CH) ** -0.5),
        's5_c_re': nrm((DEPTH, S5_GROUPS, S5_CH, S5_STATE), (2 * S5_STATE) ** -0.5),
        's5_c_im': nrm((DEPTH, S5_GROUPS, S5_CH, S5_STATE), (2 * S5_STATE) ** -0.5),
        's5_d': nrm((DEPTH, MIX_W)),
        's5_glu_w': nrm((DEPTH, MIX_W, MIX_W), MIX_W ** -0.5),
        's5_glu_b': nrm((DEPTH, MIX_W), 0.02),
        'w_branch': nrm((DEPTH, N_BRANCH, MIX_W, D_MODEL), MIX_W ** -0.5),
        'w_out': nrm((DEPTH, D_MODEL, D_MODEL), D_MODEL ** -0.5),
        'ffn_w1': nrm((N_DENSE, D_MODEL, D_FF), D_MODEL ** -0.5),
        'ffn_w3': nrm((N_DENSE, D_MODEL, D_FF), D_MODEL ** -0.5),
        'ffn_w2': nrm((N_DENSE, D_FF, D_MODEL), D_FF ** -0.5),
        'router_w': nrm((N_MOE, D_MODEL, N_EXPERTS), D_MODEL ** -0.5),
        'router_b': nrm((N_MOE, N_EXPERTS), 0.01),
        'moe_w1': nrm((N_MOE, N_EXPERTS, D_MODEL, D_FF_EXPERT), D_MODEL ** -0.5),
        'moe_w3': nrm((N_MOE, N_EXPERTS, D_MODEL, D_FF_EXPERT), D_MODEL ** -0.5),
        'moe_w2': nrm((N_MOE, N_EXPERTS, D_FF_EXPERT, D_MODEL), D_FF_EXPERT ** -0.5),
        'final_norm_g': gain((D_MODEL,)),
    }


def reference(x_prompt, x_sample, c, c_ctx, state_gla, state_hgrn, state_ssm, state_s5_re, state_s5_im,
              norm1_g, norm2_g, ada_w, ada_b, w_in, gla_wa2, gla_ba, gla_norm_g, hgrn_lb_logits, hgrn_norm_g,
              ssm_conv_w, ssm_conv_b, ssm_a_log, ssm_dt_bias, ssm_d, ssm_norm_g,
              s5_a_re, s5_a_im, s5_log_dt, s5_b_re, s5_b_im, s5_c_re, s5_c_im, s5_d, s5_glu_w, s5_glu_b,
              w_branch, w_out, ffn_w1, ffn_w3, ffn_w2, router_w, router_b, moe_w1, moe_w3, moe_w2,
              final_norm_g):
    f32 = jnp.float32
    p = jax.nn.softmax(hgrn_lb_logits.astype(f32), axis=1)
    lower_bounds = jnp.maximum(jnp.cumsum(p, axis=1) - p[:, :1], 0.0)

    x_ctx = x_prompt
    x_lat = x_sample + grid_pos_embed(x_sample.shape[1], D_MODEL).astype(x_sample.dtype)[None]
    bp = x_prompt.shape[0]
    zero_init = (jnp.zeros((bp, 2, GLA_HEADS, GLA_DK, GLA_DV), f32),
                 jnp.zeros((bp, 2, HGRN_HEADS, HGRN_DIM, HGRN_DIM), f32),
                 jnp.zeros((bp, 2, SSM_HEADS, SSM_HEADDIM, SSM_STATE), f32),
                 jnp.zeros((bp, 2, S5_GROUPS, S5_STATE), f32),
                 jnp.zeros((bp, 2, S5_GROUPS, S5_STATE), f32))
    gla_l, hgrn_l, ssm_l, s5r_l, s5i_l = [], [], [], [], []
    for l in range(DEPTH):
        lp = dict(norm1_g=norm1_g[l], norm2_g=norm2_g[l], w_in=w_in[l],
                  gla_wa2=gla_wa2[l], gla_ba=gla_ba[l], gla_norm_g=gla_norm_g[l],
                  hgrn_norm_g=hgrn_norm_g[l],
                  ssm_conv_w=ssm_conv_w[l], ssm_conv_b=ssm_conv_b[l], ssm_a_log=ssm_a_log[l],
                  ssm_dt_bias=ssm_dt_bias[l], ssm_d=ssm_d[l], ssm_norm_g=ssm_norm_g[l],
                  s5_a_re=s5_a_re[l], s5_a_im=s5_a_im[l], s5_log_dt=s5_log_dt[l],
                  s5_b_re=s5_b_re[l], s5_b_im=s5_b_im[l], s5_c_re=s5_c_re[l], s5_c_im=s5_c_im[l],
                  s5_d=s5_d[l], s5_glu_w=s5_glu_w[l], s5_glu_b=s5_glu_b[l],
                  w_branch=w_branch[l], w_out=w_out[l])
        j = l // 2
        if l % 2 == 0:
            ffn = functools.partial(swiglu, w1=ffn_w1[j], w3=ffn_w3[j], w2=ffn_w2[j])
        else:
            ffn = functools.partial(moe_swiglu, router_w=router_w[j], router_b=router_b[j],
                                    w1=moe_w1[j], w3=moe_w3[j], w2=moe_w2[j])
        lb = lower_bounds[:, l]
        mod_ctx = (jax.nn.silu(c_ctx) @ ada_w[l] + ada_b[l])[None, None, :]
        mod_lat = (jax.nn.silu(c) @ ada_w[l] + ada_b[l])[:, None, :]
        x_ctx, st = trunk_layer(x_ctx, mod_ctx, lp, lb, ffn, zero_init)
        gla_l.append(st[0]); hgrn_l.append(st[1]); ssm_l.append(st[2]); s5r_l.append(st[3]); s5i_l.append(st[4])
        cache_l = (state_gla[:, l], state_hgrn[:, l], state_ssm[:, l], state_s5_re[:, l], state_s5_im[:, l])
        x_lat, _ = trunk_layer(x_lat, mod_lat, lp, lb, ffn, cache_l)
    y_prompt = rmsnorm(x_ctx, final_norm_g)
    y_sample = rmsnorm(x_lat, final_norm_g)
    new_gla = jnp.stack(gla_l, axis=1)
    new_hgrn = jnp.stack(hgrn_l, axis=1)
    new_ssm = jnp.stack(ssm_l, axis=1)
    new_s5_re = jnp.stack(s5r_l, axis=1)
    new_s5_im = jnp.stack(s5i_l, axis=1)
    return (y_prompt, y_sample, new_gla, new_hgrn, new_ssm, new_s5_re, new_s5_im)
```

```python
import functools
import math

import numpy as np
import jax
import jax.numpy as jnp
from jax import lax
from jax.experimental import pallas as pl
from jax.experimental.pallas import tpu as pltpu

f32 = jnp.float32
bf16 = jnp.bfloat16

D = 2048
BATCH = 32
SEQ = 256
DEPTH = 4
DEC_BATCH = 4
DEC_SEQ = 4096
GRID_W = 64
MIX_W = 512
GLA_HEADS = 4
GLA_DK = 64
GLA_DV = 128
GLA_RANK = 16
GLA_NORMALIZER = 16.0
HGRN_HEADS = 4
HGRN_DIM = 128
SSM_HEADS = 8
SSM_HEADDIM = 64
SSM_GROUPS = 2
SSM_STATE = 128
S5_CH = 16
S5_STATE = 64
S5_GROUPS = 32
D_FF = 5632
N_EXPERTS = 8
D_FF_EXPERT = 1024
EPS = 1e-6

N_CTX = BATCH * SEQ
N_LAT = DEC_BATCH * DEC_SEQ
N_TOK = N_CTX + N_LAT
N_MOD = 1 + DEC_BATCH

C_GLA_Q, C_GLA_K, C_GLA_V, C_GLA_R = 0, 256, 512, 1024
C_HG_Q, C_HG_F, C_HG_I, C_HG_G = 1536, 2048, 3072, 3584
C_SSM_Z, C_SSM_XBC, C_S5_U, C_GATE = 4096, 4608, 5632, 6144
N_MAIN = C_GATE + 4 * D
SM_LR, SM_DT = 0, 32

GLA_CHUNK = 64
HGRN_CHUNK = 32
SSD_CHUNK = 128
S5_CHUNK = 16

VMEM_LIMIT = 58 * 1024 * 1024


def _cparams(sem):
    return pltpu.CompilerParams(dimension_semantics=sem, vmem_limit_bytes=VMEM_LIMIT)


def _dot(a, b):
    return jnp.dot(a, b, preferred_element_type=f32)


def _dot_nt(a, b):
    return lax.dot_general(a, b, (((1,), (1,)), ((), ())), preferred_element_type=f32)


def _dot_tn(a, b):
    return lax.dot_general(a, b, (((0,), (0,)), ((), ())), preferred_element_type=f32)


def _split3(x):
    x1 = x.astype(bf16)
    r1 = x - x1.astype(f32)
    x2 = r1.astype(bf16)
    x3 = (r1 - x2.astype(f32)).astype(bf16)
    return x1, x2, x3


def _dot_sel_lhs(t, x):
    x1, x2, x3 = _split3(x)
    return _dot(t, x1) + _dot(t, x2) + _dot(t, x3)


def _dot_sel_rhs(x, e):
    x1, x2, x3 = _split3(x)
    return _dot(x1, e) + _dot(x2, e) + _dot(x3, e)


def _dot_hi(x, w_hi, w_lo):
    x1, x2, _ = _split3(x)
    return _dot(x1, w_hi) + _dot(x2, w_hi) + _dot(x1, w_lo)


def _log_sigmoid(x):
    return jnp.minimum(x, 0.0) - jnp.log1p(jnp.exp(-jnp.abs(x)))


def _softplus(x):
    return jnp.maximum(x, 0.0) + jnp.log1p(jnp.exp(-jnp.abs(x)))


def _silu(x):
    return x * jax.nn.sigmoid(x)


def _modnorm(x, g, scale, shift):
    ms = jnp.mean(x * x, axis=-1, keepdims=True)
    y = x * lax.rsqrt(ms + EPS) * g
    return y * (1.0 + scale) + shift


def _tri(n, upper):
    ii = lax.broadcasted_iota(jnp.int32, (n, n), 0)
    jj = lax.broadcasted_iota(jnp.int32, (n, n), 1)
    return (ii <= jj) if upper else (ii >= jj)


def _mod_row(i, tm):
    start = i * tm
    return jnp.where(start < N_CTX, 0, 1 + (start - N_CTX) // DEC_SEQ)


def _assemble_body(xp_ref, xs_ref, pos_ref, o_ref, *, n_ctx_tiles):
    i = pl.program_id(0)

    @pl.when(i < n_ctx_tiles)
    def _():
        o_ref[...] = xp_ref[...]

    @pl.when(i >= n_ctx_tiles)
    def _():
        o_ref[...] = xs_ref[...] + pos_ref[...]


def assemble_tokens(xp2, xs2, pos, tm=512):
    nct = N_CTX // tm
    npos = DEC_SEQ // tm
    return pl.pallas_call(
        functools.partial(_assemble_body, n_ctx_tiles=nct),
        grid=(N_TOK // tm,),
        in_specs=[
            pl.BlockSpec((tm, D), lambda i: (jnp.minimum(i, nct - 1), 0)),
            pl.BlockSpec((tm, D), lambda i: (jnp.maximum(i - nct, 0), 0)),
            pl.BlockSpec((tm, D), lambda i: (jnp.maximum(i - nct, 0) % npos, 0)),
        ],
        out_specs=pl.BlockSpec((tm, D), lambda i: (i, 0)),
        out_shape=jax.ShapeDtypeStruct((N_TOK, D), f32),
        compiler_params=_cparams(("parallel",)),
        name="assemble_tokens",
    )(xp2, xs2, pos)


def _pos_table():
    rows = DEC_SEQ // GRID_W
    row = jnp.repeat(jnp.arange(rows, dtype=f32), GRID_W)
    col = jnp.tile(jnp.arange(GRID_W, dtype=f32), rows)

    def sincos(p, d):
        half = d // 2
        omega = 1.0 / (10000.0 ** (jnp.arange(half, dtype=f32) / half))
        ang = p[:, None] * omega[None, :]
        return jnp.concatenate([jnp.sin(ang), jnp.cos(ang)], axis=-1)

    return jnp.concatenate([sincos(row, D // 2), sincos(col, D // 2)], axis=-1)


def _ada_body(c_ref, w_ref, b_ref, o_ref):
    c = c_ref[...]
    s = _silu(c)
    s1, s2, _ = _split3(s)
    w = w_ref[0].astype(bf16)
    o_ref[0] = _dot(s1, w) + _dot(s2, w) + b_ref[0]


def ada_modulation(cond8, ada_w, ada_b, tn=1024):
    return pl.pallas_call(
        _ada_body,
        grid=(DEPTH, 6 * D // tn),
        in_specs=[
            pl.BlockSpec((8, D), lambda l, j: (0, 0)),
            pl.BlockSpec((1, D, tn), lambda l, j: (l, 0, j)),
            pl.BlockSpec((1, 1, tn), lambda l, j: (l, 0, j)),
        ],
        out_specs=pl.BlockSpec((1, 8, tn), lambda l, j: (l, 0, j)),
        out_shape=jax.ShapeDtypeStruct((DEPTH, 8, 6 * D), f32),
        compiler_params=_cparams(("parallel", "parallel")),
        name="ada_modulation",
    )(cond8, ada_w, ada_b.reshape(DEPTH, 1, 6 * D))


def _inproj_body(x_ref, sh_ref, sc_ref, g_ref, w_ref, ws_ref, o_ref, os_ref, h_s):
    @pl.when(pl.program_id(1) == 0)
    def _():
        h = _modnorm(x_ref[...], g_ref[...], sc_ref[0], sh_ref[0]).astype(bf16)
        h_s[...] = h
        os_ref[...] = _dot(h, ws_ref[...])

    o_ref[...] = _dot(h_s[...], w_ref[...])


def in_projection(x, mod, g, w_main, w_small, tm=512, tn=1024):
    n = w_main.shape[1]
    return pl.pallas_call(
        _inproj_body,
        grid=(N_TOK // tm, n // tn),
        in_specs=[
            pl.BlockSpec((tm, D), lambda i, j: (i, 0)),
            pl.BlockSpec((1, 1, D), lambda i, j: (_mod_row(i, tm) * 6 + 0, 0, 0)),
            pl.BlockSpec((1, 1, D), lambda i, j: (_mod_row(i, tm) * 6 + 1, 0, 0)),
            pl.BlockSpec((1, D), lambda i, j: (0, 0)),
            pl.BlockSpec((D, tn), lambda i, j: (0, j)),
            pl.BlockSpec((D, 128), lambda i, j: (0, 0)),
        ],
        out_specs=[
            pl.BlockSpec((tm, tn), lambda i, j: (i, j)),
            pl.BlockSpec((tm, 128), lambda i, j: (i, 0)),
        ],
        out_shape=[
            jax.ShapeDtypeStruct((N_TOK, n), f32),
            jax.ShapeDtypeStruct((N_TOK, 128), f32),
        ],
        scratch_shapes=[pltpu.VMEM((tm, D), bf16)],
        compiler_params=_cparams(("parallel", "arbitrary")),
        name="in_projection",
    )(x, mod, mod, g.reshape(1, D), w_main, w_small)


def _gls_dir(q, k, g, d, chunk):
    c = chunk
    tri = jnp.where(_tri(c, upper=(d == 1)), 1.0, 0.0).astype(bf16)
    b = _dot_sel_lhs(tri, g)
    if d == 0:
        r, be = b[c // 2 - 1:c // 2], b[c - 1:c]
    else:
        r, be = b[c // 2:c // 2 + 1], b[0:1]
    qg = q * jnp.exp(b - r)
    kg = k * jnp.exp(r - b)
    q_in = qg * jnp.exp(r)
    k_out = kg * jnp.exp(be - r)
    return qg, kg, q_in, k_out, jnp.exp(be)


def _gls_head(qg, kg, q_in, k_out, dec, v, st, mask, kmask):
    if kmask is not None:
        qg = qg * kmask
        q_in = q_in * kmask
        k_out = k_out * kmask
    att = jnp.where(mask, _dot_nt(qg.astype(bf16), kg.astype(bf16)), 0.0)
    vb = v.astype(bf16)
    o = _dot(att.astype(bf16), vb) + _dot_nt(q_in.astype(bf16), st.astype(bf16))
    st_new = st * dec + _dot_tn(vb, k_out.astype(bf16))
    return o, st_new


def _head_post(o, gate_in, ng):
    ms = jnp.mean(o * o, axis=-1, keepdims=True)
    return o * lax.rsqrt(ms + EPS) * ng * _silu(gate_in)


def _gla_body(*refs, L, has_init, emit_state):
    it = iter(refs)
    q_ref, k_ref, v_ref, r_ref, sm_ref, wah_ref, wal_ref, ba_ref, ng_ref = [next(it) for _ in range(9)]
    s0_ref = next(it) if has_init else None
    _prev = next(it) if has_init else None
    o_ref = next(it)
    st_ref = next(it) if emit_state else None
    C = GLA_CHUNK
    n = L // C
    lane = lax.broadcasted_iota(jnp.int32, (1, 128), 1)
    kmasks = [jnp.where(lane < 64, 1.0, 0.0), jnp.where(lane >= 64, 1.0, 0.0)]
    masks = [_tri(C, upper=False), _tri(C, upper=True)]
    wah = wah_ref[0]
    wal = wal_ref[0]
    ba = ba_ref[0]
    ng = ng_ref[...]

    def chunk(cidx, d, sts):
        r0 = pl.multiple_of(cidx * C, C)
        q = q_ref[pl.ds(r0, C), :] * (GLA_DK ** -0.5)
        k = k_ref[pl.ds(r0, C), :]
        sm = sm_ref[pl.ds(r0, C), :]
        pre = _dot_hi(sm, wah[:, d * 128:(d + 1) * 128], wal[:, d * 128:(d + 1) * 128]) + ba[:, d * 128:(d + 1) * 128]
        g = _log_sigmoid(pre) * (1.0 / GLA_NORMALIZER)
        qg, kg, q_in, k_out, dec = _gls_dir(q, k, g, d, C)
        outs, new = [], []
        for h in range(2):
            v = v_ref[pl.ds(r0, C), h * 128:(h + 1) * 128]
            o, s = _gls_head(qg, kg, q_in, k_out, dec, v, sts[h], masks[d], kmasks[h])
            outs.append(o)
            new.append(s)
        return r0, outs, new

    def first_half(t, carry):
        sf, sb = carry
        r0, of, sf = chunk(t, 0, sf)
        for h in range(2):
            o_ref[pl.ds(r0, C), h * 128:(h + 1) * 128] = of[h]
        r1, ob, sb = chunk(n - 1 - t, 1, sb)
        for h in range(2):
            o_ref[pl.ds(r1, C), h * 128:(h + 1) * 128] = ob[h]
        return sf, sb

    def second_half(t, carry):
        sf, sb = carry
        for d in range(2):
            cidx = t if d == 0 else n - 1 - t
            r0, o, s = chunk(cidx, d, sf if d == 0 else sb)
            for h in range(2):
                tot = o_ref[pl.ds(r0, C), h * 128:(h + 1) * 128] + o[h]
                gate = r_ref[pl.ds(r0, C), h * 128:(h + 1) * 128]
                o_ref[pl.ds(r0, C), h * 128:(h + 1) * 128] = _head_post(tot, gate, ng)
            if d == 0:
                sf = s
            else:
                sb = s
        return sf, sb

    def init(d, h):
        if not has_init:
            return jnp.zeros((128, 128), f32)
        s = s0_ref[0, 0, d, h]
        z = jnp.zeros((64, 128), f32)
        full = jnp.concatenate([s, z], axis=0) if h == 0 else jnp.concatenate([z, s], axis=0)
        return full.T

    carry = ([init(0, 0), init(0, 1)], [init(1, 0), init(1, 1)])
    carry = lax.fori_loop(0, n // 2, first_half, carry)
    sf, sb = lax.fori_loop(n // 2, n, second_half, carry)
    if emit_state:
        for d, sts in enumerate((sf, sb)):
            for h in range(2):
                st_ref[0, d, h] = sts[h].T[h * 64:(h + 1) * 64, :]


def gla_mixer(P, SM, wa_hi, wa_lo, ba, ng, *, B, L, row0, state=None, layer=0, prev=None):
    has_init = state is not None
    emit_state = not has_init
    in_specs = [
        pl.BlockSpec((L, 128), lambda b, hp: (row0 + b, C_GLA_Q // 128 + hp)),
        pl.BlockSpec((L, 128), lambda b, hp: (row0 + b, C_GLA_K // 128 + hp)),
        pl.BlockSpec((L, 256), lambda b, hp: (row0 + b, C_GLA_V // 256 + hp)),
        pl.BlockSpec((L, 256), lambda b, hp: (row0 + b, C_GLA_R // 256 + hp)),
        pl.BlockSpec((L, 128), lambda b, hp: (row0 + b, 0)),
        pl.BlockSpec((1, 128, 256), lambda b, hp: (hp, 0, 0)),
        pl.BlockSpec((1, 128, 256), lambda b, hp: (hp, 0, 0)),
        pl.BlockSpec((1, 1, 256), lambda b, hp: (hp, 0, 0)),
        pl.BlockSpec((1, 128), lambda b, hp: (0, 0)),
    ]
    args = [P, P, P, P, SM, wa_hi, wa_lo, ba, ng]
    aliases = {}
    if has_init:
        in_specs.append(pl.BlockSpec((1, 1, 2, 2, 64, 128), lambda b, hp: (b, layer, 0, hp, 0, 0)))
        args.append(state)
        in_specs.append(pl.BlockSpec(memory_space=pl.ANY))
        args.append(prev)
        aliases = {len(args) - 1: 0}
    out_specs = [pl.BlockSpec((L, 256), lambda b, hp: (row0 + b, hp))]
    out_shape = [jax.ShapeDtypeStruct((N_TOK, MIX_W), f32)]
    if emit_state:
        out_specs.append(pl.BlockSpec((1, 2, 2, 64, 128), lambda b, hp: (b, 0, hp, 0, 0)))
        out_shape.append(jax.ShapeDtypeStruct((B, 2, GLA_HEADS, GLA_DK, GLA_DV), f32))
    return pl.pallas_call(
        functools.partial(_gla_body, L=L, has_init=has_init, emit_state=emit_state),
        grid=(B, 2),
        in_specs=in_specs,
        out_specs=out_specs,
        out_shape=out_shape,
        input_output_aliases=aliases,
        compiler_params=_cparams(("parallel", "parallel")),
        name="gla_mixer",
    )(*args)


def _hgrn_body(*refs, L, has_init, emit_state):
    it = iter(refs)
    q_ref, f0_ref, f1_ref, i_ref, g_ref, lb_ref, ng_ref = [next(it) for _ in range(7)]
    s0_ref = next(it) if has_init else None
    _prev = next(it) if has_init else None
    o_ref = next(it)
    st_ref = next(it) if emit_state else None
    C = HGRN_CHUNK
    n = L // C
    masks = [_tri(C, upper=False), _tri(C, upper=True)]
    lbp = lb_ref[0]
    ng = ng_ref[...]
    f_refs = (f0_ref, f1_ref)

    def chunk(cidx, d, st):
        r0 = pl.multiple_of(cidx * C, C)
        q = q_ref[pl.ds(r0, C), :]
        fp = f_refs[d][pl.ds(r0, C), :]
        v = i_ref[pl.ds(r0, C), :]
        a = lbp[3 * d:3 * d + 1]
        bq = lbp[3 * d + 1:3 * d + 2] + _log_sigmoid(fp)
        m = jnp.maximum(a, bq)
        g = m + jnp.log1p(jnp.exp(-jnp.abs(a - bq)))
        k = lbp[3 * d + 2:3 * d + 3] * jax.nn.sigmoid(-fp)
        qg, kg, q_in, k_out, dec = _gls_dir(q, k, g, d, C)
        o, s = _gls_head(qg, kg, q_in, k_out, dec, v, st, masks[d], None)
        return r0, o, s

    def first_half(t, carry):
        sf, sb = carry
        r0, of, sf = chunk(t, 0, sf)
        o_ref[pl.ds(r0, C), :] = of
        r1, ob, sb = chunk(n - 1 - t, 1, sb)
        o_ref[pl.ds(r1, C), :] = ob
        return sf, sb

    def second_half(t, carry):
        sf, sb = carry
        r0, of, sf = chunk(t, 0, sf)
        o_ref[pl.ds(r0, C), :] = _head_post(o_ref[pl.ds(r0, C), :] + of, g_ref[pl.ds(r0, C), :], ng)
        r1, ob, sb = chunk(n - 1 - t, 1, sb)
        o_ref[pl.ds(r1, C), :] = _head_post(o_ref[pl.ds(r1, C), :] + ob, g_ref[pl.ds(r1, C), :], ng)
        return sf, sb

    def init(d):
        if not has_init:
            return jnp.zeros((128, 128), f32)
        return s0_ref[0, 0, d, 0].T

    carry = (init(0), init(1))
    carry = lax.fori_loop(0, n // 2, first_half, carry)
    sf, sb = lax.fori_loop(n // 2, n, second_half, carry)
    if emit_state:
        st_ref[0, 0, 0] = sf.T
        st_ref[0, 1, 0] = sb.T


def hgrn_mixer(P, lbp, ng, *, B, L, row0, state=None, layer=0, prev=None):
    has_init = state is not None
    emit_state = not has_init
    in_specs = [
        pl.BlockSpec((L, 128), lambda b, h: (row0 + b, C_HG_Q // 128 + h)),
        pl.BlockSpec((L, 128), lambda b, h: (row0 + b, C_HG_F // 128 + h)),
        pl.BlockSpec((L, 128), lambda b, h: (row0 + b, C_HG_F // 128 + HGRN_HEADS + h)),
        pl.BlockSpec((L, 128), lambda b, h: (row0 + b, C_HG_I // 128 + h)),
        pl.BlockSpec((L, 128), lambda b, h: (row0 + b, C_HG_G // 128 + h)),
        pl.BlockSpec((1, 8, 128), lambda b, h: (h, 0, 0)),
        pl.BlockSpec((1, 128), lambda b, h: (0, 0)),
    ]
    args = [P, P, P, P, P, lbp, ng]
    aliases = {}
    if has_init:
        in_specs.append(pl.BlockSpec((1, 1, 2, 1, 128, 128), lambda b, h: (b, layer, 0, h, 0, 0)))
        args.append(state)
        in_specs.append(pl.BlockSpec(memory_space=pl.ANY))
        args.append(prev)
        aliases = {len(args) - 1: 0}
    out_specs = [pl.BlockSpec((L, 128), lambda b, h: (row0 + b, h))]
    out_shape = [jax.ShapeDtypeStruct((N_TOK, MIX_W), f32)]
    if emit_state:
        out_specs.append(pl.BlockSpec((1, 2, 1, 128, 128), lambda b, h: (b, 0, h, 0, 0)))
        out_shape.append(jax.ShapeDtypeStruct((B, 2, HGRN_HEADS, HGRN_DIM, HGRN_DIM), f32))
    return pl.pallas_call(
        functools.partial(_hgrn_body, L=L, has_init=has_init, emit_state=emit_state),
        grid=(B, HGRN_HEADS),
        in_specs=in_specs,
        out_specs=out_specs,
        out_shape=out_shape,
        input_output_aliases=aliases,
        compiler_params=_cparams(("parallel", "parallel")),
        name="hgrn_mixer",
    )(*args)


def _ssd_body(*refs, L, has_init, emit_state):
    it = iter(refs)
    (xs_ref, bm_ref, cm_ref, z_ref, sm_ref, cw_ref, dtb_ref, alog_ref, e64_ref, e128_ref,
     dsk_ref, ng_ref) = [next(it) for _ in range(12)]
    s0_ref = next(it) if has_init else None
    _prev = next(it) if has_init else None
    o_ref = next(it)
    st_ref = next(it) if emit_state else None
    xc_s = next(it)
    C = SSD_CHUNK
    n = L // C
    cw = cw_ref[0]
    rowi = lax.broadcasted_iota(jnp.int32, (C, 1), 0)

    def conv_block(rb, _):
        r0 = pl.multiple_of(rb * C, C)
        rp = pl.multiple_of(jnp.maximum(r0 - 8, 0), 8)
        rn = pl.multiple_of(jnp.minimum(r0 + C, L - 8), 8)
        has_p = (rb > 0).astype(f32)
        has_n = (rb < n - 1).astype(f32)
        for ref, c0, w in ((xs_ref, 0, 256), (bm_ref, 256, 128), (cm_ref, 384, 128)):
            x = ref[pl.ds(r0, C), :]
            prev = ref[pl.ds(rp, 8), :][7:8] * has_p
            nxt = ref[pl.ds(rn, 8), :][0:1] * has_n
            x_dn = jnp.where(rowi == 0, prev, pltpu.roll(x, 1, axis=0))
            x_up = jnp.where(rowi == C - 1, nxt, pltpu.roll(x, C - 1, axis=0))
            y = (cw[0:1, c0:c0 + w] * x_dn + cw[1:2, c0:c0 + w] * x + cw[2:3, c0:c0 + w] * x_up
                 + cw[3:4, c0:c0 + w])
            xc_s[pl.ds(r0, C), c0:c0 + w] = _silu(y)
        return 0

    lax.fori_loop(0, n, conv_block, 0)

    masks = [_tri(C, upper=False), _tri(C, upper=True)]
    tris = [jnp.where(m, 1.0, 0.0).astype(bf16) for m in masks]
    lane256 = lax.broadcasted_iota(jnp.int32, (1, 256), 1)
    hmask = [jnp.where((lane256 >= h * 64) & (lane256 < (h + 1) * 64), 1.0, 0.0) for h in range(4)]
    dtb = dtb_ref[...]
    a_row = -jnp.exp(alog_ref[...])
    neg_inf = jnp.float32(-jnp.inf)

    def chunk(cidx, d, st):
        r0 = pl.multiple_of(cidx * C, C)
        xs = xc_s[pl.ds(r0, C), 0:256]
        bm = xc_s[pl.ds(r0, C), 256:384].astype(bf16)
        cm = xc_s[pl.ds(r0, C), 384:512].astype(bf16)
        dt_all = _softplus(sm_ref[pl.ds(r0, C), :] + dtb)
        a_all = dt_all * a_row
        dt64 = _dot_sel_rhs(dt_all, e64_ref[0, d])
        a64 = _dot_sel_rhs(a_all, e64_ref[0, d])
        a128 = _dot_sel_rhs(a_all, e128_ref[0, d])
        acs64 = _dot_sel_lhs(tris[d], a64)
        acs128 = _dot_sel_lhs(tris[d], a128)
        end = acs64[C - 1:C] if d == 0 else acs64[0:1]
        xdt = xs * dt64
        scores = _dot_nt(cm, bm)
        y = jnp.exp(acs64) * _dot(cm, st.astype(bf16))
        for h in range(4):
            colb = acs128[:, h * 128:(h + 1) * 128]
            diff = jnp.where(masks[d], colb - colb.T, neg_inf)
            p = (scores * jnp.exp(diff)).astype(bf16)
            y = y + _dot(p, (xdt * hmask[h]).astype(bf16))
        xd = (xdt * jnp.exp(end - acs64)).astype(bf16)
        st_new = st * jnp.exp(end) + _dot_tn(bm, xd)
        return r0, y, st_new

    dsk = dsk_ref[0]
    ng = ng_ref[0]

    def post(r0, y):
        xs = xc_s[pl.ds(r0, C), 0:256]
        t = (y + dsk * xs) * _silu(z_ref[pl.ds(r0, C), :])
        ms = jnp.mean(t * t, axis=-1, keepdims=True)
        return t * lax.rsqrt(ms + EPS) * ng

    def first_half(t, carry):
        sf, sb = carry
        r0, yf, sf = chunk(t, 0, sf)
        o_ref[pl.ds(r0, C), :] = yf
        r1, yb, sb = chunk(n - 1 - t, 1, sb)
        o_ref[pl.ds(r1, C), :] = yb
        return sf, sb

    def second_half(t, carry):
        sf, sb = carry
        r0, yf, sf = chunk(t, 0, sf)
        o_ref[pl.ds(r0, C), :] = post(r0, o_ref[pl.ds(r0, C), :] + yf)
        r1, yb, sb = chunk(n - 1 - t, 1, sb)
        o_ref[pl.ds(r1, C), :] = post(r1, o_ref[pl.ds(r1, C), :] + yb)
        return sf, sb

    def init(d):
        if not has_init:
            return jnp.zeros((128, 256), f32)
        return s0_ref[0, 0, d].reshape(256, 128).T

    carry = (init(0), init(1))
    carry = lax.fori_loop(0, n // 2, first_half, carry)
    sf, sb = lax.fori_loop(n // 2, n, second_half, carry)
    if emit_state:
        st_ref[0, 0] = sf.T.reshape(4, 64, 128)
        st_ref[0, 1] = sb.T.reshape(4, 64, 128)


def ssd_mixer(P, SM, cw, dtb, alog, e64, e128, dsk, ng, *, B, L, row0, state=None, layer=0, prev=None):
    has_init = state is not None
    emit_state = not has_init
    xb = C_SSM_XBC
    in_specs = [
        pl.BlockSpec((L, 256), lambda b, g: (row0 + b, xb // 256 + g)),
        pl.BlockSpec((L, 128), lambda b, g: (row0 + b, (xb + 512) // 128 + g)),
        pl.BlockSpec((L, 128), lambda b, g: (row0 + b, (xb + 768) // 128 + g)),
        pl.BlockSpec((L, 256), lambda b, g: (row0 + b, C_SSM_Z // 256 + g)),
        pl.BlockSpec((L, 128), lambda b, g: (row0 + b, 0)),
        pl.BlockSpec((1, 4, 512), lambda b, g: (g, 0, 0)),
        pl.BlockSpec((1, 128), lambda b, g: (0, 0)),
        pl.BlockSpec((1, 128), lambda b, g: (0, 0)),
        pl.BlockSpec((1, 2, 128, 256), lambda b, g: (g, 0, 0, 0)),
        pl.BlockSpec((1, 2, 128, 512), lambda b, g: (g, 0, 0, 0)),
        pl.BlockSpec((1, 1, 256), lambda b, g: (g, 0, 0)),
        pl.BlockSpec((1, 1, 256), lambda b, g: (g, 0, 0)),
    ]
    args = [P, P, P, P, SM, cw, dtb, alog, e64, e128, dsk, ng]
    aliases = {}
    if has_init:
        in_specs.append(pl.BlockSpec((1, 1, 2, 4, 64, 128), lambda b, g: (b, layer, 0, g, 0, 0)))
        args.append(state)
        in_specs.append(pl.BlockSpec(memory_space=pl.ANY))
        args.append(prev)
        aliases = {len(args) - 1: 0}
    out_specs = [pl.BlockSpec((L, 256), lambda b, g: (row0 + b, g))]
    out_shape = [jax.ShapeDtypeStruct((N_TOK, MIX_W), f32)]
    if emit_state:
        out_specs.append(pl.BlockSpec((1, 2, 4, 64, 128), lambda b, g: (b, 0, g, 0, 0)))
        out_shape.append(jax.ShapeDtypeStruct((B, 2, SSM_HEADS, SSM_HEADDIM, SSM_STATE), f32))
    return pl.pallas_call(
        functools.partial(_ssd_body, L=L, has_init=has_init, emit_state=emit_state),
        grid=(B, SSM_GROUPS),
        in_specs=in_specs,
        out_specs=out_specs,
        out_shape=out_shape,
        scratch_shapes=[pltpu.VMEM((L, 512), f32)],
        input_output_aliases=aliases,
        compiler_params=_cparams(("parallel", "parallel")),
        name="ssd_mixer",
    )(*args)


def _s5_body(*refs, n, n_c, s_b, has_init, emit_state):
    it = iter(refs)
    u_ref, wt_ref, wsf_ref, wsb_ref, wof_ref, wob_ref, lam_ref, dsk_ref = [next(it) for _ in range(8)]
    if has_init:
        h0r_ref, h0i_ref = next(it), next(it)
        _prev = next(it)
    o_ref = next(it)
    if emit_state:
        hr_ref, hi_ref = next(it), next(it)
    zf_s, zb_s, hf_s, hb_s, acc_s, za_s, zc_s = [next(it) for _ in range(7)]
    T = S5_CHUNK

    for j in range(T):
        ub = u_ref[pl.ds(j, n, stride=T), :].astype(bf16)
        a = _dot(ub, wt_ref[0, :, (T - 1 - j) * 128:(2 * T - 1 - j) * 128])
        f = _dot(ub, wsf_ref[0, j])
        b = _dot(ub, wsb_ref[0, j])
        if j == 0:
            acc_s[...] = a
            za_s[...] = f
            zc_s[...] = b
        else:
            acc_s[...] += a
            za_s[...] += f
            zc_s[...] += b
    for kk in range(8):
        zf_s[kk] = za_s[:, kk * 128:(kk + 1) * 128]
        zb_s[kk] = zc_s[:, kk * 128:(kk + 1) * 128]

    lam = lam_ref[0]

    def rows(c):
        return pl.ds(c, s_b, stride=n_c) if s_b > 1 else pl.ds(c, 1)

    def scan_step(c, h, z_s, h_s, lr, li):
        new = []
        for kk in range(8):
            h_s[kk, rows(c), :] = h[kk]
        for kk in range(4):
            l_r, l_i = lr[:, kk * 128:(kk + 1) * 128], li[:, kk * 128:(kk + 1) * 128]
            new.append(l_r * h[kk] - l_i * h[kk + 4] + z_s[kk, rows(c), :])
        for kk in range(4):
            l_r, l_i = lr[:, kk * 128:(kk + 1) * 128], li[:, kk * 128:(kk + 1) * 128]
            new.append(l_r * h[kk + 4] + l_i * h[kk] + z_s[kk + 4, rows(c), :])
        return tuple(new)

    def fstep(c, h):
        return scan_step(c, h, zf_s, hf_s, lam[0:1], lam[1:2])

    def bstep(t, h):
        return scan_step(n_c - 1 - t, h, zb_s, hb_s, lam[2:3], lam[3:4])

    def chunks8(re, im):
        return tuple([re[:, kk * 128:(kk + 1) * 128] for kk in range(4)]
                     + [im[:, kk * 128:(kk + 1) * 128] for kk in range(4)])

    if has_init:
        h0f = chunks8(h0r_ref[:, 0, 0, :], h0i_ref[:, 0, 0, :])
        h0b = chunks8(h0r_ref[:, 0, 1, :], h0i_ref[:, 0, 1, :])
    else:
        h0f = h0b = tuple(jnp.zeros((s_b, 128), f32) for _ in range(8))
    hf = lax.fori_loop(0, n_c, fstep, h0f)
    hb = lax.fori_loop(0, n_c, bstep, h0b)
    if emit_state:
        for kk in range(4):
            hr_ref[:, 0, kk * 128:(kk + 1) * 128] = hf[kk]
            hi_ref[:, 0, kk * 128:(kk + 1) * 128] = hf[kk + 4]
            hr_ref[:, 1, kk * 128:(kk + 1) * 128] = hb[kk]
            hi_ref[:, 1, kk * 128:(kk + 1) * 128] = hb[kk + 4]

    hin_f = jnp.concatenate([hf_s[kk] for kk in range(8)], axis=1).astype(bf16)
    hin_b = jnp.concatenate([hb_s[kk] for kk in range(8)], axis=1).astype(bf16)
    acc_s[...] += _dot(hin_f, wof_ref[0])
    acc_s[...] += _dot(hin_b, wob_ref[0])
    dsk = dsk_ref[0]
    for i in range(T):
        o_ref[pl.ds(i, n, stride=T), :] = (acc_s[:, i * 128:(i + 1) * 128]
                                           + dsk * u_ref[pl.ds(i, n, stride=T), :])


def s5_mixer(P, w, dsk, *, B, L, s_b, row0, state_re=None, state_im=None, layer=0, prev=None):
    has_init = state_re is not None
    emit_state = not has_init
    n_c = L // S5_CHUNK
    n = s_b * n_c
    rows = s_b * L
    in_specs = [
        pl.BlockSpec((rows, 128), lambda lb, sb: (row0 + sb, C_S5_U // 128 + lb)),
        pl.BlockSpec((1, 128, 31 * 128), lambda lb, sb: (lb, 0, 0), pipeline_mode=pl.Buffered(1)),
        pl.BlockSpec((1, 16, 128, 1024), lambda lb, sb: (lb, 0, 0, 0), pipeline_mode=pl.Buffered(1)),
        pl.BlockSpec((1, 16, 128, 1024), lambda lb, sb: (lb, 0, 0, 0), pipeline_mode=pl.Buffered(1)),
        pl.BlockSpec((1, 1024, 2048), lambda lb, sb: (lb, 0, 0), pipeline_mode=pl.Buffered(1)),
        pl.BlockSpec((1, 1024, 2048), lambda lb, sb: (lb, 0, 0), pipeline_mode=pl.Buffered(1)),
        pl.BlockSpec((1, 4, 512), lambda lb, sb: (lb, 0, 0)),
        pl.BlockSpec((1, 1, 128), lambda lb, sb: (lb, 0, 0)),
    ]
    args = [P, w["toep"], w["ws_f"], w["ws_b"], w["wo_f"], w["wo_b"], w["lam16"], dsk]
    aliases = {}
    if has_init:
        in_specs.append(pl.BlockSpec((s_b, 1, 2, 512), lambda lb, sb: (sb, layer, 0, lb)))
        in_specs.append(pl.BlockSpec((s_b, 1, 2, 512), lambda lb, sb: (sb, layer, 0, lb)))
        args += [state_re, state_im]
        in_specs.append(pl.BlockSpec(memory_space=pl.ANY))
        args.append(prev)
        aliases = {len(args) - 1: 0}
    out_specs = [pl.BlockSpec((rows, 128), lambda lb, sb: (row0 + sb, lb))]
    out_shape = [jax.ShapeDtypeStruct((N_TOK, MIX_W), f32)]
    if emit_state:
        out_specs += [pl.BlockSpec((s_b, 2, 512), lambda lb, sb: (sb, 0, lb))] * 2
        out_shape += [jax.ShapeDtypeStruct((B, 2, S5_GROUPS * S5_STATE), f32)] * 2
    return pl.pallas_call(
        functools.partial(_s5_body, n=n, n_c=n_c, s_b=s_b, has_init=has_init, emit_state=emit_state),
        grid=(4, B // s_b),
        in_specs=in_specs,
        out_specs=out_specs,
        out_shape=out_shape,
        scratch_shapes=([pltpu.VMEM((8, n, 128), f32)] * 4
                        + [pltpu.VMEM((n, 2048), f32), pltpu.VMEM((n, 1024), f32), pltpu.VMEM((n, 1024), f32)]),
        input_output_aliases=aliases,
        compiler_params=_cparams(("parallel", "parallel")),
        name="s5_mixer",
    )(*args)


def s5_derived_weights(a_re, a_im, log_dt, b_re, b_im, c_re, c_im):
    hp = lax.Precision.HIGHEST
    G, Pn, Q, T = S5_GROUPS, S5_STATE, S5_CH, S5_CHUNK
    dt = jnp.exp(log_dt)[..., None]
    lmag, ang = a_re * dt, a_im * dt
    taus = jnp.arange(T + 1, dtype=f32)[:, None, None, None]
    pr = jnp.exp(lmag[None] * taus) * jnp.cos(ang[None] * taus)
    pi = jnp.exp(lmag[None] * taus) * jnp.sin(ang[None] * taus)
    lam_r, lam_i = pr[1], pi[1]
    den = a_re * a_re + a_im * a_im
    zr = ((lam_r - 1.0) * a_re + lam_i * a_im) / den
    zi = (lam_i * a_re - (lam_r - 1.0) * a_im) / den
    bb_r = zr[..., None] * b_re[None] - zi[..., None] * b_im[None]
    bb_i = zr[..., None] * b_im[None] + zi[..., None] * b_re[None]
    lb_r = pr[..., None] * bb_r[None] - pi[..., None] * bb_i[None]
    lb_i = pr[..., None] * bb_i[None] + pi[..., None] * bb_r[None]
    kk = (jnp.einsum("gqp,tdgpr->tdgqr", c_re, lb_r, precision=hp)
          - jnp.einsum("gqp,tdgpr->tdgqr", c_im, lb_i, precision=hp))
    eye8 = jnp.eye(8, dtype=f32)

    def blockdiag(m):
        lead = m.shape[:-3]
        a, b = m.shape[-2], m.shape[-1]
        m = m.reshape(lead + (4, 8, a, b))
        out = m[..., :, :, :, None, :] * eye8[:, None, :, None]
        return out.reshape(lead + (4, 8 * a, 8 * b))

    kt = jnp.swapaxes(kk, -1, -2)
    deltas = []
    for dl in range(-(T - 1), T):
        if dl < 0:
            deltas.append(kt[-dl, 1])
        elif dl == 0:
            deltas.append(kt[0, 0] + kt[0, 1])
        else:
            deltas.append(kt[dl, 0])
    toep = blockdiag(jnp.stack(deltas, 0))
    toep = jnp.transpose(toep, (1, 2, 0, 3)).reshape(4, 128, 31 * 128)

    def state_in(lbr, lbi):
        re = blockdiag(jnp.swapaxes(lbr, -1, -2))
        im = blockdiag(jnp.swapaxes(lbi, -1, -2))
        return jnp.transpose(jnp.concatenate([re, im], axis=-1), (1, 0, 2, 3))

    ws_f = state_in(lb_r[:T, 0][::-1], lb_i[:T, 0][::-1])
    ws_b = state_in(lb_r[:T, 1], lb_i[:T, 1])

    def state_out(d, exps):
        cols = []
        for e in exps:
            cr = c_re * pr[e, d][:, None, :] - c_im * pi[e, d][:, None, :]
            ci = c_re * pi[e, d][:, None, :] + c_im * pr[e, d][:, None, :]
            re = blockdiag(jnp.swapaxes(cr, -1, -2))
            im = blockdiag(jnp.swapaxes(-ci, -1, -2))
            cols.append(jnp.concatenate([re, im], axis=1))
        return jnp.concatenate(cols, axis=-1)

    wo_f = state_out(0, [i + 1 for i in range(T)])
    wo_b = state_out(1, [T - i for i in range(T)])
    lam16 = jnp.stack([pr[T, 0], pi[T, 0], pr[T, 1], pi[T, 1]], 0)
    lam16 = jnp.transpose(lam16.reshape(4, 4, 512), (1, 0, 2))
    return dict(toep=toep.astype(bf16), ws_f=ws_f.astype(bf16), ws_b=ws_b.astype(bf16),
                wo_f=wo_f.astype(bf16), wo_b=wo_b.astype(bf16), lam16=lam16)


def _glu_body(y_ref, w_ref, b_ref, o_ref):
    g = jax.nn.gelu(y_ref[...])
    o_ref[...] = g * jax.nn.sigmoid(_dot(g.astype(bf16), w_ref[...]) + b_ref[...])


def s5_glu(y, w, b, tm=512):
    return pl.pallas_call(
        _glu_body,
        grid=(N_TOK // tm,),
        in_specs=[
            pl.BlockSpec((tm, MIX_W), lambda i: (i, 0)),
            pl.BlockSpec((MIX_W, MIX_W), lambda i: (0, 0)),
            pl.BlockSpec((1, MIX_W), lambda i: (0, 0)),
        ],
        out_specs=pl.BlockSpec((tm, MIX_W), lambda i: (i, 0)),
        out_shape=jax.ShapeDtypeStruct((N_TOK, MIX_W), f32),
        compiler_params=_cparams(("parallel",)),
        name="s5_glu",
    )(y, w, b)


def _merge_body(gp_ref, b0_ref, b1_ref, b2_ref, b3_ref, wb_ref, wo_ref, x_ref, g1_ref, o_ref, acc_s):
    b = pl.program_id(1)
    for bb, br_ref in enumerate((b0_ref, b1_ref, b2_ref, b3_ref)):
        @pl.when(b == bb)
        def _(br_ref=br_ref, bb=bb):
            up = _dot(br_ref[...].astype(bf16), wb_ref[0])
            t = jax.nn.sigmoid(gp_ref[...]) * up
            if bb == 0:
                acc_s[...] = t
            else:
                acc_s[...] += t

    @pl.when(b == 3)
    def _():
        mix = _dot(acc_s[...].astype(bf16), wo_ref[...])
        o_ref[...] = x_ref[...] + g1_ref[0] * mix


def merge_branches(P, branches, w_branch, w_out, x, mod, tm=256):
    br_spec = pl.BlockSpec((tm, MIX_W), lambda i, b: (i, 0))
    return pl.pallas_call(
        _merge_body,
        grid=(N_TOK // tm, 4),
        in_specs=[
            pl.BlockSpec((tm, D), lambda i, b: (i, C_GATE // D + b)),
            br_spec, br_spec, br_spec, br_spec,
            pl.BlockSpec((1, MIX_W, D), lambda i, b: (b, 0, 0)),
            pl.BlockSpec((D, D), lambda i, b: (0, 0)),
            pl.BlockSpec((tm, D), lambda i, b: (i, 0)),
            pl.BlockSpec((1, 1, D), lambda i, b: (_mod_row(i, tm) * 6 + 2, 0, 0)),
        ],
        out_specs=pl.BlockSpec((tm, D), lambda i, b: (i, 0)),
        out_shape=jax.ShapeDtypeStruct((N_TOK, D), f32),
        scratch_shapes=[pltpu.VMEM((tm, D), f32)],
        compiler_params=_cparams(("parallel", "arbitrary")),
        name="merge_branches",
    )(P, *branches, w_branch, w_out, x, mod)


def _ffn_body(*refs, n_e, steps_per_e, final_norm):
    it = iter(refs)
    x_ref, sh_ref, sc_ref, gt_ref, g_ref, w1_ref, w3_ref, w2_ref = [next(it) for _ in range(8)]
    if n_e > 1:
        rwh_ref, rwl_ref, rb_ref = [next(it) for _ in range(3)]
    fg_ref = next(it) if final_norm else None
    o_ref = next(it)
    h_s, acc_s = next(it), next(it)
    gate_s = next(it) if n_e > 1 else None
    j = pl.program_id(1)
    nj = pl.num_programs(1)

    @pl.when(j == 0)
    def _():
        h = _modnorm(x_ref[...], g_ref[...], sc_ref[0], sh_ref[0])
        h_s[...] = h.astype(bf16)
        if n_e > 1:
            logits = _dot_hi(h, rwh_ref[...], rwl_ref[...]) + rb_ref[...]
            lane = lax.broadcasted_iota(jnp.int32, logits.shape, 1)
            neg = jnp.float32(-jnp.inf)
            lv = jnp.where(lane < n_e, logits, neg)
            m1 = jnp.max(lv, axis=1, keepdims=True)
            i1 = jnp.min(jnp.where(lv == m1, lane, 128), axis=1, keepdims=True)
            lv2 = jnp.where(lane == i1, neg, lv)
            m2 = jnp.max(lv2, axis=1, keepdims=True)
            i2 = jnp.min(jnp.where(lv2 == m2, lane, 128), axis=1, keepdims=True)
            e2 = jnp.exp(m2 - m1)
            p1 = 1.0 / (1.0 + e2)
            gate_s[...] = jnp.where(lane == i1, p1, 0.0) + jnp.where(lane == i2, e2 * p1, 0.0)

    hb = h_s[...]
    hid = _silu(_dot(hb, w1_ref[0])) * _dot(hb, w3_ref[0])
    if n_e > 1:
        e = j // steps_per_e
        lane = lax.broadcasted_iota(jnp.int32, gate_s.shape, 1)
        ge = jnp.sum(jnp.where(lane == e, gate_s[...], 0.0), axis=1, keepdims=True)
        hid = hid * ge
    part = _dot(hid.astype(bf16), w2_ref[0])

    @pl.when(j == 0)
    def _():
        acc_s[...] = part

    @pl.when(j > 0)
    def _():
        acc_s[...] += part

    @pl.when(j == nj - 1)
    def _():
        y = x_ref[...] + gt_ref[0] * acc_s[...]
        if final_norm:
            ms = jnp.mean(y * y, axis=-1, keepdims=True)
            y = y * lax.rsqrt(ms + EPS) * fg_ref[...]
        o_ref[...] = y


def ffn_layer(x, mod, g, w1, w3, w2, router=None, final_g=None, tm=512, tf=512):
    n_e, _, f = w1.shape
    spe = f // tf
    in_specs = [
        pl.BlockSpec((tm, D), lambda i, j: (i, 0)),
        pl.BlockSpec((1, 1, D), lambda i, j: (_mod_row(i, tm) * 6 + 3, 0, 0)),
        pl.BlockSpec((1, 1, D), lambda i, j: (_mod_row(i, tm) * 6 + 4, 0, 0)),
        pl.BlockSpec((1, 1, D), lambda i, j: (_mod_row(i, tm) * 6 + 5, 0, 0)),
        pl.BlockSpec((1, D), lambda i, j: (0, 0)),
        pl.BlockSpec((1, D, tf), lambda i, j: (j // spe, 0, j % spe)),
        pl.BlockSpec((1, D, tf), lambda i, j: (j // spe, 0, j % spe)),
        pl.BlockSpec((1, tf, D), lambda i, j: (j // spe, j % spe, 0)),
    ]
    args = [x, mod, mod, mod, g.reshape(1, D), w1, w3, w2]
    scratch = [pltpu.VMEM((tm, D), bf16), pltpu.VMEM((tm, D), f32)]
    if n_e > 1:
        in_specs += [pl.BlockSpec((D, 128), lambda i, j: (0, 0)),
                     pl.BlockSpec((D, 128), lambda i, j: (0, 0)),
                     pl.BlockSpec((1, 128), lambda i, j: (0, 0))]
        args += list(router)
        scratch.append(pltpu.VMEM((tm, 128), f32))
    if final_g is not None:
        in_specs.append(pl.BlockSpec((1, D), lambda i, j: (0, 0)))
        args.append(final_g.reshape(1, D))
    return pl.pallas_call(
        functools.partial(_ffn_body, n_e=n_e, steps_per_e=spe, final_norm=final_g is not None),
        grid=(N_TOK // tm, n_e * spe),
        in_specs=in_specs,
        out_specs=pl.BlockSpec((tm, D), lambda i, j: (i, 0)),
        out_shape=jax.ShapeDtypeStruct((N_TOK, D), f32),
        scratch_shapes=scratch,
        compiler_params=_cparams(("parallel", "arbitrary")),
        name="ffn_moe" if n_e > 1 else "ffn_dense",
    )(*args)


def _split_hi_lo(w):
    hi = w.astype(bf16)
    lo = (w - hi.astype(f32)).astype(bf16)
    return hi, lo


def _reorder_w_in(w):
    main = jnp.concatenate([w[:, 0:1536], w[:, 1568:5664], w[:, 5680:14384]], axis=1)
    small = jnp.concatenate([w[:, 1536:1568], w[:, 5664:5680], jnp.zeros((D, 80), w.dtype)], axis=1)
    return main.astype(bf16), small.astype(bf16)


def _gla_gate_weights(wa2, ba):
    w = jnp.zeros((2, 128, 256), f32)
    b = jnp.zeros((2, 1, 256), f32)
    for hp in range(2):
        for d in range(2):
            w = w.at[hp, SM_LR + d * 16:SM_LR + (d + 1) * 16, d * 128:(d + 1) * 128].set(
                wa2[d][:, hp * 128:(hp + 1) * 128])
            b = b.at[hp, 0, d * 128:(d + 1) * 128].set(ba[d][hp * 128:(hp + 1) * 128])
    hi, lo = _split_hi_lo(w)
    return hi, lo, b


def _ssd_select_mats():
    e64 = np.zeros((2, 2, 128, 256), np.float32)
    e128 = np.zeros((2, 2, 128, 512), np.float32)
    for g in range(2):
        for d in range(2):
            for h in range(4):
                lane = SM_DT + d * 8 + g * 4 + h
                e64[g, d, lane, h * 64:(h + 1) * 64] = 1.0
                e128[g, d, lane, h * 128:(h + 1) * 128] = 1.0
    return jnp.asarray(e64, bf16), jnp.asarray(e128, bf16)


def _lane_row(vals16):
    return jnp.zeros((1, 128), f32).at[0, SM_DT:SM_DT + 16].set(vals16.reshape(16))


def kernel(x_prompt, x_sample, c, c_ctx, state_gla, state_hgrn, state_ssm, state_s5_re, state_s5_im, norm1_g, norm2_g, ada_w, ada_b, w_in, gla_wa2, gla_ba, gla_norm_g, hgrn_lb_logits, hgrn_norm_g, ssm_conv_w, ssm_conv_b, ssm_a_log, ssm_dt_bias, ssm_d, ssm_norm_g, s5_a_re, s5_a_im, s5_log_dt, s5_b_re, s5_b_im, s5_c_re, s5_c_im, s5_d, s5_glu_w, s5_glu_b, w_branch, w_out, ffn_w1, ffn_w3, ffn_w2, router_w, router_b, moe_w1, moe_w3, moe_w2, final_norm_g):
    p = jax.nn.softmax(hgrn_lb_logits.astype(f32), axis=1)
    lower = jnp.maximum(jnp.cumsum(p, axis=1) - p[:, :1], 0.0)

    x = assemble_tokens(x_prompt.reshape(N_CTX, D), x_sample.reshape(N_LAT, D), _pos_table())

    cond8 = jnp.concatenate([c_ctx[None], c, jnp.zeros((8 - N_MOD, D), f32)], axis=0)
    mod_all = ada_modulation(cond8, ada_w, ada_b)

    e64, e128 = _ssd_select_mats()
    s5_re = state_s5_re.reshape(DEC_BATCH, DEPTH, 2, S5_GROUPS * S5_STATE)
    s5_im = state_s5_im.reshape(DEC_BATCH, DEPTH, 2, S5_GROUPS * S5_STATE)
    lat0 = N_CTX // DEC_SEQ

    new_gla, new_hgrn, new_ssm, new_s5r, new_s5i = [], [], [], [], []
    for l in range(DEPTH):
        mod = mod_all[l, :N_MOD].reshape(N_MOD * 6, 1, D)
        w_main, w_small = _reorder_w_in(w_in[l])
        P, SM = in_projection(x, mod, norm1_g[l], w_main, w_small)

        wah, wal, ba = _gla_gate_weights(gla_wa2[l], gla_ba[l])
        ng = gla_norm_g[l].reshape(1, GLA_DV)
        o_gla, st = gla_mixer(P, SM, wah, wal, ba, ng, B=BATCH, L=SEQ, row0=0)
        new_gla.append(st)
        (o_gla,) = gla_mixer(P, SM, wah, wal, ba, ng, B=DEC_BATCH, L=DEC_SEQ, row0=lat0,
                             state=state_gla, layer=l, prev=o_gla)

        lb = lower[:, l].reshape(2, HGRN_HEADS, HGRN_DIM)
        rows = []
        for d in range(2):
            rows += [jnp.log(lb[d]), jnp.log1p(-lb[d]), 1.0 - lb[d]]
        rows += [jnp.zeros_like(lb[0])] * 2
        lbp = jnp.stack(rows, axis=1)
        hng = hgrn_norm_g[l].reshape(1, HGRN_DIM)
        o_hg, st = hgrn_mixer(P, lbp, hng, B=BATCH, L=SEQ, row0=0)
        new_hgrn.append(st)
        (o_hg,) = hgrn_mixer(P, lbp, hng, B=DEC_BATCH, L=DEC_SEQ, row0=lat0,
                             state=state_hgrn, layer=l, prev=o_hg)

        cwb = jnp.concatenate([ssm_conv_w[l], ssm_conv_b[l][None]], axis=0)
        cw = jnp.stack([jnp.concatenate([cwb[:, g * 256:(g + 1) * 256],
                                         cwb[:, 512 + g * 128:512 + (g + 1) * 128],
                                         cwb[:, 768 + g * 128:768 + (g + 1) * 128]], axis=1)
                        for g in range(2)], axis=0)
        dtb = _lane_row(ssm_dt_bias[l])
        alog = _lane_row(ssm_a_log[l])
        dsk = jnp.repeat(ssm_d[l], SSM_HEADDIM).reshape(2, 1, 256)
        sng = ssm_norm_g[l].reshape(2, 1, 256)
        o_ssm, st = ssd_mixer(P, SM, cw, dtb, alog, e64, e128, dsk, sng, B=BATCH, L=SEQ, row0=0)
        new_ssm.append(st)
        (o_ssm,) = ssd_mixer(P, SM, cw, dtb, alog, e64, e128, dsk, sng, B=DEC_BATCH, L=DEC_SEQ, row0=lat0,
                             state=state_ssm, layer=l, prev=o_ssm)

        w5 = s5_derived_weights(s5_a_re[l], s5_a_im[l], s5_log_dt[l], s5_b_re[l], s5_b_im[l],
                                s5_c_re[l], s5_c_im[l])
        d5 = s5_d[l].reshape(4, 1, 128)
        y5, hr, hi = s5_mixer(P, w5, d5, B=BATCH, L=SEQ, s_b=min(8, BATCH), row0=0)
        new_s5r.append(hr.reshape(BATCH, 2, S5_GROUPS, S5_STATE))
        new_s5i.append(hi.reshape(BATCH, 2, S5_GROUPS, S5_STATE))
        (y5,) = s5_mixer(P, w5, d5, B=DEC_BATCH, L=DEC_SEQ, s_b=1, row0=lat0,
                         state_re=s5_re, state_im=s5_im, layer=l, prev=y5)
        o_s5 = s5_glu(y5, s5_glu_w[l].astype(bf16), s5_glu_b[l].reshape(1, MIX_W))

        x = merge_branches(P, (o_gla, o_hg, o_ssm, o_s5), w_branch[l].astype(bf16),
                           w_out[l].astype(bf16), x, mod)

        j = l // 2
        fg = final_norm_g if l == DEPTH - 1 else None
        if l % 2 == 0:
            x = ffn_layer(x, mod, norm2_g[l], ffn_w1[j][None].astype(bf16), ffn_w3[j][None].astype(bf16),
                          ffn_w2[j][None].astype(bf16), final_g=fg)
        else:
            rw = jnp.concatenate([router_w[j], jnp.zeros((D, 128 - N_EXPERTS), f32)], axis=1)
            rwh, rwl = _split_hi_lo(rw)
            rb = jnp.concatenate([router_b[j], jnp.zeros((128 - N_EXPERTS,), f32)]).reshape(1, 128)
            x = ffn_layer(x, mod, norm2_g[l], moe_w1[j].astype(bf16), moe_w3[j].astype(bf16),
                          moe_w2[j].astype(bf16), router=(rwh, rwl, rb), final_g=fg)

    y_prompt = x[:N_CTX].reshape(BATCH, SEQ, D)
    y_sample = x[N_CTX:].reshape(DEC_BATCH, DEC_SEQ, D)
    return (y_prompt, y_sample,
            jnp.stack(new_gla, axis=1), jnp.stack(new_hgrn, axis=1), jnp.stack(new_ssm, axis=1),
            jnp.stack(new_s5r, axis=1), jnp.stack(new_s5i, axis=1))
```

```python
import functools

import numpy as np
import jax
import jax.numpy as jnp
from jax import lax
from jax.experimental import pallas as pl
from jax.experimental.pallas import tpu as pltpu

f32 = jnp.float32
bf16 = jnp.bfloat16

D = 2048
BATCH = 32
SEQ = 256
DEPTH = 4
DEC_BATCH = 4
DEC_SEQ = 4096
GRID_W = 64
MIX_W = 512
GLA_HEADS = 4
GLA_DK = 64
GLA_DV = 128
GLA_RANK = 16
GLA_NORMALIZER = 16.0
HGRN_HEADS = 4
HGRN_DIM = 128
SSM_HEADS = 8
SSM_HEADDIM = 64
SSM_GROUPS = 2
SSM_STATE = 128
S5_CH = 16
S5_STATE = 64
S5_GROUPS = 32
D_FF = 5632
N_EXPERTS = 8
D_FF_EXPERT = 1024
EPS = 1e-6

N_CTX = BATCH * SEQ
N_LAT = DEC_BATCH * DEC_SEQ
N_TOK = N_CTX + N_LAT
MOD_ROWS = 8

F_HG_F, F_S5_U, F_SMALL = 0, 1024, 1536
N_PF = 2048
B_GATE, B_GLA_Q, B_GLA_K, B_GLA_V, B_GLA_R = 0, 8192, 8448, 8704, 9216
B_HG_Q, B_HG_I, B_HG_G, B_SSM_Z, B_SSM_XBC = 9728, 10240, 10752, 11264, 11776
N_PB = 12800
SM_LR, SM_DT = 0, 32

GLS_CHUNK = 128
GLS_SUB = 32
SSD_CHUNK = 128
S5_CHUNK = 16

VMEM_LIMIT = 58 * 1024 * 1024


def _cparams(sem):
    return pltpu.CompilerParams(dimension_semantics=sem, vmem_limit_bytes=VMEM_LIMIT)


def _dot(a, b):
    return jnp.dot(a, b, preferred_element_type=f32)


def _dot_nt(a, b):
    return lax.dot_general(a, b, (((1,), (1,)), ((), ())), preferred_element_type=f32)


def _dot_tn(a, b):
    return lax.dot_general(a, b, (((0,), (0,)), ((), ())), preferred_element_type=f32)


def _split3(x):
    x1 = x.astype(bf16)
    r1 = x - x1.astype(f32)
    x2 = r1.astype(bf16)
    x3 = (r1 - x2.astype(f32)).astype(bf16)
    return x1, x2, x3


def _dot_sel_lhs(t, x):
    x1, x2, x3 = _split3(x)
    return _dot(t, x1) + _dot(t, x2) + _dot(t, x3)


def _dot_sel_rhs(x, e):
    x1, x2, x3 = _split3(x)
    return _dot(x1, e) + _dot(x2, e) + _dot(x3, e)


def _dot_hi(x, w_hi, w_lo):
    x1, x2, _ = _split3(x)
    return _dot(x1, w_hi) + _dot(x2, w_hi) + _dot(x1, w_lo)


def _log_sigmoid(x):
    return jnp.minimum(x, 0.0) - jnp.log1p(jnp.exp(-jnp.abs(x)))


def _softplus(x):
    return jnp.maximum(x, 0.0) + jnp.log1p(jnp.exp(-jnp.abs(x)))


def _silu(x):
    return x * jax.nn.sigmoid(x)


def _modnorm(x, g, scale, shift):
    ms = jnp.mean(x * x, axis=-1, keepdims=True)
    y = x * lax.rsqrt(ms + EPS) * g
    return y * (1.0 + scale) + shift


def _tri(n, upper):
    ii = lax.broadcasted_iota(jnp.int32, (n, n), 0)
    jj = lax.broadcasted_iota(jnp.int32, (n, n), 1)
    return (ii <= jj) if upper else (ii >= jj)


def _mod_row(i, tm):
    start = i * tm
    return jnp.where(start < N_CTX, 0, 1 + (start - N_CTX) // DEC_SEQ)


def _mod_spec(layer, k, tm):
    return pl.BlockSpec((1, 1, 1, D), lambda i, j: (layer, _mod_row(i, tm) * 6 + k, 0, 0))


def _assemble_body(xp_ref, xs_ref, pos_ref, o_ref, *, n_ctx_tiles):
    i = pl.program_id(0)

    @pl.when(i < n_ctx_tiles)
    def _():
        o_ref[...] = xp_ref[...]

    @pl.when(i >= n_ctx_tiles)
    def _():
        o_ref[...] = xs_ref[...] + pos_ref[...]


def assemble_tokens(xp2, xs2, pos, tm=512):
    nct = N_CTX // tm
    npos = DEC_SEQ // tm
    return pl.pallas_call(
        functools.partial(_assemble_body, n_ctx_tiles=nct),
        grid=(N_TOK // tm,),
        in_specs=[
            pl.BlockSpec((tm, D), lambda i: (jnp.minimum(i, nct - 1), 0)),
            pl.BlockSpec((tm, D), lambda i: (jnp.maximum(i - nct, 0), 0)),
            pl.BlockSpec((tm, D), lambda i: (jnp.maximum(i - nct, 0) % npos, 0)),
        ],
        out_specs=pl.BlockSpec((tm, D), lambda i: (i, 0)),
        out_shape=jax.ShapeDtypeStruct((N_TOK, D), f32),
        compiler_params=_cparams(("parallel",)),
        name="assemble_tokens",
    )(xp2, xs2, pos)


def _pos_table():
    rows = DEC_SEQ // GRID_W
    row = jnp.repeat(jnp.arange(rows, dtype=f32), GRID_W)
    col = jnp.tile(jnp.arange(GRID_W, dtype=f32), rows)

    def sincos(p, d):
        half = d // 2
        omega = 1.0 / (10000.0 ** (jnp.arange(half, dtype=f32) / half))
        ang = p[:, None] * omega[None, :]
        return jnp.concatenate([jnp.sin(ang), jnp.cos(ang)], axis=-1)

    return jnp.concatenate([sincos(row, D // 2), sincos(col, D // 2)], axis=-1)


def _ada_body(c_ref, w_ref, b_ref, o_ref):
    s = _silu(c_ref[...])
    s1, s2, _ = _split3(s)
    w = w_ref[0].astype(bf16)
    o_ref[0] = _dot(s1, w) + _dot(s2, w) + b_ref[0]


def ada_modulation(cond8, ada_w, ada_b, tn=1024):
    return pl.pallas_call(
        _ada_body,
        grid=(DEPTH, 6 * D // tn),
        in_specs=[
            pl.BlockSpec((MOD_ROWS, D), lambda l, j: (0, 0)),
            pl.BlockSpec((1, D, tn), lambda l, j: (l, 0, j)),
            pl.BlockSpec((1, 1, tn), lambda l, j: (l, 0, j)),
        ],
        out_specs=pl.BlockSpec((1, MOD_ROWS, tn), lambda l, j: (l, 0, j)),
        out_shape=jax.ShapeDtypeStruct((DEPTH, MOD_ROWS, 6 * D), f32),
        compiler_params=_cparams(("parallel", "parallel")),
        name="ada_modulation",
    )(cond8, ada_w, ada_b.reshape(DEPTH, 1, 6 * D))


def _inproj_body(x_ref, sh_ref, sc_ref, g_ref, w_ref, pf_ref, pb_ref, h_s, *, nf):
    j = pl.program_id(1)

    @pl.when(j == 0)
    def _():
        h_s[...] = _modnorm(x_ref[...], g_ref[0], sc_ref[0, 0], sh_ref[0, 0]).astype(bf16)

    y = _dot(h_s[...], w_ref[0])

    @pl.when(j < nf)
    def _():
        pf_ref[...] = y

    @pl.when(j >= nf)
    def _():
        pb_ref[...] = y.astype(bf16)


def in_projection(x, mod, g, w_all, layer, tm=1024, tn=512):
    nf = N_PF // tn
    nb = N_PB // tn
    return pl.pallas_call(
        functools.partial(_inproj_body, nf=nf),
        grid=(N_TOK // tm, nf + nb),
        in_specs=[
            pl.BlockSpec((tm, D), lambda i, j: (i, 0)),
            _mod_spec(layer, 0, tm),
            _mod_spec(layer, 1, tm),
            pl.BlockSpec((1, 1, D), lambda i, j: (layer, 0, 0)),
            pl.BlockSpec((1, D, tn), lambda i, j: (layer, 0, j)),
        ],
        out_specs=[
            pl.BlockSpec((tm, tn), lambda i, j: (i, jnp.minimum(j, nf - 1))),
            pl.BlockSpec((tm, tn), lambda i, j: (i, jnp.maximum(j - nf, 0))),
        ],
        out_shape=[
            jax.ShapeDtypeStruct((N_TOK, N_PF), f32),
            jax.ShapeDtypeStruct((N_TOK, N_PB), bf16),
        ],
        scratch_shapes=[pltpu.VMEM((tm, D), bf16)],
        compiler_params=_cparams(("parallel", "arbitrary")),
        name="in_projection",
    )(x, mod, mod, g, w_all)


def _gls_dir(q, k, g, d):
    c, sub = GLS_CHUNK, GLS_SUB
    nb = c // sub
    tri = jnp.where(_tri(c, upper=(d == 1)), 1.0, 0.0).astype(bf16)
    b = _dot_sel_lhs(tri, g)
    zero = jnp.zeros((1, 128), f32)
    if d == 0:
        mid = [b[sub * i + sub // 2 - 1:sub * i + sub // 2] for i in range(nb)]
        end = [b[sub * i + sub - 1:sub * i + sub] for i in range(nb)]
        start = [zero] + end[:-1]
        total = end[-1]
    else:
        mid = [b[sub * i + sub // 2:sub * i + sub // 2 + 1] for i in range(nb)]
        end = [b[sub * i:sub * i + 1] for i in range(nb)]
        start = end[1:] + [zero]
        total = end[0]

    def rows(vals):
        return jnp.concatenate([jnp.broadcast_to(v, (sub, 128)) for v in vals], axis=0)

    r_mid, r_start, r_end = rows(mid), rows(start), rows(end)
    qd = q * jnp.exp(b - r_mid)
    kd = k * jnp.exp(r_mid - b)
    qo = q * jnp.exp(b - r_start)
    ke = k * jnp.exp(r_end - b)
    q_in = qo * rows([jnp.exp(s) for s in start])
    k_out = ke * rows([jnp.exp(total - e) for e in end])
    kr = []
    for i in range(nb):
        prior = range(i) if d == 0 else range(i + 1, nb)
        if len(prior) == 0:
            kr.append(None)
            continue
        blocks = []
        for jb in range(nb):
            if jb in prior:
                blocks.append(ke[jb * sub:(jb + 1) * sub] * jnp.exp(start[i] - end[jb]))
            else:
                blocks.append(jnp.zeros((sub, 128), f32))
        kr.append(jnp.concatenate(blocks, axis=0).astype(bf16))
    return qd, kd.astype(bf16), qo, kr, q_in, k_out, jnp.exp(total)


def _gls_head(parts, v, st, mask, kmask):
    qd, kd, qo, kr, q_in, k_out, dec = parts
    sub = GLS_SUB
    if kmask is not None:
        qd, qo, q_in, k_out = qd * kmask, qo * kmask, q_in * kmask, k_out * kmask
    att = jnp.where(mask, _dot_nt(qd.astype(bf16), kd), 0.0)
    qob = qo.astype(bf16)
    offs = []
    for i, kri in enumerate(kr):
        if kri is None:
            offs.append(jnp.zeros((sub, GLS_CHUNK), f32))
        else:
            offs.append(_dot_nt(qob[i * sub:(i + 1) * sub], kri))
    att = att + jnp.concatenate(offs, axis=0)
    vb = v.astype(bf16)
    o = _dot(att.astype(bf16), vb) + _dot_nt(q_in.astype(bf16), st.astype(bf16))
    st_new = st * dec + _dot_tn(vb, k_out.astype(bf16))
    return o, st_new


def _gls_masks():
    c, sub = GLS_CHUNK, GLS_SUB
    ii = lax.broadcasted_iota(jnp.int32, (c, c), 0)
    jj = lax.broadcasted_iota(jnp.int32, (c, c), 1)
    same = (ii // sub) == (jj // sub)
    return [same & (ii >= jj), same & (ii <= jj)]


def _head_post(o, gate_in, ng):
    ms = jnp.mean(o * o, axis=-1, keepdims=True)
    return o * lax.rsqrt(ms + EPS) * ng * _silu(gate_in.astype(f32))


def _scan_loops(n, chunk_fn, post_fn, o_ref, init):
    C = GLS_CHUNK

    def store(r0, outs, final):
        for h, o in enumerate(outs):
            sl = (pl.ds(r0, C), slice(h * 128, (h + 1) * 128))
            if final:
                o_ref[sl] = post_fn(r0, h, o_ref[sl] + o)
            else:
                o_ref[sl] = o

    def half(final):
        def body(t, carry):
            sf, sb = carry
            r0, of, sf = chunk_fn(t, 0, sf)
            store(r0, of, final)
            r1, ob, sb = chunk_fn(n - 1 - t, 1, sb)
            store(r1, ob, final)
            return sf, sb
        return body

    carry = lax.fori_loop(0, n // 2, half(False), init)
    return lax.fori_loop(n // 2, n, half(True), carry)


def _gla_body(*refs, L, has_init, chain_state):
    it = iter(refs)
    q_ref, k_ref, v_ref, r_ref, sm_ref, wah_ref, wal_ref, ba_ref, ng_ref = [next(it) for _ in range(9)]
    s0_ref = next(it) if has_init else None
    if has_init or chain_state:
        next(it)
    o_ref = next(it)
    st_ref = None if has_init else next(it)
    C = GLS_CHUNK
    n = L // C
    lane = lax.broadcasted_iota(jnp.int32, (1, 128), 1)
    kmasks = [jnp.where(lane < 64, 1.0, 0.0), jnp.where(lane >= 64, 1.0, 0.0)]
    masks = _gls_masks()
    wah, wal, ba, ng = wah_ref[0, 0], wal_ref[0, 0], ba_ref[0, 0], ng_ref[0]

    def chunk(cidx, d, sts):
        r0 = pl.multiple_of(cidx * C, C)
        q = q_ref[pl.ds(r0, C), :].astype(f32) * (GLA_DK ** -0.5)
        k = k_ref[pl.ds(r0, C), :].astype(f32)
        sm = sm_ref[pl.ds(r0, C), :]
        sl = slice(d * 128, (d + 1) * 128)
        g = _log_sigmoid(_dot_hi(sm, wah[:, sl], wal[:, sl]) + ba[:, sl]) * (1.0 / GLA_NORMALIZER)
        parts = _gls_dir(q, k, g, d)
        outs, new = [], []
        for h in range(2):
            v = v_ref[pl.ds(r0, C), h * 128:(h + 1) * 128]
            o, s = _gls_head(parts, v, sts[h], masks[d], kmasks[h])
            outs.append(o)
            new.append(s)
        return r0, outs, new

    def post(r0, h, tot):
        return _head_post(tot, r_ref[pl.ds(r0, C), h * 128:(h + 1) * 128], ng)

    def init(d, h):
        if not has_init:
            return jnp.zeros((128, 128), f32)
        s = s0_ref[0, 0, d, h]
        z = jnp.zeros((64, 128), f32)
        full = jnp.concatenate([s, z], axis=0) if h == 0 else jnp.concatenate([z, s], axis=0)
        return full.T

    sf, sb = _scan_loops(n, chunk, post, o_ref, ([init(0, 0), init(0, 1)], [init(1, 0), init(1, 1)]))
    if st_ref is not None:
        for d, sts in enumerate((sf, sb)):
            for h in range(2):
                st_ref[0, 0, d, h] = sts[h].T[h * 64:(h + 1) * 64, :]


def gla_mixer(PF, PB, wa_hi, wa_lo, ba, ng, *, B, L, row0, layer, state=None, prev=None, st_prev=None):
    has_init = state is not None
    chain_state = st_prev is not None
    in_specs = [
        pl.BlockSpec((L, 128), lambda b, hp: (row0 + b, B_GLA_Q // 128 + hp)),
        pl.BlockSpec((L, 128), lambda b, hp: (row0 + b, B_GLA_K // 128 + hp)),
        pl.BlockSpec((L, 256), lambda b, hp: (row0 + b, B_GLA_V // 256 + hp)),
        pl.BlockSpec((L, 256), lambda b, hp: (row0 + b, B_GLA_R // 256 + hp)),
        pl.BlockSpec((L, 128), lambda b, hp: (row0 + b, F_SMALL // 128)),
        pl.BlockSpec((1, 1, 128, 256), lambda b, hp: (layer, hp, 0, 0)),
        pl.BlockSpec((1, 1, 128, 256), lambda b, hp: (layer, hp, 0, 0)),
        pl.BlockSpec((1, 1, 1, 256), lambda b, hp: (layer, hp, 0, 0)),
        pl.BlockSpec((1, 1, 128), lambda b, hp: (layer, 0, 0)),
    ]
    args = [PB, PB, PB, PB, PF, wa_hi, wa_lo, ba, ng]
    aliases = {}
    out_specs = [pl.BlockSpec((L, 256), lambda b, hp: (row0 + b, hp))]
    out_shape = [jax.ShapeDtypeStruct((N_TOK, MIX_W), f32)]
    if has_init:
        in_specs.append(pl.BlockSpec((1, 1, 2, 2, 64, 128), lambda b, hp: (b, layer, 0, hp, 0, 0)))
        args.append(state)
        in_specs.append(pl.BlockSpec(memory_space=pl.ANY))
        args.append(prev)
        aliases = {len(args) - 1: 0}
    else:
        if chain_state:
            in_specs.append(pl.BlockSpec(memory_space=pl.ANY))
            args.append(st_prev)
            aliases = {len(args) - 1: 1}
        out_specs.append(pl.BlockSpec((1, 1, 2, 2, 64, 128), lambda b, hp: (b, layer, 0, hp, 0, 0)))
        out_shape.append(jax.ShapeDtypeStruct((B, DEPTH, 2, GLA_HEADS, GLA_DK, GLA_DV), f32))
    return pl.pallas_call(
        functools.partial(_gla_body, L=L, has_init=has_init, chain_state=chain_state),
        grid=(B, 2),
        in_specs=in_specs,
        out_specs=out_specs,
        out_shape=out_shape,
        input_output_aliases=aliases,
        compiler_params=_cparams(("parallel", "parallel")),
        name="gla_mixer",
    )(*args)


def _hgrn_body(*refs, L, has_init, chain_state):
    it = iter(refs)
    q_ref, f0_ref, f1_ref, i_ref, g_ref, lb_ref, ng_ref = [next(it) for _ in range(7)]
    s0_ref = next(it) if has_init else None
    if has_init or chain_state:
        next(it)
    o_ref = next(it)
    st_ref = None if has_init else next(it)
    C = GLS_CHUNK
    n = L // C
    masks = _gls_masks()
    ng = ng_ref[0]
    f_refs = (f0_ref, f1_ref)

    def chunk(cidx, d, sts):
        r0 = pl.multiple_of(cidx * C, C)
        outs, new = [], []
        for h in range(2):
            sl = slice(h * 128, (h + 1) * 128)
            lbp = lb_ref[0, h]
            q = q_ref[pl.ds(r0, C), sl].astype(f32)
            fp = f_refs[d][pl.ds(r0, C), sl]
            a = lbp[3 * d:3 * d + 1]
            bq = lbp[3 * d + 1:3 * d + 2] + _log_sigmoid(fp)
            g = jnp.maximum(a, bq) + jnp.log1p(jnp.exp(-jnp.abs(a - bq)))
            k = lbp[3 * d + 2:3 * d + 3] * jax.nn.sigmoid(-fp)
            parts = _gls_dir(q, k, g, d)
            o, s = _gls_head(parts, i_ref[pl.ds(r0, C), sl], sts[h], masks[d], None)
            outs.append(o)
            new.append(s)
        return r0, outs, new

    def post(r0, h, tot):
        return _head_post(tot, g_ref[pl.ds(r0, C), h * 128:(h + 1) * 128], ng)

    def init(d, h):
        if not has_init:
            return jnp.zeros((128, 128), f32)
        return s0_ref[0, 0, d, h].T

    sf, sb = _scan_loops(n, chunk, post, o_ref, ([init(0, 0), init(0, 1)], [init(1, 0), init(1, 1)]))
    if st_ref is not None:
        for d, sts in enumerate((sf, sb)):
            for h in range(2):
                st_ref[0, 0, d, h] = sts[h].T


def hgrn_mixer(PF, PB, lbp, ng, *, B, L, row0, layer, state=None, prev=None, st_prev=None):
    has_init = state is not None
    chain_state = st_prev is not None
    in_specs = [
        pl.BlockSpec((L, 256), lambda b, hp: (row0 + b, B_HG_Q // 256 + hp)),
        pl.BlockSpec((L, 256), lambda b, hp: (row0 + b, F_HG_F // 256 + hp)),
        pl.BlockSpec((L, 256), lambda b, hp: (row0 + b, F_HG_F // 256 + 2 + hp)),
        pl.BlockSpec((L, 256), lambda b, hp: (row0 + b, B_HG_I // 256 + hp)),
        pl.BlockSpec((L, 256), lambda b, hp: (row0 + b, B_HG_G // 256 + hp)),
        pl.BlockSpec((1, 2, 8, 128), lambda b, hp: (layer, hp, 0, 0)),
        pl.BlockSpec((1, 1, 128), lambda b, hp: (layer, 0, 0)),
    ]
    args = [PB, PF, PF, PB, PB, lbp, ng]
    aliases = {}
    out_specs = [pl.BlockSpec((L, 256), lambda b, hp: (row0 + b, hp))]
    out_shape = [jax.ShapeDtypeStruct((N_TOK, MIX_W), f32)]
    if has_init:
        in_specs.append(pl.BlockSpec((1, 1, 2, 2, 128, 128), lambda b, hp: (b, layer, 0, hp, 0, 0)))
        args.append(state)
        in_specs.append(pl.BlockSpec(memory_space=pl.ANY))
        args.append(prev)
        aliases = {len(args) - 1: 0}
    else:
        if chain_state:
            in_specs.append(pl.BlockSpec(memory_space=pl.ANY))
            args.append(st_prev)
            aliases = {len(args) - 1: 1}
        out_specs.append(pl.BlockSpec((1, 1, 2, 2, 128, 128), lambda b, hp: (b, layer, 0, hp, 0, 0)))
        out_shape.append(jax.ShapeDtypeStruct((B, DEPTH, 2, HGRN_HEADS, HGRN_DIM, HGRN_DIM), f32))
    return pl.pallas_call(
        functools.partial(_hgrn_body, L=L, has_init=has_init, chain_state=chain_state),
        grid=(B, 2),
        in_specs=in_specs,
        out_specs=out_specs,
        out_shape=out_shape,
        input_output_aliases=aliases,
        compiler_params=_cparams(("parallel", "parallel")),
        name="hgrn_mixer",
    )(*args)


def _ssd_body(*refs, L, has_init, chain_state):
    it = iter(refs)
    (xs_ref, bm_ref, cm_ref, z_ref, sm_ref, cw_ref, dtb_ref, alog_ref, e64_ref, e128_ref,
     dsk_ref, ng_ref) = [next(it) for _ in range(12)]
    s0_ref = next(it) if has_init else None
    if has_init or chain_state:
        next(it)
    o_ref = next(it)
    st_ref = None if has_init else next(it)
    xc_s = next(it)
    C = SSD_CHUNK
    n = L // C
    cw = cw_ref[0, 0]
    rowi = lax.broadcasted_iota(jnp.int32, (C, 1), 0)

    def conv_block(rb, _):
        r0 = pl.multiple_of(rb * C, C)
        rp = pl.multiple_of(jnp.maximum(r0 - 16, 0), 16)
        rn = pl.multiple_of(jnp.minimum(r0 + C, L - 16), 16)
        has_p = (rb > 0).astype(f32)
        has_n = (rb < n - 1).astype(f32)
        for ref, c0, w in ((xs_ref, 0, 256), (bm_ref, 256, 128), (cm_ref, 384, 128)):
            x = ref[pl.ds(r0, C), :].astype(f32)
            prev = ref[pl.ds(rp, 16), :].astype(f32)[15:16] * has_p
            nxt = ref[pl.ds(rn, 16), :].astype(f32)[0:1] * has_n
            x_dn = jnp.where(rowi == 0, prev, pltpu.roll(x, 1, axis=0))
            x_up = jnp.where(rowi == C - 1, nxt, pltpu.roll(x, C - 1, axis=0))
            y = (cw[0:1, c0:c0 + w] * x_dn + cw[1:2, c0:c0 + w] * x + cw[2:3, c0:c0 + w] * x_up
                 + cw[3:4, c0:c0 + w])
            xc_s[pl.ds(r0, C), c0:c0 + w] = _silu(y)
        return 0

    lax.fori_loop(0, n, conv_block, 0)

    masks = [_tri(C, upper=False), _tri(C, upper=True)]
    tris = [jnp.where(m, 1.0, 0.0).astype(bf16) for m in masks]
    lane256 = lax.broadcasted_iota(jnp.int32, (1, 256), 1)
    hmask = [jnp.where((lane256 >= h * 64) & (lane256 < (h + 1) * 64), 1.0, 0.0) for h in range(4)]
    dtb = dtb_ref[0]
    a_row = -jnp.exp(alog_ref[0])
    neg_inf = jnp.float32(-jnp.inf)

    def chunk(cidx, d, st):
        r0 = pl.multiple_of(cidx * C, C)
        xs = xc_s[pl.ds(r0, C), 0:256]
        bm = xc_s[pl.ds(r0, C), 256:384].astype(bf16)
        cm = xc_s[pl.ds(r0, C), 384:512].astype(bf16)
        dt_all = _softplus(sm_ref[pl.ds(r0, C), :] + dtb)
        a_all = dt_all * a_row
        dt64 = _dot_sel_rhs(dt_all, e64_ref[0, d])
        a64 = _dot_sel_rhs(a_all, e64_ref[0, d])
        a128 = _dot_sel_rhs(a_all, e128_ref[0, d])
        acs64 = _dot_sel_lhs(tris[d], a64)
        acs128 = _dot_sel_lhs(tris[d], a128)
        end = acs64[C - 1:C] if d == 0 else acs64[0:1]
        xdt = xs * dt64
        scores = _dot_nt(cm, bm)
        y = jnp.exp(acs64) * _dot(cm, st.astype(bf16))
        for h in range(4):
            colb = acs128[:, h * 128:(h + 1) * 128]
            diff = jnp.where(masks[d], colb - colb.T, neg_inf)
            p = (scores * jnp.exp(diff)).astype(bf16)
            y = y + _dot(p, (xdt * hmask[h]).astype(bf16))
        xd = (xdt * jnp.exp(end - acs64)).astype(bf16)
        st_new = st * jnp.exp(end) + _dot_tn(bm, xd)
        return r0, y, st_new

    dsk = dsk_ref[0, 0]
    ng = ng_ref[0, 0]

    def post(r0, y):
        xs = xc_s[pl.ds(r0, C), 0:256]
        t = (y + dsk * xs) * _silu(z_ref[pl.ds(r0, C), :].astype(f32))
        ms = jnp.mean(t * t, axis=-1, keepdims=True)
        return t * lax.rsqrt(ms + EPS) * ng

    def first_half(t, carry):
        sf, sb = carry
        r0, yf, sf = chunk(t, 0, sf)
        o_ref[pl.ds(r0, C), :] = yf
        r1, yb, sb = chunk(n - 1 - t, 1, sb)
        o_ref[pl.ds(r1, C), :] = yb
        return sf, sb

    def second_half(t, carry):
        sf, sb = carry
        r0, yf, sf = chunk(t, 0, sf)
        o_ref[pl.ds(r0, C), :] = post(r0, o_ref[pl.ds(r0, C), :] + yf)
        r1, yb, sb = chunk(n - 1 - t, 1, sb)
        o_ref[pl.ds(r1, C), :] = post(r1, o_ref[pl.ds(r1, C), :] + yb)
        return sf, sb

    def init(d):
        if not has_init:
            return jnp.zeros((128, 256), f32)
        return s0_ref[0, 0, d].reshape(256, 128).T

    carry = (init(0), init(1))
    carry = lax.fori_loop(0, n // 2, first_half, carry)
    sf, sb = lax.fori_loop(n // 2, n, second_half, carry)
    if st_ref is not None:
        st_ref[0, 0, 0] = sf.T.reshape(4, 64, 128)
        st_ref[0, 0, 1] = sb.T.reshape(4, 64, 128)


def ssd_mixer(PF, PB, cw, dtb, alog, e64, e128, dsk, ng, *, B, L, row0, layer, state=None, prev=None,
              st_prev=None):
    has_init = state is not None
    chain_state = st_prev is not None
    xb = B_SSM_XBC
    in_specs = [
        pl.BlockSpec((L, 256), lambda b, g: (row0 + b, xb // 256 + g)),
        pl.BlockSpec((L, 128), lambda b, g: (row0 + b, (xb + 512) // 128 + g)),
        pl.BlockSpec((L, 128), lambda b, g: (row0 + b, (xb + 768) // 128 + g)),
        pl.BlockSpec((L, 256), lambda b, g: (row0 + b, B_SSM_Z // 256 + g)),
        pl.BlockSpec((L, 128), lambda b, g: (row0 + b, F_SMALL // 128)),
        pl.BlockSpec((1, 1, 4, 512), lambda b, g: (layer, g, 0, 0)),
        pl.BlockSpec((1, 1, 128), lambda b, g: (layer, 0, 0)),
        pl.BlockSpec((1, 1, 128), lambda b, g: (layer, 0, 0)),
        pl.BlockSpec((1, 2, 128, 256), lambda b, g: (g, 0, 0, 0)),
        pl.BlockSpec((1, 2, 128, 512), lambda b, g: (g, 0, 0, 0)),
        pl.BlockSpec((1, 1, 1, 256), lambda b, g: (layer, g, 0, 0)),
        pl.BlockSpec((1, 1, 1, 256), lambda b, g: (layer, g, 0, 0)),
    ]
    args = [PB, PB, PB, PB, PF, cw, dtb, alog, e64, e128, dsk, ng]
    aliases = {}
    out_specs = [pl.BlockSpec((L, 256), lambda b, g: (row0 + b, g))]
    out_shape = [jax.ShapeDtypeStruct((N_TOK, MIX_W), f32)]
    if has_init:
        in_specs.append(pl.BlockSpec((1, 1, 2, 4, 64, 128), lambda b, g: (b, layer, 0, g, 0, 0)))
        args.append(state)
        in_specs.append(pl.BlockSpec(memory_space=pl.ANY))
        args.append(prev)
        aliases = {len(args) - 1: 0}
    else:
        if chain_state:
            in_specs.append(pl.BlockSpec(memory_space=pl.ANY))
            args.append(st_prev)
            aliases = {len(args) - 1: 1}
        out_specs.append(pl.BlockSpec((1, 1, 2, 4, 64, 128), lambda b, g: (b, layer, 0, g, 0, 0)))
        out_shape.append(jax.ShapeDtypeStruct((B, DEPTH, 2, SSM_HEADS, SSM_HEADDIM, SSM_STATE), f32))
    return pl.pallas_call(
        functools.partial(_ssd_body, L=L, has_init=has_init, chain_state=chain_state),
        grid=(B, SSM_GROUPS),
        in_specs=in_specs,
        out_specs=out_specs,
        out_shape=out_shape,
        scratch_shapes=[pltpu.VMEM((L, 512), f32)],
        input_output_aliases=aliases,
        compiler_params=_cparams(("parallel", "parallel")),
        name="ssd_mixer",
    )(*args)


def _s5_body(*refs, n, n_c, s_b, has_init, chain_state):
    it = iter(refs)
    u_ref, wt_ref, wsf_ref, wsb_ref, wof_ref, wob_ref, lam_ref, dsk_ref = [next(it) for _ in range(8)]
    if has_init:
        h0r_ref, h0i_ref = next(it), next(it)
        next(it)
    elif chain_state:
        next(it)
        next(it)
    o_ref = next(it)
    if not has_init:
        hr_ref, hi_ref = next(it), next(it)
    zf_s, zb_s, hf_s, hb_s, acc_s, za_s, zc_s = [next(it) for _ in range(7)]
    T = S5_CHUNK

    for j in range(T):
        ub = u_ref[pl.ds(j, n, stride=T), :].astype(bf16)
        a = _dot(ub, wt_ref[0, 0, :, (T - 1 - j) * 128:(2 * T - 1 - j) * 128])
        f = _dot(ub, wsf_ref[0, 0, j])
        b = _dot(ub, wsb_ref[0, 0, j])
        if j == 0:
            acc_s[...] = a
            za_s[...] = f
            zc_s[...] = b
        else:
            acc_s[...] += a
            za_s[...] += f
            zc_s[...] += b
    for kk in range(8):
        zf_s[kk] = za_s[:, kk * 128:(kk + 1) * 128]
        zb_s[kk] = zc_s[:, kk * 128:(kk + 1) * 128]

    lam = lam_ref[0, 0]

    def rows(c):
        return pl.ds(c, s_b, stride=n_c) if s_b > 1 else pl.ds(c, 1)

    def scan_step(c, h, z_s, h_s, lr, li):
        new = []
        for kk in range(8):
            h_s[kk, rows(c), :] = h[kk]
        for kk in range(4):
            l_r, l_i = lr[:, kk * 128:(kk + 1) * 128], li[:, kk * 128:(kk + 1) * 128]
            new.append(l_r * h[kk] - l_i * h[kk + 4] + z_s[kk, rows(c), :])
        for kk in range(4):
            l_r, l_i = lr[:, kk * 128:(kk + 1) * 128], li[:, kk * 128:(kk + 1) * 128]
            new.append(l_r * h[kk + 4] + l_i * h[kk] + z_s[kk + 4, rows(c), :])
        return tuple(new)

    def fstep(c, h):
        return scan_step(c, h, zf_s, hf_s, lam[0:1], lam[1:2])

    def bstep(t, h):
        return scan_step(n_c - 1 - t, h, zb_s, hb_s, lam[2:3], lam[3:4])

    def chunks8(re, im):
        return tuple([re[:, kk * 128:(kk + 1) * 128] for kk in range(4)]
                     + [im[:, kk * 128:(kk + 1) * 128] for kk in range(4)])

    if has_init:
        h0f = chunks8(h0r_ref[:, 0, 0, :], h0i_ref[:, 0, 0, :])
        h0b = chunks8(h0r_ref[:, 0, 1, :], h0i_ref[:, 0, 1, :])
    else:
        h0f = h0b = tuple(jnp.zeros((s_b, 128), f32) for _ in range(8))
    hf = lax.fori_loop(0, n_c, fstep, h0f)
    hb = lax.fori_loop(0, n_c, bstep, h0b)
    if not has_init:
        for kk in range(4):
            hr_ref[:, 0, 0, kk * 128:(kk + 1) * 128] = hf[kk]
            hi_ref[:, 0, 0, kk * 128:(kk + 1) * 128] = hf[kk + 4]
            hr_ref[:, 0, 1, kk * 128:(kk + 1) * 128] = hb[kk]
            hi_ref[:, 0, 1, kk * 128:(kk + 1) * 128] = hb[kk + 4]

    hin_f = jnp.concatenate([hf_s[kk] for kk in range(8)], axis=1).astype(bf16)
    hin_b = jnp.concatenate([hb_s[kk] for kk in range(8)], axis=1).astype(bf16)
    acc_s[...] += _dot(hin_f, wof_ref[0, 0])
    acc_s[...] += _dot(hin_b, wob_ref[0, 0])
    dsk = dsk_ref[0, 0]
    for i in range(T):
        o_ref[pl.ds(i, n, stride=T), :] = (acc_s[:, i * 128:(i + 1) * 128]
                                           + dsk * u_ref[pl.ds(i, n, stride=T), :])


def s5_mixer(PF, w, dsk, *, B, L, s_b, row0, layer, state_re=None, state_im=None, prev=None, st_prev=None):
    has_init = state_re is not None
    chain_state = st_prev is not None
    n_c = L // S5_CHUNK
    n = s_b * n_c
    rows = s_b * L
    one = pl.Buffered(1)
    in_specs = [
        pl.BlockSpec((rows, 128), lambda lb, sb: (row0 + sb, F_S5_U // 128 + lb)),
        pl.BlockSpec((1, 1, 128, 31 * 128), lambda lb, sb: (layer, lb, 0, 0), pipeline_mode=one),
        pl.BlockSpec((1, 1, 16, 128, 1024), lambda lb, sb: (layer, lb, 0, 0, 0), pipeline_mode=one),
        pl.BlockSpec((1, 1, 16, 128, 1024), lambda lb, sb: (layer, lb, 0, 0, 0), pipeline_mode=one),
        pl.BlockSpec((1, 1, 1024, 2048), lambda lb, sb: (layer, lb, 0, 0), pipeline_mode=one),
        pl.BlockSpec((1, 1, 1024, 2048), lambda lb, sb: (layer, lb, 0, 0), pipeline_mode=one),
        pl.BlockSpec((1, 1, 4, 512), lambda lb, sb: (layer, lb, 0, 0)),
        pl.BlockSpec((1, 1, 1, 128), lambda lb, sb: (layer, lb, 0, 0)),
    ]
    args = [PF, w["toep"], w["ws_f"], w["ws_b"], w["wo_f"], w["wo_b"], w["lam16"], dsk]
    aliases = {}
    out_specs = [pl.BlockSpec((rows, 128), lambda lb, sb: (row0 + sb, lb))]
    out_shape = [jax.ShapeDtypeStruct((N_TOK, MIX_W), f32)]
    if has_init:
        in_specs.append(pl.BlockSpec((s_b, 1, 2, 512), lambda lb, sb: (sb, layer, 0, lb)))
        in_specs.append(pl.BlockSpec((s_b, 1, 2, 512), lambda lb, sb: (sb, layer, 0, lb)))
        args += [state_re, state_im]
        in_specs.append(pl.BlockSpec(memory_space=pl.ANY))
        args.append(prev)
        aliases = {len(args) - 1: 0}
    else:
        if chain_state:
            in_specs += [pl.BlockSpec(memory_space=pl.ANY)] * 2
            args += list(st_prev)
            aliases = {len(args) - 2: 1, len(args) - 1: 2}
        out_specs += [pl.BlockSpec((s_b, 1, 2, 512), lambda lb, sb: (sb, layer, 0, lb))] * 2
        out_shape += [jax.ShapeDtypeStruct((B, DEPTH, 2, S5_GROUPS * S5_STATE), f32)] * 2
    return pl.pallas_call(
        functools.partial(_s5_body, n=n, n_c=n_c, s_b=s_b, has_init=has_init, chain_state=chain_state),
        grid=(4, B // s_b),
        in_specs=in_specs,
        out_specs=out_specs,
        out_shape=out_shape,
        scratch_shapes=([pltpu.VMEM((8, n, 128), f32)] * 4
                        + [pltpu.VMEM((n, 2048), f32), pltpu.VMEM((n, 1024), f32), pltpu.VMEM((n, 1024), f32)]),
        input_output_aliases=aliases,
        compiler_params=_cparams(("parallel", "parallel")),
        name="s5_mixer",
    )(*args)


def _s5_layer_weights(a_re, a_im, log_dt, b_re, b_im, c_re, c_im):
    hp = lax.Precision.HIGHEST
    T = S5_CHUNK
    dt = jnp.exp(log_dt)[..., None]
    lmag, ang = a_re * dt, a_im * dt
    taus = jnp.arange(T + 1, dtype=f32)[:, None, None, None]
    mag = jnp.exp(lmag[None] * taus)
    pr = mag * jnp.cos(ang[None] * taus)
    pi = mag * jnp.sin(ang[None] * taus)
    lam_r, lam_i = pr[1], pi[1]
    den = a_re * a_re + a_im * a_im
    zr = ((lam_r - 1.0) * a_re + lam_i * a_im) / den
    zi = (lam_i * a_re - (lam_r - 1.0) * a_im) / den
    bb_r = zr[..., None] * b_re[None] - zi[..., None] * b_im[None]
    bb_i = zr[..., None] * b_im[None] + zi[..., None] * b_re[None]
    lb_r = pr[..., None] * bb_r[None] - pi[..., None] * bb_i[None]
    lb_i = pr[..., None] * bb_i[None] + pi[..., None] * bb_r[None]
    pr, pi, lb_r, lb_i = lax.optimization_barrier((pr, pi, lb_r, lb_i))
    kk = jnp.sum(c_re[None, None, :, :, :, None] * lb_r[:, :, :, None, :, :]
                 - c_im[None, None, :, :, :, None] * lb_i[:, :, :, None, :, :], axis=4)
    eye8 = jnp.eye(8, dtype=f32)

    def blockdiag(m):
        lead = m.shape[:-3]
        a, b = m.shape[-2], m.shape[-1]
        m = m.reshape(lead + (4, 8, a, b))
        out = m[..., :, :, :, None, :] * eye8[:, None, :, None]
        return out.reshape(lead + (4, 8 * a, 8 * b))

    kt = jnp.swapaxes(kk, -1, -2)
    down = np.arange(T - 1, 0, -1)
    deltas = jnp.concatenate([kt[down, 1], kt[0:1, 0] + kt[0:1, 1], kt[1:T, 0]], axis=0)
    toep = blockdiag(deltas)
    toep = jnp.transpose(toep, (1, 2, 0, 3)).reshape(4, 128, 31 * 128)

    def state_in(lbr, lbi):
        re = blockdiag(jnp.swapaxes(lbr, -1, -2))
        im = blockdiag(jnp.swapaxes(lbi, -1, -2))
        return jnp.transpose(jnp.concatenate([re, im], axis=-1), (1, 0, 2, 3))

    rev16 = np.arange(T - 1, -1, -1)
    ws_f = state_in(lb_r[rev16, 0], lb_i[rev16, 0])
    ws_b = state_in(lb_r[:T, 1], lb_i[:T, 1])

    def state_out(prd, pid):
        cr = c_re[None] * prd[:, :, None, :] - c_im[None] * pid[:, :, None, :]
        ci = c_re[None] * pid[:, :, None, :] + c_im[None] * prd[:, :, None, :]
        re = blockdiag(jnp.swapaxes(cr, -1, -2))
        im = blockdiag(jnp.swapaxes(-ci, -1, -2))
        full = jnp.concatenate([re, im], axis=2)
        return jnp.transpose(full, (1, 2, 0, 3)).reshape(4, 1024, T * 128)

    wo_f = state_out(pr[1:T + 1, 0], pi[1:T + 1, 0])
    down16 = np.arange(T, 0, -1)
    wo_b = state_out(pr[down16, 1], pi[down16, 1])
    lam16 = jnp.stack([pr[T, 0], pi[T, 0], pr[T, 1], pi[T, 1]], 0)
    lam16 = jnp.transpose(lam16.reshape(4, 4, 512), (1, 0, 2))
    return dict(toep=toep.astype(bf16), ws_f=ws_f.astype(bf16), ws_b=ws_b.astype(bf16),
                wo_f=wo_f.astype(bf16), wo_b=wo_b.astype(bf16), lam16=lam16)


def _glu_body(y_ref, w_ref, b_ref, o_ref):
    g = jax.nn.gelu(y_ref[...])
    o_ref[...] = g * jax.nn.sigmoid(_dot(g.astype(bf16), w_ref[0]) + b_ref[0])


def s5_glu(y, w, b, layer, tm=512):
    return pl.pallas_call(
        _glu_body,
        grid=(N_TOK // tm,),
        in_specs=[
            pl.BlockSpec((tm, MIX_W), lambda i: (i, 0)),
            pl.BlockSpec((1, MIX_W, MIX_W), lambda i: (layer, 0, 0)),
            pl.BlockSpec((1, 1, MIX_W), lambda i: (layer, 0, 0)),
        ],
        out_specs=pl.BlockSpec((tm, MIX_W), lambda i: (i, 0)),
        out_shape=jax.ShapeDtypeStruct((N_TOK, MIX_W), f32),
        compiler_params=_cparams(("parallel",)),
        name="s5_glu",
    )(y, w, b)


def _merge_body(gp_ref, b0_ref, b1_ref, b2_ref, b3_ref, wb_ref, wo_ref, x_ref, g1_ref, o_ref, acc_s):
    b = pl.program_id(1)
    for bb, br_ref in enumerate((b0_ref, b1_ref, b2_ref, b3_ref)):
        @pl.when(b == bb)
        def _(br_ref=br_ref, bb=bb):
            up = _dot(br_ref[...].astype(bf16), wb_ref[0, 0])
            t = jax.nn.sigmoid(gp_ref[...].astype(f32)) * up
            if bb == 0:
                acc_s[...] = t
            else:
                acc_s[...] += t

    @pl.when(b == 3)
    def _():
        mix = _dot(acc_s[...].astype(bf16), wo_ref[0])
        o_ref[...] = x_ref[...] + g1_ref[0, 0] * mix


def merge_branches(PB, branches, w_branch, w_out, x, mod, layer, tm=512):
    br_spec = pl.BlockSpec((tm, MIX_W), lambda i, b: (i, 0))
    return pl.pallas_call(
        _merge_body,
        grid=(N_TOK // tm, 4),
        in_specs=[
            pl.BlockSpec((tm, D), lambda i, b: (i, B_GATE // D + b)),
            br_spec, br_spec, br_spec, br_spec,
            pl.BlockSpec((1, 1, MIX_W, D), lambda i, b: (layer, b, 0, 0)),
            pl.BlockSpec((1, D, D), lambda i, b: (layer, 0, 0), pipeline_mode=pl.Buffered(1)),
            pl.BlockSpec((tm, D), lambda i, b: (i, 0)),
            _mod_spec(layer, 2, tm),
        ],
        out_specs=pl.BlockSpec((tm, D), lambda i, b: (i, 0)),
        out_shape=jax.ShapeDtypeStruct((N_TOK, D), f32),
        scratch_shapes=[pltpu.VMEM((tm, D), f32)],
        compiler_params=_cparams(("parallel", "arbitrary")),
        name="merge_branches",
    )(PB, *branches, w_branch, w_out, x, mod)


def _ffn_body(*refs, n_e, steps_per_e, final_norm, n_ctx_tiles):
    it = iter(refs)
    x_ref, sh_ref, sc_ref, gt_ref, g_ref, w1_ref, w3_ref, w2_ref = [next(it) for _ in range(8)]
    if n_e > 1:
        rwh_ref, rwl_ref, rb_ref = [next(it) for _ in range(3)]
    fg_ref = next(it) if final_norm else None
    o_ref = next(it)
    o2_ref = next(it) if final_norm else None
    h_s = next(it)
    acc_s = next(it) if final_norm else o_ref
    gate_s = next(it) if n_e > 1 else None
    i = pl.program_id(0)
    j = pl.program_id(1)
    nj = pl.num_programs(1)

    @pl.when(j == 0)
    def _():
        h = _modnorm(x_ref[...], g_ref[0], sc_ref[0, 0], sh_ref[0, 0])
        h_s[...] = h.astype(bf16)
        if n_e > 1:
            logits = _dot_hi(h, rwh_ref[0], rwl_ref[0]) + rb_ref[0]
            lane = lax.broadcasted_iota(jnp.int32, logits.shape, 1)
            neg = jnp.float32(-jnp.inf)
            lv = jnp.where(lane < n_e, logits, neg)
            m1 = jnp.max(lv, axis=1, keepdims=True)
            i1 = jnp.min(jnp.where(lv == m1, lane, 128), axis=1, keepdims=True)
            lv2 = jnp.where(lane == i1, neg, lv)
            m2 = jnp.max(lv2, axis=1, keepdims=True)
            i2 = jnp.min(jnp.where(lv2 == m2, lane, 128), axis=1, keepdims=True)
            e2 = jnp.exp(m2 - m1)
            p1 = 1.0 / (1.0 + e2)
            gate_s[...] = jnp.where(lane == i1, p1, 0.0) + jnp.where(lane == i2, e2 * p1, 0.0)

    hb = h_s[...]
    hid = _silu(_dot(hb, w1_ref[0, 0])) * _dot(hb, w3_ref[0, 0])
    if n_e > 1:
        e = j // steps_per_e
        lane = lax.broadcasted_iota(jnp.int32, gate_s.shape, 1)
        hid = hid * jnp.sum(jnp.where(lane == e, gate_s[...], 0.0), axis=1, keepdims=True)
    hidb = hid.astype(bf16)
    cb = 512

    @pl.when(j == 0)
    def _():
        for c0 in range(0, D, cb):
            acc_s[:, c0:c0 + cb] = _dot(hidb, w2_ref[0, 0, :, c0:c0 + cb])

    @pl.when(j > 0)
    def _():
        for c0 in range(0, D, cb):
            acc_s[:, c0:c0 + cb] += _dot(hidb, w2_ref[0, 0, :, c0:c0 + cb])

    @pl.when(j == nj - 1)
    def _():
        y = x_ref[...] + gt_ref[0, 0] * acc_s[...]
        if final_norm:
            ms = jnp.mean(y * y, axis=-1, keepdims=True)
            y = y * lax.rsqrt(ms + EPS) * fg_ref[...]

            @pl.when(i < n_ctx_tiles)
            def _():
                o_ref[...] = y

            @pl.when(i >= n_ctx_tiles)
            def _():
                o2_ref[...] = y
        else:
            o_ref[...] = y


def ffn_layer(x, mod, g, w1, w3, w2, layer, widx, router=None, final_g=None, tm=1024, tf=512):
    n_e, f = w1.shape[1], w1.shape[3]
    spe = f // tf
    final = final_g is not None
    nct = N_CTX // tm
    in_specs = [
        pl.BlockSpec((tm, D), lambda i, j: (i, 0), pipeline_mode=pl.Buffered(1)),
        _mod_spec(layer, 3, tm),
        _mod_spec(layer, 4, tm),
        _mod_spec(layer, 5, tm),
        pl.BlockSpec((1, 1, D), lambda i, j: (layer, 0, 0)),
        pl.BlockSpec((1, 1, D, tf), lambda i, j: (widx, j // spe, 0, j % spe)),
        pl.BlockSpec((1, 1, D, tf), lambda i, j: (widx, j // spe, 0, j % spe)),
        pl.BlockSpec((1, 1, tf, D), lambda i, j: (widx, j // spe, j % spe, 0)),
    ]
    args = [x, mod, mod, mod, g, w1, w3, w2]
    scratch = [pltpu.VMEM((tm, D), bf16)]
    if final:
        scratch.append(pltpu.VMEM((tm, D), f32))
    if n_e > 1:
        in_specs += [pl.BlockSpec((1, D, 128), lambda i, j: (widx, 0, 0)),
                     pl.BlockSpec((1, D, 128), lambda i, j: (widx, 0, 0)),
                     pl.BlockSpec((1, 1, 128), lambda i, j: (widx, 0, 0))]
        args += list(router)
        scratch.append(pltpu.VMEM((tm, 128), f32))
    if final:
        in_specs.append(pl.BlockSpec((1, D), lambda i, j: (0, 0)))
        args.append(final_g.reshape(1, D))
        out_specs = [pl.BlockSpec((tm, D), lambda i, j: (jnp.minimum(i, nct - 1), 0)),
                     pl.BlockSpec((tm, D), lambda i, j: (jnp.maximum(i - nct, 0), 0))]
        out_shape = [jax.ShapeDtypeStruct((N_CTX, D), f32), jax.ShapeDtypeStruct((N_LAT, D), f32)]
    else:
        out_specs = pl.BlockSpec((tm, D), lambda i, j: (i, 0))
        out_shape = jax.ShapeDtypeStruct((N_TOK, D), f32)
    return pl.pallas_call(
        functools.partial(_ffn_body, n_e=n_e, steps_per_e=spe, final_norm=final, n_ctx_tiles=nct),
        grid=(N_TOK // tm, n_e * spe),
        in_specs=in_specs,
        out_specs=out_specs,
        out_shape=out_shape,
        scratch_shapes=scratch,
        compiler_params=_cparams(("arbitrary" if final else "parallel", "arbitrary")),
        name="ffn_moe" if n_e > 1 else "ffn_dense",
    )(*args)


def _split_hi_lo(w):
    hi = w.astype(bf16)
    lo = (w - hi.astype(f32)).astype(bf16)
    return hi, lo


def _reorder_w_in(w):
    pad = jnp.zeros(w.shape[:2] + (N_PF - F_SMALL - 48,), w.dtype)
    return jnp.concatenate([
        w[..., 2080:3104], w[..., 5680:6192], w[..., 1536:1568], w[..., 5664:5680], pad,
        w[..., 6192:14384], w[..., 0:1536], w[..., 1568:2080], w[..., 3104:5664],
    ], axis=-1).astype(bf16)


def _gla_gate_weights(wa2, ba):
    t = wa2.reshape(DEPTH, 2, GLA_RANK, 2, 128)
    w = jnp.einsum("ldrhc,de->lhdrec", t, jnp.eye(2, dtype=f32))
    w = w.reshape(DEPTH, 2, 2 * GLA_RANK, 256)
    w = jnp.pad(w, ((0, 0), (0, 0), (SM_LR, 128 - SM_LR - 2 * GLA_RANK), (0, 0)))
    b = jnp.transpose(ba.reshape(DEPTH, 2, 2, 128), (0, 2, 1, 3)).reshape(DEPTH, 2, 1, 256)
    hi, lo = _split_hi_lo(w)
    return hi, lo, b


def _hgrn_bounds(logits):
    p = jax.nn.softmax(logits.astype(f32), axis=1)
    lower = jnp.maximum(jnp.cumsum(p, axis=1) - p[:, :1], 0.0)
    lb = jnp.transpose(lower, (1, 0, 2)).reshape(DEPTH, 2, HGRN_HEADS, HGRN_DIM)
    rows = jnp.stack([jnp.log(lb), jnp.log1p(-lb), 1.0 - lb], axis=3)
    rows = jnp.transpose(rows, (0, 2, 1, 3, 4)).reshape(DEPTH, HGRN_HEADS, 6, HGRN_DIM)
    return jnp.pad(rows, ((0, 0), (0, 0), (0, 2), (0, 0)))


def _ssd_select_mats():
    e64 = np.zeros((2, 2, 128, 256), np.float32)
    e128 = np.zeros((2, 2, 128, 512), np.float32)
    for g in range(2):
        for d in range(2):
            for h in range(4):
                lane = SM_DT + d * 8 + g * 4 + h
                e64[g, d, lane, h * 64:(h + 1) * 64] = 1.0
                e128[g, d, lane, h * 128:(h + 1) * 128] = 1.0
    return jnp.asarray(e64, bf16), jnp.asarray(e128, bf16)


def _ssd_params(conv_w, conv_b, dt_bias, a_log, d_skip, norm_g):
    cwb = jnp.concatenate([conv_w, conv_b[:, None]], axis=1)
    cw = jnp.concatenate([cwb[..., 0:512].reshape(DEPTH, 4, 2, 256),
                          cwb[..., 512:768].reshape(DEPTH, 4, 2, 128),
                          cwb[..., 768:1024].reshape(DEPTH, 4, 2, 128)], axis=-1)
    cw = jnp.transpose(cw, (0, 2, 1, 3))
    lanes = ((0, 0), (SM_DT, 128 - SM_DT - 16))
    dtb = jnp.pad(dt_bias.reshape(DEPTH, 16), lanes).reshape(DEPTH, 1, 128)
    alog = jnp.pad(a_log.reshape(DEPTH, 16), lanes).reshape(DEPTH, 1, 128)
    dsk = jnp.repeat(d_skip, SSM_HEADDIM, axis=1).reshape(DEPTH, 2, 1, 256)
    return cw, dtb, alog, dsk, norm_g.reshape(DEPTH, 2, 1, 256)


def kernel(x_prompt, x_sample, c, c_ctx, state_gla, state_hgrn, state_ssm, state_s5_re, state_s5_im, norm1_g, norm2_g, ada_w, ada_b, w_in, gla_wa2, gla_ba, gla_norm_g, hgrn_lb_logits, hgrn_norm_g, ssm_conv_w, ssm_conv_b, ssm_a_log, ssm_dt_bias, ssm_d, ssm_norm_g, s5_a_re, s5_a_im, s5_log_dt, s5_b_re, s5_b_im, s5_c_re, s5_c_im, s5_d, s5_glu_w, s5_glu_b, w_branch, w_out, ffn_w1, ffn_w3, ffn_w2, router_w, router_b, moe_w1, moe_w3, moe_w2, final_norm_g):
    w_all = _reorder_w_in(w_in)
    wah, wal, gba = _gla_gate_weights(gla_wa2, gla_ba)
    gng = gla_norm_g.reshape(DEPTH, 1, GLA_DV)
    lbp = _hgrn_bounds(hgrn_lb_logits)
    hng = hgrn_norm_g.reshape(DEPTH, 1, HGRN_DIM)
    e64, e128 = _ssd_select_mats()
    cw, dtb, alog, dsk, sng = _ssd_params(ssm_conv_w, ssm_conv_b, ssm_dt_bias, ssm_a_log, ssm_d, ssm_norm_g)
    w5 = jax.vmap(_s5_layer_weights)(s5_a_re, s5_a_im, s5_log_dt, s5_b_re, s5_b_im, s5_c_re, s5_c_im)
    d5 = s5_d.reshape(DEPTH, 4, 1, 128)
    glu_w = s5_glu_w.astype(bf16)
    glu_b = s5_glu_b.reshape(DEPTH, 1, MIX_W)
    wbr = w_branch.astype(bf16)
    wout = w_out.astype(bf16)
    n1 = norm1_g.reshape(DEPTH, 1, D)
    n2 = norm2_g.reshape(DEPTH, 1, D)
    fw1, fw3, fw2 = ffn_w1.astype(bf16)[:, None], ffn_w3.astype(bf16)[:, None], ffn_w2.astype(bf16)[:, None]
    mw1, mw3, mw2 = moe_w1.astype(bf16), moe_w3.astype(bf16), moe_w2.astype(bf16)
    rw = jnp.pad(router_w, ((0, 0), (0, 0), (0, 128 - N_EXPERTS)))
    router = _split_hi_lo(rw) + (jnp.pad(router_b, ((0, 0), (0, 128 - N_EXPERTS))).reshape(-1, 1, 128),)
    s5_re = state_s5_re.reshape(DEC_BATCH, DEPTH, 2, S5_GROUPS * S5_STATE)
    s5_im = state_s5_im.reshape(DEC_BATCH, DEPTH, 2, S5_GROUPS * S5_STATE)

    x = assemble_tokens(x_prompt.reshape(N_CTX, D), x_sample.reshape(N_LAT, D), _pos_table())
    cond8 = jnp.concatenate([c_ctx[None], c, jnp.zeros((MOD_ROWS - 1 - DEC_BATCH, D), f32)], axis=0)
    mod = ada_modulation(cond8, ada_w, ada_b).reshape(DEPTH, MOD_ROWS * 6, 1, D)

    lat0 = N_CTX // DEC_SEQ
    ctx = dict(B=BATCH, L=SEQ, row0=0)
    lat = dict(B=DEC_BATCH, L=DEC_SEQ, row0=lat0)
    st_gla = st_hg = st_ssm = st_s5 = None
    for l in range(DEPTH):
        PF, PB = in_projection(x, mod, n1, w_all, l)

        o_gla, st_gla = gla_mixer(PF, PB, wah, wal, gba, gng, layer=l, st_prev=st_gla, **ctx)
        (o_gla,) = gla_mixer(PF, PB, wah, wal, gba, gng, layer=l, state=state_gla, prev=o_gla, **lat)

        o_hg, st_hg = hgrn_mixer(PF, PB, lbp, hng, layer=l, st_prev=st_hg, **ctx)
        (o_hg,) = hgrn_mixer(PF, PB, lbp, hng, layer=l, state=state_hgrn, prev=o_hg, **lat)

        o_ssm, st_ssm = ssd_mixer(PF, PB, cw, dtb, alog, e64, e128, dsk, sng, layer=l, st_prev=st_ssm, **ctx)
        (o_ssm,) = ssd_mixer(PF, PB, cw, dtb, alog, e64, e128, dsk, sng, layer=l, state=state_ssm,
                             prev=o_ssm, **lat)

        y5, hr, hi = s5_mixer(PF, w5, d5, B=BATCH, L=SEQ, s_b=min(8, BATCH), row0=0, layer=l, st_prev=st_s5)
        st_s5 = (hr, hi)
        (y5,) = s5_mixer(PF, w5, d5, B=DEC_BATCH, L=DEC_SEQ, s_b=1, row0=lat0, layer=l,
                         state_re=s5_re, state_im=s5_im, prev=y5)
        o_s5 = s5_glu(y5, glu_w, glu_b, l)

        x = merge_branches(PB, (o_gla, o_hg, o_ssm, o_s5), wbr, wout, x, mod, l)

        last = l == DEPTH - 1
        kw = dict(final_g=final_norm_g, tm=512) if last else {}
        if l % 2 == 0:
            x = ffn_layer(x, mod, n2, fw1, fw3, fw2, l, l // 2, **kw)
        else:
            x = ffn_layer(x, mod, n2, mw1, mw3, mw2, l, l // 2, router=router, **kw)

    y_ctx, y_lat = x
    return (y_ctx.reshape(BATCH, SEQ, D), y_lat.reshape(DEC_BATCH, DEC_SEQ, D),
            st_gla, st_hg, st_ssm,
            st_s5[0].reshape(BATCH, DEPTH, 2, S5_GROUPS, S5_STATE),
            st_s5[1].reshape(BATCH, DEPTH, 2, S5_GROUPS, S5_STATE))
```

```python
import functools

import numpy as np
import jax
import jax.numpy as jnp
from jax import lax
from jax.experimental import pallas as pl
from jax.experimental.pallas import tpu as pltpu

f32 = jnp.float32
bf16 = jnp.bfloat16

D = 2048
BATCH = 32
SEQ = 256
DEPTH = 4
DEC_BATCH = 4
DEC_SEQ = 4096
GRID_W = 64
MIX_W = 512
GLA_HEADS = 4
GLA_DK = 64
GLA_DV = 128
GLA_RANK = 16
GLA_NORMALIZER = 16.0
HGRN_HEADS = 4
HGRN_DIM = 128
SSM_HEADS = 8
SSM_HEADDIM = 64
SSM_GROUPS = 2
SSM_STATE = 128
S5_CH = 16
S5_STATE = 64
S5_GROUPS = 32
D_FF = 5632
N_EXPERTS = 8
D_FF_EXPERT = 1024
EPS = 1e-6

N_CTX = BATCH * SEQ
N_LAT = DEC_BATCH * DEC_SEQ
N_TOK = N_CTX + N_LAT
MOD_ROWS = 8

F_HG_F, F_S5_U, F_SMALL = 0, 1024, 1536
N_PF = 2048
B_GATE, B_GLA_Q, B_GLA_K, B_GLA_V, B_GLA_R = 0, 8192, 8448, 8704, 9216
B_HG_Q, B_HG_I, B_HG_G, B_SSM_Z, B_SSM_XBC = 9728, 10240, 10752, 11264, 11776
N_PB = 12800
SM_LR, SM_DT = 0, 32

GLS_CHUNK = 128
GLS_SUB = 32
SSD_CHUNK = 128
S5_CHUNK = 16

VMEM_LIMIT = 58 * 1024 * 1024


def _cparams(sem):
    return pltpu.CompilerParams(dimension_semantics=sem, vmem_limit_bytes=VMEM_LIMIT)


def _dot(a, b):
    return jnp.dot(a, b, preferred_element_type=f32)


def _dot_nt(a, b):
    return lax.dot_general(a, b, (((1,), (1,)), ((), ())), preferred_element_type=f32)


def _dot_tn(a, b):
    return lax.dot_general(a, b, (((0,), (0,)), ((), ())), preferred_element_type=f32)


def _split2(x):
    hi = x.astype(bf16)
    return hi, (x - hi.astype(f32)).astype(bf16)


def _dot_sel_lhs(t, x):
    x1, x2 = _split2(x)
    return _dot(t, x1) + _dot(t, x2)


def _dot_sel_rhs(x, e):
    x1, x2 = _split2(x)
    return _dot(x1, e) + _dot(x2, e)


def _dot_sel_rhs3(x, e):
    x1, x2 = _split2(x)
    x3 = (x - x1.astype(f32) - x2.astype(f32)).astype(bf16)
    return _dot(x1, e) + _dot(x2, e) + _dot(x3, e)


def _dot_hi(x, w_hi, w_lo):
    x1, x2 = _split2(x)
    return _dot(x1, w_hi) + _dot(x2, w_hi) + _dot(x1, w_lo)


def _log_sigmoid(x):
    return jnp.minimum(x, 0.0) - jnp.log1p(jnp.exp(-jnp.abs(x)))


def _softplus(x):
    return jnp.maximum(x, 0.0) + jnp.log1p(jnp.exp(-jnp.abs(x)))


def _silu(x):
    return x * jax.nn.sigmoid(x)


def _modnorm(x, g, scale, shift):
    ms = jnp.mean(x * x, axis=-1, keepdims=True)
    y = x * lax.rsqrt(ms + EPS) * g
    return y * (1.0 + scale) + shift


def _tri(n, upper):
    ii = lax.broadcasted_iota(jnp.int32, (n, n), 0)
    jj = lax.broadcasted_iota(jnp.int32, (n, n), 1)
    return (ii <= jj) if upper else (ii >= jj)


def _mod_row(i, tm):
    start = i * tm
    return jnp.where(start < N_CTX, 0, 1 + (start - N_CTX) // DEC_SEQ)


def _mod_spec(layer, k, tm):
    return pl.BlockSpec((1, 1, 1, D), lambda i, j: (layer, _mod_row(i, tm) * 6 + k, 0, 0))


def _assemble_body(xp_ref, xs_ref, pos_ref, o_ref, *, n_ctx_tiles):
    i = pl.program_id(0)

    @pl.when(i < n_ctx_tiles)
    def _():
        o_ref[...] = xp_ref[...]

    @pl.when(i >= n_ctx_tiles)
    def _():
        o_ref[...] = xs_ref[...] + pos_ref[...]


def assemble_tokens(xp2, xs2, pos, tm=512):
    nct = N_CTX // tm
    npos = DEC_SEQ // tm
    return pl.pallas_call(
        functools.partial(_assemble_body, n_ctx_tiles=nct),
        grid=(N_TOK // tm,),
        in_specs=[
            pl.BlockSpec((tm, D), lambda i: (jnp.minimum(i, nct - 1), 0)),
            pl.BlockSpec((tm, D), lambda i: (jnp.maximum(i - nct, 0), 0)),
            pl.BlockSpec((tm, D), lambda i: (jnp.maximum(i - nct, 0) % npos, 0)),
        ],
        out_specs=pl.BlockSpec((tm, D), lambda i: (i, 0)),
        out_shape=jax.ShapeDtypeStruct((N_TOK, D), f32),
        compiler_params=_cparams(("parallel",)),
        name="assemble_tokens",
    )(xp2, xs2, pos)


def _pos_table():
    rows = DEC_SEQ // GRID_W
    row = jnp.repeat(jnp.arange(rows, dtype=f32), GRID_W)
    col = jnp.tile(jnp.arange(GRID_W, dtype=f32), rows)

    def sincos(p, d):
        half = d // 2
        omega = 1.0 / (10000.0 ** (jnp.arange(half, dtype=f32) / half))
        ang = p[:, None] * omega[None, :]
        return jnp.concatenate([jnp.sin(ang), jnp.cos(ang)], axis=-1)

    return jnp.concatenate([sincos(row, D // 2), sincos(col, D // 2)], axis=-1)


def _ada_body(c_ref, w_ref, b_ref, o_ref):
    s = _silu(c_ref[...])
    s1, s2 = _split2(s)
    w = w_ref[0].astype(bf16)
    o_ref[0] = _dot(s1, w) + _dot(s2, w) + b_ref[0]


def ada_modulation(cond8, ada_w, ada_b, tn=1024):
    return pl.pallas_call(
        _ada_body,
        grid=(DEPTH, 6 * D // tn),
        in_specs=[
            pl.BlockSpec((MOD_ROWS, D), lambda l, j: (0, 0)),
            pl.BlockSpec((1, D, tn), lambda l, j: (l, 0, j)),
            pl.BlockSpec((1, 1, tn), lambda l, j: (l, 0, j)),
        ],
        out_specs=pl.BlockSpec((1, MOD_ROWS, tn), lambda l, j: (l, 0, j)),
        out_shape=jax.ShapeDtypeStruct((DEPTH, MOD_ROWS, 6 * D), f32),
        compiler_params=_cparams(("parallel", "parallel")),
        name="ada_modulation",
    )(cond8, ada_w, ada_b.reshape(DEPTH, 1, 6 * D))


def _inproj_body(x_ref, sh_ref, sc_ref, g_ref, w_ref, pf_ref, pb_ref, h_s, *, nf):
    j = pl.program_id(1)

    @pl.when(j == 0)
    def _():
        h_s[...] = _modnorm(x_ref[...], g_ref[0], sc_ref[0, 0], sh_ref[0, 0]).astype(bf16)

    y = _dot(h_s[...], w_ref[0])

    @pl.when(j < nf)
    def _():
        pf_ref[...] = y

    @pl.when(j >= nf)
    def _():
        pb_ref[...] = y.astype(bf16)


def in_projection(x, mod, g, w_all, layer, tm=1024, tn=512):
    nf = N_PF // tn
    nb = N_PB // tn
    return pl.pallas_call(
        functools.partial(_inproj_body, nf=nf),
        grid=(N_TOK // tm, nf + nb),
        in_specs=[
            pl.BlockSpec((tm, D), lambda i, j: (i, 0)),
            _mod_spec(layer, 0, tm),
            _mod_spec(layer, 1, tm),
            pl.BlockSpec((1, 1, D), lambda i, j: (layer, 0, 0)),
            pl.BlockSpec((1, D, tn), lambda i, j: (layer, 0, j)),
        ],
        out_specs=[
            pl.BlockSpec((tm, tn), lambda i, j: (i, jnp.minimum(j, nf - 1))),
            pl.BlockSpec((tm, tn), lambda i, j: (i, jnp.maximum(j - nf, 0))),
        ],
        out_shape=[
            jax.ShapeDtypeStruct((N_TOK, N_PF), f32),
            jax.ShapeDtypeStruct((N_TOK, N_PB), bf16),
        ],
        scratch_shapes=[pltpu.VMEM((tm, D), bf16)],
        compiler_params=_cparams(("parallel", "arbitrary")),
        name="in_projection",
    )(x, mod, mod, g, w_all)


def _gls_dir(q, k, g, d):
    c, sub = GLS_CHUNK, GLS_SUB
    nb = c // sub
    tri = jnp.where(_tri(c, upper=(d == 1)), 1.0, 0.0).astype(bf16)
    b = _dot_sel_lhs(tri, g)
    zero = jnp.zeros((1, 128), f32)
    if d == 0:
        mid = [b[sub * i + sub // 2 - 1:sub * i + sub // 2] for i in range(nb)]
        end = [b[sub * i + sub - 1:sub * i + sub] for i in range(nb)]
        start = [zero] + end[:-1]
        total = end[-1]
    else:
        mid = [b[sub * i + sub // 2:sub * i + sub // 2 + 1] for i in range(nb)]
        end = [b[sub * i:sub * i + 1] for i in range(nb)]
        start = end[1:] + [zero]
        total = end[0]

    def rows(vals):
        return jnp.concatenate([jnp.broadcast_to(v, (sub, 128)) for v in vals], axis=0)

    r_mid = rows(mid)
    qd = q * jnp.exp(b - r_mid)
    kd = k * jnp.exp(r_mid - b)
    qo = qd * rows([jnp.exp(m - s) for m, s in zip(mid, start)])
    ke = kd * rows([jnp.exp(e - m) for m, e in zip(mid, end)])
    q_in = qo * rows([jnp.exp(s) for s in start])
    k_out = ke * rows([jnp.exp(total - e) for e in end])
    kr = []
    for i in range(nb):
        prior = range(i) if d == 0 else range(i + 1, nb)
        if len(prior) == 0:
            kr.append(None)
            continue
        blocks = []
        for jb in range(nb):
            if jb in prior:
                blocks.append(ke[jb * sub:(jb + 1) * sub] * jnp.exp(start[i] - end[jb]))
            else:
                blocks.append(jnp.zeros((sub, 128), f32))
        kr.append(jnp.concatenate(blocks, axis=0).astype(bf16))
    return qd, kd.astype(bf16), qo, kr, q_in, k_out, jnp.exp(total)


def _gls_head(parts, v, st, mask, kmask):
    qd, kd, qo, kr, q_in, k_out, dec = parts
    sub = GLS_SUB
    if kmask is not None:
        qd, qo, q_in, k_out = qd * kmask, qo * kmask, q_in * kmask, k_out * kmask
    att = jnp.where(mask, _dot_nt(qd.astype(bf16), kd), 0.0)
    qob = qo.astype(bf16)
    offs = []
    for i, kri in enumerate(kr):
        if kri is None:
            offs.append(jnp.zeros((sub, GLS_CHUNK), f32))
        else:
            offs.append(_dot_nt(qob[i * sub:(i + 1) * sub], kri))
    att = att + jnp.concatenate(offs, axis=0)
    vb = v.astype(bf16)
    o = _dot(att.astype(bf16), vb) + _dot_nt(q_in.astype(bf16), st.astype(bf16))
    st_new = st * dec + _dot_tn(vb, k_out.astype(bf16))
    return o, st_new


def _gls_masks():
    c, sub = GLS_CHUNK, GLS_SUB
    ii = lax.broadcasted_iota(jnp.int32, (c, c), 0)
    jj = lax.broadcasted_iota(jnp.int32, (c, c), 1)
    same = (ii // sub) == (jj // sub)
    return [same & (ii >= jj), same & (ii <= jj)]


def _head_post(o, gate_in, ng):
    ms = jnp.mean(o * o, axis=-1, keepdims=True)
    return o * lax.rsqrt(ms + EPS) * ng * _silu(gate_in.astype(f32))


def _scan_loops(n, chunk_fn, post_fn, o_ref, init):
    C = GLS_CHUNK

    def store(r0, outs, final):
        for h, o in enumerate(outs):
            sl = (pl.ds(r0, C), slice(h * 128, (h + 1) * 128))
            if final:
                o_ref[sl] = post_fn(r0, h, o_ref[sl] + o)
            else:
                o_ref[sl] = o

    def half(final):
        def body(t, carry):
            sf, sb = carry
            r0, of, sf = chunk_fn(t, 0, sf)
            store(r0, of, final)
            r1, ob, sb = chunk_fn(n - 1 - t, 1, sb)
            store(r1, ob, final)
            return sf, sb
        return body

    unroll = 2 if (n // 2) % 2 == 0 else 1
    carry = lax.fori_loop(0, n // 2, half(False), init, unroll=unroll)
    return lax.fori_loop(n // 2, n, half(True), carry, unroll=unroll)


def _gla_body(*refs, L, has_init, chain_state):
    it = iter(refs)
    q_ref, k_ref, v_ref, r_ref, sm_ref, wah_ref, wal_ref, ba_ref, ng_ref = [next(it) for _ in range(9)]
    s0_ref = next(it) if has_init else None
    if has_init or chain_state:
        next(it)
    o_ref = next(it)
    st_ref = None if has_init else next(it)
    C = GLS_CHUNK
    n = L // C
    lane = lax.broadcasted_iota(jnp.int32, (1, 128), 1)
    kmasks = [jnp.where(lane < 64, 1.0, 0.0), jnp.where(lane >= 64, 1.0, 0.0)]
    masks = _gls_masks()
    wah, wal, ba, ng = wah_ref[0, 0], wal_ref[0, 0], ba_ref[0, 0], ng_ref[0]

    def chunk(cidx, d, sts):
        r0 = pl.multiple_of(cidx * C, C)
        q = q_ref[pl.ds(r0, C), :].astype(f32) * (GLA_DK ** -0.5)
        k = k_ref[pl.ds(r0, C), :].astype(f32)
        sm = sm_ref[pl.ds(r0, C), :]
        sl = slice(d * 128, (d + 1) * 128)
        g = _log_sigmoid(_dot_hi(sm, wah[:, sl], wal[:, sl]) + ba[:, sl]) * (1.0 / GLA_NORMALIZER)
        parts = _gls_dir(q, k, g, d)
        outs, new = [], []
        for h in range(2):
            v = v_ref[pl.ds(r0, C), h * 128:(h + 1) * 128]
            o, s = _gls_head(parts, v, sts[h], masks[d], kmasks[h])
            outs.append(o)
            new.append(s)
        return r0, outs, new

    def post(r0, h, tot):
        return _head_post(tot, r_ref[pl.ds(r0, C), h * 128:(h + 1) * 128], ng)

    def init(d, h):
        if not has_init:
            return jnp.zeros((128, 128), f32)
        s = s0_ref[0, 0, d, h]
        z = jnp.zeros((64, 128), f32)
        full = jnp.concatenate([s, z], axis=0) if h == 0 else jnp.concatenate([z, s], axis=0)
        return full.T

    sf, sb = _scan_loops(n, chunk, post, o_ref, ([init(0, 0), init(0, 1)], [init(1, 0), init(1, 1)]))
    if st_ref is not None:
        for d, sts in enumerate((sf, sb)):
            for h in range(2):
                st_ref[0, 0, d, h] = sts[h].T[h * 64:(h + 1) * 64, :]


def gla_mixer(PF, PB, wa_hi, wa_lo, ba, ng, *, B, L, row0, layer, state=None, prev=None, st_prev=None):
    has_init = state is not None
    chain_state = st_prev is not None
    in_specs = [
        pl.BlockSpec((L, 128), lambda b, hp: (row0 + b, B_GLA_Q // 128 + hp)),
        pl.BlockSpec((L, 128), lambda b, hp: (row0 + b, B_GLA_K // 128 + hp)),
        pl.BlockSpec((L, 256), lambda b, hp: (row0 + b, B_GLA_V // 256 + hp)),
        pl.BlockSpec((L, 256), lambda b, hp: (row0 + b, B_GLA_R // 256 + hp)),
        pl.BlockSpec((L, 128), lambda b, hp: (row0 + b, F_SMALL // 128)),
        pl.BlockSpec((1, 1, 128, 256), lambda b, hp: (layer, hp, 0, 0)),
        pl.BlockSpec((1, 1, 128, 256), lambda b, hp: (layer, hp, 0, 0)),
        pl.BlockSpec((1, 1, 1, 256), lambda b, hp: (layer, hp, 0, 0)),
        pl.BlockSpec((1, 1, 128), lambda b, hp: (layer, 0, 0)),
    ]
    args = [PB, PB, PB, PB, PF, wa_hi, wa_lo, ba, ng]
    aliases = {}
    out_specs = [pl.BlockSpec((L, 256), lambda b, hp: (row0 + b, hp))]
    out_shape = [jax.ShapeDtypeStruct((N_TOK, MIX_W), f32)]
    if has_init:
        in_specs.append(pl.BlockSpec((1, 1, 2, 2, 64, 128), lambda b, hp: (b, layer, 0, hp, 0, 0)))
        args.append(state)
        in_specs.append(pl.BlockSpec(memory_space=pl.ANY))
        args.append(prev)
        aliases = {len(args) - 1: 0}
    else:
        if chain_state:
            in_specs.append(pl.BlockSpec(memory_space=pl.ANY))
            args.append(st_prev)
            aliases = {len(args) - 1: 1}
        out_specs.append(pl.BlockSpec((1, 1, 2, 2, 64, 128), lambda b, hp: (b, layer, 0, hp, 0, 0)))
        out_shape.append(jax.ShapeDtypeStruct((B, DEPTH, 2, GLA_HEADS, GLA_DK, GLA_DV), f32))
    return pl.pallas_call(
        functools.partial(_gla_body, L=L, has_init=has_init, chain_state=chain_state),
        grid=(B, 2),
        in_specs=in_specs,
        out_specs=out_specs,
        out_shape=out_shape,
        input_output_aliases=aliases,
        compiler_params=_cparams(("parallel", "parallel")),
        name="gla_mixer",
    )(*args)


def _hgrn_body(*refs, L, has_init, chain_state):
    it = iter(refs)
    q_ref, f0_ref, f1_ref, i_ref, g_ref, lb_ref, ng_ref = [next(it) for _ in range(7)]
    s0_ref = next(it) if has_init else None
    if has_init or chain_state:
        next(it)
    o_ref = next(it)
    st_ref = None if has_init else next(it)
    C = GLS_CHUNK
    n = L // C
    masks = _gls_masks()
    ng = ng_ref[0]
    f_refs = (f0_ref, f1_ref)

    def chunk(cidx, d, sts):
        r0 = pl.multiple_of(cidx * C, C)
        outs, new = [], []
        for h in range(2):
            sl = slice(h * 128, (h + 1) * 128)
            lbp = lb_ref[0, h]
            q = q_ref[pl.ds(r0, C), sl].astype(f32)
            fp = f_refs[d][pl.ds(r0, C), sl]
            lb, one_m_lb = lbp[2 * d:2 * d + 1], lbp[2 * d + 1:2 * d + 2]
            e = jnp.exp(-jnp.abs(fp))
            inv = 1.0 / (1.0 + e)
            pos = fp >= 0.0
            sig = jnp.where(pos, inv, e * inv)
            k = one_m_lb * jnp.where(pos, e * inv, inv)
            g = jnp.log(lb + one_m_lb * sig)
            parts = _gls_dir(q, k, g, d)
            o, s = _gls_head(parts, i_ref[pl.ds(r0, C), sl], sts[h], masks[d], None)
            outs.append(o)
            new.append(s)
        return r0, outs, new

    def post(r0, h, tot):
        return _head_post(tot, g_ref[pl.ds(r0, C), h * 128:(h + 1) * 128], ng)

    def init(d, h):
        if not has_init:
            return jnp.zeros((128, 128), f32)
        return s0_ref[0, 0, d, h].T

    sf, sb = _scan_loops(n, chunk, post, o_ref, ([init(0, 0), init(0, 1)], [init(1, 0), init(1, 1)]))
    if st_ref is not None:
        for d, sts in enumerate((sf, sb)):
            for h in range(2):
                st_ref[0, 0, d, h] = sts[h].T


def hgrn_mixer(PF, PB, lbp, ng, *, B, L, row0, layer, state=None, prev=None, st_prev=None):
    has_init = state is not None
    chain_state = st_prev is not None
    in_specs = [
        pl.BlockSpec((L, 256), lambda b, hp: (row0 + b, B_HG_Q // 256 + hp)),
        pl.BlockSpec((L, 256), lambda b, hp: (row0 + b, F_HG_F // 256 + hp)),
        pl.BlockSpec((L, 256), lambda b, hp: (row0 + b, F_HG_F // 256 + 2 + hp)),
        pl.BlockSpec((L, 256), lambda b, hp: (row0 + b, B_HG_I // 256 + hp)),
        pl.BlockSpec((L, 256), lambda b, hp: (row0 + b, B_HG_G // 256 + hp)),
        pl.BlockSpec((1, 2, 8, 128), lambda b, hp: (layer, hp, 0, 0)),
        pl.BlockSpec((1, 1, 128), lambda b, hp: (layer, 0, 0)),
    ]
    args = [PB, PF, PF, PB, PB, lbp, ng]
    aliases = {}
    out_specs = [pl.BlockSpec((L, 256), lambda b, hp: (row0 + b, hp))]
    out_shape = [jax.ShapeDtypeStruct((N_TOK, MIX_W), f32)]
    if has_init:
        in_specs.append(pl.BlockSpec((1, 1, 2, 2, 128, 128), lambda b, hp: (b, layer, 0, hp, 0, 0)))
        args.append(state)
        in_specs.append(pl.BlockSpec(memory_space=pl.ANY))
        args.append(prev)
        aliases = {len(args) - 1: 0}
    else:
        if chain_state:
            in_specs.append(pl.BlockSpec(memory_space=pl.ANY))
            args.append(st_prev)
            aliases = {len(args) - 1: 1}
        out_specs.append(pl.BlockSpec((1, 1, 2, 2, 128, 128), lambda b, hp: (b, layer, 0, hp, 0, 0)))
        out_shape.append(jax.ShapeDtypeStruct((B, DEPTH, 2, HGRN_HEADS, HGRN_DIM, HGRN_DIM), f32))
    return pl.pallas_call(
        functools.partial(_hgrn_body, L=L, has_init=has_init, chain_state=chain_state),
        grid=(B, 2),
        in_specs=in_specs,
        out_specs=out_specs,
        out_shape=out_shape,
        input_output_aliases=aliases,
        compiler_params=_cparams(("parallel", "parallel")),
        name="hgrn_mixer",
    )(*args)


def _ssd_body(*refs, L, has_init, chain_state):
    it = iter(refs)
    (xs_ref, bm_ref, cm_ref, z_ref, sm_ref, cw_ref, dtb_ref, alog_ref, e64_ref, e128_ref,
     dsk_ref, ng_ref) = [next(it) for _ in range(12)]
    s0_ref = next(it) if has_init else None
    if has_init or chain_state:
        next(it)
    o_ref = next(it)
    st_ref = None if has_init else next(it)
    xc_s = next(it)
    C = SSD_CHUNK
    n = L // C
    cw = cw_ref[0, 0]
    rowi = lax.broadcasted_iota(jnp.int32, (C, 1), 0)

    def conv_block(rb, _):
        r0 = pl.multiple_of(rb * C, C)
        rp = pl.multiple_of(jnp.maximum(r0 - 16, 0), 16)
        rn = pl.multiple_of(jnp.minimum(r0 + C, L - 16), 16)
        has_p = jnp.where(rb > 0, 1.0, 0.0)
        has_n = jnp.where(rb < n - 1, 1.0, 0.0)
        for ref, c0, w in ((xs_ref, 0, 256), (bm_ref, 256, 128), (cm_ref, 384, 128)):
            x = ref[pl.ds(r0, C), :].astype(f32)
            prev = ref[pl.ds(rp, 16), :].astype(f32)[15:16] * has_p
            nxt = ref[pl.ds(rn, 16), :].astype(f32)[0:1] * has_n
            x_dn = jnp.where(rowi == 0, prev, pltpu.roll(x, 1, axis=0))
            x_up = jnp.where(rowi == C - 1, nxt, pltpu.roll(x, C - 1, axis=0))
            y = (cw[0:1, c0:c0 + w] * x_dn + cw[1:2, c0:c0 + w] * x + cw[2:3, c0:c0 + w] * x_up
                 + cw[3:4, c0:c0 + w])
            xc_s[pl.ds(r0, C), c0:c0 + w] = _silu(y)
        return 0

    lax.fori_loop(0, n, conv_block, 0)

    masks = [_tri(C, upper=False), _tri(C, upper=True)]
    tris = [jnp.where(m, 1.0, 0.0).astype(bf16) for m in masks]
    lane256 = lax.broadcasted_iota(jnp.int32, (1, 256), 1)
    hmask = [jnp.where((lane256 >= h * 64) & (lane256 < (h + 1) * 64), 1.0, 0.0) for h in range(4)]
    dtb = dtb_ref[0]
    a_row = -jnp.exp(alog_ref[0])
    neg_inf = jnp.float32(-jnp.inf)

    def chunk(cidx, d, st):
        r0 = pl.multiple_of(cidx * C, C)
        xs = xc_s[pl.ds(r0, C), 0:256]
        bm = xc_s[pl.ds(r0, C), 256:384].astype(bf16)
        cm = xc_s[pl.ds(r0, C), 384:512].astype(bf16)
        dt_all = _softplus(sm_ref[pl.ds(r0, C), :] + dtb)
        a_all = dt_all * a_row
        dt64 = _dot_sel_rhs(dt_all, e64_ref[0, d])
        acs_all = _dot_sel_lhs(tris[d], a_all)
        acs64 = _dot_sel_rhs3(acs_all, e64_ref[0, d])
        acs128 = _dot_sel_rhs3(acs_all, e128_ref[0, d])
        end = acs64[C - 1:C] if d == 0 else acs64[0:1]
        xdt = xs * dt64
        scores = _dot_nt(cm, bm)
        y = jnp.exp(acs64) * _dot(cm, st.astype(bf16))
        for h in range(4):
            colb = acs128[:, h * 128:(h + 1) * 128]
            diff = jnp.where(masks[d], colb - colb.T, neg_inf)
            p = (scores * jnp.exp(diff)).astype(bf16)
            y = y + _dot(p, (xdt * hmask[h]).astype(bf16))
        xd = (xdt * jnp.exp(end - acs64)).astype(bf16)
        st_new = st * jnp.exp(end) + _dot_tn(bm, xd)
        return r0, y, st_new

    dsk = dsk_ref[0, 0]
    ng = ng_ref[0, 0]

    def post(r0, y):
        xs = xc_s[pl.ds(r0, C), 0:256]
        t = (y + dsk * xs) * _silu(z_ref[pl.ds(r0, C), :].astype(f32))
        ms = jnp.mean(t * t, axis=-1, keepdims=True)
        return t * lax.rsqrt(ms + EPS) * ng

    def first_half(t, carry):
        sf, sb = carry
        r0, yf, sf = chunk(t, 0, sf)
        o_ref[pl.ds(r0, C), :] = yf
        r1, yb, sb = chunk(n - 1 - t, 1, sb)
        o_ref[pl.ds(r1, C), :] = yb
        return sf, sb

    def second_half(t, carry):
        sf, sb = carry
        r0, yf, sf = chunk(t, 0, sf)
        o_ref[pl.ds(r0, C), :] = post(r0, o_ref[pl.ds(r0, C), :] + yf)
        r1, yb, sb = chunk(n - 1 - t, 1, sb)
        o_ref[pl.ds(r1, C), :] = post(r1, o_ref[pl.ds(r1, C), :] + yb)
        return sf, sb

    def init(d):
        if not has_init:
            return jnp.zeros((128, 256), f32)
        return s0_ref[0, 0, d].reshape(256, 128).T

    carry = (init(0), init(1))
    unroll = 2 if (n // 2) % 2 == 0 else 1
    carry = lax.fori_loop(0, n // 2, first_half, carry, unroll=unroll)
    sf, sb = lax.fori_loop(n // 2, n, second_half, carry, unroll=unroll)
    if st_ref is not None:
        st_ref[0, 0, 0] = sf.T.reshape(4, 64, 128)
        st_ref[0, 0, 1] = sb.T.reshape(4, 64, 128)


def ssd_mixer(PF, PB, cw, dtb, alog, e64, e128, dsk, ng, *, B, L, row0, layer, state=None, prev=None,
              st_prev=None):
    has_init = state is not None
    chain_state = st_prev is not None
    xb = B_SSM_XBC
    in_specs = [
        pl.BlockSpec((L, 256), lambda b, g: (row0 + b, xb // 256 + g)),
        pl.BlockSpec((L, 128), lambda b, g: (row0 + b, (xb + 512) // 128 + g)),
        pl.BlockSpec((L, 128), lambda b, g: (row0 + b, (xb + 768) // 128 + g)),
        pl.BlockSpec((L, 256), lambda b, g: (row0 + b, B_SSM_Z // 256 + g)),
        pl.BlockSpec((L, 128), lambda b, g: (row0 + b, F_SMALL // 128)),
        pl.BlockSpec((1, 1, 4, 512), lambda b, g: (layer, g, 0, 0)),
        pl.BlockSpec((1, 1, 128), lambda b, g: (layer, 0, 0)),
        pl.BlockSpec((1, 1, 128), lambda b, g: (layer, 0, 0)),
        pl.BlockSpec((1, 2, 128, 256), lambda b, g: (g, 0, 0, 0)),
        pl.BlockSpec((1, 2, 128, 512), lambda b, g: (g, 0, 0, 0)),
        pl.BlockSpec((1, 1, 1, 256), lambda b, g: (layer, g, 0, 0)),
        pl.BlockSpec((1, 1, 1, 256), lambda b, g: (layer, g, 0, 0)),
    ]
    args = [PB, PB, PB, PB, PF, cw, dtb, alog, e64, e128, dsk, ng]
    aliases = {}
    out_specs = [pl.BlockSpec((L, 256), lambda b, g: (row0 + b, g))]
    out_shape = [jax.ShapeDtypeStruct((N_TOK, MIX_W), f32)]
    if has_init:
        in_specs.append(pl.BlockSpec((1, 1, 2, 4, 64, 128), lambda b, g: (b, layer, 0, g, 0, 0)))
        args.append(state)
        in_specs.append(pl.BlockSpec(memory_space=pl.ANY))
        args.append(prev)
        aliases = {len(args) - 1: 0}
    else:
        if chain_state:
            in_specs.append(pl.BlockSpec(memory_space=pl.ANY))
            args.append(st_prev)
            aliases = {len(args) - 1: 1}
        out_specs.append(pl.BlockSpec((1, 1, 2, 4, 64, 128), lambda b, g: (b, layer, 0, g, 0, 0)))
        out_shape.append(jax.ShapeDtypeStruct((B, DEPTH, 2, SSM_HEADS, SSM_HEADDIM, SSM_STATE), f32))
    return pl.pallas_call(
        functools.partial(_ssd_body, L=L, has_init=has_init, chain_state=chain_state),
        grid=(B, SSM_GROUPS),
        in_specs=in_specs,
        out_specs=out_specs,
        out_shape=out_shape,
        scratch_shapes=[pltpu.VMEM((L, 512), f32)],
        input_output_aliases=aliases,
        compiler_params=_cparams(("parallel", "parallel")),
        name="ssd_mixer",
    )(*args)


def _s5_body(*refs, n, n_c, s_b, has_init, chain_state):
    it = iter(refs)
    u_ref, wt_ref, wsf_ref, wsb_ref, wof_ref, wob_ref, lam_ref, dsk_ref = [next(it) for _ in range(8)]
    if has_init:
        h0_ref = next(it)
        next(it)
    elif chain_state:
        next(it)
    o_ref = next(it)
    if not has_init:
        hs_ref = next(it)
    zf_s, zb_s, hf_s, hb_s, acc_s, za_s, zc_s = [next(it) for _ in range(7)]
    T = S5_CHUNK

    for j in range(T):
        ub = u_ref[pl.ds(j, n, stride=T), :].astype(bf16)
        a = _dot(ub, wt_ref[0, 0, :, (T - 1 - j) * 128:(2 * T - 1 - j) * 128])
        f = _dot(ub, wsf_ref[0, 0, 0, j])
        b = _dot(ub, wsb_ref[0, 0, 0, j])
        if j == 0:
            acc_s[...] = a
            za_s[...] = f
            zc_s[...] = b
        else:
            acc_s[...] += a
            za_s[...] += f
            zc_s[...] += b
    for kk in range(8):
        zf_s[kk] = za_s[:, kk * 128:(kk + 1) * 128]
        zb_s[kk] = zc_s[:, kk * 128:(kk + 1) * 128]

    lam = lam_ref[0, 0]

    def rows(c):
        return pl.ds(c, s_b, stride=n_c) if s_b > 1 else pl.ds(c, 1)

    def scan_step(c, h, z_s, h_s, la, lb):
        new = []
        for kk in range(8):
            h_s[kk, rows(c), :] = h[kk]
            sl = slice(kk * 128, (kk + 1) * 128)
            new.append(la[:, sl] * h[kk] + lb[:, sl] * pltpu.roll(h[kk], 64, axis=1) + z_s[kk, rows(c), :])
        return tuple(new)

    def fstep(c, h):
        return scan_step(c, h, zf_s, hf_s, lam[0:1], lam[1:2])

    def bstep(t, h):
        return scan_step(n_c - 1 - t, h, zb_s, hb_s, lam[2:3], lam[3:4])

    if has_init:
        h0f = tuple(h0_ref[:, 0, 0, kk * 128:(kk + 1) * 128] for kk in range(8))
        h0b = tuple(h0_ref[:, 0, 1, kk * 128:(kk + 1) * 128] for kk in range(8))
    else:
        h0f = h0b = tuple(jnp.zeros((s_b, 128), f32) for _ in range(8))
    hf = lax.fori_loop(0, n_c, fstep, h0f)
    hb = lax.fori_loop(0, n_c, bstep, h0b)
    if not has_init:
        for kk in range(8):
            hs_ref[:, 0, 0, kk * 128:(kk + 1) * 128] = hf[kk]
            hs_ref[:, 0, 1, kk * 128:(kk + 1) * 128] = hb[kk]

    hin_f = jnp.concatenate([hf_s[kk] for kk in range(8)], axis=1).astype(bf16)
    hin_b = jnp.concatenate([hb_s[kk] for kk in range(8)], axis=1).astype(bf16)
    acc_s[...] += _dot(hin_f, wof_ref[0, 0, 0])
    acc_s[...] += _dot(hin_b, wob_ref[0, 0, 0])
    dsk = dsk_ref[0, 0]
    for i in range(T):
        o_ref[pl.ds(i, n, stride=T), :] = (acc_s[:, i * 128:(i + 1) * 128]
                                           + dsk * u_ref[pl.ds(i, n, stride=T), :])


def s5_mixer(PF, w, dsk, *, B, L, s_b, row0, layer, state=None, prev=None, st_prev=None):
    has_init = state is not None
    chain_state = st_prev is not None
    n_c = L // S5_CHUNK
    n = s_b * n_c
    rows = s_b * L
    one = pl.Buffered(1)
    in_specs = [
        pl.BlockSpec((rows, 128), lambda lb, sb: (row0 + sb, F_S5_U // 128 + lb)),
        pl.BlockSpec((1, 1, 128, 31 * 128), lambda lb, sb: (layer, lb, 0, 0), pipeline_mode=one),
        pl.BlockSpec((1, 1, 1, 16, 128, 1024), lambda lb, sb: (layer, 0, lb, 0, 0, 0), pipeline_mode=one),
        pl.BlockSpec((1, 1, 1, 16, 128, 1024), lambda lb, sb: (layer, 1, lb, 0, 0, 0), pipeline_mode=one),
        pl.BlockSpec((1, 1, 1, 1024, 2048), lambda lb, sb: (layer, 0, lb, 0, 0), pipeline_mode=one),
        pl.BlockSpec((1, 1, 1, 1024, 2048), lambda lb, sb: (layer, 1, lb, 0, 0), pipeline_mode=one),
        pl.BlockSpec((1, 1, 4, 1024), lambda lb, sb: (layer, lb, 0, 0)),
        pl.BlockSpec((1, 1, 1, 128), lambda lb, sb: (layer, lb, 0, 0)),
    ]
    args = [PF, w["toep"], w["ws"], w["ws"], w["wo"], w["wo"], w["lam16"], dsk]
    aliases = {}
    out_specs = [pl.BlockSpec((rows, 128), lambda lb, sb: (row0 + sb, lb))]
    out_shape = [jax.ShapeDtypeStruct((N_TOK, MIX_W), f32)]
    if has_init:
        in_specs.append(pl.BlockSpec((s_b, 1, 2, 1024), lambda lb, sb: (sb, layer, 0, lb)))
        args.append(state)
        in_specs.append(pl.BlockSpec(memory_space=pl.ANY))
        args.append(prev)
        aliases = {len(args) - 1: 0}
    else:
        if chain_state:
            in_specs.append(pl.BlockSpec(memory_space=pl.ANY))
            args.append(st_prev)
            aliases = {len(args) - 1: 1}
        out_specs.append(pl.BlockSpec((s_b, 1, 2, 1024), lambda lb, sb: (sb, layer, 0, lb)))
        out_shape.append(jax.ShapeDtypeStruct((B, DEPTH, 2, 2 * S5_GROUPS * S5_STATE), f32))
    return pl.pallas_call(
        functools.partial(_s5_body, n=n, n_c=n_c, s_b=s_b, has_init=has_init, chain_state=chain_state),
        grid=(4, B // s_b),
        in_specs=in_specs,
        out_specs=out_specs,
        out_shape=out_shape,
        scratch_shapes=([pltpu.VMEM((8, n, 128), f32)] * 4
                        + [pltpu.VMEM((n, 2048), f32), pltpu.VMEM((n, 1024), f32), pltpu.VMEM((n, 1024), f32)]),
        input_output_aliases=aliases,
        compiler_params=_cparams(("parallel", "parallel")),
        name="s5_mixer",
    )(*args)


def _s5_layer_weights(a_re, a_im, log_dt, b_re, b_im, c_re, c_im):
    hp = lax.Precision.HIGHEST
    T = S5_CHUNK
    dt = jnp.exp(log_dt)[..., None]
    lmag, ang = a_re * dt, a_im * dt
    taus = jnp.arange(T + 1, dtype=f32)[:, None, None, None]
    mag = jnp.exp(lmag[None] * taus)
    pr = mag * jnp.cos(ang[None] * taus)
    pi = mag * jnp.sin(ang[None] * taus)
    lam_r, lam_i = pr[1], pi[1]
    den = a_re * a_re + a_im * a_im
    zr = ((lam_r - 1.0) * a_re + lam_i * a_im) / den
    zi = (lam_i * a_re - (lam_r - 1.0) * a_im) / den
    bb_r = zr[..., None] * b_re[None] - zi[..., None] * b_im[None]
    bb_i = zr[..., None] * b_im[None] + zi[..., None] * b_re[None]
    lb_r = pr[..., None] * bb_r[None] - pi[..., None] * bb_i[None]
    lb_i = pr[..., None] * bb_i[None] + pi[..., None] * bb_r[None]
    pr, pi, lb_r, lb_i = lax.optimization_barrier((pr, pi, lb_r, lb_i))
    kk = jnp.sum(c_re[None, None, :, :, :, None] * lb_r[:, :, :, None, :, :]
                 - c_im[None, None, :, :, :, None] * lb_i[:, :, :, None, :, :], axis=4)
    eye8 = jnp.eye(8, dtype=f32)

    def blockdiag(m):
        lead = m.shape[:-3]
        a, b = m.shape[-2], m.shape[-1]
        m = m.reshape(lead + (4, 8, a, b))
        out = m[..., :, :, :, None, :] * eye8[:, None, :, None]
        return out.reshape(lead + (4, 8 * a, 8 * b))

    kt = jnp.swapaxes(kk, -1, -2)
    down = np.arange(T - 1, 0, -1)
    deltas = jnp.concatenate([kt[down, 1], kt[0:1, 0] + kt[0:1, 1], kt[1:T, 0]], axis=0)
    toep = blockdiag(deltas)
    toep = jnp.transpose(toep, (1, 2, 0, 3)).reshape(4, 128, 31 * 128)

    def state_in(lbr, lbi):
        return jnp.concatenate([jnp.swapaxes(lbr, -1, -2), jnp.swapaxes(lbi, -1, -2)], axis=-1)

    rev16 = np.arange(T - 1, -1, -1)
    ws_c = jnp.stack([state_in(lb_r[rev16, 0], lb_i[rev16, 0]),
                      state_in(lb_r[:T, 1], lb_i[:T, 1])], axis=0)

    def state_out(prd, pid):
        cr = c_re[None] * prd[:, :, None, :] - c_im[None] * pid[:, :, None, :]
        ci = c_re[None] * pid[:, :, None, :] + c_im[None] * prd[:, :, None, :]
        return jnp.concatenate([cr, -ci], axis=-1)

    down16 = np.arange(T, 0, -1)
    wo_c = jnp.stack([state_out(pr[1:T + 1, 0], pi[1:T + 1, 0]),
                      state_out(pr[down16, 1], pi[down16, 1])], axis=0)
    la = jnp.concatenate([pr[T], pr[T]], axis=-1)
    lb = jnp.concatenate([-pi[T], pi[T]], axis=-1)
    lam16 = jnp.stack([la[0], lb[0], la[1], lb[1]], axis=0)
    lam16 = jnp.transpose(lam16.reshape(4, 4, 1024), (1, 0, 2))
    return dict(toep=toep.astype(bf16), ws_c=ws_c, wo_c=wo_c, lam16=lam16)


def _s5_expand_in_body(c_ref, o_ref):
    o_ref[...] = jnp.zeros(o_ref.shape, o_ref.dtype)
    for j in range(S5_CHUNK):
        for g in range(8):
            o_ref[0, 0, 0, j, g * 16:(g + 1) * 16, g * 128:(g + 1) * 128] = c_ref[0, 0, j, g].astype(bf16)


def _s5_expand_out_body(c_ref, e_ref, o_ref):
    for i in range(S5_CHUNK):
        for g in range(8):
            tile = _dot_tn(c_ref[0, 0, i, g].astype(bf16), e_ref[g])
            o_ref[0, 0, 0, g * 128:(g + 1) * 128, i * 128:(i + 1) * 128] = tile.astype(bf16)


def s5_expand_weights(ws_c, wo_c):
    c_spec = pl.BlockSpec((1, 1, S5_CHUNK, 8, S5_CH, 128), lambda l, d, lb: (l, d, 0, lb, 0, 0))
    ws = pl.pallas_call(
        _s5_expand_in_body,
        grid=(DEPTH, 2, 4),
        in_specs=[c_spec],
        out_specs=pl.BlockSpec((1, 1, 1, S5_CHUNK, 128, 1024), lambda l, d, lb: (l, d, lb, 0, 0, 0)),
        out_shape=jax.ShapeDtypeStruct((DEPTH, 2, 4, S5_CHUNK, 128, 1024), bf16),
        compiler_params=_cparams(("parallel", "parallel", "parallel")),
        name="s5_expand_in",
    )(ws_c)
    place = np.zeros((8, S5_CH, 128), np.float32)
    for g in range(8):
        place[g, np.arange(S5_CH), g * S5_CH + np.arange(S5_CH)] = 1.0
    wo = pl.pallas_call(
        _s5_expand_out_body,
        grid=(DEPTH, 2, 4),
        in_specs=[c_spec, pl.BlockSpec((8, S5_CH, 128), lambda l, d, lb: (0, 0, 0))],
        out_specs=pl.BlockSpec((1, 1, 1, 1024, 2048), lambda l, d, lb: (l, d, lb, 0, 0)),
        out_shape=jax.ShapeDtypeStruct((DEPTH, 2, 4, 1024, 2048), bf16),
        compiler_params=_cparams(("parallel", "parallel", "parallel")),
        name="s5_expand_out",
    )(wo_c, jnp.asarray(place, bf16))
    return ws, wo


def _glu_body(y_ref, w_ref, b_ref, o_ref):
    g = jax.nn.gelu(y_ref[...])
    o_ref[...] = g * jax.nn.sigmoid(_dot(g.astype(bf16), w_ref[0]) + b_ref[0])


def s5_glu(y, w, b, layer, tm=512):
    return pl.pallas_call(
        _glu_body,
        grid=(N_TOK // tm,),
        in_specs=[
            pl.BlockSpec((tm, MIX_W), lambda i: (i, 0)),
            pl.BlockSpec((1, MIX_W, MIX_W), lambda i: (layer, 0, 0)),
            pl.BlockSpec((1, 1, MIX_W), lambda i: (layer, 0, 0)),
        ],
        out_specs=pl.BlockSpec((tm, MIX_W), lambda i: (i, 0)),
        out_shape=jax.ShapeDtypeStruct((N_TOK, MIX_W), f32),
        compiler_params=_cparams(("parallel",)),
        name="s5_glu",
    )(y, w, b)


def _merge_body(gp_ref, b0_ref, b1_ref, b2_ref, b3_ref, wb_ref, wo_ref, x_ref, g1_ref, o_ref, acc_s):
    b = pl.program_id(1)
    for bb, br_ref in enumerate((b0_ref, b1_ref, b2_ref, b3_ref)):
        @pl.when(b == bb)
        def _(br_ref=br_ref, bb=bb):
            up = _dot(br_ref[...].astype(bf16), wb_ref[0, 0])
            t = jax.nn.sigmoid(gp_ref[...].astype(f32)) * up
            if bb == 0:
                acc_s[...] = t
            else:
                acc_s[...] += t

    @pl.when(b == 3)
    def _():
        mix = _dot(acc_s[...].astype(bf16), wo_ref[0])
        o_ref[...] = x_ref[...] + g1_ref[0, 0] * mix


def merge_branches(PB, branches, w_branch, w_out, x, mod, layer, tm=512):
    br_spec = pl.BlockSpec((tm, MIX_W), lambda i, b: (i, 0))
    return pl.pallas_call(
        _merge_body,
        grid=(N_TOK // tm, 4),
        in_specs=[
            pl.BlockSpec((tm, D), lambda i, b: (i, B_GATE // D + b)),
            br_spec, br_spec, br_spec, br_spec,
            pl.BlockSpec((1, 1, MIX_W, D), lambda i, b: (layer, b, 0, 0)),
            pl.BlockSpec((1, D, D), lambda i, b: (layer, 0, 0), pipeline_mode=pl.Buffered(1)),
            pl.BlockSpec((tm, D), lambda i, b: (i, 0)),
            _mod_spec(layer, 2, tm),
        ],
        out_specs=pl.BlockSpec((tm, D), lambda i, b: (i, 0)),
        out_shape=jax.ShapeDtypeStruct((N_TOK, D), f32),
        scratch_shapes=[pltpu.VMEM((tm, D), f32)],
        compiler_params=_cparams(("parallel", "arbitrary")),
        name="merge_branches",
    )(PB, *branches, w_branch, w_out, x, mod)


def _ffn_body(*refs, n_e, steps_per_e, final_norm, n_ctx_tiles):
    it = iter(refs)
    x_ref, sh_ref, sc_ref, gt_ref, g_ref, w1_ref, w3_ref, w2_ref = [next(it) for _ in range(8)]
    if n_e > 1:
        rwh_ref, rwl_ref, rb_ref = [next(it) for _ in range(3)]
    fg_ref = next(it) if final_norm else None
    o_ref = next(it)
    o2_ref = next(it) if final_norm else None
    h_s = next(it)
    acc_s = next(it) if final_norm else o_ref
    gate_s = next(it) if n_e > 1 else None
    i = pl.program_id(0)
    j = pl.program_id(1)
    nj = pl.num_programs(1)

    @pl.when(j == 0)
    def _():
        h = _modnorm(x_ref[...], g_ref[0], sc_ref[0, 0], sh_ref[0, 0])
        h_s[...] = h.astype(bf16)
        if n_e > 1:
            logits = _dot_hi(h, rwh_ref[0], rwl_ref[0]) + rb_ref[0]
            lane = lax.broadcasted_iota(jnp.int32, logits.shape, 1)
            neg = jnp.float32(-jnp.inf)
            lv = jnp.where(lane < n_e, logits, neg)
            m1 = jnp.max(lv, axis=1, keepdims=True)
            i1 = jnp.min(jnp.where(lv == m1, lane, 128), axis=1, keepdims=True)
            lv2 = jnp.where(lane == i1, neg, lv)
            m2 = jnp.max(lv2, axis=1, keepdims=True)
            i2 = jnp.min(jnp.where(lv2 == m2, lane, 128), axis=1, keepdims=True)
            e2 = jnp.exp(m2 - m1)
            p1 = 1.0 / (1.0 + e2)
            gate_s[...] = jnp.where(lane == i1, p1, 0.0) + jnp.where(lane == i2, e2 * p1, 0.0)

    hb = h_s[...]
    hid = _silu(_dot(hb, w1_ref[0, 0])) * _dot(hb, w3_ref[0, 0])
    if n_e > 1:
        e = j // steps_per_e
        lane = lax.broadcasted_iota(jnp.int32, gate_s.shape, 1)
        hid = hid * jnp.sum(jnp.where(lane == e, gate_s[...], 0.0), axis=1, keepdims=True)
    hidb = hid.astype(bf16)
    cb = 512

    @pl.when(j == 0)
    def _():
        for c0 in range(0, D, cb):
            acc_s[:, c0:c0 + cb] = _dot(hidb, w2_ref[0, 0, :, c0:c0 + cb])

    @pl.when(j > 0)
    def _():
        for c0 in range(0, D, cb):
            acc_s[:, c0:c0 + cb] += _dot(hidb, w2_ref[0, 0, :, c0:c0 + cb])

    @pl.when(j == nj - 1)
    def _():
        y = x_ref[...] + gt_ref[0, 0] * acc_s[...]
        if final_norm:
            ms = jnp.mean(y * y, axis=-1, keepdims=True)
            y = y * lax.rsqrt(ms + EPS) * fg_ref[...]

            @pl.when(i < n_ctx_tiles)
            def _():
                o_ref[...] = y

            @pl.when(i >= n_ctx_tiles)
            def _():
                o2_ref[...] = y
        else:
            o_ref[...] = y


def ffn_layer(x, mod, g, w1, w3, w2, layer, widx, router=None, final_g=None, tm=1024, tf=512):
    n_e, f = w1.shape[1], w1.shape[3]
    spe = f // tf
    final = final_g is not None
    nct = N_CTX // tm
    in_specs = [
        pl.BlockSpec((tm, D), lambda i, j: (i, 0), pipeline_mode=pl.Buffered(1)),
        _mod_spec(layer, 3, tm),
        _mod_spec(layer, 4, tm),
        _mod_spec(layer, 5, tm),
        pl.BlockSpec((1, 1, D), lambda i, j: (layer, 0, 0)),
        pl.BlockSpec((1, 1, D, tf), lambda i, j: (widx, j // spe, 0, j % spe)),
        pl.BlockSpec((1, 1, D, tf), lambda i, j: (widx, j // spe, 0, j % spe)),
        pl.BlockSpec((1, 1, tf, D), lambda i, j: (widx, j // spe, j % spe, 0)),
    ]
    args = [x, mod, mod, mod, g, w1, w3, w2]
    scratch = [pltpu.VMEM((tm, D), bf16)]
    if final:
        scratch.append(pltpu.VMEM((tm, D), f32))
    if n_e > 1:
        in_specs += [pl.BlockSpec((1, D, 128), lambda i, j: (widx, 0, 0)),
                     pl.BlockSpec((1, D, 128), lambda i, j: (widx, 0, 0)),
                     pl.BlockSpec((1, 1, 128), lambda i, j: (widx, 0, 0))]
        args += list(router)
        scratch.append(pltpu.VMEM((tm, 128), f32))
    if final:
        in_specs.append(pl.BlockSpec((1, D), lambda i, j: (0, 0)))
        args.append(final_g.reshape(1, D))
        out_specs = [pl.BlockSpec((tm, D), lambda i, j: (jnp.minimum(i, nct - 1), 0)),
                     pl.BlockSpec((tm, D), lambda i, j: (jnp.maximum(i - nct, 0), 0))]
        out_shape = [jax.ShapeDtypeStruct((N_CTX, D), f32), jax.ShapeDtypeStruct((N_LAT, D), f32)]
    else:
        out_specs = pl.BlockSpec((tm, D), lambda i, j: (i, 0))
        out_shape = jax.ShapeDtypeStruct((N_TOK, D), f32)
    return pl.pallas_call(
        functools.partial(_ffn_body, n_e=n_e, steps_per_e=spe, final_norm=final, n_ctx_tiles=nct),
        grid=(N_TOK // tm, n_e * spe),
        in_specs=in_specs,
        out_specs=out_specs,
        out_shape=out_shape,
        scratch_shapes=scratch,
        compiler_params=_cparams(("arbitrary" if final else "parallel", "arbitrary")),
        name="ffn_moe" if n_e > 1 else "ffn_dense",
    )(*args)


def _split_hi_lo(w):
    hi = w.astype(bf16)
    lo = (w - hi.astype(f32)).astype(bf16)
    return hi, lo


_W_IN_SEGMENTS = ((2080, 3104), (5680, 6192), (1536, 1568), (5664, 5680), None,
                  (6192, 14384), (0, 1536), (1568, 2080), (3104, 5664))
_W_IN_COLS = 14384


def _reorder_body(w_ref, o_ref):
    c = 0
    for seg in _W_IN_SEGMENTS:
        if seg is None:
            width = N_PF - F_SMALL - 48
            o_ref[0, :, c:c + width] = jnp.zeros((o_ref.shape[1], width), bf16)
        else:
            width = seg[1] - seg[0]
            o_ref[0, :, c:c + width] = w_ref[0, :, seg[0]:seg[1]].astype(bf16)
        c += width


def _reorder_w_in(w, rows=128):
    return pl.pallas_call(
        _reorder_body,
        grid=(DEPTH, D // rows),
        in_specs=[pl.BlockSpec((1, rows, _W_IN_COLS), lambda l, r: (l, r, 0))],
        out_specs=pl.BlockSpec((1, rows, N_PF + N_PB), lambda l, r: (l, r, 0)),
        out_shape=jax.ShapeDtypeStruct((DEPTH, D, N_PF + N_PB), bf16),
        compiler_params=_cparams(("parallel", "parallel")),
        name="reorder_w_in",
    )(w)


def _gla_gate_weights(wa2, ba):
    t = wa2.reshape(DEPTH, 2, GLA_RANK, 2, 128)
    w = jnp.einsum("ldrhc,de->lhdrec", t, jnp.eye(2, dtype=f32))
    w = w.reshape(DEPTH, 2, 2 * GLA_RANK, 256)
    w = jnp.pad(w, ((0, 0), (0, 0), (SM_LR, 128 - SM_LR - 2 * GLA_RANK), (0, 0)))
    b = jnp.transpose(ba.reshape(DEPTH, 2, 2, 128), (0, 2, 1, 3)).reshape(DEPTH, 2, 1, 256)
    hi, lo = _split_hi_lo(w)
    return hi, lo, b


def _hgrn_bounds(logits):
    p = jax.nn.softmax(logits.astype(f32), axis=1)
    lower = jnp.maximum(jnp.cumsum(p, axis=1) - p[:, :1], 0.0)
    lb = jnp.transpose(lower, (1, 0, 2)).reshape(DEPTH, 2, HGRN_HEADS, HGRN_DIM)
    rows = jnp.stack([lb, 1.0 - lb], axis=3)
    rows = jnp.transpose(rows, (0, 2, 1, 3, 4)).reshape(DEPTH, HGRN_HEADS, 4, HGRN_DIM)
    return jnp.pad(rows, ((0, 0), (0, 0), (0, 4), (0, 0)))


def _ssd_select_mats():
    e64 = np.zeros((2, 2, 128, 256), np.float32)
    e128 = np.zeros((2, 2, 128, 512), np.float32)
    for g in range(2):
        for d in range(2):
            for h in range(4):
                lane = SM_DT + d * 8 + g * 4 + h
                e64[g, d, lane, h * 64:(h + 1) * 64] = 1.0
                e128[g, d, lane, h * 128:(h + 1) * 128] = 1.0
    return jnp.asarray(e64, bf16), jnp.asarray(e128, bf16)


def _ssd_params(conv_w, conv_b, dt_bias, a_log, d_skip, norm_g):
    cwb = jnp.concatenate([conv_w, conv_b[:, None]], axis=1)
    cw = jnp.concatenate([cwb[..., 0:512].reshape(DEPTH, 4, 2, 256),
                          cwb[..., 512:768].reshape(DEPTH, 4, 2, 128),
                          cwb[..., 768:1024].reshape(DEPTH, 4, 2, 128)], axis=-1)
    cw = jnp.transpose(cw, (0, 2, 1, 3))
    lanes = ((0, 0), (SM_DT, 128 - SM_DT - 16))
    dtb = jnp.pad(dt_bias.reshape(DEPTH, 16), lanes).reshape(DEPTH, 1, 128)
    alog = jnp.pad(a_log.reshape(DEPTH, 16), lanes).reshape(DEPTH, 1, 128)
    dsk = jnp.repeat(d_skip, SSM_HEADDIM, axis=1).reshape(DEPTH, 2, 1, 256)
    return cw, dtb, alog, dsk, norm_g.reshape(DEPTH, 2, 1, 256)


def kernel(x_prompt, x_sample, c, c_ctx, state_gla, state_hgrn, state_ssm, state_s5_re, state_s5_im, norm1_g, norm2_g, ada_w, ada_b, w_in, gla_wa2, gla_ba, gla_norm_g, hgrn_lb_logits, hgrn_norm_g, ssm_conv_w, ssm_conv_b, ssm_a_log, ssm_dt_bias, ssm_d, ssm_norm_g, s5_a_re, s5_a_im, s5_log_dt, s5_b_re, s5_b_im, s5_c_re, s5_c_im, s5_d, s5_glu_w, s5_glu_b, w_branch, w_out, ffn_w1, ffn_w3, ffn_w2, router_w, router_b, moe_w1, moe_w3, moe_w2, final_norm_g):
    w_all = _reorder_w_in(w_in)
    wah, wal, gba = _gla_gate_weights(gla_wa2, gla_ba)
    gng = gla_norm_g.reshape(DEPTH, 1, GLA_DV)
    lbp = _hgrn_bounds(hgrn_lb_logits)
    hng = hgrn_norm_g.reshape(DEPTH, 1, HGRN_DIM)
    e64, e128 = _ssd_select_mats()
    cw, dtb, alog, dsk, sng = _ssd_params(ssm_conv_w, ssm_conv_b, ssm_dt_bias, ssm_a_log, ssm_d, ssm_norm_g)
    w5 = jax.vmap(_s5_layer_weights)(s5_a_re, s5_a_im, s5_log_dt, s5_b_re, s5_b_im, s5_c_re, s5_c_im)
    w5["ws"], w5["wo"] = s5_expand_weights(w5["ws_c"], w5["wo_c"])
    d5 = s5_d.reshape(DEPTH, 4, 1, 128)
    glu_w = s5_glu_w.astype(bf16)
    glu_b = s5_glu_b.reshape(DEPTH, 1, MIX_W)
    wbr = w_branch.astype(bf16)
    wout = w_out.astype(bf16)
    n1 = norm1_g.reshape(DEPTH, 1, D)
    n2 = norm2_g.reshape(DEPTH, 1, D)
    fw1, fw3, fw2 = ffn_w1.astype(bf16)[:, None], ffn_w3.astype(bf16)[:, None], ffn_w2.astype(bf16)[:, None]
    mw1, mw3, mw2 = moe_w1.astype(bf16), moe_w3.astype(bf16), moe_w2.astype(bf16)
    rw = jnp.pad(router_w, ((0, 0), (0, 0), (0, 128 - N_EXPERTS)))
    router = _split_hi_lo(rw) + (jnp.pad(router_b, ((0, 0), (0, 128 - N_EXPERTS))).reshape(-1, 1, 128),)
    s5_state = jnp.concatenate([state_s5_re, state_s5_im], axis=-1).reshape(
        DEC_BATCH, DEPTH, 2, 2 * S5_GROUPS * S5_STATE)

    x = assemble_tokens(x_prompt.reshape(N_CTX, D), x_sample.reshape(N_LAT, D), _pos_table())
    cond8 = jnp.concatenate([c_ctx[None], c, jnp.zeros((MOD_ROWS - 1 - DEC_BATCH, D), f32)], axis=0)
    mod = ada_modulation(cond8, ada_w, ada_b).reshape(DEPTH, MOD_ROWS * 6, 1, D)

    lat0 = N_CTX // DEC_SEQ
    ctx = dict(B=BATCH, L=SEQ, row0=0)
    lat = dict(B=DEC_BATCH, L=DEC_SEQ, row0=lat0)
    st_gla = st_hg = st_ssm = st_s5 = None
    for l in range(DEPTH):
        PF, PB = in_projection(x, mod, n1, w_all, l)

        o_gla, st_gla = gla_mixer(PF, PB, wah, wal, gba, gng, layer=l, st_prev=st_gla, **ctx)
        (o_gla,) = gla_mixer(PF, PB, wah, wal, gba, gng, layer=l, state=state_gla, prev=o_gla, **lat)

        o_hg, st_hg = hgrn_mixer(PF, PB, lbp, hng, layer=l, st_prev=st_hg, **ctx)
        (o_hg,) = hgrn_mixer(PF, PB, lbp, hng, layer=l, state=state_hgrn, prev=o_hg, **lat)

        o_ssm, st_ssm = ssd_mixer(PF, PB, cw, dtb, alog, e64, e128, dsk, sng, layer=l, st_prev=st_ssm, **ctx)
        (o_ssm,) = ssd_mixer(PF, PB, cw, dtb, alog, e64, e128, dsk, sng, layer=l, state=state_ssm,
                             prev=o_ssm, **lat)

        y5, st_s5 = s5_mixer(PF, w5, d5, B=BATCH, L=SEQ, s_b=min(8, BATCH), row0=0, layer=l, st_prev=st_s5)
        (y5,) = s5_mixer(PF, w5, d5, B=DEC_BATCH, L=DEC_SEQ, s_b=1, row0=lat0, layer=l,
                         state=s5_state, prev=y5)
        o_s5 = s5_glu(y5, glu_w, glu_b, l)

        x = merge_branches(PB, (o_gla, o_hg, o_ssm, o_s5), wbr, wout, x, mod, l)

        last = l == DEPTH - 1
        kw = dict(final_g=final_norm_g, tm=512) if last else {}
        if l % 2 == 0:
            x = ffn_layer(x, mod, n2, fw1, fw3, fw2, l, l // 2, **kw)
        else:
            x = ffn_layer(x, mod, n2, mw1, mw3, mw2, l, l // 2, router=router, **kw)

    y_ctx, y_lat = x
    st_s5 = st_s5.reshape(BATCH, DEPTH, 2, S5_GROUPS, 2 * S5_STATE)
    return (y_ctx.reshape(BATCH, SEQ, D), y_lat.reshape(DEC_BATCH, DEC_SEQ, D),
            st_gla, st_hg, st_ssm, st_s5[..., :S5_STATE], st_s5[..., S5_STATE:])
```

```python
import functools

import numpy as np
import jax
import jax.numpy as jnp
from jax import lax
from jax.experimental import pallas as pl
from jax.experimental.pallas import tpu as pltpu

f32 = jnp.float32
bf16 = jnp.bfloat16

D = 2048
BATCH = 32
SEQ = 256
DEPTH = 4
DEC_BATCH = 4
DEC_SEQ = 4096
GRID_W = 64
MIX_W = 512
GLA_HEADS = 4
GLA_DK = 64
GLA_DV = 128
GLA_RANK = 16
GLA_NORMALIZER = 16.0
HGRN_HEADS = 4
HGRN_DIM = 128
SSM_HEADS = 8
SSM_HEADDIM = 64
SSM_GROUPS = 2
SSM_STATE = 128
S5_CH = 16
S5_STATE = 64
S5_GROUPS = 32
D_FF = 5632
N_EXPERTS = 8
D_FF_EXPERT = 1024
EPS = 1e-6

N_CTX = BATCH * SEQ
N_LAT = DEC_BATCH * DEC_SEQ
N_TOK = N_CTX + N_LAT
MOD_ROWS = 8

F_HG_F, F_S5_U, F_SMALL = 0, 1024, 1536
N_PF = 2048
B_GATE, B_GLA_Q, B_GLA_K, B_GLA_V, B_GLA_R = 0, 8192, 8448, 8704, 9216
B_HG_Q, B_HG_I, B_HG_G, B_SSM_Z, B_SSM_XBC = 9728, 10240, 10752, 11264, 11776
N_PB = 12800
SM_LR, SM_DT = 0, 32

GLS_CHUNK = 128
GLS_SUB = 32
SSD_CHUNK = 128
S5_CHUNK = 16

VMEM_LIMIT = 58 * 1024 * 1024


def _cparams(sem):
    return pltpu.CompilerParams(dimension_semantics=sem, vmem_limit_bytes=VMEM_LIMIT)


def _dot(a, b):
    return jnp.dot(a, b, preferred_element_type=f32)


def _dot_nt(a, b):
    return lax.dot_general(a, b, (((1,), (1,)), ((), ())), preferred_element_type=f32)


def _dot_tn(a, b):
    return lax.dot_general(a, b, (((0,), (0,)), ((), ())), preferred_element_type=f32)


def _split2(x):
    hi = x.astype(bf16)
    return hi, (x - hi.astype(f32)).astype(bf16)


def _dot_sel_lhs(t, x):
    x1, x2 = _split2(x)
    return _dot(t, x1) + _dot(t, x2)


def _dot_sel_rhs(x, e):
    x1, x2 = _split2(x)
    return _dot(x1, e) + _dot(x2, e)


def _dot_sel_rhs3(x, e):
    x1, x2 = _split2(x)
    x3 = (x - x1.astype(f32) - x2.astype(f32)).astype(bf16)
    return _dot(x1, e) + _dot(x2, e) + _dot(x3, e)


def _dot_hi(x, w_hi, w_lo):
    x1, x2 = _split2(x)
    return _dot(x1, w_hi) + _dot(x2, w_hi) + _dot(x1, w_lo)


def _log_sigmoid(x):
    return jnp.minimum(x, 0.0) - jnp.log1p(jnp.exp(-jnp.abs(x)))


def _softplus(x):
    return jnp.maximum(x, 0.0) + jnp.log1p(jnp.exp(-jnp.abs(x)))


def _silu(x):
    return x * jax.nn.sigmoid(x)


def _modnorm(x, g, scale, shift):
    ms = jnp.mean(x * x, axis=-1, keepdims=True)
    y = x * lax.rsqrt(ms + EPS) * g
    return y * (1.0 + scale) + shift


def _tri(n, upper):
    ii = lax.broadcasted_iota(jnp.int32, (n, n), 0)
    jj = lax.broadcasted_iota(jnp.int32, (n, n), 1)
    return (ii <= jj) if upper else (ii >= jj)


def _mod_row(i, tm):
    start = i * tm
    return jnp.where(start < N_CTX, 0, 1 + (start - N_CTX) // DEC_SEQ)


def _mod_spec(layer, k, tm):
    return pl.BlockSpec((1, 1, 1, D), lambda i, j: (layer, _mod_row(i, tm) * 6 + k, 0, 0))


def _assemble_body(xp_ref, xs_ref, pos_ref, o_ref, *, n_ctx_tiles):
    i = pl.program_id(0)

    @pl.when(i < n_ctx_tiles)
    def _():
        o_ref[...] = xp_ref[...]

    @pl.when(i >= n_ctx_tiles)
    def _():
        o_ref[...] = xs_ref[...] + pos_ref[...]


def assemble_tokens(xp2, xs2, pos, tm=512):
    nct = N_CTX // tm
    npos = DEC_SEQ // tm
    return pl.pallas_call(
        functools.partial(_assemble_body, n_ctx_tiles=nct),
        grid=(N_TOK // tm,),
        in_specs=[
            pl.BlockSpec((tm, D), lambda i: (jnp.minimum(i, nct - 1), 0)),
            pl.BlockSpec((tm, D), lambda i: (jnp.maximum(i - nct, 0), 0)),
            pl.BlockSpec((tm, D), lambda i: (jnp.maximum(i - nct, 0) % npos, 0)),
        ],
        out_specs=pl.BlockSpec((tm, D), lambda i: (i, 0)),
        out_shape=jax.ShapeDtypeStruct((N_TOK, D), f32),
        compiler_params=_cparams(("parallel",)),
        name="assemble_tokens",
    )(xp2, xs2, pos)


def _pos_table():
    rows = DEC_SEQ // GRID_W
    row = jnp.repeat(jnp.arange(rows, dtype=f32), GRID_W)
    col = jnp.tile(jnp.arange(GRID_W, dtype=f32), rows)

    def sincos(p, d):
        half = d // 2
        omega = 1.0 / (10000.0 ** (jnp.arange(half, dtype=f32) / half))
        ang = p[:, None] * omega[None, :]
        return jnp.concatenate([jnp.sin(ang), jnp.cos(ang)], axis=-1)

    return jnp.concatenate([sincos(row, D // 2), sincos(col, D // 2)], axis=-1)


def _ada_body(c_ref, w_ref, b_ref, o_ref):
    s = _silu(c_ref[...])
    s1, s2 = _split2(s)
    w = w_ref[0].astype(bf16)
    o_ref[0] = _dot(s1, w) + _dot(s2, w) + b_ref[0]


def ada_modulation(cond8, ada_w, ada_b, tn=1024):
    return pl.pallas_call(
        _ada_body,
        grid=(DEPTH, 6 * D // tn),
        in_specs=[
            pl.BlockSpec((MOD_ROWS, D), lambda l, j: (0, 0)),
            pl.BlockSpec((1, D, tn), lambda l, j: (l, 0, j)),
            pl.BlockSpec((1, 1, tn), lambda l, j: (l, 0, j)),
        ],
        out_specs=pl.BlockSpec((1, MOD_ROWS, tn), lambda l, j: (l, 0, j)),
        out_shape=jax.ShapeDtypeStruct((DEPTH, MOD_ROWS, 6 * D), f32),
        compiler_params=_cparams(("parallel", "parallel")),
        name="ada_modulation",
    )(cond8, ada_w, ada_b.reshape(DEPTH, 1, 6 * D))


def _inproj_body(x_ref, sh_ref, sc_ref, g_ref, w_ref, pf_ref, pb_ref, h_s, *, nf):
    j = pl.program_id(1)

    @pl.when(j == 0)
    def _():
        h_s[...] = _modnorm(x_ref[...], g_ref[0], sc_ref[0, 0], sh_ref[0, 0]).astype(bf16)

    y = _dot(h_s[...], w_ref[0])

    @pl.when(j < nf)
    def _():
        pf_ref[...] = y

    @pl.when(j >= nf)
    def _():
        pb_ref[...] = y.astype(bf16)


def in_projection(x, mod, g, w_all, layer, tm=1024, tn=512):
    nf = N_PF // tn
    nb = N_PB // tn
    return pl.pallas_call(
        functools.partial(_inproj_body, nf=nf),
        grid=(N_TOK // tm, nf + nb),
        in_specs=[
            pl.BlockSpec((tm, D), lambda i, j: (i, 0)),
            _mod_spec(layer, 0, tm),
            _mod_spec(layer, 1, tm),
            pl.BlockSpec((1, 1, D), lambda i, j: (layer, 0, 0)),
            pl.BlockSpec((1, D, tn), lambda i, j: (layer, 0, j)),
        ],
        out_specs=[
            pl.BlockSpec((tm, tn), lambda i, j: (i, jnp.minimum(j, nf - 1))),
            pl.BlockSpec((tm, tn), lambda i, j: (i, jnp.maximum(j - nf, 0))),
        ],
        out_shape=[
            jax.ShapeDtypeStruct((N_TOK, N_PF), f32),
            jax.ShapeDtypeStruct((N_TOK, N_PB), bf16),
        ],
        scratch_shapes=[pltpu.VMEM((tm, D), bf16)],
        compiler_params=_cparams(("parallel", "arbitrary")),
        name="in_projection",
    )(x, mod, mod, g, w_all)


def _gls_dir(q, k, g, d):
    c, sub = GLS_CHUNK, GLS_SUB
    nb = c // sub
    tri = jnp.where(_tri(c, upper=(d == 1)), 1.0, 0.0).astype(bf16)
    b = _dot_sel_lhs(tri, g)
    zero = jnp.zeros((1, 128), f32)
    if d == 0:
        mid = [b[sub * i + sub // 2 - 1:sub * i + sub // 2] for i in range(nb)]
        end = [b[sub * i + sub - 1:sub * i + sub] for i in range(nb)]
        start = [zero] + end[:-1]
        total = end[-1]
    else:
        mid = [b[sub * i + sub // 2:sub * i + sub // 2 + 1] for i in range(nb)]
        end = [b[sub * i:sub * i + 1] for i in range(nb)]
        start = end[1:] + [zero]
        total = end[0]

    def rows(vals):
        return jnp.concatenate([jnp.broadcast_to(v, (sub, 128)) for v in vals], axis=0)

    r_mid = rows(mid)
    qd = q * jnp.exp(b - r_mid)
    kd = k * jnp.exp(r_mid - b)
    qo = qd * rows([jnp.exp(m - s) for m, s in zip(mid, start)])
    ke = kd * rows([jnp.exp(e - m) for m, e in zip(mid, end)])
    q_in = qo * rows([jnp.exp(s) for s in start])
    k_out = ke * rows([jnp.exp(total - e) for e in end])
    kr = []
    for i in range(nb):
        prior = range(i) if d == 0 else range(i + 1, nb)
        if len(prior) == 0:
            kr.append(None)
            continue
        blocks = []
        for jb in range(nb):
            if jb in prior:
                blocks.append(ke[jb * sub:(jb + 1) * sub] * jnp.exp(start[i] - end[jb]))
            else:
                blocks.append(jnp.zeros((sub, 128), f32))
        kr.append(jnp.concatenate(blocks, axis=0).astype(bf16))
    return qd, kd.astype(bf16), qo, kr, q_in, k_out, jnp.exp(total)


def _gls_head(parts, v, st, mask, kmask):
    qd, kd, qo, kr, q_in, k_out, dec = parts
    sub = GLS_SUB
    if kmask is not None:
        qd, qo, q_in, k_out = qd * kmask, qo * kmask, q_in * kmask, k_out * kmask
    att = jnp.where(mask, _dot_nt(qd.astype(bf16), kd), 0.0)
    qob = qo.astype(bf16)
    offs = []
    for i, kri in enumerate(kr):
        if kri is None:
            offs.append(jnp.zeros((sub, GLS_CHUNK), f32))
        else:
            offs.append(_dot_nt(qob[i * sub:(i + 1) * sub], kri))
    att = att + jnp.concatenate(offs, axis=0)
    vb = v.astype(bf16)
    o = _dot(att.astype(bf16), vb) + _dot_nt(q_in.astype(bf16), st.astype(bf16))
    st_new = st * dec + _dot_tn(vb, k_out.astype(bf16))
    return o, st_new


def _gls_masks():
    c, sub = GLS_CHUNK, GLS_SUB
    ii = lax.broadcasted_iota(jnp.int32, (c, c), 0)
    jj = lax.broadcasted_iota(jnp.int32, (c, c), 1)
    same = (ii // sub) == (jj // sub)
    return [same & (ii >= jj), same & (ii <= jj)]


def _head_post(o, gate_in, ng):
    ms = jnp.mean(o * o, axis=-1, keepdims=True)
    return o * lax.rsqrt(ms + EPS) * ng * _silu(gate_in.astype(f32))


def _scan_loops(n, chunk_fn, post_fn, o_ref, init):
    C = GLS_CHUNK

    def store(r0, outs, final):
        for h, o in enumerate(outs):
            sl = (pl.ds(r0, C), slice(h * 128, (h + 1) * 128))
            if final:
                o_ref[sl] = post_fn(r0, h, o_ref[sl] + o)
            else:
                o_ref[sl] = o

    def half(final):
        def body(t, carry):
            sf, sb = carry
            r0, of, sf = chunk_fn(t, 0, sf)
            store(r0, of, final)
            r1, ob, sb = chunk_fn(n - 1 - t, 1, sb)
            store(r1, ob, final)
            return sf, sb
        return body

    unroll = 2 if (n // 2) % 2 == 0 else 1
    carry = lax.fori_loop(0, n // 2, half(False), init, unroll=unroll)
    return lax.fori_loop(n // 2, n, half(True), carry, unroll=unroll)


def _gla_body(*refs, L, has_init, chain_state):
    it = iter(refs)
    q_ref, k_ref, v_ref, r_ref, sm_ref, wah_ref, wal_ref, ba_ref, ng_ref = [next(it) for _ in range(9)]
    s0_ref = next(it) if has_init else None
    if has_init or chain_state:
        next(it)
    o_ref = next(it)
    st_ref = None if has_init else next(it)
    C = GLS_CHUNK
    n = L // C
    lane = lax.broadcasted_iota(jnp.int32, (1, 128), 1)
    kmasks = [jnp.where(lane < 64, 1.0, 0.0), jnp.where(lane >= 64, 1.0, 0.0)]
    masks = _gls_masks()
    wah, wal, ba, ng = wah_ref[0, 0], wal_ref[0, 0], ba_ref[0, 0], ng_ref[0]

    def chunk(cidx, d, sts):
        r0 = pl.multiple_of(cidx * C, C)
        q = q_ref[pl.ds(r0, C), :].astype(f32) * (GLA_DK ** -0.5)
        k = k_ref[pl.ds(r0, C), :].astype(f32)
        sm = sm_ref[pl.ds(r0, C), :]
        sl = slice(d * 128, (d + 1) * 128)
        g = _log_sigmoid(_dot_hi(sm, wah[:, sl], wal[:, sl]) + ba[:, sl]) * (1.0 / GLA_NORMALIZER)
        parts = _gls_dir(q, k, g, d)
        outs, new = [], []
        for h in range(2):
            v = v_ref[pl.ds(r0, C), h * 128:(h + 1) * 128]
            o, s = _gls_head(parts, v, sts[h], masks[d], kmasks[h])
            outs.append(o)
            new.append(s)
        return r0, outs, new

    def post(r0, h, tot):
        return _head_post(tot, r_ref[pl.ds(r0, C), h * 128:(h + 1) * 128], ng)

    def init(d, h):
        if not has_init:
            return jnp.zeros((128, 128), f32)
        s = s0_ref[0, 0, d, h]
        z = jnp.zeros((64, 128), f32)
        full = jnp.concatenate([s, z], axis=0) if h == 0 else jnp.concatenate([z, s], axis=0)
        return full.T

    sf, sb = _scan_loops(n, chunk, post, o_ref, ([init(0, 0), init(0, 1)], [init(1, 0), init(1, 1)]))
    if st_ref is not None:
        for d, sts in enumerate((sf, sb)):
            for h in range(2):
                st_ref[0, 0, d, h] = sts[h].T[h * 64:(h + 1) * 64, :]


def gla_mixer(PF, PB, wa_hi, wa_lo, ba, ng, *, B, L, row0, layer, state=None, prev=None, st_prev=None):
    has_init = state is not None
    chain_state = st_prev is not None
    in_specs = [
        pl.BlockSpec((L, 128), lambda b, hp: (row0 + b, B_GLA_Q // 128 + hp)),
        pl.BlockSpec((L, 128), lambda b, hp: (row0 + b, B_GLA_K // 128 + hp)),
        pl.BlockSpec((L, 256), lambda b, hp: (row0 + b, B_GLA_V // 256 + hp)),
        pl.BlockSpec((L, 256), lambda b, hp: (row0 + b, B_GLA_R // 256 + hp)),
        pl.BlockSpec((L, 128), lambda b, hp: (row0 + b, F_SMALL // 128)),
        pl.BlockSpec((1, 1, 128, 256), lambda b, hp: (layer, hp, 0, 0)),
        pl.BlockSpec((1, 1, 128, 256), lambda b, hp: (layer, hp, 0, 0)),
        pl.BlockSpec((1, 1, 1, 256), lambda b, hp: (layer, hp, 0, 0)),
        pl.BlockSpec((1, 1, 128), lambda b, hp: (layer, 0, 0)),
    ]
    args = [PB, PB, PB, PB, PF, wa_hi, wa_lo, ba, ng]
    aliases = {}
    out_specs = [pl.BlockSpec((L, 256), lambda b, hp: (row0 + b, hp))]
    out_shape = [jax.ShapeDtypeStruct((N_TOK, MIX_W), f32)]
    if has_init:
        in_specs.append(pl.BlockSpec((1, 1, 2, 2, 64, 128), lambda b, hp: (b, layer, 0, hp, 0, 0)))
        args.append(state)
        in_specs.append(pl.BlockSpec(memory_space=pl.ANY))
        args.append(prev)
        aliases = {len(args) - 1: 0}
    else:
        if chain_state:
            in_specs.append(pl.BlockSpec(memory_space=pl.ANY))
            args.append(st_prev)
            aliases = {len(args) - 1: 1}
        out_specs.append(pl.BlockSpec((1, 1, 2, 2, 64, 128), lambda b, hp: (b, layer, 0, hp, 0, 0)))
        out_shape.append(jax.ShapeDtypeStruct((B, DEPTH, 2, GLA_HEADS, GLA_DK, GLA_DV), f32))
    return pl.pallas_call(
        functools.partial(_gla_body, L=L, has_init=has_init, chain_state=chain_state),
        grid=(B, 2),
        in_specs=in_specs,
        out_specs=out_specs,
        out_shape=out_shape,
        input_output_aliases=aliases,
        compiler_params=_cparams(("parallel", "parallel")),
        name="gla_mixer",
    )(*args)


def _hgrn_body(*refs, L, has_init, chain_state):
    it = iter(refs)
    q_ref, f0_ref, f1_ref, i_ref, g_ref, lb_ref, ng_ref = [next(it) for _ in range(7)]
    s0_ref = next(it) if has_init else None
    if has_init or chain_state:
        next(it)
    o_ref = next(it)
    st_ref = None if has_init else next(it)
    C = GLS_CHUNK
    n = L // C
    masks = _gls_masks()
    ng = ng_ref[0]
    f_refs = (f0_ref, f1_ref)

    def chunk(cidx, d, sts):
        r0 = pl.multiple_of(cidx * C, C)
        outs, new = [], []
        for h in range(2):
            sl = slice(h * 128, (h + 1) * 128)
            lbp = lb_ref[0, h]
            q = q_ref[pl.ds(r0, C), sl].astype(f32)
            fp = f_refs[d][pl.ds(r0, C), sl]
            lb, one_m_lb = lbp[2 * d:2 * d + 1], lbp[2 * d + 1:2 * d + 2]
            e = jnp.exp(-jnp.abs(fp))
            inv = 1.0 / (1.0 + e)
            pos = fp >= 0.0
            sig = jnp.where(pos, inv, e * inv)
            k = one_m_lb * jnp.where(pos, e * inv, inv)
            g = jnp.log(lb + one_m_lb * sig)
            parts = _gls_dir(q, k, g, d)
            o, s = _gls_head(parts, i_ref[pl.ds(r0, C), sl], sts[h], masks[d], None)
            outs.append(o)
            new.append(s)
        return r0, outs, new

    def post(r0, h, tot):
        return _head_post(tot, g_ref[pl.ds(r0, C), h * 128:(h + 1) * 128], ng)

    def init(d, h):
        if not has_init:
            return jnp.zeros((128, 128), f32)
        return s0_ref[0, 0, d, h].T

    sf, sb = _scan_loops(n, chunk, post, o_ref, ([init(0, 0), init(0, 1)], [init(1, 0), init(1, 1)]))
    if st_ref is not None:
        for d, sts in enumerate((sf, sb)):
            for h in range(2):
                st_ref[0, 0, d, h] = sts[h].T


def hgrn_mixer(PF, PB, lbp, ng, *, B, L, row0, layer, state=None, prev=None, st_prev=None):
    has_init = state is not None
    chain_state = st_prev is not None
    in_specs = [
        pl.BlockSpec((L, 256), lambda b, hp: (row0 + b, B_HG_Q // 256 + hp)),
        pl.BlockSpec((L, 256), lambda b, hp: (row0 + b, F_HG_F // 256 + hp)),
        pl.BlockSpec((L, 256), lambda b, hp: (row0 + b, F_HG_F // 256 + 2 + hp)),
        pl.BlockSpec((L, 256), lambda b, hp: (row0 + b, B_HG_I // 256 + hp)),
        pl.BlockSpec((L, 256), lambda b, hp: (row0 + b, B_HG_G // 256 + hp)),
        pl.BlockSpec((1, 2, 8, 128), lambda b, hp: (layer, hp, 0, 0)),
        pl.BlockSpec((1, 1, 128), lambda b, hp: (layer, 0, 0)),
    ]
    args = [PB, PF, PF, PB, PB, lbp, ng]
    aliases = {}
    out_specs = [pl.BlockSpec((L, 256), lambda b, hp: (row0 + b, hp))]
    out_shape = [jax.ShapeDtypeStruct((N_TOK, MIX_W), f32)]
    if has_init:
        in_specs.append(pl.BlockSpec((1, 1, 2, 2, 128, 128), lambda b, hp: (b, layer, 0, hp, 0, 0)))
        args.append(state)
        in_specs.append(pl.BlockSpec(memory_space=pl.ANY))
        args.append(prev)
        aliases = {len(args) - 1: 0}
    else:
        if chain_state:
            in_specs.append(pl.BlockSpec(memory_space=pl.ANY))
            args.append(st_prev)
            aliases = {len(args) - 1: 1}
        out_specs.append(pl.BlockSpec((1, 1, 2, 2, 128, 128), lambda b, hp: (b, layer, 0, hp, 0, 0)))
        out_shape.append(jax.ShapeDtypeStruct((B, DEPTH, 2, HGRN_HEADS, HGRN_DIM, HGRN_DIM), f32))
    return pl.pallas_call(
        functools.partial(_hgrn_body, L=L, has_init=has_init, chain_state=chain_state),
        grid=(B, 2),
        in_specs=in_specs,
        out_specs=out_specs,
        out_shape=out_shape,
        input_output_aliases=aliases,
        compiler_params=_cparams(("parallel", "parallel")),
        name="hgrn_mixer",
    )(*args)


def _ssd_body(*refs, L, has_init, chain_state):
    it = iter(refs)
    (xs_ref, bm_ref, cm_ref, z_ref, sm_ref, cw_ref, dtb_ref, alog_ref, e64_ref, e128_ref,
     dsk_ref, ng_ref) = [next(it) for _ in range(12)]
    s0_ref = next(it) if has_init else None
    if has_init or chain_state:
        next(it)
    o_ref = next(it)
    st_ref = None if has_init else next(it)
    xc_s = next(it)
    C = SSD_CHUNK
    n = L // C
    cw = cw_ref[0, 0]
    rowi = lax.broadcasted_iota(jnp.int32, (C, 1), 0)

    def conv_block(rb, _):
        r0 = pl.multiple_of(rb * C, C)
        rp = pl.multiple_of(jnp.maximum(r0 - 16, 0), 16)
        rn = pl.multiple_of(jnp.minimum(r0 + C, L - 16), 16)
        has_p = jnp.where(rb > 0, 1.0, 0.0)
        has_n = jnp.where(rb < n - 1, 1.0, 0.0)
        for ref, c0, w in ((xs_ref, 0, 256), (bm_ref, 256, 128), (cm_ref, 384, 128)):
            x = ref[pl.ds(r0, C), :].astype(f32)
            prev = ref[pl.ds(rp, 16), :].astype(f32)[15:16] * has_p
            nxt = ref[pl.ds(rn, 16), :].astype(f32)[0:1] * has_n
            x_dn = jnp.where(rowi == 0, prev, pltpu.roll(x, 1, axis=0))
            x_up = jnp.where(rowi == C - 1, nxt, pltpu.roll(x, C - 1, axis=0))
            y = (cw[0:1, c0:c0 + w] * x_dn + cw[1:2, c0:c0 + w] * x + cw[2:3, c0:c0 + w] * x_up
                 + cw[3:4, c0:c0 + w])
            xc_s[pl.ds(r0, C), c0:c0 + w] = _silu(y)
        return 0

    lax.fori_loop(0, n, conv_block, 0)

    masks = [_tri(C, upper=False), _tri(C, upper=True)]
    tris = [jnp.where(m, 1.0, 0.0).astype(bf16) for m in masks]
    lane256 = lax.broadcasted_iota(jnp.int32, (1, 256), 1)
    hmask = [jnp.where((lane256 >= h * 64) & (lane256 < (h + 1) * 64), 1.0, 0.0) for h in range(4)]
    dtb = dtb_ref[0]
    a_row = -jnp.exp(alog_ref[0])
    neg_inf = jnp.float32(-jnp.inf)

    def chunk(cidx, d, st):
        r0 = pl.multiple_of(cidx * C, C)
        xs = xc_s[pl.ds(r0, C), 0:256]
        bm = xc_s[pl.ds(r0, C), 256:384].astype(bf16)
        cm = xc_s[pl.ds(r0, C), 384:512].astype(bf16)
        dt_all = _softplus(sm_ref[pl.ds(r0, C), :] + dtb)
        a_all = dt_all * a_row
        dt64 = _dot_sel_rhs(dt_all, e64_ref[0, d])
        acs_all = _dot_sel_lhs(tris[d], a_all)
        acs64 = _dot_sel_rhs3(acs_all, e64_ref[0, d])
        acs128 = _dot_sel_rhs3(acs_all, e128_ref[0, d])
        end = acs64[C - 1:C] if d == 0 else acs64[0:1]
        xdt = xs * dt64
        scores = _dot_nt(cm, bm)
        y = jnp.exp(acs64) * _dot(cm, st.astype(bf16))
        for h in range(4):
            colb = acs128[:, h * 128:(h + 1) * 128]
            diff = jnp.where(masks[d], colb - colb.T, neg_inf)
            p = (scores * jnp.exp(diff)).astype(bf16)
            y = y + _dot(p, (xdt * hmask[h]).astype(bf16))
        xd = (xdt * jnp.exp(end - acs64)).astype(bf16)
        st_new = st * jnp.exp(end) + _dot_tn(bm, xd)
        return r0, y, st_new

    dsk = dsk_ref[0, 0]
    ng = ng_ref[0, 0]

    def post(r0, y):
        xs = xc_s[pl.ds(r0, C), 0:256]
        t = (y + dsk * xs) * _silu(z_ref[pl.ds(r0, C), :].astype(f32))
        ms = jnp.mean(t * t, axis=-1, keepdims=True)
        return t * lax.rsqrt(ms + EPS) * ng

    def first_half(t, carry):
        sf, sb = carry
        r0, yf, sf = chunk(t, 0, sf)
        o_ref[pl.ds(r0, C), :] = yf
        r1, yb, sb = chunk(n - 1 - t, 1, sb)
        o_ref[pl.ds(r1, C), :] = yb
        return sf, sb

    def second_half(t, carry):
        sf, sb = carry
        r0, yf, sf = chunk(t, 0, sf)
        o_ref[pl.ds(r0, C), :] = post(r0, o_ref[pl.ds(r0, C), :] + yf)
        r1, yb, sb = chunk(n - 1 - t, 1, sb)
        o_ref[pl.ds(r1, C), :] = post(r1, o_ref[pl.ds(r1, C), :] + yb)
        return sf, sb

    def init(d):
        if not has_init:
            return jnp.zeros((128, 256), f32)
        return s0_ref[0, 0, d].reshape(256, 128).T

    carry = (init(0), init(1))
    unroll = 2 if (n // 2) % 2 == 0 else 1
    carry = lax.fori_loop(0, n // 2, first_half, carry, unroll=unroll)
    sf, sb = lax.fori_loop(n // 2, n, second_half, carry, unroll=unroll)
    if st_ref is not None:
        st_ref[0, 0, 0] = sf.T.reshape(4, 64, 128)
        st_ref[0, 0, 1] = sb.T.reshape(4, 64, 128)


def ssd_mixer(PF, PB, cw, dtb, alog, e64, e128, dsk, ng, *, B, L, row0, layer, state=None, prev=None,
              st_prev=None):
    has_init = state is not None
    chain_state = st_prev is not None
    xb = B_SSM_XBC
    in_specs = [
        pl.BlockSpec((L, 256), lambda b, g: (row0 + b, xb // 256 + g)),
        pl.BlockSpec((L, 128), lambda b, g: (row0 + b, (xb + 512) // 128 + g)),
        pl.BlockSpec((L, 128), lambda b, g: (row0 + b, (xb + 768) // 128 + g)),
        pl.BlockSpec((L, 256), lambda b, g: (row0 + b, B_SSM_Z // 256 + g)),
        pl.BlockSpec((L, 128), lambda b, g: (row0 + b, F_SMALL // 128)),
        pl.BlockSpec((1, 1, 4, 512), lambda b, g: (layer, g, 0, 0)),
        pl.BlockSpec((1, 1, 128), lambda b, g: (layer, 0, 0)),
        pl.BlockSpec((1, 1, 128), lambda b, g: (layer, 0, 0)),
        pl.BlockSpec((1, 2, 128, 256), lambda b, g: (g, 0, 0, 0)),
        pl.BlockSpec((1, 2, 128, 512), lambda b, g: (g, 0, 0, 0)),
        pl.BlockSpec((1, 1, 1, 256), lambda b, g: (layer, g, 0, 0)),
        pl.BlockSpec((1, 1, 1, 256), lambda b, g: (layer, g, 0, 0)),
    ]
    args = [PB, PB, PB, PB, PF, cw, dtb, alog, e64, e128, dsk, ng]
    aliases = {}
    out_specs = [pl.BlockSpec((L, 256), lambda b, g: (row0 + b, g))]
    out_shape = [jax.ShapeDtypeStruct((N_TOK, MIX_W), f32)]
    if has_init:
        in_specs.append(pl.BlockSpec((1, 1, 2, 4, 64, 128), lambda b, g: (b, layer, 0, g, 0, 0)))
        args.append(state)
        in_specs.append(pl.BlockSpec(memory_space=pl.ANY))
        args.append(prev)
        aliases = {len(args) - 1: 0}
    else:
        if chain_state:
            in_specs.append(pl.BlockSpec(memory_space=pl.ANY))
            args.append(st_prev)
            aliases = {len(args) - 1: 1}
        out_specs.append(pl.BlockSpec((1, 1, 2, 4, 64, 128), lambda b, g: (b, layer, 0, g, 0, 0)))
        out_shape.append(jax.ShapeDtypeStruct((B, DEPTH, 2, SSM_HEADS, SSM_HEADDIM, SSM_STATE), f32))
    return pl.pallas_call(
        functools.partial(_ssd_body, L=L, has_init=has_init, chain_state=chain_state),
        grid=(B, SSM_GROUPS),
        in_specs=in_specs,
        out_specs=out_specs,
        out_shape=out_shape,
        scratch_shapes=[pltpu.VMEM((L, 512), f32)],
        input_output_aliases=aliases,
        compiler_params=_cparams(("parallel", "parallel")),
        name="ssd_mixer",
    )(*args)


def _s5_body(*refs, n, n_c, s_b, has_init, chain_state):
    it = iter(refs)
    u_ref, wt_ref, wsf_ref, wsb_ref, wof_ref, wob_ref, lam_ref, dsk_ref = [next(it) for _ in range(8)]
    if has_init:
        h0_ref = next(it)
        next(it)
    elif chain_state:
        next(it)
    o_ref = next(it)
    if not has_init:
        hs_ref = next(it)
    zf_s, zb_s, hf_s, hb_s, zfs_s, zbs_s, acc_s, za_s, zc_s = [next(it) for _ in range(9)]
    T = S5_CHUNK

    for j in range(T):
        ub = u_ref[pl.ds(j, n, stride=T), :].astype(bf16)
        a = _dot(ub, wt_ref[0, 0, :, (T - 1 - j) * 128:(2 * T - 1 - j) * 128])
        f = _dot(ub, wsf_ref[0, 0, 0, j])
        b = _dot(ub, wsb_ref[0, 0, 0, j])
        if j == 0:
            acc_s[...] = a
            za_s[...] = f
            zc_s[...] = b
        else:
            acc_s[...] += a
            za_s[...] += f
            zc_s[...] += b
    for kk in range(8):
        zf_s[kk] = za_s[:, kk * 128:(kk + 1) * 128]
        zb_s[kk] = zc_s[:, kk * 128:(kk + 1) * 128]

    lam = lam_ref[0, 0]

    for kk in range(8):
        zfs_s[kk] = pltpu.roll(zf_s[kk], 64, axis=1)
        zbs_s[kk] = pltpu.roll(zb_s[kk], 64, axis=1)

    def rows(c):
        return pl.ds(c, s_b, stride=n_c) if s_b > 1 else pl.ds(c, 1)

    def scan_step(c, h, hs, z_s, zs_s, h_s, la, lb):
        new, news = [], []
        for kk in range(8):
            h_s[kk, rows(c), :] = h[kk]
            sl = slice(kk * 128, (kk + 1) * 128)
            new.append(la[:, sl] * h[kk] + lb[:, sl] * hs[kk] + z_s[kk, rows(c), :])
            news.append(la[:, sl] * hs[kk] - lb[:, sl] * h[kk] + zs_s[kk, rows(c), :])
        return tuple(new), tuple(news)

    def step(t, carry):
        hf, hfs, hb, hbs = carry
        hf, hfs = scan_step(t, hf, hfs, zf_s, zfs_s, hf_s, lam[0:1], lam[1:2])
        hb, hbs = scan_step(n_c - 1 - t, hb, hbs, zb_s, zbs_s, hb_s, lam[2:3], lam[3:4])
        return hf, hfs, hb, hbs

    if has_init:
        h0f = tuple(h0_ref[:, 0, 0, kk * 128:(kk + 1) * 128] for kk in range(8))
        h0b = tuple(h0_ref[:, 0, 1, kk * 128:(kk + 1) * 128] for kk in range(8))
    else:
        h0f = h0b = tuple(jnp.zeros((s_b, 128), f32) for _ in range(8))
    swap = lambda hh: tuple(pltpu.roll(v, 64, axis=1) for v in hh)
    hf, _, hb, _ = lax.fori_loop(0, n_c, step, (h0f, swap(h0f), h0b, swap(h0b)))
    if not has_init:
        for kk in range(8):
            hs_ref[:, 0, 0, kk * 128:(kk + 1) * 128] = hf[kk]
            hs_ref[:, 0, 1, kk * 128:(kk + 1) * 128] = hb[kk]

    hin_f = jnp.concatenate([hf_s[kk] for kk in range(8)], axis=1).astype(bf16)
    hin_b = jnp.concatenate([hb_s[kk] for kk in range(8)], axis=1).astype(bf16)
    acc_s[...] += _dot(hin_f, wof_ref[0, 0, 0])
    acc_s[...] += _dot(hin_b, wob_ref[0, 0, 0])
    dsk = dsk_ref[0, 0]
    for i in range(T):
        o_ref[pl.ds(i, n, stride=T), :] = (acc_s[:, i * 128:(i + 1) * 128]
                                           + dsk * u_ref[pl.ds(i, n, stride=T), :])


def s5_mixer(PF, w, dsk, *, B, L, s_b, row0, layer, state=None, prev=None, st_prev=None):
    has_init = state is not None
    chain_state = st_prev is not None
    n_c = L // S5_CHUNK
    n = s_b * n_c
    rows = s_b * L
    one = pl.Buffered(1)
    in_specs = [
        pl.BlockSpec((rows, 128), lambda lb, sb: (row0 + sb, F_S5_U // 128 + lb)),
        pl.BlockSpec((1, 1, 128, 31 * 128), lambda lb, sb: (layer, lb, 0, 0), pipeline_mode=one),
        pl.BlockSpec((1, 1, 1, 16, 128, 1024), lambda lb, sb: (layer, 0, lb, 0, 0, 0), pipeline_mode=one),
        pl.BlockSpec((1, 1, 1, 16, 128, 1024), lambda lb, sb: (layer, 1, lb, 0, 0, 0), pipeline_mode=one),
        pl.BlockSpec((1, 1, 1, 1024, 2048), lambda lb, sb: (layer, 0, lb, 0, 0), pipeline_mode=one),
        pl.BlockSpec((1, 1, 1, 1024, 2048), lambda lb, sb: (layer, 1, lb, 0, 0), pipeline_mode=one),
        pl.BlockSpec((1, 1, 4, 1024), lambda lb, sb: (layer, lb, 0, 0)),
        pl.BlockSpec((1, 1, 1, 128), lambda lb, sb: (layer, lb, 0, 0)),
    ]
    args = [PF, w["toep"], w["ws"], w["ws"], w["wo"], w["wo"], w["lam16"], dsk]
    aliases = {}
    out_specs = [pl.BlockSpec((rows, 128), lambda lb, sb: (row0 + sb, lb))]
    out_shape = [jax.ShapeDtypeStruct((N_TOK, MIX_W), f32)]
    if has_init:
        in_specs.append(pl.BlockSpec((s_b, 1, 2, 1024), lambda lb, sb: (sb, layer, 0, lb)))
        args.append(state)
        in_specs.append(pl.BlockSpec(memory_space=pl.ANY))
        args.append(prev)
        aliases = {len(args) - 1: 0}
    else:
        if chain_state:
            in_specs.append(pl.BlockSpec(memory_space=pl.ANY))
            args.append(st_prev)
            aliases = {len(args) - 1: 1}
        out_specs.append(pl.BlockSpec((s_b, 1, 2, 1024), lambda lb, sb: (sb, layer, 0, lb)))
        out_shape.append(jax.ShapeDtypeStruct((B, DEPTH, 2, 2 * S5_GROUPS * S5_STATE), f32))
    return pl.pallas_call(
        functools.partial(_s5_body, n=n, n_c=n_c, s_b=s_b, has_init=has_init, chain_state=chain_state),
        grid=(4, B // s_b),
        in_specs=in_specs,
        out_specs=out_specs,
        out_shape=out_shape,
        scratch_shapes=([pltpu.VMEM((8, n, 128), f32)] * 6
                        + [pltpu.VMEM((n, 2048), f32), pltpu.VMEM((n, 1024), f32), pltpu.VMEM((n, 1024), f32)]),
        input_output_aliases=aliases,
        compiler_params=_cparams(("parallel", "parallel")),
        name="s5_mixer",
    )(*args)


def _s5_layer_weights(a_re, a_im, log_dt, b_re, b_im, c_re, c_im):
    hp = lax.Precision.HIGHEST
    T = S5_CHUNK
    dt = jnp.exp(log_dt)[..., None]
    lmag, ang = a_re * dt, a_im * dt
    taus = jnp.arange(T + 1, dtype=f32)[:, None, None, None]
    mag = jnp.exp(lmag[None] * taus)
    pr = mag * jnp.cos(ang[None] * taus)
    pi = mag * jnp.sin(ang[None] * taus)
    lam_r, lam_i = pr[1], pi[1]
    den = a_re * a_re + a_im * a_im
    zr = ((lam_r - 1.0) * a_re + lam_i * a_im) / den
    zi = (lam_i * a_re - (lam_r - 1.0) * a_im) / den
    bb_r = zr[..., None] * b_re[None] - zi[..., None] * b_im[None]
    bb_i = zr[..., None] * b_im[None] + zi[..., None] * b_re[None]
    lb_r = pr[..., None] * bb_r[None] - pi[..., None] * bb_i[None]
    lb_i = pr[..., None] * bb_i[None] + pi[..., None] * bb_r[None]
    pr, pi, lb_r, lb_i = lax.optimization_barrier((pr, pi, lb_r, lb_i))
    kk = jnp.sum(c_re[None, None, :, :, :, None] * lb_r[:, :, :, None, :, :]
                 - c_im[None, None, :, :, :, None] * lb_i[:, :, :, None, :, :], axis=4)
    eye8 = jnp.eye(8, dtype=f32)

    def blockdiag(m):
        lead = m.shape[:-3]
        a, b = m.shape[-2], m.shape[-1]
        m = m.reshape(lead + (4, 8, a, b))
        out = m[..., :, :, :, None, :] * eye8[:, None, :, None]
        return out.reshape(lead + (4, 8 * a, 8 * b))

    kt = jnp.swapaxes(kk, -1, -2)
    down = np.arange(T - 1, 0, -1)
    deltas = jnp.concatenate([kt[down, 1], kt[0:1, 0] + kt[0:1, 1], kt[1:T, 0]], axis=0)
    toep = blockdiag(deltas)
    toep = jnp.transpose(toep, (1, 2, 0, 3)).reshape(4, 128, 31 * 128)

    def state_in(lbr, lbi):
        return jnp.concatenate([jnp.swapaxes(lbr, -1, -2), jnp.swapaxes(lbi, -1, -2)], axis=-1)

    rev16 = np.arange(T - 1, -1, -1)
    ws_c = jnp.stack([state_in(lb_r[rev16, 0], lb_i[rev16, 0]),
                      state_in(lb_r[:T, 1], lb_i[:T, 1])], axis=0)

    def state_out(prd, pid):
        cr = c_re[None] * prd[:, :, None, :] - c_im[None] * pid[:, :, None, :]
        ci = c_re[None] * pid[:, :, None, :] + c_im[None] * prd[:, :, None, :]
        return jnp.concatenate([cr, -ci], axis=-1)

    down16 = np.arange(T, 0, -1)
    wo_c = jnp.stack([state_out(pr[1:T + 1, 0], pi[1:T + 1, 0]),
                      state_out(pr[down16, 1], pi[down16, 1])], axis=0)
    la = jnp.concatenate([pr[T], pr[T]], axis=-1)
    lb = jnp.concatenate([-pi[T], pi[T]], axis=-1)
    lam16 = jnp.stack([la[0], lb[0], la[1], lb[1]], axis=0)
    lam16 = jnp.transpose(lam16.reshape(4, 4, 1024), (1, 0, 2))
    return dict(toep=toep.astype(bf16), ws_c=ws_c, wo_c=wo_c, lam16=lam16)


def _s5_expand_in_body(c_ref, o_ref):
    o_ref[...] = jnp.zeros(o_ref.shape, o_ref.dtype)
    for j in range(S5_CHUNK):
        for g in range(8):
            o_ref[0, 0, 0, j, g * 16:(g + 1) * 16, g * 128:(g + 1) * 128] = c_ref[0, 0, j, g].astype(bf16)


def _s5_expand_out_body(c_ref, e_ref, o_ref):
    for i in range(S5_CHUNK):
        for g in range(8):
            tile = _dot_tn(c_ref[0, 0, i, g].astype(bf16), e_ref[g])
            o_ref[0, 0, 0, g * 128:(g + 1) * 128, i * 128:(i + 1) * 128] = tile.astype(bf16)


def s5_expand_weights(ws_c, wo_c):
    c_spec = pl.BlockSpec((1, 1, S5_CHUNK, 8, S5_CH, 128), lambda l, d, lb: (l, d, 0, lb, 0, 0))
    ws = pl.pallas_call(
        _s5_expand_in_body,
        grid=(DEPTH, 2, 4),
        in_specs=[c_spec],
        out_specs=pl.BlockSpec((1, 1, 1, S5_CHUNK, 128, 1024), lambda l, d, lb: (l, d, lb, 0, 0, 0)),
        out_shape=jax.ShapeDtypeStruct((DEPTH, 2, 4, S5_CHUNK, 128, 1024), bf16),
        compiler_params=_cparams(("parallel", "parallel", "parallel")),
        name="s5_expand_in",
    )(ws_c)
    place = np.zeros((8, S5_CH, 128), np.float32)
    for g in range(8):
        place[g, np.arange(S5_CH), g * S5_CH + np.arange(S5_CH)] = 1.0
    wo = pl.pallas_call(
        _s5_expand_out_body,
        grid=(DEPTH, 2, 4),
        in_specs=[c_spec, pl.BlockSpec((8, S5_CH, 128), lambda l, d, lb: (0, 0, 0))],
        out_specs=pl.BlockSpec((1, 1, 1, 1024, 2048), lambda l, d, lb: (l, d, lb, 0, 0)),
        out_shape=jax.ShapeDtypeStruct((DEPTH, 2, 4, 1024, 2048), bf16),
        compiler_params=_cparams(("parallel", "parallel", "parallel")),
        name="s5_expand_out",
    )(wo_c, jnp.asarray(place, bf16))
    return ws, wo


def _glu_body(y_ref, w_ref, b_ref, o_ref):
    g = jax.nn.gelu(y_ref[...])
    o_ref[...] = g * jax.nn.sigmoid(_dot(g.astype(bf16), w_ref[0]) + b_ref[0])


def s5_glu(y, w, b, layer, tm=512):
    return pl.pallas_call(
        _glu_body,
        grid=(N_TOK // tm,),
        in_specs=[
            pl.BlockSpec((tm, MIX_W), lambda i: (i, 0)),
            pl.BlockSpec((1, MIX_W, MIX_W), lambda i: (layer, 0, 0)),
            pl.BlockSpec((1, 1, MIX_W), lambda i: (layer, 0, 0)),
        ],
        out_specs=pl.BlockSpec((tm, MIX_W), lambda i: (i, 0)),
        out_shape=jax.ShapeDtypeStruct((N_TOK, MIX_W), f32),
        compiler_params=_cparams(("parallel",)),
        name="s5_glu",
    )(y, w, b)


def _merge_body(gp_ref, b0_ref, b1_ref, b2_ref, b3_ref, wb_ref, wo_ref, x_ref, g1_ref, o_ref, acc_s):
    b = pl.program_id(1)
    for bb, br_ref in enumerate((b0_ref, b1_ref, b2_ref, b3_ref)):
        @pl.when(b == bb)
        def _(br_ref=br_ref, bb=bb):
            up = _dot(br_ref[...].astype(bf16), wb_ref[0, 0])
            t = (0.5 * jnp.tanh(0.5 * gp_ref[...].astype(f32)) + 0.5) * up
            if bb == 0:
                acc_s[...] = t
            else:
                acc_s[...] += t

    @pl.when(b == 3)
    def _():
        mix = _dot(acc_s[...].astype(bf16), wo_ref[0])
        o_ref[...] = x_ref[...] + g1_ref[0, 0] * mix


def merge_branches(PB, branches, w_branch, w_out, x, mod, layer, tm=512):
    br_spec = pl.BlockSpec((tm, MIX_W), lambda i, b: (i, 0))
    return pl.pallas_call(
        _merge_body,
        grid=(N_TOK // tm, 4),
        in_specs=[
            pl.BlockSpec((tm, D), lambda i, b: (i, B_GATE // D + b)),
            br_spec, br_spec, br_spec, br_spec,
            pl.BlockSpec((1, 1, MIX_W, D), lambda i, b: (layer, b, 0, 0)),
            pl.BlockSpec((1, D, D), lambda i, b: (layer, 0, 0), pipeline_mode=pl.Buffered(1)),
            pl.BlockSpec((tm, D), lambda i, b: (i, 0)),
            _mod_spec(layer, 2, tm),
        ],
        out_specs=pl.BlockSpec((tm, D), lambda i, b: (i, 0)),
        out_shape=jax.ShapeDtypeStruct((N_TOK, D), f32),
        scratch_shapes=[pltpu.VMEM((tm, D), f32)],
        compiler_params=_cparams(("parallel", "arbitrary")),
        name="merge_branches",
    )(PB, *branches, w_branch, w_out, x, mod)


def _ffn_body(*refs, n_e, steps_per_e, final_norm):
    it = iter(refs)
    x_ref, sh_ref, sc_ref, gt_ref, g_ref, w1_ref, w3_ref, w2_ref = [next(it) for _ in range(8)]
    if n_e > 1:
        rwh_ref, rwl_ref, rb_ref = [next(it) for _ in range(3)]
    fg_ref = next(it) if final_norm else None
    o_ref = next(it)
    h_s = next(it)
    acc_s = o_ref
    gate_s = next(it) if n_e > 1 else None
    j = pl.program_id(1)
    nj = pl.num_programs(1)

    @pl.when(j == 0)
    def _():
        h = _modnorm(x_ref[...], g_ref[0], sc_ref[0, 0], sh_ref[0, 0])
        h_s[...] = h.astype(bf16)
        if n_e > 1:
            logits = _dot_hi(h, rwh_ref[0], rwl_ref[0]) + rb_ref[0]
            lane = lax.broadcasted_iota(jnp.int32, logits.shape, 1)
            neg = jnp.float32(-jnp.inf)
            lv = jnp.where(lane < n_e, logits, neg)
            m1 = jnp.max(lv, axis=1, keepdims=True)
            i1 = jnp.min(jnp.where(lv == m1, lane, 128), axis=1, keepdims=True)
            lv2 = jnp.where(lane == i1, neg, lv)
            m2 = jnp.max(lv2, axis=1, keepdims=True)
            i2 = jnp.min(jnp.where(lv2 == m2, lane, 128), axis=1, keepdims=True)
            e2 = jnp.exp(m2 - m1)
            p1 = 1.0 / (1.0 + e2)
            gate_s[...] = jnp.where(lane == i1, p1, 0.0) + jnp.where(lane == i2, e2 * p1, 0.0)

    hb = h_s[...]
    hid = _silu(_dot(hb, w1_ref[0, 0])) * _dot(hb, w3_ref[0, 0])
    if n_e > 1:
        e = j // steps_per_e
        lane = lax.broadcasted_iota(jnp.int32, gate_s.shape, 1)
        hid = hid * jnp.sum(jnp.where(lane == e, gate_s[...], 0.0), axis=1, keepdims=True)
    hidb = hid.astype(bf16)
    cb = 512

    @pl.when(j == 0)
    def _():
        for c0 in range(0, D, cb):
            acc_s[:, c0:c0 + cb] = _dot(hidb, w2_ref[0, 0, :, c0:c0 + cb])

    @pl.when(j > 0)
    def _():
        for c0 in range(0, D, cb):
            acc_s[:, c0:c0 + cb] += _dot(hidb, w2_ref[0, 0, :, c0:c0 + cb])

    @pl.when(j == nj - 1)
    def _():
        y = x_ref[...] + gt_ref[0, 0] * acc_s[...]
        if final_norm:
            ms = jnp.mean(y * y, axis=-1, keepdims=True)
            y = y * lax.rsqrt(ms + EPS) * fg_ref[...]
        o_ref[...] = y


def ffn_layer(x, mod, g, w1, w3, w2, layer, widx, router=None, final_g=None, tm=1024, tf=512):
    n_e, f = w1.shape[1], w1.shape[3]
    spe = f // tf
    final = final_g is not None
    in_specs = [
        pl.BlockSpec((tm, D), lambda i, j: (i, 0), pipeline_mode=pl.Buffered(1)),
        _mod_spec(layer, 3, tm),
        _mod_spec(layer, 4, tm),
        _mod_spec(layer, 5, tm),
        pl.BlockSpec((1, 1, D), lambda i, j: (layer, 0, 0)),
        pl.BlockSpec((1, 1, D, tf), lambda i, j: (widx, j // spe, 0, j % spe)),
        pl.BlockSpec((1, 1, D, tf), lambda i, j: (widx, j // spe, 0, j % spe)),
        pl.BlockSpec((1, 1, tf, D), lambda i, j: (widx, j // spe, j % spe, 0)),
    ]
    args = [x, mod, mod, mod, g, w1, w3, w2]
    scratch = [pltpu.VMEM((tm, D), bf16)]
    if n_e > 1:
        in_specs += [pl.BlockSpec((1, D, 128), lambda i, j: (widx, 0, 0)),
                     pl.BlockSpec((1, D, 128), lambda i, j: (widx, 0, 0)),
                     pl.BlockSpec((1, 1, 128), lambda i, j: (widx, 0, 0))]
        args += list(router)
        scratch.append(pltpu.VMEM((tm, 128), f32))
    if final:
        in_specs.append(pl.BlockSpec((1, D), lambda i, j: (0, 0)))
        args.append(final_g.reshape(1, D))
    return pl.pallas_call(
        functools.partial(_ffn_body, n_e=n_e, steps_per_e=spe, final_norm=final),
        grid=(N_TOK // tm, n_e * spe),
        in_specs=in_specs,
        out_specs=pl.BlockSpec((tm, D), lambda i, j: (i, 0)),
        out_shape=jax.ShapeDtypeStruct((N_TOK, D), f32),
        scratch_shapes=scratch,
        compiler_params=_cparams(("parallel", "arbitrary")),
        name="ffn_moe" if n_e > 1 else "ffn_dense",
    )(*args)


def _split_hi_lo(w):
    hi = w.astype(bf16)
    lo = (w - hi.astype(f32)).astype(bf16)
    return hi, lo


_W_IN_SEGMENTS = ((2080, 3104), (5680, 6192), (1536, 1568), (5664, 5680), None,
                  (6192, 14384), (0, 1536), (1568, 2080), (3104, 5664))
_W_IN_COLS = 14384


def _reorder_body(wt_ref, o_ref):
    c = 0
    small = []
    for seg in _W_IN_SEGMENTS:
        if seg is None:
            rows = small + [jnp.zeros((128 - sum(s.shape[0] for s in small), 128), f32)]
            o_ref[0, :, F_SMALL:F_SMALL + 128] = jnp.concatenate(rows, axis=0).T.astype(bf16)
            width = N_PF - F_SMALL - 128
            o_ref[0, :, F_SMALL + 128:N_PF] = jnp.zeros((128, width), bf16)
            c = N_PF
            continue
        a, b = seg
        if (b - a) % 128:
            small.append(wt_ref[0, a:b, :])
            continue
        for r in range(a, b, 128):
            o_ref[0, :, c:c + 128] = wt_ref[0, r:r + 128, :].T.astype(bf16)
            c += 128


def _reorder_w_in(w):
    wt = jnp.swapaxes(w, 1, 2)
    return pl.pallas_call(
        _reorder_body,
        grid=(DEPTH, D // 128),
        in_specs=[pl.BlockSpec((1, _W_IN_COLS, 128), lambda l, r: (l, 0, r))],
        out_specs=pl.BlockSpec((1, 128, N_PF + N_PB), lambda l, r: (l, r, 0)),
        out_shape=jax.ShapeDtypeStruct((DEPTH, D, N_PF + N_PB), bf16),
        compiler_params=_cparams(("parallel", "parallel")),
        name="reorder_w_in",
    )(wt)


def _gla_gate_weights(wa2, ba):
    t = wa2.reshape(DEPTH, 2, GLA_RANK, 2, 128)
    w = jnp.einsum("ldrhc,de->lhdrec", t, jnp.eye(2, dtype=f32))
    w = w.reshape(DEPTH, 2, 2 * GLA_RANK, 256)
    w = jnp.pad(w, ((0, 0), (0, 0), (SM_LR, 128 - SM_LR - 2 * GLA_RANK), (0, 0)))
    b = jnp.transpose(ba.reshape(DEPTH, 2, 2, 128), (0, 2, 1, 3)).reshape(DEPTH, 2, 1, 256)
    hi, lo = _split_hi_lo(w)
    return hi, lo, b


def _hgrn_bounds(logits):
    p = jax.nn.softmax(logits.astype(f32), axis=1)
    lower = jnp.maximum(jnp.cumsum(p, axis=1) - p[:, :1], 0.0)
    lb = jnp.transpose(lower, (1, 0, 2)).reshape(DEPTH, 2, HGRN_HEADS, HGRN_DIM)
    rows = jnp.stack([lb, 1.0 - lb], axis=3)
    rows = jnp.transpose(rows, (0, 2, 1, 3, 4)).reshape(DEPTH, HGRN_HEADS, 4, HGRN_DIM)
    return jnp.pad(rows, ((0, 0), (0, 0), (0, 4), (0, 0)))


def _ssd_select_mats():
    e64 = np.zeros((2, 2, 128, 256), np.float32)
    e128 = np.zeros((2, 2, 128, 512), np.float32)
    for g in range(2):
        for d in range(2):
            for h in range(4):
                lane = SM_DT + d * 8 + g * 4 + h
                e64[g, d, lane, h * 64:(h + 1) * 64] = 1.0
                e128[g, d, lane, h * 128:(h + 1) * 128] = 1.0
    return jnp.asarray(e64, bf16), jnp.asarray(e128, bf16)


def _ssd_params(conv_w, conv_b, dt_bias, a_log, d_skip, norm_g):
    cwb = jnp.concatenate([conv_w, conv_b[:, None]], axis=1)
    cw = jnp.concatenate([cwb[..., 0:512].reshape(DEPTH, 4, 2, 256),
                          cwb[..., 512:768].reshape(DEPTH, 4, 2, 128),
                          cwb[..., 768:1024].reshape(DEPTH, 4, 2, 128)], axis=-1)
    cw = jnp.transpose(cw, (0, 2, 1, 3))
    lanes = ((0, 0), (SM_DT, 128 - SM_DT - 16))
    dtb = jnp.pad(dt_bias.reshape(DEPTH, 16), lanes).reshape(DEPTH, 1, 128)
    alog = jnp.pad(a_log.reshape(DEPTH, 16), lanes).reshape(DEPTH, 1, 128)
    dsk = jnp.repeat(d_skip, SSM_HEADDIM, axis=1).reshape(DEPTH, 2, 1, 256)
    return cw, dtb, alog, dsk, norm_g.reshape(DEPTH, 2, 1, 256)


def kernel(x_prompt, x_sample, c, c_ctx, state_gla, state_hgrn, state_ssm, state_s5_re, state_s5_im, norm1_g, norm2_g, ada_w, ada_b, w_in, gla_wa2, gla_ba, gla_norm_g, hgrn_lb_logits, hgrn_norm_g, ssm_conv_w, ssm_conv_b, ssm_a_log, ssm_dt_bias, ssm_d, ssm_norm_g, s5_a_re, s5_a_im, s5_log_dt, s5_b_re, s5_b_im, s5_c_re, s5_c_im, s5_d, s5_glu_w, s5_glu_b, w_branch, w_out, ffn_w1, ffn_w3, ffn_w2, router_w, router_b, moe_w1, moe_w3, moe_w2, final_norm_g):
    w_all = _reorder_w_in(w_in)
    wah, wal, gba = _gla_gate_weights(gla_wa2, gla_ba)
    gng = gla_norm_g.reshape(DEPTH, 1, GLA_DV)
    lbp = _hgrn_bounds(hgrn_lb_logits)
    hng = hgrn_norm_g.reshape(DEPTH, 1, HGRN_DIM)
    e64, e128 = _ssd_select_mats()
    cw, dtb, alog, dsk, sng = _ssd_params(ssm_conv_w, ssm_conv_b, ssm_dt_bias, ssm_a_log, ssm_d, ssm_norm_g)
    w5 = jax.vmap(_s5_layer_weights)(s5_a_re, s5_a_im, s5_log_dt, s5_b_re, s5_b_im, s5_c_re, s5_c_im)
    w5["ws"], w5["wo"] = s5_expand_weights(w5["ws_c"], w5["wo_c"])
    d5 = s5_d.reshape(DEPTH, 4, 1, 128)
    glu_w = s5_glu_w.astype(bf16)
    glu_b = s5_glu_b.reshape(DEPTH, 1, MIX_W)
    wbr = w_branch.astype(bf16)
    wout = w_out.astype(bf16)
    n1 = norm1_g.reshape(DEPTH, 1, D)
    n2 = norm2_g.reshape(DEPTH, 1, D)
    fw1, fw3, fw2 = ffn_w1.astype(bf16)[:, None], ffn_w3.astype(bf16)[:, None], ffn_w2.astype(bf16)[:, None]
    mw1, mw3, mw2 = moe_w1.astype(bf16), moe_w3.astype(bf16), moe_w2.astype(bf16)
    rw = jnp.pad(router_w, ((0, 0), (0, 0), (0, 128 - N_EXPERTS)))
    router = _split_hi_lo(rw) + (jnp.pad(router_b, ((0, 0), (0, 128 - N_EXPERTS))).reshape(-1, 1, 128),)
    s5_state = jnp.concatenate([state_s5_re, state_s5_im], axis=-1).reshape(
        DEC_BATCH, DEPTH, 2, 2 * S5_GROUPS * S5_STATE)

    x = assemble_tokens(x_prompt.reshape(N_CTX, D), x_sample.reshape(N_LAT, D), _pos_table())
    cond8 = jnp.concatenate([c_ctx[None], c, jnp.zeros((MOD_ROWS - 1 - DEC_BATCH, D), f32)], axis=0)
    mod = ada_modulation(cond8, ada_w, ada_b).reshape(DEPTH, MOD_ROWS * 6, 1, D)

    lat0 = N_CTX // DEC_SEQ
    ctx = dict(B=BATCH, L=SEQ, row0=0)
    lat = dict(B=DEC_BATCH, L=DEC_SEQ, row0=lat0)
    st_gla = st_hg = st_ssm = st_s5 = None
    for l in range(DEPTH):
        PF, PB = in_projection(x, mod, n1, w_all, l)

        o_gla, st_gla = gla_mixer(PF, PB, wah, wal, gba, gng, layer=l, st_prev=st_gla, **ctx)
        (o_gla,) = gla_mixer(PF, PB, wah, wal, gba, gng, layer=l, state=state_gla, prev=o_gla, **lat)

        o_hg, st_hg = hgrn_mixer(PF, PB, lbp, hng, layer=l, st_prev=st_hg, **ctx)
        (o_hg,) = hgrn_mixer(PF, PB, lbp, hng, layer=l, state=state_hgrn, prev=o_hg, **lat)

        o_ssm, st_ssm = ssd_mixer(PF, PB, cw, dtb, alog, e64, e128, dsk, sng, layer=l, st_prev=st_ssm, **ctx)
        (o_ssm,) = ssd_mixer(PF, PB, cw, dtb, alog, e64, e128, dsk, sng, layer=l, state=state_ssm,
                             prev=o_ssm, **lat)

        y5, st_s5 = s5_mixer(PF, w5, d5, B=BATCH, L=SEQ, s_b=min(8, BATCH), row0=0, layer=l, st_prev=st_s5)
        (y5,) = s5_mixer(PF, w5, d5, B=DEC_BATCH, L=DEC_SEQ, s_b=1, row0=lat0, layer=l,
                         state=s5_state, prev=y5)
        o_s5 = s5_glu(y5, glu_w, glu_b, l)

        x = merge_branches(PB, (o_gla, o_hg, o_ssm, o_s5), wbr, wout, x, mod, l)

        last = l == DEPTH - 1
        kw = dict(final_g=final_norm_g) if last else {}
        if l % 2 == 0:
            x = ffn_layer(x, mod, n2, fw1, fw3, fw2, l, l // 2, **kw)
        else:
            x = ffn_layer(x, mod, n2, mw1, mw3, mw2, l, l // 2, router=router, **kw)

    y_ctx, y_lat = x[:N_CTX], x[N_CTX:]
    st_s5 = st_s5.reshape(BATCH, DEPTH, 2, S5_GROUPS, 2 * S5_STATE)
    return (y_ctx.reshape(BATCH, SEQ, D), y_lat.reshape(DEC_BATCH, DEC_SEQ, D),
            st_gla, st_hg, st_ssm, st_s5[..., :S5_STATE], st_s5[..., S5_STATE:])
```

```python
import functools

import numpy as np
import jax
import jax.numpy as jnp
from jax import lax
from jax.experimental import pallas as pl
from jax.experimental.pallas import tpu as pltpu

f32 = jnp.float32
bf16 = jnp.bfloat16

D = 2048
BATCH = 32
SEQ = 256
DEPTH = 4
DEC_BATCH = 4
DEC_SEQ = 4096
GRID_W = 64
MIX_W = 512
GLA_HEADS = 4
GLA_DK = 64
GLA_DV = 128
GLA_RANK = 16
GLA_NORMALIZER = 16.0
HGRN_HEADS = 4
HGRN_DIM = 128
SSM_HEADS = 8
SSM_HEADDIM = 64
SSM_GROUPS = 2
SSM_STATE = 128
S5_CH = 16
S5_STATE = 64
S5_GROUPS = 32
D_FF = 5632
N_EXPERTS = 8
D_FF_EXPERT = 1024
EPS = 1e-6

N_CTX = BATCH * SEQ
N_LAT = DEC_BATCH * DEC_SEQ
N_TOK = N_CTX + N_LAT
MOD_ROWS = 8

F_HG_F, F_S5_U = 0, 1024
N_PF = 1536
B_GATE, B_GLA_Q, B_GLA_K, B_GLA_V, B_GLA_R = 0, 8192, 8448, 8704, 9216
B_HG_Q, B_HG_I, B_HG_G, B_SSM_Z, B_SSM_XBC = 9728, 10240, 10752, 11264, 11776
N_PB = 12800
SM_LR, SM_DT = 0, 32

GLS_CHUNK = 128
GLS_SUB = 32
SSD_CHUNK = 128
S5_CHUNK = 16

VMEM_LIMIT = 58 * 1024 * 1024


def _cparams(sem):
    return pltpu.CompilerParams(dimension_semantics=sem, vmem_limit_bytes=VMEM_LIMIT)


def _dot(a, b):
    return jnp.dot(a, b, preferred_element_type=f32)


def _dot_nt(a, b):
    return lax.dot_general(a, b, (((1,), (1,)), ((), ())), preferred_element_type=f32)


def _dot_tn(a, b):
    return lax.dot_general(a, b, (((0,), (0,)), ((), ())), preferred_element_type=f32)


def _split2(x):
    hi = x.astype(bf16)
    return hi, (x - hi.astype(f32)).astype(bf16)


def _dot_sel_lhs(t, x):
    x1, x2 = _split2(x)
    return _dot(t, x1) + _dot(t, x2)


def _dot_sel_rhs(x, e):
    x1, x2 = _split2(x)
    return _dot(x1, e) + _dot(x2, e)


def _dot_sel_rhs3(x, e):
    x1, x2 = _split2(x)
    x3 = (x - x1.astype(f32) - x2.astype(f32)).astype(bf16)
    return _dot(x1, e) + _dot(x2, e) + _dot(x3, e)


def _dot_hi(x, w_hi, w_lo):
    x1, x2 = _split2(x)
    return _dot(x1, w_hi) + _dot(x2, w_hi) + _dot(x1, w_lo)


def _log_sigmoid(x):
    return jnp.minimum(x, 0.0) - jnp.log1p(jnp.exp(-jnp.abs(x)))


def _softplus(x):
    return jnp.maximum(x, 0.0) + jnp.log1p(jnp.exp(-jnp.abs(x)))


def _silu(x):
    return x * jax.nn.sigmoid(x)


def _modnorm(x, g, scale, shift):
    ms = jnp.mean(x * x, axis=-1, keepdims=True)
    y = x * lax.rsqrt(ms + EPS) * g
    return y * (1.0 + scale) + shift


def _tri(n, upper):
    ii = lax.broadcasted_iota(jnp.int32, (n, n), 0)
    jj = lax.broadcasted_iota(jnp.int32, (n, n), 1)
    return (ii <= jj) if upper else (ii >= jj)


def _mod_row(i, tm):
    start = i * tm
    return jnp.where(start < N_CTX, 0, 1 + (start - N_CTX) // DEC_SEQ)


def _mod_spec(layer, k, tm):
    return pl.BlockSpec((1, 1, 1, D), lambda i, j: (layer, _mod_row(i, tm) * 6 + k, 0, 0))


def _assemble_body(xp_ref, xs_ref, pos_ref, o_ref, *, n_ctx_tiles):
    i = pl.program_id(0)

    @pl.when(i < n_ctx_tiles)
    def _():
        o_ref[...] = xp_ref[...]

    @pl.when(i >= n_ctx_tiles)
    def _():
        o_ref[...] = xs_ref[...] + pos_ref[...]


def assemble_tokens(xp2, xs2, pos, tm=512):
    nct = N_CTX // tm
    npos = DEC_SEQ // tm
    return pl.pallas_call(
        functools.partial(_assemble_body, n_ctx_tiles=nct),
        grid=(N_TOK // tm,),
        in_specs=[
            pl.BlockSpec((tm, D), lambda i: (jnp.minimum(i, nct - 1), 0)),
            pl.BlockSpec((tm, D), lambda i: (jnp.maximum(i - nct, 0), 0)),
            pl.BlockSpec((tm, D), lambda i: (jnp.maximum(i - nct, 0) % npos, 0)),
        ],
        out_specs=pl.BlockSpec((tm, D), lambda i: (i, 0)),
        out_shape=jax.ShapeDtypeStruct((N_TOK, D), f32),
        compiler_params=_cparams(("parallel",)),
        name="assemble_tokens",
    )(xp2, xs2, pos)


def _pos_table():
    rows = DEC_SEQ // GRID_W
    row = jnp.repeat(jnp.arange(rows, dtype=f32), GRID_W)
    col = jnp.tile(jnp.arange(GRID_W, dtype=f32), rows)

    def sincos(p, d):
        half = d // 2
        omega = 1.0 / (10000.0 ** (jnp.arange(half, dtype=f32) / half))
        ang = p[:, None] * omega[None, :]
        return jnp.concatenate([jnp.sin(ang), jnp.cos(ang)], axis=-1)

    return jnp.concatenate([sincos(row, D // 2), sincos(col, D // 2)], axis=-1)


def _ada_body(c_ref, w_ref, b_ref, o_ref):
    s = _silu(c_ref[...])
    s1, s2 = _split2(s)
    w = w_ref[0].astype(bf16)
    o_ref[0] = _dot(s1, w) + _dot(s2, w) + b_ref[0]


def ada_modulation(cond8, ada_w, ada_b, tn=1024):
    return pl.pallas_call(
        _ada_body,
        grid=(DEPTH, 6 * D // tn),
        in_specs=[
            pl.BlockSpec((MOD_ROWS, D), lambda l, j: (0, 0)),
            pl.BlockSpec((1, D, tn), lambda l, j: (l, 0, j)),
            pl.BlockSpec((1, 1, tn), lambda l, j: (l, 0, j)),
        ],
        out_specs=pl.BlockSpec((1, MOD_ROWS, tn), lambda l, j: (l, 0, j)),
        out_shape=jax.ShapeDtypeStruct((DEPTH, MOD_ROWS, 6 * D), f32),
        compiler_params=_cparams(("parallel", "parallel")),
        name="ada_modulation",
    )(cond8, ada_w, ada_b.reshape(DEPTH, 1, 6 * D))


def _inproj_body(x_ref, sh_ref, sc_ref, g_ref, w_ref, ws_ref, pf_ref, pb_ref, sm_ref, h_s, *, nf):
    j = pl.program_id(1)

    @pl.when(j == 0)
    def _():
        h = _modnorm(x_ref[...], g_ref[0], sc_ref[0, 0], sh_ref[0, 0]).astype(bf16)
        h_s[...] = h
        sm_ref[...] = _dot(h, ws_ref[0])

    y = _dot(h_s[...], w_ref[0])

    @pl.when(j < nf)
    def _():
        pf_ref[...] = y

    @pl.when(j >= nf)
    def _():
        pb_ref[...] = y.astype(bf16)


def in_projection(x, mod, g, w_all, w_small, layer, tm=1024, tn=512):
    nf = N_PF // tn
    nb = N_PB // tn
    return pl.pallas_call(
        functools.partial(_inproj_body, nf=nf),
        grid=(N_TOK // tm, nf + nb),
        in_specs=[
            pl.BlockSpec((tm, D), lambda i, j: (i, 0)),
            _mod_spec(layer, 0, tm),
            _mod_spec(layer, 1, tm),
            pl.BlockSpec((1, 1, D), lambda i, j: (layer, 0, 0)),
            pl.BlockSpec((1, D, tn), lambda i, j: (layer, 0, j)),
            pl.BlockSpec((1, D, 128), lambda i, j: (layer, 0, 0)),
        ],
        out_specs=[
            pl.BlockSpec((tm, tn), lambda i, j: (i, jnp.minimum(j, nf - 1))),
            pl.BlockSpec((tm, tn), lambda i, j: (i, jnp.maximum(j - nf, 0))),
            pl.BlockSpec((tm, 128), lambda i, j: (i, 0)),
        ],
        out_shape=[
            jax.ShapeDtypeStruct((N_TOK, N_PF), f32),
            jax.ShapeDtypeStruct((N_TOK, N_PB), bf16),
            jax.ShapeDtypeStruct((N_TOK, 128), f32),
        ],
        scratch_shapes=[pltpu.VMEM((tm, D), bf16)],
        compiler_params=_cparams(("parallel", "arbitrary")),
        name="in_projection",
    )(x, mod, mod, g, w_all, w_small)


def _gls_dir(q, k, g, d):
    c, sub = GLS_CHUNK, GLS_SUB
    nb = c // sub
    tri = jnp.where(_tri(c, upper=(d == 1)), 1.0, 0.0).astype(bf16)
    b = _dot_sel_lhs(tri, g)
    zero = jnp.zeros((1, 128), f32)
    if d == 0:
        mid = [b[sub * i + sub // 2 - 1:sub * i + sub // 2] for i in range(nb)]
        end = [b[sub * i + sub - 1:sub * i + sub] for i in range(nb)]
        start = [zero] + end[:-1]
        total = end[-1]
    else:
        mid = [b[sub * i + sub // 2:sub * i + sub // 2 + 1] for i in range(nb)]
        end = [b[sub * i:sub * i + 1] for i in range(nb)]
        start = end[1:] + [zero]
        total = end[0]

    def rows(vals):
        return jnp.concatenate([jnp.broadcast_to(v, (sub, 128)) for v in vals], axis=0)

    r_mid = rows(mid)
    qd = q * jnp.exp(b - r_mid)
    kd = k * jnp.exp(r_mid - b)
    qo = qd * rows([jnp.exp(m - s) for m, s in zip(mid, start)])
    ke = kd * rows([jnp.exp(e - m) for m, e in zip(mid, end)])
    q_in = qo * rows([jnp.exp(s) for s in start])
    k_out = ke * rows([jnp.exp(total - e) for e in end])
    kr = []
    for i in range(nb):
        prior = range(i) if d == 0 else range(i + 1, nb)
        if len(prior) == 0:
            kr.append(None)
            continue
        blocks = []
        for jb in range(nb):
            if jb in prior:
                blocks.append(ke[jb * sub:(jb + 1) * sub] * jnp.exp(start[i] - end[jb]))
            else:
                blocks.append(jnp.zeros((sub, 128), f32))
        kr.append(jnp.concatenate(blocks, axis=0).astype(bf16))
    return qd, kd.astype(bf16), qo, kr, q_in, k_out, jnp.exp(total)


def _gls_head(parts, v, st, mask, kmask):
    qd, kd, qo, kr, q_in, k_out, dec = parts
    sub = GLS_SUB
    if kmask is not None:
        qd, qo, q_in, k_out = qd * kmask, qo * kmask, q_in * kmask, k_out * kmask
    att = jnp.where(mask, _dot_nt(qd.astype(bf16), kd), 0.0)
    qob = qo.astype(bf16)
    offs = []
    for i, kri in enumerate(kr):
        if kri is None:
            offs.append(jnp.zeros((sub, GLS_CHUNK), f32))
        else:
            offs.append(_dot_nt(qob[i * sub:(i + 1) * sub], kri))
    att = att + jnp.concatenate(offs, axis=0)
    vb = v.astype(bf16)
    o = _dot(att.astype(bf16), vb) + _dot_nt(q_in.astype(bf16), st.astype(bf16))
    st_new = st * dec + _dot_tn(vb, k_out.astype(bf16))
    return o, st_new


def _gls_masks():
    c, sub = GLS_CHUNK, GLS_SUB
    ii = lax.broadcasted_iota(jnp.int32, (c, c), 0)
    jj = lax.broadcasted_iota(jnp.int32, (c, c), 1)
    same = (ii // sub) == (jj // sub)
    return [same & (ii >= jj), same & (ii <= jj)]


def _head_post(o, gate_in, ng):
    ms = jnp.mean(o * o, axis=-1, keepdims=True)
    return o * lax.rsqrt(ms + EPS) * ng * _silu(gate_in.astype(f32))


def _scan_loops(n, chunk_fn, post_fn, o_ref, init):
    C = GLS_CHUNK

    def store(r0, outs, final):
        for h, o in enumerate(outs):
            sl = (pl.ds(r0, C), slice(h * 128, (h + 1) * 128))
            if final:
                o_ref[sl] = post_fn(r0, h, o_ref[sl] + o)
            else:
                o_ref[sl] = o

    def half(final):
        def body(t, carry):
            sf, sb = carry
            r0, of, sf = chunk_fn(t, 0, sf)
            store(r0, of, final)
            r1, ob, sb = chunk_fn(n - 1 - t, 1, sb)
            store(r1, ob, final)
            return sf, sb
        return body

    unroll = 2 if (n // 2) % 2 == 0 else 1
    carry = lax.fori_loop(0, n // 2, half(False), init, unroll=unroll)
    return lax.fori_loop(n // 2, n, half(True), carry, unroll=unroll)


def _gla_body(*refs, L, has_init, chain_state):
    it = iter(refs)
    q_ref, k_ref, v_ref, r_ref, sm_ref, wah_ref, wal_ref, ba_ref, ng_ref = [next(it) for _ in range(9)]
    s0_ref = next(it) if has_init else None
    if has_init or chain_state:
        next(it)
    o_ref = next(it)
    st_ref = None if has_init else next(it)
    C = GLS_CHUNK
    n = L // C
    lane = lax.broadcasted_iota(jnp.int32, (1, 128), 1)
    kmasks = [jnp.where(lane < 64, 1.0, 0.0), jnp.where(lane >= 64, 1.0, 0.0)]
    masks = _gls_masks()
    wah, wal, ba, ng = wah_ref[0, 0], wal_ref[0, 0], ba_ref[0, 0], ng_ref[0]

    def chunk(cidx, d, sts):
        r0 = pl.multiple_of(cidx * C, C)
        q = q_ref[pl.ds(r0, C), :].astype(f32) * (GLA_DK ** -0.5)
        k = k_ref[pl.ds(r0, C), :].astype(f32)
        sm = sm_ref[pl.ds(r0, C), :]
        sl = slice(d * 128, (d + 1) * 128)
        g = _log_sigmoid(_dot_hi(sm, wah[:, sl], wal[:, sl]) + ba[:, sl]) * (1.0 / GLA_NORMALIZER)
        parts = _gls_dir(q, k, g, d)
        outs, new = [], []
        for h in range(2):
            v = v_ref[pl.ds(r0, C), h * 128:(h + 1) * 128]
            o, s = _gls_head(parts, v, sts[h], masks[d], kmasks[h])
            outs.append(o)
            new.append(s)
        return r0, outs, new

    def post(r0, h, tot):
        return _head_post(tot, r_ref[pl.ds(r0, C), h * 128:(h + 1) * 128], ng)

    def init(d, h):
        if not has_init:
            return jnp.zeros((128, 128), f32)
        s = s0_ref[0, 0, d, h]
        z = jnp.zeros((64, 128), f32)
        full = jnp.concatenate([s, z], axis=0) if h == 0 else jnp.concatenate([z, s], axis=0)
        return full.T

    sf, sb = _scan_loops(n, chunk, post, o_ref, ([init(0, 0), init(0, 1)], [init(1, 0), init(1, 1)]))
    if st_ref is not None:
        for d, sts in enumerate((sf, sb)):
            for h in range(2):
                st_ref[0, 0, d, h] = sts[h].T[h * 64:(h + 1) * 64, :]


def gla_mixer(SM, PB, wa_hi, wa_lo, ba, ng, *, B, L, row0, layer, state=None, prev=None, st_prev=None):
    has_init = state is not None
    chain_state = st_prev is not None
    in_specs = [
        pl.BlockSpec((L, 128), lambda b, hp: (row0 + b, B_GLA_Q // 128 + hp)),
        pl.BlockSpec((L, 128), lambda b, hp: (row0 + b, B_GLA_K // 128 + hp)),
        pl.BlockSpec((L, 256), lambda b, hp: (row0 + b, B_GLA_V // 256 + hp)),
        pl.BlockSpec((L, 256), lambda b, hp: (row0 + b, B_GLA_R // 256 + hp)),
        pl.BlockSpec((L, 128), lambda b, hp: (row0 + b, 0)),
        pl.BlockSpec((1, 1, 128, 256), lambda b, hp: (layer, hp, 0, 0)),
        pl.BlockSpec((1, 1, 128, 256), lambda b, hp: (layer, hp, 0, 0)),
        pl.BlockSpec((1, 1, 1, 256), lambda b, hp: (layer, hp, 0, 0)),
        pl.BlockSpec((1, 1, 128), lambda b, hp: (layer, 0, 0)),
    ]
    args = [PB, PB, PB, PB, SM, wa_hi, wa_lo, ba, ng]
    aliases = {}
    out_specs = [pl.BlockSpec((L, 256), lambda b, hp: (row0 + b, hp))]
    out_shape = [jax.ShapeDtypeStruct((N_TOK, MIX_W), f32)]
    if has_init:
        in_specs.append(pl.BlockSpec((1, 1, 2, 2, 64, 128), lambda b, hp: (b, layer, 0, hp, 0, 0)))
        args.append(state)
        in_specs.append(pl.BlockSpec(memory_space=pl.ANY))
        args.append(prev)
        aliases = {len(args) - 1: 0}
    else:
        if chain_state:
            in_specs.append(pl.BlockSpec(memory_space=pl.ANY))
            args.append(st_prev)
            aliases = {len(args) - 1: 1}
        out_specs.append(pl.BlockSpec((1, 1, 2, 2, 64, 128), lambda b, hp: (b, layer, 0, hp, 0, 0)))
        out_shape.append(jax.ShapeDtypeStruct((B, DEPTH, 2, GLA_HEADS, GLA_DK, GLA_DV), f32))
    return pl.pallas_call(
        functools.partial(_gla_body, L=L, has_init=has_init, chain_state=chain_state),
        grid=(B, 2),
        in_specs=in_specs,
        out_specs=out_specs,
        out_shape=out_shape,
        input_output_aliases=aliases,
        compiler_params=_cparams(("parallel", "parallel")),
        name="gla_mixer",
    )(*args)


def _hgrn_body(*refs, L, has_init, chain_state):
    it = iter(refs)
    q_ref, f0_ref, f1_ref, i_ref, g_ref, lb_ref, ng_ref = [next(it) for _ in range(7)]
    s0_ref = next(it) if has_init else None
    if has_init or chain_state:
        next(it)
    o_ref = next(it)
    st_ref = None if has_init else next(it)
    C = GLS_CHUNK
    n = L // C
    masks = _gls_masks()
    ng = ng_ref[0]
    f_refs = (f0_ref, f1_ref)

    def chunk(cidx, d, sts):
        r0 = pl.multiple_of(cidx * C, C)
        outs, new = [], []
        for h in range(2):
            sl = slice(h * 128, (h + 1) * 128)
            lbp = lb_ref[0, h]
            q = q_ref[pl.ds(r0, C), sl].astype(f32)
            fp = f_refs[d][pl.ds(r0, C), sl]
            lb, one_m_lb = lbp[2 * d:2 * d + 1], lbp[2 * d + 1:2 * d + 2]
            e = jnp.exp(-jnp.abs(fp))
            inv = 1.0 / (1.0 + e)
            pos = fp >= 0.0
            sig = jnp.where(pos, inv, e * inv)
            k = one_m_lb * jnp.where(pos, e * inv, inv)
            g = jnp.log(lb + one_m_lb * sig)
            parts = _gls_dir(q, k, g, d)
            o, s = _gls_head(parts, i_ref[pl.ds(r0, C), sl], sts[h], masks[d], None)
            outs.append(o)
            new.append(s)
        return r0, outs, new

    def post(r0, h, tot):
        return _head_post(tot, g_ref[pl.ds(r0, C), h * 128:(h + 1) * 128], ng)

    def init(d, h):
        if not has_init:
            return jnp.zeros((128, 128), f32)
        return s0_ref[0, 0, d, h].T

    sf, sb = _scan_loops(n, chunk, post, o_ref, ([init(0, 0), init(0, 1)], [init(1, 0), init(1, 1)]))
    if st_ref is not None:
        for d, sts in enumerate((sf, sb)):
            for h in range(2):
                st_ref[0, 0, d, h] = sts[h].T


def hgrn_mixer(PF, PB, lbp, ng, *, B, L, row0, layer, state=None, prev=None, st_prev=None):
    has_init = state is not None
    chain_state = st_prev is not None
    in_specs = [
        pl.BlockSpec((L, 256), lambda b, hp: (row0 + b, B_HG_Q // 256 + hp)),
        pl.BlockSpec((L, 256), lambda b, hp: (row0 + b, F_HG_F // 256 + hp)),
        pl.BlockSpec((L, 256), lambda b, hp: (row0 + b, F_HG_F // 256 + 2 + hp)),
        pl.BlockSpec((L, 256), lambda b, hp: (row0 + b, B_HG_I // 256 + hp)),
        pl.BlockSpec((L, 256), lambda b, hp: (row0 + b, B_HG_G // 256 + hp)),
        pl.BlockSpec((1, 2, 8, 128), lambda b, hp: (layer, hp, 0, 0)),
        pl.BlockSpec((1, 1, 128), lambda b, hp: (layer, 0, 0)),
    ]
    args = [PB, PF, PF, PB, PB, lbp, ng]
    aliases = {}
    out_specs = [pl.BlockSpec((L, 256), lambda b, hp: (row0 + b, hp))]
    out_shape = [jax.ShapeDtypeStruct((N_TOK, MIX_W), f32)]
    if has_init:
        in_specs.append(pl.BlockSpec((1, 1, 2, 2, 128, 128), lambda b, hp: (b, layer, 0, hp, 0, 0)))
        args.append(state)
        in_specs.append(pl.BlockSpec(memory_space=pl.ANY))
        args.append(prev)
        aliases = {len(args) - 1: 0}
    else:
        if chain_state:
            in_specs.append(pl.BlockSpec(memory_space=pl.ANY))
            args.append(st_prev)
            aliases = {len(args) - 1: 1}
        out_specs.append(pl.BlockSpec((1, 1, 2, 2, 128, 128), lambda b, hp: (b, layer, 0, hp, 0, 0)))
        out_shape.append(jax.ShapeDtypeStruct((B, DEPTH, 2, HGRN_HEADS, HGRN_DIM, HGRN_DIM), f32))
    return pl.pallas_call(
        functools.partial(_hgrn_body, L=L, has_init=has_init, chain_state=chain_state),
        grid=(B, 2),
        in_specs=in_specs,
        out_specs=out_specs,
        out_shape=out_shape,
        input_output_aliases=aliases,
        compiler_params=_cparams(("parallel", "parallel")),
        name="hgrn_mixer",
    )(*args)


def _ssd_body(*refs, L, has_init, chain_state):
    it = iter(refs)
    (xs_ref, bm_ref, cm_ref, z_ref, sm_ref, cw_ref, dtb_ref, alog_ref, e64_ref, e128_ref,
     dsk_ref, ng_ref) = [next(it) for _ in range(12)]
    s0_ref = next(it) if has_init else None
    if has_init or chain_state:
        next(it)
    o_ref = next(it)
    st_ref = None if has_init else next(it)
    xc_s = next(it)
    C = SSD_CHUNK
    n = L // C
    cw = cw_ref[0, 0]
    rowi = lax.broadcasted_iota(jnp.int32, (C, 1), 0)

    def conv_block(rb, _):
        r0 = pl.multiple_of(rb * C, C)
        rp = pl.multiple_of(jnp.maximum(r0 - 16, 0), 16)
        rn = pl.multiple_of(jnp.minimum(r0 + C, L - 16), 16)
        has_p = jnp.where(rb > 0, 1.0, 0.0)
        has_n = jnp.where(rb < n - 1, 1.0, 0.0)
        for ref, c0, w in ((xs_ref, 0, 256), (bm_ref, 256, 128), (cm_ref, 384, 128)):
            x = ref[pl.ds(r0, C), :].astype(f32)
            prev = ref[pl.ds(rp, 16), :].astype(f32)[15:16] * has_p
            nxt = ref[pl.ds(rn, 16), :].astype(f32)[0:1] * has_n
            x_dn = jnp.where(rowi == 0, prev, pltpu.roll(x, 1, axis=0))
            x_up = jnp.where(rowi == C - 1, nxt, pltpu.roll(x, C - 1, axis=0))
            y = (cw[0:1, c0:c0 + w] * x_dn + cw[1:2, c0:c0 + w] * x + cw[2:3, c0:c0 + w] * x_up
                 + cw[3:4, c0:c0 + w])
            xc_s[pl.ds(r0, C), c0:c0 + w] = _silu(y)
        return 0

    lax.fori_loop(0, n, conv_block, 0)

    masks = [_tri(C, upper=False), _tri(C, upper=True)]
    tris = [jnp.where(m, 1.0, 0.0).astype(bf16) for m in masks]
    lane256 = lax.broadcasted_iota(jnp.int32, (1, 256), 1)
    hmask = [jnp.where((lane256 >= h * 64) & (lane256 < (h + 1) * 64), 1.0, 0.0) for h in range(4)]
    dtb = dtb_ref[0]
    a_row = -jnp.exp(alog_ref[0])
    neg_inf = jnp.float32(-jnp.inf)

    def chunk(cidx, d, st):
        r0 = pl.multiple_of(cidx * C, C)
        xs = xc_s[pl.ds(r0, C), 0:256]
        bm = xc_s[pl.ds(r0, C), 256:384].astype(bf16)
        cm = xc_s[pl.ds(r0, C), 384:512].astype(bf16)
        dt_all = _softplus(sm_ref[pl.ds(r0, C), :] + dtb)
        a_all = dt_all * a_row
        dt64 = _dot_sel_rhs(dt_all, e64_ref[0, d])
        acs_all = _dot_sel_lhs(tris[d], a_all)
        acs64 = _dot_sel_rhs3(acs_all, e64_ref[0, d])
        acs128 = _dot_sel_rhs3(acs_all, e128_ref[0, d])
        end = acs64[C - 1:C] if d == 0 else acs64[0:1]
        xdt = xs * dt64
        scores = _dot_nt(cm, bm)
        y = jnp.exp(acs64) * _dot(cm, st.astype(bf16))
        for h in range(4):
            colb = acs128[:, h * 128:(h + 1) * 128]
            diff = jnp.where(masks[d], colb - colb.T, neg_inf)
            p = (scores * jnp.exp(diff)).astype(bf16)
            y = y + _dot(p, (xdt * hmask[h]).astype(bf16))
        xd = (xdt * jnp.exp(end - acs64)).astype(bf16)
        st_new = st * jnp.exp(end) + _dot_tn(bm, xd)
        return r0, y, st_new

    dsk = dsk_ref[0, 0]
    ng = ng_ref[0, 0]

    def post(r0, y):
        xs = xc_s[pl.ds(r0, C), 0:256]
        t = (y + dsk * xs) * _silu(z_ref[pl.ds(r0, C), :].astype(f32))
        ms = jnp.mean(t * t, axis=-1, keepdims=True)
        return t * lax.rsqrt(ms + EPS) * ng

    def first_half(t, carry):
        sf, sb = carry
        r0, yf, sf = chunk(t, 0, sf)
        o_ref[pl.ds(r0, C), :] = yf
        r1, yb, sb = chunk(n - 1 - t, 1, sb)
        o_ref[pl.ds(r1, C), :] = yb
        return sf, sb

    def second_half(t, carry):
        sf, sb = carry
        r0, yf, sf = chunk(t, 0, sf)
        o_ref[pl.ds(r0, C), :] = post(r0, o_ref[pl.ds(r0, C), :] + yf)
        r1, yb, sb = chunk(n - 1 - t, 1, sb)
        o_ref[pl.ds(r1, C), :] = post(r1, o_ref[pl.ds(r1, C), :] + yb)
        return sf, sb

    def init(d):
        if not has_init:
            return jnp.zeros((128, 256), f32)
        return s0_ref[0, 0, d].reshape(256, 128).T

    carry = (init(0), init(1))
    unroll = 2 if (n // 2) % 2 == 0 else 1
    carry = lax.fori_loop(0, n // 2, first_half, carry, unroll=unroll)
    sf, sb = lax.fori_loop(n // 2, n, second_half, carry, unroll=unroll)
    if st_ref is not None:
        st_ref[0, 0, 0] = sf.T.reshape(4, 64, 128)
        st_ref[0, 0, 1] = sb.T.reshape(4, 64, 128)


def ssd_mixer(SM, PB, cw, dtb, alog, e64, e128, dsk, ng, *, B, L, row0, layer, state=None, prev=None,
              st_prev=None):
    has_init = state is not None
    chain_state = st_prev is not None
    xb = B_SSM_XBC
    in_specs = [
        pl.BlockSpec((L, 256), lambda b, g: (row0 + b, xb // 256 + g)),
        pl.BlockSpec((L, 128), lambda b, g: (row0 + b, (xb + 512) // 128 + g)),
        pl.BlockSpec((L, 128), lambda b, g: (row0 + b, (xb + 768) // 128 + g)),
        pl.BlockSpec((L, 256), lambda b, g: (row0 + b, B_SSM_Z // 256 + g)),
        pl.BlockSpec((L, 128), lambda b, g: (row0 + b, 0)),
        pl.BlockSpec((1, 1, 4, 512), lambda b, g: (layer, g, 0, 0)),
        pl.BlockSpec((1, 1, 128), lambda b, g: (layer, 0, 0)),
        pl.BlockSpec((1, 1, 128), lambda b, g: (layer, 0, 0)),
        pl.BlockSpec((1, 2, 128, 256), lambda b, g: (g, 0, 0, 0)),
        pl.BlockSpec((1, 2, 128, 512), lambda b, g: (g, 0, 0, 0)),
        pl.BlockSpec((1, 1, 1, 256), lambda b, g: (layer, g, 0, 0)),
        pl.BlockSpec((1, 1, 1, 256), lambda b, g: (layer, g, 0, 0)),
    ]
    args = [PB, PB, PB, PB, SM, cw, dtb, alog, e64, e128, dsk, ng]
    aliases = {}
    out_specs = [pl.BlockSpec((L, 256), lambda b, g: (row0 + b, g))]
    out_shape = [jax.ShapeDtypeStruct((N_TOK, MIX_W), f32)]
    if has_init:
        in_specs.append(pl.BlockSpec((1, 1, 2, 4, 64, 128), lambda b, g: (b, layer, 0, g, 0, 0)))
        args.append(state)
        in_specs.append(pl.BlockSpec(memory_space=pl.ANY))
        args.append(prev)
        aliases = {len(args) - 1: 0}
    else:
        if chain_state:
            in_specs.append(pl.BlockSpec(memory_space=pl.ANY))
            args.append(st_prev)
            aliases = {len(args) - 1: 1}
        out_specs.append(pl.BlockSpec((1, 1, 2, 4, 64, 128), lambda b, g: (b, layer, 0, g, 0, 0)))
        out_shape.append(jax.ShapeDtypeStruct((B, DEPTH, 2, SSM_HEADS, SSM_HEADDIM, SSM_STATE), f32))
    return pl.pallas_call(
        functools.partial(_ssd_body, L=L, has_init=has_init, chain_state=chain_state),
        grid=(B, SSM_GROUPS),
        in_specs=in_specs,
        out_specs=out_specs,
        out_shape=out_shape,
        scratch_shapes=[pltpu.VMEM((L, 512), f32)],
        input_output_aliases=aliases,
        compiler_params=_cparams(("parallel", "parallel")),
        name="ssd_mixer",
    )(*args)


def _s5_body(*refs, n, n_c, s_b, has_init, chain_state):
    it = iter(refs)
    u_ref, wt_ref, wsf_ref, wsb_ref, wof_ref, wob_ref, lam_ref, dsk_ref = [next(it) for _ in range(8)]
    if has_init:
        h0_ref = next(it)
        next(it)
    elif chain_state:
        next(it)
    o_ref = next(it)
    if not has_init:
        hs_ref = next(it)
    zf_s, zb_s, hf_s, hb_s, zfs_s, zbs_s, acc_s, za_s, zc_s = [next(it) for _ in range(9)]
    T = S5_CHUNK

    for j in range(T):
        ub = u_ref[pl.ds(j, n, stride=T), :].astype(bf16)
        a = _dot(ub, wt_ref[0, 0, :, (T - 1 - j) * 128:(2 * T - 1 - j) * 128])
        f = _dot(ub, wsf_ref[0, 0, 0, j])
        b = _dot(ub, wsb_ref[0, 0, 0, j])
        if j == 0:
            acc_s[...] = a
            za_s[...] = f
            zc_s[...] = b
        else:
            acc_s[...] += a
            za_s[...] += f
            zc_s[...] += b
    for kk in range(8):
        zf_s[kk] = za_s[:, kk * 128:(kk + 1) * 128]
        zb_s[kk] = zc_s[:, kk * 128:(kk + 1) * 128]

    lam = lam_ref[0, 0]

    for kk in range(8):
        zfs_s[kk] = pltpu.roll(zf_s[kk], 64, axis=1)
        zbs_s[kk] = pltpu.roll(zb_s[kk], 64, axis=1)

    def rows(c):
        return pl.ds(c, s_b, stride=n_c) if s_b > 1 else pl.ds(c, 1)

    def scan_step(c, h, hs, z_s, zs_s, h_s, la, lb):
        new, news = [], []
        for kk in range(8):
            h_s[kk, rows(c), :] = h[kk]
            sl = slice(kk * 128, (kk + 1) * 128)
            new.append(la[:, sl] * h[kk] + lb[:, sl] * hs[kk] + z_s[kk, rows(c), :])
            news.append(la[:, sl] * hs[kk] - lb[:, sl] * h[kk] + zs_s[kk, rows(c), :])
        return tuple(new), tuple(news)

    def step(t, carry):
        hf, hfs, hb, hbs = carry
        hf, hfs = scan_step(t, hf, hfs, zf_s, zfs_s, hf_s, lam[0:1], lam[1:2])
        hb, hbs = scan_step(n_c - 1 - t, hb, hbs, zb_s, zbs_s, hb_s, lam[2:3], lam[3:4])
        return hf, hfs, hb, hbs

    if has_init:
        h0f = tuple(h0_ref[:, 0, 0, kk * 128:(kk + 1) * 128] for kk in range(8))
        h0b = tuple(h0_ref[:, 0, 1, kk * 128:(kk + 1) * 128] for kk in range(8))
    else:
        h0f = h0b = tuple(jnp.zeros((s_b, 128), f32) for _ in range(8))
    swap = lambda hh: tuple(pltpu.roll(v, 64, axis=1) for v in hh)
    hf, _, hb, _ = lax.fori_loop(0, n_c, step, (h0f, swap(h0f), h0b, swap(h0b)))
    if not has_init:
        for kk in range(8):
            hs_ref[:, 0, 0, kk * 128:(kk + 1) * 128] = hf[kk]
            hs_ref[:, 0, 1, kk * 128:(kk + 1) * 128] = hb[kk]

    hin_f = jnp.concatenate([hf_s[kk] for kk in range(8)], axis=1).astype(bf16)
    hin_b = jnp.concatenate([hb_s[kk] for kk in range(8)], axis=1).astype(bf16)
    acc_s[...] += _dot(hin_f, wof_ref[0, 0, 0])
    acc_s[...] += _dot(hin_b, wob_ref[0, 0, 0])
    dsk = dsk_ref[0, 0]
    for i in range(T):
        o_ref[pl.ds(i, n, stride=T), :] = (acc_s[:, i * 128:(i + 1) * 128]
                                           + dsk * u_ref[pl.ds(i, n, stride=T), :])


def s5_mixer(PF, w, dsk, *, B, L, s_b, row0, layer, state=None, prev=None, st_prev=None):
    has_init = state is not None
    chain_state = st_prev is not None
    n_c = L // S5_CHUNK
    n = s_b * n_c
    rows = s_b * L
    one = pl.Buffered(1)
    in_specs = [
        pl.BlockSpec((rows, 128), lambda lb, sb: (row0 + sb, F_S5_U // 128 + lb)),
        pl.BlockSpec((1, 1, 128, 31 * 128), lambda lb, sb: (layer, lb, 0, 0), pipeline_mode=one),
        pl.BlockSpec((1, 1, 1, 16, 128, 1024), lambda lb, sb: (layer, 0, lb, 0, 0, 0), pipeline_mode=one),
        pl.BlockSpec((1, 1, 1, 16, 128, 1024), lambda lb, sb: (layer, 1, lb, 0, 0, 0), pipeline_mode=one),
        pl.BlockSpec((1, 1, 1, 1024, 2048), lambda lb, sb: (layer, 0, lb, 0, 0), pipeline_mode=one),
        pl.BlockSpec((1, 1, 1, 1024, 2048), lambda lb, sb: (layer, 1, lb, 0, 0), pipeline_mode=one),
        pl.BlockSpec((1, 1, 4, 1024), lambda lb, sb: (layer, lb, 0, 0)),
        pl.BlockSpec((1, 1, 1, 128), lambda lb, sb: (layer, lb, 0, 0)),
    ]
    args = [PF, w["toep"], w["ws"], w["ws"], w["wo"], w["wo"], w["lam16"], dsk]
    aliases = {}
    out_specs = [pl.BlockSpec((rows, 128), lambda lb, sb: (row0 + sb, lb))]
    out_shape = [jax.ShapeDtypeStruct((N_TOK, MIX_W), f32)]
    if has_init:
        in_specs.append(pl.BlockSpec((s_b, 1, 2, 1024), lambda lb, sb: (sb, layer, 0, lb)))
        args.append(state)
        in_specs.append(pl.BlockSpec(memory_space=pl.ANY))
        args.append(prev)
        aliases = {len(args) - 1: 0}
    else:
        if chain_state:
            in_specs.append(pl.BlockSpec(memory_space=pl.ANY))
            args.append(st_prev)
            aliases = {len(args) - 1: 1}
        out_specs.append(pl.BlockSpec((s_b, 1, 2, 1024), lambda lb, sb: (sb, layer, 0, lb)))
        out_shape.append(jax.ShapeDtypeStruct((B, DEPTH, 2, 2 * S5_GROUPS * S5_STATE), f32))
    return pl.pallas_call(
        functools.partial(_s5_body, n=n, n_c=n_c, s_b=s_b, has_init=has_init, chain_state=chain_state),
        grid=(4, B // s_b),
        in_specs=in_specs,
        out_specs=out_specs,
        out_shape=out_shape,
        scratch_shapes=([pltpu.VMEM((8, n, 128), f32)] * 6
                        + [pltpu.VMEM((n, 2048), f32), pltpu.VMEM((n, 1024), f32), pltpu.VMEM((n, 1024), f32)]),
        input_output_aliases=aliases,
        compiler_params=_cparams(("parallel", "parallel")),
        name="s5_mixer",
    )(*args)


def _s5_layer_weights(a_re, a_im, log_dt, b_re, b_im, c_re, c_im):
    hp = lax.Precision.HIGHEST
    T = S5_CHUNK
    dt = jnp.exp(log_dt)[..., None]
    lmag, ang = a_re * dt, a_im * dt
    taus = jnp.arange(T + 1, dtype=f32)[:, None, None, None]
    mag = jnp.exp(lmag[None] * taus)
    pr = mag * jnp.cos(ang[None] * taus)
    pi = mag * jnp.sin(ang[None] * taus)
    lam_r, lam_i = pr[1], pi[1]
    den = a_re * a_re + a_im * a_im
    zr = ((lam_r - 1.0) * a_re + lam_i * a_im) / den
    zi = (lam_i * a_re - (lam_r - 1.0) * a_im) / den
    bb_r = zr[..., None] * b_re[None] - zi[..., None] * b_im[None]
    bb_i = zr[..., None] * b_im[None] + zi[..., None] * b_re[None]
    lb_r = pr[..., None] * bb_r[None] - pi[..., None] * bb_i[None]
    lb_i = pr[..., None] * bb_i[None] + pi[..., None] * bb_r[None]
    pr, pi, lb_r, lb_i = lax.optimization_barrier((pr, pi, lb_r, lb_i))
    kk = jnp.sum(c_re[None, None, :, :, :, None] * lb_r[:, :, :, None, :, :]
                 - c_im[None, None, :, :, :, None] * lb_i[:, :, :, None, :, :], axis=4)
    eye8 = jnp.eye(8, dtype=f32)

    def blockdiag(m):
        lead = m.shape[:-3]
        a, b = m.shape[-2], m.shape[-1]
        m = m.reshape(lead + (4, 8, a, b))
        out = m[..., :, :, :, None, :] * eye8[:, None, :, None]
        return out.reshape(lead + (4, 8 * a, 8 * b))

    kt = jnp.swapaxes(kk, -1, -2)
    down = np.arange(T - 1, 0, -1)
    deltas = jnp.concatenate([kt[down, 1], kt[0:1, 0] + kt[0:1, 1], kt[1:T, 0]], axis=0)
    toep = blockdiag(deltas)
    toep = jnp.transpose(toep, (1, 2, 0, 3)).reshape(4, 128, 31 * 128)

    def state_in(lbr, lbi):
        return jnp.concatenate([jnp.swapaxes(lbr, -1, -2), jnp.swapaxes(lbi, -1, -2)], axis=-1)

    rev16 = np.arange(T - 1, -1, -1)
    ws_c = jnp.stack([state_in(lb_r[rev16, 0], lb_i[rev16, 0]),
                      state_in(lb_r[:T, 1], lb_i[:T, 1])], axis=0)

    def state_out(prd, pid):
        cr = c_re[None] * prd[:, :, None, :] - c_im[None] * pid[:, :, None, :]
        ci = c_re[None] * pid[:, :, None, :] + c_im[None] * prd[:, :, None, :]
        return jnp.concatenate([cr, -ci], axis=-1)

    down16 = np.arange(T, 0, -1)
    wo_c = jnp.stack([state_out(pr[1:T + 1, 0], pi[1:T + 1, 0]),
                      state_out(pr[down16, 1], pi[down16, 1])], axis=0)
    la = jnp.concatenate([pr[T], pr[T]], axis=-1)
    lb = jnp.concatenate([-pi[T], pi[T]], axis=-1)
    lam16 = jnp.stack([la[0], lb[0], la[1], lb[1]], axis=0)
    lam16 = jnp.transpose(lam16.reshape(4, 4, 1024), (1, 0, 2))
    return dict(toep=toep.astype(bf16), ws_c=ws_c, wo_c=wo_c, lam16=lam16)


def _s5_expand_in_body(c_ref, o_ref):
    o_ref[...] = jnp.zeros(o_ref.shape, o_ref.dtype)
    for j in range(S5_CHUNK):
        for g in range(8):
            o_ref[0, 0, 0, j, g * 16:(g + 1) * 16, g * 128:(g + 1) * 128] = c_ref[0, 0, j, g].astype(bf16)


def _s5_expand_out_body(c_ref, e_ref, o_ref):
    for i in range(S5_CHUNK):
        for g in range(8):
            tile = _dot_tn(c_ref[0, 0, i, g].astype(bf16), e_ref[g])
            o_ref[0, 0, 0, g * 128:(g + 1) * 128, i * 128:(i + 1) * 128] = tile.astype(bf16)


def s5_expand_weights(ws_c, wo_c):
    c_spec = pl.BlockSpec((1, 1, S5_CHUNK, 8, S5_CH, 128), lambda l, d, lb: (l, d, 0, lb, 0, 0))
    ws = pl.pallas_call(
        _s5_expand_in_body,
        grid=(DEPTH, 2, 4),
        in_specs=[c_spec],
        out_specs=pl.BlockSpec((1, 1, 1, S5_CHUNK, 128, 1024), lambda l, d, lb: (l, d, lb, 0, 0, 0)),
        out_shape=jax.ShapeDtypeStruct((DEPTH, 2, 4, S5_CHUNK, 128, 1024), bf16),
        compiler_params=_cparams(("parallel", "parallel", "parallel")),
        name="s5_expand_in",
    )(ws_c)
    place = np.zeros((8, S5_CH, 128), np.float32)
    for g in range(8):
        place[g, np.arange(S5_CH), g * S5_CH + np.arange(S5_CH)] = 1.0
    wo = pl.pallas_call(
        _s5_expand_out_body,
        grid=(DEPTH, 2, 4),
        in_specs=[c_spec, pl.BlockSpec((8, S5_CH, 128), lambda l, d, lb: (0, 0, 0))],
        out_specs=pl.BlockSpec((1, 1, 1, 1024, 2048), lambda l, d, lb: (l, d, lb, 0, 0)),
        out_shape=jax.ShapeDtypeStruct((DEPTH, 2, 4, 1024, 2048), bf16),
        compiler_params=_cparams(("parallel", "parallel", "parallel")),
        name="s5_expand_out",
    )(wo_c, jnp.asarray(place, bf16))
    return ws, wo


def _glu_body(y_ref, w_ref, b_ref, o_ref):
    g = jax.nn.gelu(y_ref[...])
    o_ref[...] = g * jax.nn.sigmoid(_dot(g.astype(bf16), w_ref[0]) + b_ref[0])


def s5_glu(y, w, b, layer, tm=512):
    return pl.pallas_call(
        _glu_body,
        grid=(N_TOK // tm,),
        in_specs=[
            pl.BlockSpec((tm, MIX_W), lambda i: (i, 0)),
            pl.BlockSpec((1, MIX_W, MIX_W), lambda i: (layer, 0, 0)),
            pl.BlockSpec((1, 1, MIX_W), lambda i: (layer, 0, 0)),
        ],
        out_specs=pl.BlockSpec((tm, MIX_W), lambda i: (i, 0)),
        out_shape=jax.ShapeDtypeStruct((N_TOK, MIX_W), f32),
        compiler_params=_cparams(("parallel",)),
        name="s5_glu",
    )(y, w, b)


def _merge_body(gp_ref, b0_ref, b1_ref, b2_ref, b3_ref, wb_ref, wo_ref, x_ref, g1_ref, o_ref, acc_s):
    b = pl.program_id(1)
    for bb, br_ref in enumerate((b0_ref, b1_ref, b2_ref, b3_ref)):
        @pl.when(b == bb)
        def _(br_ref=br_ref, bb=bb):
            up = _dot(br_ref[...].astype(bf16), wb_ref[0, bb])
            t = (0.5 * jnp.tanh(0.5 * gp_ref[...].astype(f32)) + 0.5) * up
            if bb == 0:
                acc_s[...] = t
            else:
                acc_s[...] += t

    @pl.when(b == 3)
    def _():
        mix = _dot(acc_s[...].astype(bf16), wo_ref[0])
        o_ref[...] = x_ref[...] + g1_ref[0, 0] * mix


def merge_branches(PB, branches, w_branch, w_out, x, mod, layer, tm=512):
    br_spec = pl.BlockSpec((tm, MIX_W), lambda i, b: (i, 0))
    return pl.pallas_call(
        _merge_body,
        grid=(N_TOK // tm, 4),
        in_specs=[
            pl.BlockSpec((tm, D), lambda i, b: (i, B_GATE // D + b)),
            br_spec, br_spec, br_spec, br_spec,
            pl.BlockSpec((1, 4, MIX_W, D), lambda i, b: (layer, 0, 0, 0), pipeline_mode=pl.Buffered(1)),
            pl.BlockSpec((1, D, D), lambda i, b: (layer, 0, 0), pipeline_mode=pl.Buffered(1)),
            pl.BlockSpec((tm, D), lambda i, b: (i, 0)),
            _mod_spec(layer, 2, tm),
        ],
        out_specs=pl.BlockSpec((tm, D), lambda i, b: (i, 0)),
        out_shape=jax.ShapeDtypeStruct((N_TOK, D), f32),
        scratch_shapes=[pltpu.VMEM((tm, D), f32)],
        compiler_params=_cparams(("parallel", "arbitrary")),
        name="merge_branches",
    )(PB, *branches, w_branch, w_out, x, mod)


def _ffn_body(*refs, n_e, steps_per_e, final_norm):
    it = iter(refs)
    x_ref, sh_ref, sc_ref, gt_ref, g_ref, w1_ref, w3_ref, w2_ref = [next(it) for _ in range(8)]
    if n_e > 1:
        rwh_ref, rwl_ref, rb_ref = [next(it) for _ in range(3)]
    fg_ref = next(it) if final_norm else None
    o_ref = next(it)
    h_s = next(it)
    acc_s = o_ref
    gate_s = next(it) if n_e > 1 else None
    j = pl.program_id(1)
    nj = pl.num_programs(1)

    @pl.when(j == 0)
    def _():
        h = _modnorm(x_ref[...], g_ref[0], sc_ref[0, 0], sh_ref[0, 0])
        h_s[...] = h.astype(bf16)
        if n_e > 1:
            logits = _dot_hi(h, rwh_ref[0], rwl_ref[0]) + rb_ref[0]
            lane = lax.broadcasted_iota(jnp.int32, logits.shape, 1)
            neg = jnp.float32(-jnp.inf)
            lv = jnp.where(lane < n_e, logits, neg)
            m1 = jnp.max(lv, axis=1, keepdims=True)
            i1 = jnp.min(jnp.where(lv == m1, lane, 128), axis=1, keepdims=True)
            lv2 = jnp.where(lane == i1, neg, lv)
            m2 = jnp.max(lv2, axis=1, keepdims=True)
            i2 = jnp.min(jnp.where(lv2 == m2, lane, 128), axis=1, keepdims=True)
            e2 = jnp.exp(m2 - m1)
            p1 = 1.0 / (1.0 + e2)
            gate_s[...] = jnp.where(lane == i1, p1, 0.0) + jnp.where(lane == i2, e2 * p1, 0.0)

    hb = h_s[...]
    hid = _silu(_dot(hb, w1_ref[0, 0])) * _dot(hb, w3_ref[0, 0])
    if n_e > 1:
        e = j // steps_per_e
        lane = lax.broadcasted_iota(jnp.int32, gate_s.shape, 1)
        hid = hid * jnp.sum(jnp.where(lane == e, gate_s[...], 0.0), axis=1, keepdims=True)
    hidb = hid.astype(bf16)
    cb = 512

    @pl.when(j == 0)
    def _():
        for c0 in range(0, D, cb):
            acc_s[:, c0:c0 + cb] = _dot(hidb, w2_ref[0, 0, :, c0:c0 + cb])

    @pl.when(j > 0)
    def _():
        for c0 in range(0, D, cb):
            acc_s[:, c0:c0 + cb] += _dot(hidb, w2_ref[0, 0, :, c0:c0 + cb])

    @pl.when(j == nj - 1)
    def _():
        y = x_ref[...] + gt_ref[0, 0] * acc_s[...]
        if final_norm:
            ms = jnp.mean(y * y, axis=-1, keepdims=True)
            y = y * lax.rsqrt(ms + EPS) * fg_ref[...]
        o_ref[...] = y


def ffn_layer(x, mod, g, w1, w3, w2, layer, widx, router=None, final_g=None, tm=1024, tf=512):
    n_e, f = w1.shape[1], w1.shape[3]
    spe = f // tf
    final = final_g is not None
    in_specs = [
        pl.BlockSpec((tm, D), lambda i, j: (i, 0), pipeline_mode=pl.Buffered(1)),
        _mod_spec(layer, 3, tm),
        _mod_spec(layer, 4, tm),
        _mod_spec(layer, 5, tm),
        pl.BlockSpec((1, 1, D), lambda i, j: (layer, 0, 0)),
        pl.BlockSpec((1, 1, D, tf), lambda i, j: (widx, j // spe, 0, j % spe)),
        pl.BlockSpec((1, 1, D, tf), lambda i, j: (widx, j // spe, 0, j % spe)),
        pl.BlockSpec((1, 1, tf, D), lambda i, j: (widx, j // spe, j % spe, 0)),
    ]
    args = [x, mod, mod, mod, g, w1, w3, w2]
    scratch = [pltpu.VMEM((tm, D), bf16)]
    if n_e > 1:
        in_specs += [pl.BlockSpec((1, D, 128), lambda i, j: (widx, 0, 0)),
                     pl.BlockSpec((1, D, 128), lambda i, j: (widx, 0, 0)),
                     pl.BlockSpec((1, 1, 128), lambda i, j: (widx, 0, 0))]
        args += list(router)
        scratch.append(pltpu.VMEM((tm, 128), f32))
    if final:
        in_specs.append(pl.BlockSpec((1, D), lambda i, j: (0, 0)))
        args.append(final_g.reshape(1, D))
    return pl.pallas_call(
        functools.partial(_ffn_body, n_e=n_e, steps_per_e=spe, final_norm=final),
        grid=(N_TOK // tm, n_e * spe),
        in_specs=in_specs,
        out_specs=pl.BlockSpec((tm, D), lambda i, j: (i, 0)),
        out_shape=jax.ShapeDtypeStruct((N_TOK, D), f32),
        scratch_shapes=scratch,
        compiler_params=_cparams(("parallel", "arbitrary")),
        name="ffn_moe" if n_e > 1 else "ffn_dense",
    )(*args)


def _split_hi_lo(w):
    hi = w.astype(bf16)
    lo = (w - hi.astype(f32)).astype(bf16)
    return hi, lo


_W_IN_SEGMENTS = ((2080, 3104), (5680, 6192), (6192, 14384), (0, 1536), (1568, 2080), (3104, 5664))
_W_IN_SMALL = ((1536, 1568), (5664, 5680))
_W_IN_COLS = 14384


def _reorder_body(wt_ref, o_ref, os_ref):
    c = 0
    for a, b in _W_IN_SEGMENTS:
        for r in range(a, b, 128):
            o_ref[0, :, c:c + 128] = wt_ref[0, r:r + 128, :].T.astype(bf16)
            c += 128
    rows = [wt_ref[0, a:b, :] for a, b in _W_IN_SMALL]
    rows.append(jnp.zeros((128 - sum(r.shape[0] for r in rows), 128), f32))
    os_ref[0] = jnp.concatenate(rows, axis=0).T.astype(bf16)


def _reorder_w_in(w):
    wt = jnp.swapaxes(w, 1, 2)
    return pl.pallas_call(
        _reorder_body,
        grid=(DEPTH, D // 128),
        in_specs=[pl.BlockSpec((1, _W_IN_COLS, 128), lambda l, r: (l, 0, r))],
        out_specs=[pl.BlockSpec((1, 128, N_PF + N_PB), lambda l, r: (l, r, 0)),
                   pl.BlockSpec((1, 128, 128), lambda l, r: (l, r, 0))],
        out_shape=[jax.ShapeDtypeStruct((DEPTH, D, N_PF + N_PB), bf16),
                   jax.ShapeDtypeStruct((DEPTH, D, 128), bf16)],
        compiler_params=_cparams(("parallel", "parallel")),
        name="reorder_w_in",
    )(wt)


def _gla_gate_weights(wa2, ba):
    t = wa2.reshape(DEPTH, 2, GLA_RANK, 2, 128)
    w = jnp.einsum("ldrhc,de->lhdrec", t, jnp.eye(2, dtype=f32))
    w = w.reshape(DEPTH, 2, 2 * GLA_RANK, 256)
    w = jnp.pad(w, ((0, 0), (0, 0), (SM_LR, 128 - SM_LR - 2 * GLA_RANK), (0, 0)))
    b = jnp.transpose(ba.reshape(DEPTH, 2, 2, 128), (0, 2, 1, 3)).reshape(DEPTH, 2, 1, 256)
    hi, lo = _split_hi_lo(w)
    return hi, lo, b


def _hgrn_bounds(logits):
    p = jax.nn.softmax(logits.astype(f32), axis=1)
    lower = jnp.maximum(jnp.cumsum(p, axis=1) - p[:, :1], 0.0)
    lb = jnp.transpose(lower, (1, 0, 2)).reshape(DEPTH, 2, HGRN_HEADS, HGRN_DIM)
    rows = jnp.stack([lb, 1.0 - lb], axis=3)
    rows = jnp.transpose(rows, (0, 2, 1, 3, 4)).reshape(DEPTH, HGRN_HEADS, 4, HGRN_DIM)
    return jnp.pad(rows, ((0, 0), (0, 0), (0, 4), (0, 0)))


def _ssd_select_mats():
    e64 = np.zeros((2, 2, 128, 256), np.float32)
    e128 = np.zeros((2, 2, 128, 512), np.float32)
    for g in range(2):
        for d in range(2):
            for h in range(4):
                lane = SM_DT + d * 8 + g * 4 + h
                e64[g, d, lane, h * 64:(h + 1) * 64] = 1.0
                e128[g, d, lane, h * 128:(h + 1) * 128] = 1.0
    return jnp.asarray(e64, bf16), jnp.asarray(e128, bf16)


def _ssd_params(conv_w, conv_b, dt_bias, a_log, d_skip, norm_g):
    cwb = jnp.concatenate([conv_w, conv_b[:, None]], axis=1)
    cw = jnp.concatenate([cwb[..., 0:512].reshape(DEPTH, 4, 2, 256),
                          cwb[..., 512:768].reshape(DEPTH, 4, 2, 128),
                          cwb[..., 768:1024].reshape(DEPTH, 4, 2, 128)], axis=-1)
    cw = jnp.transpose(cw, (0, 2, 1, 3))
    lanes = ((0, 0), (SM_DT, 128 - SM_DT - 16))
    dtb = jnp.pad(dt_bias.reshape(DEPTH, 16), lanes).reshape(DEPTH, 1, 128)
    alog = jnp.pad(a_log.reshape(DEPTH, 16), lanes).reshape(DEPTH, 1, 128)
    dsk = jnp.repeat(d_skip, SSM_HEADDIM, axis=1).reshape(DEPTH, 2, 1, 256)
    return cw, dtb, alog, dsk, norm_g.reshape(DEPTH, 2, 1, 256)


def kernel(x_prompt, x_sample, c, c_ctx, state_gla, state_hgrn, state_ssm, state_s5_re, state_s5_im, norm1_g, norm2_g, ada_w, ada_b, w_in, gla_wa2, gla_ba, gla_norm_g, hgrn_lb_logits, hgrn_norm_g, ssm_conv_w, ssm_conv_b, ssm_a_log, ssm_dt_bias, ssm_d, ssm_norm_g, s5_a_re, s5_a_im, s5_log_dt, s5_b_re, s5_b_im, s5_c_re, s5_c_im, s5_d, s5_glu_w, s5_glu_b, w_branch, w_out, ffn_w1, ffn_w3, ffn_w2, router_w, router_b, moe_w1, moe_w3, moe_w2, final_norm_g):
    w_all, w_small = _reorder_w_in(w_in)
    wah, wal, gba = _gla_gate_weights(gla_wa2, gla_ba)
    gng = gla_norm_g.reshape(DEPTH, 1, GLA_DV)
    lbp = _hgrn_bounds(hgrn_lb_logits)
    hng = hgrn_norm_g.reshape(DEPTH, 1, HGRN_DIM)
    e64, e128 = _ssd_select_mats()
    cw, dtb, alog, dsk, sng = _ssd_params(ssm_conv_w, ssm_conv_b, ssm_dt_bias, ssm_a_log, ssm_d, ssm_norm_g)
    w5 = jax.vmap(_s5_layer_weights)(s5_a_re, s5_a_im, s5_log_dt, s5_b_re, s5_b_im, s5_c_re, s5_c_im)
    w5["ws"], w5["wo"] = s5_expand_weights(w5["ws_c"], w5["wo_c"])
    d5 = s5_d.reshape(DEPTH, 4, 1, 128)
    glu_w = s5_glu_w.astype(bf16)
    glu_b = s5_glu_b.reshape(DEPTH, 1, MIX_W)
    wbr = w_branch.astype(bf16)
    wout = w_out.astype(bf16)
    n1 = norm1_g.reshape(DEPTH, 1, D)
    n2 = norm2_g.reshape(DEPTH, 1, D)
    fw1, fw3, fw2 = ffn_w1.astype(bf16)[:, None], ffn_w3.astype(bf16)[:, None], ffn_w2.astype(bf16)[:, None]
    mw1, mw3, mw2 = moe_w1.astype(bf16), moe_w3.astype(bf16), moe_w2.astype(bf16)
    rw = jnp.pad(router_w, ((0, 0), (0, 0), (0, 128 - N_EXPERTS)))
    router = _split_hi_lo(rw) + (jnp.pad(router_b, ((0, 0), (0, 128 - N_EXPERTS))).reshape(-1, 1, 128),)
    s5_state = jnp.concatenate([state_s5_re, state_s5_im], axis=-1).reshape(
        DEC_BATCH, DEPTH, 2, 2 * S5_GROUPS * S5_STATE)

    x = assemble_tokens(x_prompt.reshape(N_CTX, D), x_sample.reshape(N_LAT, D), _pos_table())
    cond8 = jnp.concatenate([c_ctx[None], c, jnp.zeros((MOD_ROWS - 1 - DEC_BATCH, D), f32)], axis=0)
    mod = ada_modulation(cond8, ada_w, ada_b).reshape(DEPTH, MOD_ROWS * 6, 1, D)

    lat0 = N_CTX // DEC_SEQ
    ctx = dict(B=BATCH, L=SEQ, row0=0)
    lat = dict(B=DEC_BATCH, L=DEC_SEQ, row0=lat0)
    st_gla = st_hg = st_ssm = st_s5 = None
    for l in range(DEPTH):
        PF, PB, SM = in_projection(x, mod, n1, w_all, w_small, l)

        o_gla, st_gla = gla_mixer(SM, PB, wah, wal, gba, gng, layer=l, st_prev=st_gla, **ctx)
        (o_gla,) = gla_mixer(SM, PB, wah, wal, gba, gng, layer=l, state=state_gla, prev=o_gla, **lat)

        o_hg, st_hg = hgrn_mixer(PF, PB, lbp, hng, layer=l, st_prev=st_hg, **ctx)
        (o_hg,) = hgrn_mixer(PF, PB, lbp, hng, layer=l, state=state_hgrn, prev=o_hg, **lat)

        o_ssm, st_ssm = ssd_mixer(SM, PB, cw, dtb, alog, e64, e128, dsk, sng, layer=l, st_prev=st_ssm, **ctx)
        (o_ssm,) = ssd_mixer(SM, PB, cw, dtb, alog, e64, e128, dsk, sng, layer=l, state=state_ssm,
                             prev=o_ssm, **lat)

        y5, st_s5 = s5_mixer(PF, w5, d5, B=BATCH, L=SEQ, s_b=min(8, BATCH), row0=0, layer=l, st_prev=st_s5)
        (y5,) = s5_mixer(PF, w5, d5, B=DEC_BATCH, L=DEC_SEQ, s_b=1, row0=lat0, layer=l,
                         state=s5_state, prev=y5)
        o_s5 = s5_glu(y5, glu_w, glu_b, l)

        x = merge_branches(PB, (o_gla, o_hg, o_ssm, o_s5), wbr, wout, x, mod, l)

        last = l == DEPTH - 1
        kw = dict(final_g=final_norm_g) if last else {}
        if l % 2 == 0:
            x = ffn_layer(x, mod, n2, fw1, fw3, fw2, l, l // 2, **kw)
        else:
            x = ffn_layer(x, mod, n2, mw1, mw3, mw2, l, l // 2, router=router, **kw)

    y_ctx, y_lat = x[:N_CTX], x[N_CTX:]
    st_s5 = st_s5.reshape(BATCH, DEPTH, 2, S5_GROUPS, 2 * S5_STATE)
    return (y_ctx.reshape(BATCH, SEQ, D), y_lat.reshape(DEC_BATCH, DEC_SEQ, D),
            st_gla, st_hg, st_ssm, st_s5[..., :S5_STATE], st_s5[..., S5_STATE:])
```

```python
import functools

import numpy as np
import jax
import jax.numpy as jnp
from jax import lax
from jax.experimental import pallas as pl
from jax.experimental.pallas import tpu as pltpu

f32 = jnp.float32
bf16 = jnp.bfloat16

D = 2048
BATCH = 32
SEQ = 256
DEPTH = 4
DEC_BATCH = 4
DEC_SEQ = 4096
GRID_W = 64
MIX_W = 512
GLA_HEADS = 4
GLA_DK = 64
GLA_DV = 128
GLA_RANK = 16
GLA_NORMALIZER = 16.0
HGRN_HEADS = 4
HGRN_DIM = 128
SSM_HEADS = 8
SSM_HEADDIM = 64
SSM_GROUPS = 2
SSM_STATE = 128
S5_CH = 16
S5_STATE = 64
S5_GROUPS = 32
D_FF = 5632
N_EXPERTS = 8
D_FF_EXPERT = 1024
EPS = 1e-6

N_CTX = BATCH * SEQ
N_LAT = DEC_BATCH * DEC_SEQ
N_TOK = N_CTX + N_LAT
MOD_ROWS = 8

F_HG_F, F_S5_U = 0, 1024
N_PF = 1536
B_GATE, B_GLA_Q, B_GLA_K, B_GLA_V, B_GLA_R = 0, 8192, 8448, 8704, 9216
B_HG_Q, B_HG_I, B_HG_G, B_SSM_Z, B_SSM_XBC = 9728, 10240, 10752, 11264, 11776
N_PB = 12800
SM_LR, SM_DT = 0, 32

GLS_CHUNK = 128
GLS_SUB = 32
SSD_CHUNK = 128
S5_CHUNK = 16

VMEM_LIMIT = 58 * 1024 * 1024


def _cparams(sem):
    return pltpu.CompilerParams(dimension_semantics=sem, vmem_limit_bytes=VMEM_LIMIT)


def _dot(a, b):
    return jnp.dot(a, b, preferred_element_type=f32)


def _dot_nt(a, b):
    return lax.dot_general(a, b, (((1,), (1,)), ((), ())), preferred_element_type=f32)


def _dot_tn(a, b):
    return lax.dot_general(a, b, (((0,), (0,)), ((), ())), preferred_element_type=f32)


def _split2(x):
    hi = x.astype(bf16)
    return hi, (x - hi.astype(f32)).astype(bf16)


def _dot_sel_lhs(t, x):
    x1, x2 = _split2(x)
    return _dot(t, x1) + _dot(t, x2)


def _dot_sel_rhs(x, e):
    x1, x2 = _split2(x)
    return _dot(x1, e) + _dot(x2, e)


def _dot_sel_rhs3(x, e):
    x1, x2 = _split2(x)
    x3 = (x - x1.astype(f32) - x2.astype(f32)).astype(bf16)
    return _dot(x1, e) + _dot(x2, e) + _dot(x3, e)


def _dot_hi(x, w_hi, w_lo):
    x1, x2 = _split2(x)
    return _dot(x1, w_hi) + _dot(x2, w_hi) + _dot(x1, w_lo)


def _log_sigmoid(x):
    return jnp.minimum(x, 0.0) - jnp.log1p(jnp.exp(-jnp.abs(x)))


def _softplus(x):
    return jnp.maximum(x, 0.0) + jnp.log1p(jnp.exp(-jnp.abs(x)))


def _silu(x):
    return x * jax.nn.sigmoid(x)


def _modnorm(x, g, scale, shift):
    ms = jnp.mean(x * x, axis=-1, keepdims=True)
    y = x * lax.rsqrt(ms + EPS) * g
    return y * (1.0 + scale) + shift


def _tri(n, upper):
    ii = lax.broadcasted_iota(jnp.int32, (n, n), 0)
    jj = lax.broadcasted_iota(jnp.int32, (n, n), 1)
    return (ii <= jj) if upper else (ii >= jj)


def _mod_row(i, tm):
    start = i * tm
    return jnp.where(start < N_CTX, 0, 1 + (start - N_CTX) // DEC_SEQ)


def _mod_spec(layer, k, tm, tile0=0):
    return pl.BlockSpec((1, 1, 1, D), lambda i, j: (layer, _mod_row(tile0 + i, tm) * 6 + k, 0, 0))


def _assemble_body(xp_ref, xs_ref, pos_ref, o_ref, *, n_ctx_tiles):
    i = pl.program_id(0)

    @pl.when(i < n_ctx_tiles)
    def _():
        o_ref[...] = xp_ref[...]

    @pl.when(i >= n_ctx_tiles)
    def _():
        o_ref[...] = xs_ref[...] + pos_ref[...]


def assemble_tokens(xp2, xs2, pos, tm=512):
    nct = N_CTX // tm
    npos = DEC_SEQ // tm
    return pl.pallas_call(
        functools.partial(_assemble_body, n_ctx_tiles=nct),
        grid=(N_TOK // tm,),
        in_specs=[
            pl.BlockSpec((tm, D), lambda i: (jnp.minimum(i, nct - 1), 0)),
            pl.BlockSpec((tm, D), lambda i: (jnp.maximum(i - nct, 0), 0)),
            pl.BlockSpec((tm, D), lambda i: (jnp.maximum(i - nct, 0) % npos, 0)),
        ],
        out_specs=pl.BlockSpec((tm, D), lambda i: (i, 0)),
        out_shape=jax.ShapeDtypeStruct((N_TOK, D), f32),
        compiler_params=_cparams(("parallel",)),
        name="assemble_tokens",
    )(xp2, xs2, pos)


def _pos_table():
    rows = DEC_SEQ // GRID_W
    row = jnp.repeat(jnp.arange(rows, dtype=f32), GRID_W)
    col = jnp.tile(jnp.arange(GRID_W, dtype=f32), rows)

    def sincos(p, d):
        half = d // 2
        omega = 1.0 / (10000.0 ** (jnp.arange(half, dtype=f32) / half))
        ang = p[:, None] * omega[None, :]
        return jnp.concatenate([jnp.sin(ang), jnp.cos(ang)], axis=-1)

    return jnp.concatenate([sincos(row, D // 2), sincos(col, D // 2)], axis=-1)


def _ada_body(c_ref, w_ref, b_ref, o_ref):
    s = _silu(c_ref[...])
    s1, s2 = _split2(s)
    w = w_ref[0].astype(bf16)
    o_ref[0] = _dot(s1, w) + _dot(s2, w) + b_ref[0]


def ada_modulation(cond8, ada_w, ada_b, tn=1024):
    return pl.pallas_call(
        _ada_body,
        grid=(DEPTH, 6 * D // tn),
        in_specs=[
            pl.BlockSpec((MOD_ROWS, D), lambda l, j: (0, 0)),
            pl.BlockSpec((1, D, tn), lambda l, j: (l, 0, j)),
            pl.BlockSpec((1, 1, tn), lambda l, j: (l, 0, j)),
        ],
        out_specs=pl.BlockSpec((1, MOD_ROWS, tn), lambda l, j: (l, 0, j)),
        out_shape=jax.ShapeDtypeStruct((DEPTH, MOD_ROWS, 6 * D), f32),
        compiler_params=_cparams(("parallel", "parallel")),
        name="ada_modulation",
    )(cond8, ada_w, ada_b.reshape(DEPTH, 1, 6 * D))


def _inproj_body(x_ref, sh_ref, sc_ref, g_ref, w_ref, ws_ref, pf_ref, pb_ref, sm_ref, h_s, *, nf):
    j = pl.program_id(1)

    @pl.when(j == 0)
    def _():
        h = _modnorm(x_ref[...], g_ref[0], sc_ref[0, 0], sh_ref[0, 0]).astype(bf16)
        h_s[...] = h
        sm_ref[...] = _dot(h, ws_ref[0])

    y = _dot(h_s[...], w_ref[0])

    @pl.when(j < nf)
    def _():
        pf_ref[...] = y

    @pl.when(j >= nf)
    def _():
        pb_ref[...] = y.astype(bf16)


def in_projection(x, mod, g, w_all, w_small, layer, tm=1024, tn=512):
    nf = N_PF // tn
    nb = N_PB // tn
    return pl.pallas_call(
        functools.partial(_inproj_body, nf=nf),
        grid=(N_TOK // tm, nf + nb),
        in_specs=[
            pl.BlockSpec((tm, D), lambda i, j: (i, 0)),
            _mod_spec(layer, 0, tm),
            _mod_spec(layer, 1, tm),
            pl.BlockSpec((1, 1, D), lambda i, j: (layer, 0, 0)),
            pl.BlockSpec((1, D, tn), lambda i, j: (layer, 0, j)),
            pl.BlockSpec((1, D, 128), lambda i, j: (layer, 0, 0)),
        ],
        out_specs=[
            pl.BlockSpec((tm, tn), lambda i, j: (i, jnp.minimum(j, nf - 1))),
            pl.BlockSpec((tm, tn), lambda i, j: (i, jnp.maximum(j - nf, 0))),
            pl.BlockSpec((tm, 128), lambda i, j: (i, 0)),
        ],
        out_shape=[
            jax.ShapeDtypeStruct((N_TOK, N_PF), f32),
            jax.ShapeDtypeStruct((N_TOK, N_PB), bf16),
            jax.ShapeDtypeStruct((N_TOK, 128), f32),
        ],
        scratch_shapes=[pltpu.VMEM((tm, D), bf16)],
        compiler_params=_cparams(("parallel", "arbitrary")),
        name="in_projection",
    )(x, mod, mod, g, w_all, w_small)


def _gls_dir(q, k, g, d):
    c, sub = GLS_CHUNK, GLS_SUB
    nb = c // sub
    tri = jnp.where(_tri(c, upper=(d == 1)), 1.0, 0.0).astype(bf16)
    b = _dot_sel_lhs(tri, g)
    zero = jnp.zeros((1, 128), f32)
    if d == 0:
        mid = [b[sub * i + sub // 2 - 1:sub * i + sub // 2] for i in range(nb)]
        end = [b[sub * i + sub - 1:sub * i + sub] for i in range(nb)]
        start = [zero] + end[:-1]
        total = end[-1]
    else:
        mid = [b[sub * i + sub // 2:sub * i + sub // 2 + 1] for i in range(nb)]
        end = [b[sub * i:sub * i + 1] for i in range(nb)]
        start = end[1:] + [zero]
        total = end[0]

    def rows(vals):
        return jnp.concatenate([jnp.broadcast_to(v, (sub, 128)) for v in vals], axis=0)

    r_mid = rows(mid)
    qd = q * jnp.exp(b - r_mid)
    kd = k * jnp.exp(r_mid - b)
    qo = qd * rows([jnp.exp(m - s) for m, s in zip(mid, start)])
    ke = kd * rows([jnp.exp(e - m) for m, e in zip(mid, end)])
    q_in = qo * rows([jnp.exp(s) for s in start])
    k_out = ke * rows([jnp.exp(total - e) for e in end])
    kr = []
    for i in range(nb):
        prior = range(i) if d == 0 else range(i + 1, nb)
        if len(prior) == 0:
            kr.append(None)
            continue
        blocks = []
        for jb in range(nb):
            if jb in prior:
                blocks.append(ke[jb * sub:(jb + 1) * sub] * jnp.exp(start[i] - end[jb]))
            else:
                blocks.append(jnp.zeros((sub, 128), f32))
        kr.append(jnp.concatenate(blocks, axis=0).astype(bf16))
    return qd, kd.astype(bf16), qo, kr, q_in, k_out, jnp.exp(total)


def _gls_head(parts, v, st, mask, kmask):
    qd, kd, qo, kr, q_in, k_out, dec = parts
    sub = GLS_SUB
    if kmask is not None:
        qd, qo, q_in, k_out = qd * kmask, qo * kmask, q_in * kmask, k_out * kmask
    att = jnp.where(mask, _dot_nt(qd.astype(bf16), kd), 0.0)
    qob = qo.astype(bf16)
    offs = []
    for i, kri in enumerate(kr):
        if kri is None:
            offs.append(jnp.zeros((sub, GLS_CHUNK), f32))
        else:
            offs.append(_dot_nt(qob[i * sub:(i + 1) * sub], kri))
    att = att + jnp.concatenate(offs, axis=0)
    vb = v.astype(bf16)
    o = _dot(att.astype(bf16), vb) + _dot_nt(q_in.astype(bf16), st.astype(bf16))
    st_new = st * dec + _dot_tn(vb, k_out.astype(bf16))
    return o, st_new


def _gls_masks():
    c, sub = GLS_CHUNK, GLS_SUB
    ii = lax.broadcasted_iota(jnp.int32, (c, c), 0)
    jj = lax.broadcasted_iota(jnp.int32, (c, c), 1)
    same = (ii // sub) == (jj // sub)
    return [same & (ii >= jj), same & (ii <= jj)]


def _head_post(o, gate_in, ng):
    ms = jnp.mean(o * o, axis=-1, keepdims=True)
    return o * lax.rsqrt(ms + EPS) * ng * _silu(gate_in.astype(f32))


def _scan_loops(n, nseq, chunk_fn, post_fn, o_ref, init):
    C = GLS_CHUNK

    def store(r0, outs, final):
        for h, o in enumerate(outs):
            sl = (pl.ds(r0, C), slice(h * 128, (h + 1) * 128))
            if final:
                o_ref[sl] = post_fn(r0, h, o_ref[sl] + o)
            else:
                o_ref[sl] = o

    def half(final):
        def body(t, carry):
            new = []
            for s, (sf, sb) in enumerate(carry):
                r0, of, sf = chunk_fn(s * n + t, 0, sf)
                store(r0, of, final)
                r1, ob, sb = chunk_fn(s * n + n - 1 - t, 1, sb)
                store(r1, ob, final)
                new.append((sf, sb))
            return tuple(new)
        return body

    unroll = 2 if (n // 2) % 2 == 0 else 1
    carry = lax.fori_loop(0, n // 2, half(False), tuple(init), unroll=unroll)
    return lax.fori_loop(n // 2, n, half(True), carry, unroll=unroll)


def _gla_body(*refs, L, nseq, has_init, chain_state):
    it = iter(refs)
    q_ref, k_ref, v_ref, r_ref, sm_ref, wah_ref, wal_ref, ba_ref, ng_ref = [next(it) for _ in range(9)]
    s0_ref = next(it) if has_init else None
    if has_init or chain_state:
        next(it)
    o_ref = next(it)
    st_ref = None if has_init else next(it)
    C = GLS_CHUNK
    n = L // C
    lane = lax.broadcasted_iota(jnp.int32, (1, 128), 1)
    kmasks = [jnp.where(lane < 64, 1.0, 0.0), jnp.where(lane >= 64, 1.0, 0.0)]
    masks = _gls_masks()
    wah, wal, ba, ng = wah_ref[0, 0], wal_ref[0, 0], ba_ref[0, 0], ng_ref[0]

    def chunk(cidx, d, sts):
        r0 = pl.multiple_of(cidx * C, C)
        q = q_ref[pl.ds(r0, C), :].astype(f32) * (GLA_DK ** -0.5)
        k = k_ref[pl.ds(r0, C), :].astype(f32)
        sm = sm_ref[pl.ds(r0, C), :]
        sl = slice(d * 128, (d + 1) * 128)
        g = _log_sigmoid(_dot_hi(sm, wah[:, sl], wal[:, sl]) + ba[:, sl]) * (1.0 / GLA_NORMALIZER)
        parts = _gls_dir(q, k, g, d)
        outs, new = [], []
        for h in range(2):
            v = v_ref[pl.ds(r0, C), h * 128:(h + 1) * 128]
            o, s = _gls_head(parts, v, sts[h], masks[d], kmasks[h])
            outs.append(o)
            new.append(s)
        return r0, outs, new

    def post(r0, h, tot):
        return _head_post(tot, r_ref[pl.ds(r0, C), h * 128:(h + 1) * 128], ng)

    def init(d, h):
        if not has_init:
            return jnp.zeros((128, 128), f32)
        s = s0_ref[0, 0, d, h]
        z = jnp.zeros((64, 128), f32)
        full = jnp.concatenate([s, z], axis=0) if h == 0 else jnp.concatenate([z, s], axis=0)
        return full.T

    states = _scan_loops(n, nseq, chunk, post, o_ref,
                         [([init(0, 0), init(0, 1)], [init(1, 0), init(1, 1)]) for _ in range(nseq)])
    if st_ref is not None:
        for s, dirs in enumerate(states):
            for d, sts in enumerate(dirs):
                for h in range(2):
                    st_ref[s, 0, d, h] = sts[h].T[h * 64:(h + 1) * 64, :]


def gla_mixer(SM, PB, wa_hi, wa_lo, ba, ng, *, B, L, row0, layer, nseq=1, state=None, prev=None, st_prev=None):
    has_init = state is not None
    chain_state = st_prev is not None
    R = nseq * L
    in_specs = [
        pl.BlockSpec((R, 128), lambda b, hp: (row0 + b, B_GLA_Q // 128 + hp)),
        pl.BlockSpec((R, 128), lambda b, hp: (row0 + b, B_GLA_K // 128 + hp)),
        pl.BlockSpec((R, 256), lambda b, hp: (row0 + b, B_GLA_V // 256 + hp)),
        pl.BlockSpec((R, 256), lambda b, hp: (row0 + b, B_GLA_R // 256 + hp)),
        pl.BlockSpec((R, 128), lambda b, hp: (row0 + b, 0)),
        pl.BlockSpec((1, 1, 128, 256), lambda b, hp: (layer, hp, 0, 0)),
        pl.BlockSpec((1, 1, 128, 256), lambda b, hp: (layer, hp, 0, 0)),
        pl.BlockSpec((1, 1, 1, 256), lambda b, hp: (layer, hp, 0, 0)),
        pl.BlockSpec((1, 1, 128), lambda b, hp: (layer, 0, 0)),
    ]
    args = [PB, PB, PB, PB, SM, wa_hi, wa_lo, ba, ng]
    aliases = {}
    out_specs = [pl.BlockSpec((R, 256), lambda b, hp: (row0 + b, hp))]
    out_shape = [jax.ShapeDtypeStruct((N_TOK, MIX_W), f32)]
    if has_init:
        assert nseq == 1
        in_specs.append(pl.BlockSpec((1, 1, 2, 2, 64, 128), lambda b, hp: (b, layer, 0, hp, 0, 0)))
        args.append(state)
        in_specs.append(pl.BlockSpec(memory_space=pl.ANY))
        args.append(prev)
        aliases = {len(args) - 1: 0}
    else:
        if chain_state:
            in_specs.append(pl.BlockSpec(memory_space=pl.ANY))
            args.append(st_prev)
            aliases = {len(args) - 1: 1}
        out_specs.append(pl.BlockSpec((nseq, 1, 2, 2, 64, 128), lambda b, hp: (b, layer, 0, hp, 0, 0)))
        out_shape.append(jax.ShapeDtypeStruct((B, DEPTH, 2, GLA_HEADS, GLA_DK, GLA_DV), f32))
    return pl.pallas_call(
        functools.partial(_gla_body, L=L, nseq=nseq, has_init=has_init, chain_state=chain_state),
        grid=(B // nseq, 2),
        in_specs=in_specs,
        out_specs=out_specs,
        out_shape=out_shape,
        input_output_aliases=aliases,
        compiler_params=_cparams(("parallel", "parallel")),
        name="gla_mixer",
    )(*args)


def _hgrn_body(*refs, L, nseq, has_init, chain_state):
    it = iter(refs)
    q_ref, f0_ref, f1_ref, i_ref, g_ref, lb_ref, ng_ref = [next(it) for _ in range(7)]
    s0_ref = next(it) if has_init else None
    if has_init or chain_state:
        next(it)
    o_ref = next(it)
    st_ref = None if has_init else next(it)
    C = GLS_CHUNK
    n = L // C
    masks = _gls_masks()
    ng = ng_ref[0]
    f_refs = (f0_ref, f1_ref)

    def chunk(cidx, d, sts):
        r0 = pl.multiple_of(cidx * C, C)
        outs, new = [], []
        for h in range(2):
            sl = slice(h * 128, (h + 1) * 128)
            lbp = lb_ref[0, h]
            q = q_ref[pl.ds(r0, C), sl].astype(f32)
            fp = f_refs[d][pl.ds(r0, C), sl]
            lb, one_m_lb = lbp[2 * d:2 * d + 1], lbp[2 * d + 1:2 * d + 2]
            e = jnp.exp(-jnp.abs(fp))
            inv = 1.0 / (1.0 + e)
            pos = fp >= 0.0
            sig = jnp.where(pos, inv, e * inv)
            k = one_m_lb * jnp.where(pos, e * inv, inv)
            g = jnp.log(lb + one_m_lb * sig)
            parts = _gls_dir(q, k, g, d)
            o, s = _gls_head(parts, i_ref[pl.ds(r0, C), sl], sts[h], masks[d], None)
            outs.append(o)
            new.append(s)
        return r0, outs, new

    def post(r0, h, tot):
        return _head_post(tot, g_ref[pl.ds(r0, C), h * 128:(h + 1) * 128], ng)

    def init(d, h):
        if not has_init:
            return jnp.zeros((128, 128), f32)
        return s0_ref[0, 0, d, h].T

    states = _scan_loops(n, nseq, chunk, post, o_ref,
                         [([init(0, 0), init(0, 1)], [init(1, 0), init(1, 1)]) for _ in range(nseq)])
    if st_ref is not None:
        for s, dirs in enumerate(states):
            for d, sts in enumerate(dirs):
                for h in range(2):
                    st_ref[s, 0, d, h] = sts[h].T


def hgrn_mixer(PF, PB, lbp, ng, *, B, L, row0, layer, nseq=1, state=None, prev=None, st_prev=None):
    has_init = state is not None
    chain_state = st_prev is not None
    R = nseq * L
    in_specs = [
        pl.BlockSpec((R, 256), lambda b, hp: (row0 + b, B_HG_Q // 256 + hp)),
        pl.BlockSpec((R, 256), lambda b, hp: (row0 + b, F_HG_F // 256 + hp)),
        pl.BlockSpec((R, 256), lambda b, hp: (row0 + b, F_HG_F // 256 + 2 + hp)),
        pl.BlockSpec((R, 256), lambda b, hp: (row0 + b, B_HG_I // 256 + hp)),
        pl.BlockSpec((R, 256), lambda b, hp: (row0 + b, B_HG_G // 256 + hp)),
        pl.BlockSpec((1, 2, 8, 128), lambda b, hp: (layer, hp, 0, 0)),
        pl.BlockSpec((1, 1, 128), lambda b, hp: (layer, 0, 0)),
    ]
    args = [PB, PF, PF, PB, PB, lbp, ng]
    aliases = {}
    out_specs = [pl.BlockSpec((R, 256), lambda b, hp: (row0 + b, hp))]
    out_shape = [jax.ShapeDtypeStruct((N_TOK, MIX_W), f32)]
    if has_init:
        assert nseq == 1
        in_specs.append(pl.BlockSpec((1, 1, 2, 2, 128, 128), lambda b, hp: (b, layer, 0, hp, 0, 0)))
        args.append(state)
        in_specs.append(pl.BlockSpec(memory_space=pl.ANY))
        args.append(prev)
        aliases = {len(args) - 1: 0}
    else:
        if chain_state:
            in_specs.append(pl.BlockSpec(memory_space=pl.ANY))
            args.append(st_prev)
            aliases = {len(args) - 1: 1}
        out_specs.append(pl.BlockSpec((nseq, 1, 2, 2, 128, 128), lambda b, hp: (b, layer, 0, hp, 0, 0)))
        out_shape.append(jax.ShapeDtypeStruct((B, DEPTH, 2, HGRN_HEADS, HGRN_DIM, HGRN_DIM), f32))
    return pl.pallas_call(
        functools.partial(_hgrn_body, L=L, nseq=nseq, has_init=has_init, chain_state=chain_state),
        grid=(B // nseq, 2),
        in_specs=in_specs,
        out_specs=out_specs,
        out_shape=out_shape,
        input_output_aliases=aliases,
        compiler_params=_cparams(("parallel", "parallel")),
        name="hgrn_mixer",
    )(*args)


def _ssd_body(*refs, L, has_init, chain_state):
    it = iter(refs)
    (xs_ref, bm_ref, cm_ref, z_ref, sm_ref, cw_ref, dtb_ref, alog_ref, e64_ref, e128_ref,
     dsk_ref, ng_ref) = [next(it) for _ in range(12)]
    s0_ref = next(it) if has_init else None
    if has_init or chain_state:
        next(it)
    o_ref = next(it)
    st_ref = None if has_init else next(it)
    xc_s = next(it)
    C = SSD_CHUNK
    n = L // C
    cw = cw_ref[0, 0]
    rowi = lax.broadcasted_iota(jnp.int32, (C, 1), 0)

    def conv_block(rb, _):
        r0 = pl.multiple_of(rb * C, C)
        rp = pl.multiple_of(jnp.maximum(r0 - 16, 0), 16)
        rn = pl.multiple_of(jnp.minimum(r0 + C, L - 16), 16)
        has_p = jnp.where(rb > 0, 1.0, 0.0)
        has_n = jnp.where(rb < n - 1, 1.0, 0.0)
        for ref, c0, w in ((xs_ref, 0, 256), (bm_ref, 256, 128), (cm_ref, 384, 128)):
            x = ref[pl.ds(r0, C), :].astype(f32)
            prev = ref[pl.ds(rp, 16), :].astype(f32)[15:16] * has_p
            nxt = ref[pl.ds(rn, 16), :].astype(f32)[0:1] * has_n
            x_dn = jnp.where(rowi == 0, prev, pltpu.roll(x, 1, axis=0))
            x_up = jnp.where(rowi == C - 1, nxt, pltpu.roll(x, C - 1, axis=0))
            y = (cw[0:1, c0:c0 + w] * x_dn + cw[1:2, c0:c0 + w] * x + cw[2:3, c0:c0 + w] * x_up
                 + cw[3:4, c0:c0 + w])
            xc_s[pl.ds(r0, C), c0:c0 + w] = _silu(y)
        return 0

    lax.fori_loop(0, n, conv_block, 0)

    masks = [_tri(C, upper=False), _tri(C, upper=True)]
    tris = [jnp.where(m, 1.0, 0.0).astype(bf16) for m in masks]
    lane256 = lax.broadcasted_iota(jnp.int32, (1, 256), 1)
    hmask = [jnp.where((lane256 >= h * 64) & (lane256 < (h + 1) * 64), 1.0, 0.0) for h in range(4)]
    dtb = dtb_ref[0]
    a_row = -jnp.exp(alog_ref[0])
    neg_inf = jnp.float32(-jnp.inf)

    def chunk(cidx, d, st):
        r0 = pl.multiple_of(cidx * C, C)
        xs = xc_s[pl.ds(r0, C), 0:256]
        bm = xc_s[pl.ds(r0, C), 256:384].astype(bf16)
        cm = xc_s[pl.ds(r0, C), 384:512].astype(bf16)
        dt_all = _softplus(sm_ref[pl.ds(r0, C), :] + dtb)
        a_all = dt_all * a_row
        dt64 = _dot_sel_rhs(dt_all, e64_ref[0, d])
        acs_all = _dot_sel_lhs(tris[d], a_all)
        acs64 = _dot_sel_rhs3(acs_all, e64_ref[0, d])
        acs128 = _dot_sel_rhs3(acs_all, e128_ref[0, d])
        end = acs64[C - 1:C] if d == 0 else acs64[0:1]
        xdt = xs * dt64
        scores = _dot_nt(cm, bm)
        y = jnp.exp(acs64) * _dot(cm, st.astype(bf16))
        for h in range(4):
            colb = acs128[:, h * 128:(h + 1) * 128]
            diff = jnp.where(masks[d], colb - colb.T, neg_inf)
            p = (scores * jnp.exp(diff)).astype(bf16)
            y = y + _dot(p, (xdt * hmask[h]).astype(bf16))
        xd = (xdt * jnp.exp(end - acs64)).astype(bf16)
        st_new = st * jnp.exp(end) + _dot_tn(bm, xd)
        return r0, y, st_new

    dsk = dsk_ref[0, 0]
    ng = ng_ref[0, 0]

    def post(r0, y):
        xs = xc_s[pl.ds(r0, C), 0:256]
        t = (y + dsk * xs) * _silu(z_ref[pl.ds(r0, C), :].astype(f32))
        ms = jnp.mean(t * t, axis=-1, keepdims=True)
        return t * lax.rsqrt(ms + EPS) * ng

    def first_half(t, carry):
        sf, sb = carry
        r0, yf, sf = chunk(t, 0, sf)
        o_ref[pl.ds(r0, C), :] = yf
        r1, yb, sb = chunk(n - 1 - t, 1, sb)
        o_ref[pl.ds(r1, C), :] = yb
        return sf, sb

    def second_half(t, carry):
        sf, sb = carry
        r0, yf, sf = chunk(t, 0, sf)
        o_ref[pl.ds(r0, C), :] = post(r0, o_ref[pl.ds(r0, C), :] + yf)
        r1, yb, sb = chunk(n - 1 - t, 1, sb)
        o_ref[pl.ds(r1, C), :] = post(r1, o_ref[pl.ds(r1, C), :] + yb)
        return sf, sb

    def init(d):
        if not has_init:
            return jnp.zeros((128, 256), f32)
        return s0_ref[0, 0, d].reshape(256, 128).T

    carry = (init(0), init(1))
    unroll = 2 if (n // 2) % 2 == 0 else 1
    carry = lax.fori_loop(0, n // 2, first_half, carry, unroll=unroll)
    sf, sb = lax.fori_loop(n // 2, n, second_half, carry, unroll=unroll)
    if st_ref is not None:
        st_ref[0, 0, 0] = sf.T.reshape(4, 64, 128)
        st_ref[0, 0, 1] = sb.T.reshape(4, 64, 128)


def ssd_mixer(SM, PB, cw, dtb, alog, e64, e128, dsk, ng, *, B, L, row0, layer, state=None, prev=None,
              st_prev=None):
    has_init = state is not None
    chain_state = st_prev is not None
    xb = B_SSM_XBC
    in_specs = [
        pl.BlockSpec((L, 256), lambda b, g: (row0 + b, xb // 256 + g)),
        pl.BlockSpec((L, 128), lambda b, g: (row0 + b, (xb + 512) // 128 + g)),
        pl.BlockSpec((L, 128), lambda b, g: (row0 + b, (xb + 768) // 128 + g)),
        pl.BlockSpec((L, 256), lambda b, g: (row0 + b, B_SSM_Z // 256 + g)),
        pl.BlockSpec((L, 128), lambda b, g: (row0 + b, 0)),
        pl.BlockSpec((1, 1, 4, 512), lambda b, g: (layer, g, 0, 0)),
        pl.BlockSpec((1, 1, 128), lambda b, g: (layer, 0, 0)),
        pl.BlockSpec((1, 1, 128), lambda b, g: (layer, 0, 0)),
        pl.BlockSpec((1, 2, 128, 256), lambda b, g: (g, 0, 0, 0)),
        pl.BlockSpec((1, 2, 128, 512), lambda b, g: (g, 0, 0, 0)),
        pl.BlockSpec((1, 1, 1, 256), lambda b, g: (layer, g, 0, 0)),
        pl.BlockSpec((1, 1, 1, 256), lambda b, g: (layer, g, 0, 0)),
    ]
    args = [PB, PB, PB, PB, SM, cw, dtb, alog, e64, e128, dsk, ng]
    aliases = {}
    out_specs = [pl.BlockSpec((L, 256), lambda b, g: (row0 + b, g))]
    out_shape = [jax.ShapeDtypeStruct((N_TOK, MIX_W), f32)]
    if has_init:
        in_specs.append(pl.BlockSpec((1, 1, 2, 4, 64, 128), lambda b, g: (b, layer, 0, g, 0, 0)))
        args.append(state)
        in_specs.append(pl.BlockSpec(memory_space=pl.ANY))
        args.append(prev)
        aliases = {len(args) - 1: 0}
    else:
        if chain_state:
            in_specs.append(pl.BlockSpec(memory_space=pl.ANY))
            args.append(st_prev)
            aliases = {len(args) - 1: 1}
        out_specs.append(pl.BlockSpec((1, 1, 2, 4, 64, 128), lambda b, g: (b, layer, 0, g, 0, 0)))
        out_shape.append(jax.ShapeDtypeStruct((B, DEPTH, 2, SSM_HEADS, SSM_HEADDIM, SSM_STATE), f32))
    return pl.pallas_call(
        functools.partial(_ssd_body, L=L, has_init=has_init, chain_state=chain_state),
        grid=(B, SSM_GROUPS),
        in_specs=in_specs,
        out_specs=out_specs,
        out_shape=out_shape,
        scratch_shapes=[pltpu.VMEM((L, 512), f32)],
        input_output_aliases=aliases,
        compiler_params=_cparams(("parallel", "parallel")),
        name="ssd_mixer",
    )(*args)


def _s5_body(*refs, n, n_c, s_b, has_init, chain_state):
    it = iter(refs)
    u_ref, wt_ref, wsf_ref, wsb_ref, wof_ref, wob_ref, lam_ref, dsk_ref = [next(it) for _ in range(8)]
    if has_init:
        h0_ref = next(it)
        next(it)
    elif chain_state:
        next(it)
    o_ref = next(it)
    if not has_init:
        hs_ref = next(it)
    zf_s, zb_s, hf_s, hb_s, zfs_s, zbs_s, acc_s, za_s, zc_s = [next(it) for _ in range(9)]
    T = S5_CHUNK

    for j in range(T):
        ub = u_ref[pl.ds(j, n, stride=T), :].astype(bf16)
        a = _dot(ub, wt_ref[0, 0, :, (T - 1 - j) * 128:(2 * T - 1 - j) * 128])
        f = _dot(ub, wsf_ref[0, 0, 0, j])
        b = _dot(ub, wsb_ref[0, 0, 0, j])
        if j == 0:
            acc_s[...] = a
            za_s[...] = f
            zc_s[...] = b
        else:
            acc_s[...] += a
            za_s[...] += f
            zc_s[...] += b
    for kk in range(8):
        zf_s[kk] = za_s[:, kk * 128:(kk + 1) * 128]
        zb_s[kk] = zc_s[:, kk * 128:(kk + 1) * 128]

    lam = lam_ref[0, 0]

    for kk in range(8):
        zfs_s[kk] = pltpu.roll(zf_s[kk], 64, axis=1)
        zbs_s[kk] = pltpu.roll(zb_s[kk], 64, axis=1)

    def rows(c):
        return pl.ds(c, s_b, stride=n_c) if s_b > 1 else pl.ds(c, 1)

    def scan_step(c, h, hs, z_s, zs_s, h_s, la, lb):
        new, news = [], []
        for kk in range(8):
            h_s[kk, rows(c), :] = h[kk]
            sl = slice(kk * 128, (kk + 1) * 128)
            new.append(la[:, sl] * h[kk] + lb[:, sl] * hs[kk] + z_s[kk, rows(c), :])
            news.append(la[:, sl] * hs[kk] - lb[:, sl] * h[kk] + zs_s[kk, rows(c), :])
        return tuple(new), tuple(news)

    def step(t, carry):
        hf, hfs, hb, hbs = carry
        hf, hfs = scan_step(t, hf, hfs, zf_s, zfs_s, hf_s, lam[0:1], lam[1:2])
        hb, hbs = scan_step(n_c - 1 - t, hb, hbs, zb_s, zbs_s, hb_s, lam[2:3], lam[3:4])
        return hf, hfs, hb, hbs

    if has_init:
        h0f = tuple(h0_ref[:, 0, 0, kk * 128:(kk + 1) * 128] for kk in range(8))
        h0b = tuple(h0_ref[:, 0, 1, kk * 128:(kk + 1) * 128] for kk in range(8))
    else:
        h0f = h0b = tuple(jnp.zeros((s_b, 128), f32) for _ in range(8))
    swap = lambda hh: tuple(pltpu.roll(v, 64, axis=1) for v in hh)
    hf, _, hb, _ = lax.fori_loop(0, n_c, step, (h0f, swap(h0f), h0b, swap(h0b)))
    if not has_init:
        for kk in range(8):
            hs_ref[:, 0, 0, kk * 128:(kk + 1) * 128] = hf[kk]
            hs_ref[:, 0, 1, kk * 128:(kk + 1) * 128] = hb[kk]

    hin_f = jnp.concatenate([hf_s[kk] for kk in range(8)], axis=1).astype(bf16)
    hin_b = jnp.concatenate([hb_s[kk] for kk in range(8)], axis=1).astype(bf16)
    acc_s[...] += _dot(hin_f, wof_ref[0, 0, 0])
    acc_s[...] += _dot(hin_b, wob_ref[0, 0, 0])
    dsk = dsk_ref[0, 0]
    for i in range(T):
        o_ref[pl.ds(i, n, stride=T), :] = (acc_s[:, i * 128:(i + 1) * 128]
                                           + dsk * u_ref[pl.ds(i, n, stride=T), :])


def s5_mixer(PF, w, dsk, *, B, L, s_b, row0, layer, state=None, prev=None, st_prev=None):
    has_init = state is not None
    chain_state = st_prev is not None
    n_c = L // S5_CHUNK
    n = s_b * n_c
    rows = s_b * L
    one = pl.Buffered(1)
    in_specs = [
        pl.BlockSpec((rows, 128), lambda lb, sb: (row0 + sb, F_S5_U // 128 + lb)),
        pl.BlockSpec((1, 1, 128, 31 * 128), lambda lb, sb: (layer, lb, 0, 0), pipeline_mode=one),
        pl.BlockSpec((1, 1, 1, 16, 128, 1024), lambda lb, sb: (layer, 0, lb, 0, 0, 0), pipeline_mode=one),
        pl.BlockSpec((1, 1, 1, 16, 128, 1024), lambda lb, sb: (layer, 1, lb, 0, 0, 0), pipeline_mode=one),
        pl.BlockSpec((1, 1, 1, 1024, 2048), lambda lb, sb: (layer, 0, lb, 0, 0), pipeline_mode=one),
        pl.BlockSpec((1, 1, 1, 1024, 2048), lambda lb, sb: (layer, 1, lb, 0, 0), pipeline_mode=one),
        pl.BlockSpec((1, 1, 4, 1024), lambda lb, sb: (layer, lb, 0, 0)),
        pl.BlockSpec((1, 1, 1, 128), lambda lb, sb: (layer, lb, 0, 0)),
    ]
    args = [PF, w["toep"], w["ws"], w["ws"], w["wo"], w["wo"], w["lam16"], dsk]
    aliases = {}
    out_specs = [pl.BlockSpec((rows, 128), lambda lb, sb: (row0 + sb, lb))]
    out_shape = [jax.ShapeDtypeStruct((N_TOK, MIX_W), f32)]
    if has_init:
        in_specs.append(pl.BlockSpec((s_b, 1, 2, 1024), lambda lb, sb: (sb, layer, 0, lb)))
        args.append(state)
        in_specs.append(pl.BlockSpec(memory_space=pl.ANY))
        args.append(prev)
        aliases = {len(args) - 1: 0}
    else:
        if chain_state:
            in_specs.append(pl.BlockSpec(memory_space=pl.ANY))
            args.append(st_prev)
            aliases = {len(args) - 1: 1}
        out_specs.append(pl.BlockSpec((s_b, 1, 2, 1024), lambda lb, sb: (sb, layer, 0, lb)))
        out_shape.append(jax.ShapeDtypeStruct((B, DEPTH, 2, 2 * S5_GROUPS * S5_STATE), f32))
    return pl.pallas_call(
        functools.partial(_s5_body, n=n, n_c=n_c, s_b=s_b, has_init=has_init, chain_state=chain_state),
        grid=(4, B // s_b),
        in_specs=in_specs,
        out_specs=out_specs,
        out_shape=out_shape,
        scratch_shapes=([pltpu.VMEM((8, n, 128), f32)] * 6
                        + [pltpu.VMEM((n, 2048), f32), pltpu.VMEM((n, 1024), f32), pltpu.VMEM((n, 1024), f32)]),
        input_output_aliases=aliases,
        compiler_params=_cparams(("parallel", "parallel")),
        name="s5_mixer",
    )(*args)


def _s5_layer_weights(a_re, a_im, log_dt, b_re, b_im, c_re, c_im):
    hp = lax.Precision.HIGHEST
    T = S5_CHUNK
    dt = jnp.exp(log_dt)[..., None]
    lmag, ang = a_re * dt, a_im * dt
    taus = jnp.arange(T + 1, dtype=f32)[:, None, None, None]
    mag = jnp.exp(lmag[None] * taus)
    pr = mag * jnp.cos(ang[None] * taus)
    pi = mag * jnp.sin(ang[None] * taus)
    lam_r, lam_i = pr[1], pi[1]
    den = a_re * a_re + a_im * a_im
    zr = ((lam_r - 1.0) * a_re + lam_i * a_im) / den
    zi = (lam_i * a_re - (lam_r - 1.0) * a_im) / den
    bb_r = zr[..., None] * b_re[None] - zi[..., None] * b_im[None]
    bb_i = zr[..., None] * b_im[None] + zi[..., None] * b_re[None]
    lb_r = pr[..., None] * bb_r[None] - pi[..., None] * bb_i[None]
    lb_i = pr[..., None] * bb_i[None] + pi[..., None] * bb_r[None]
    pr, pi, lb_r, lb_i = lax.optimization_barrier((pr, pi, lb_r, lb_i))
    kk = jnp.sum(c_re[None, None, :, :, :, None] * lb_r[:, :, :, None, :, :]
                 - c_im[None, None, :, :, :, None] * lb_i[:, :, :, None, :, :], axis=4)
    eye8 = jnp.eye(8, dtype=f32)

    def blockdiag(m):
        lead = m.shape[:-3]
        a, b = m.shape[-2], m.shape[-1]
        m = m.reshape(lead + (4, 8, a, b))
        out = m[..., :, :, :, None, :] * eye8[:, None, :, None]
        return out.reshape(lead + (4, 8 * a, 8 * b))

    kt = jnp.swapaxes(kk, -1, -2)
    down = np.arange(T - 1, 0, -1)
    deltas = jnp.concatenate([kt[down, 1], kt[0:1, 0] + kt[0:1, 1], kt[1:T, 0]], axis=0)
    toep = blockdiag(deltas)
    toep = jnp.transpose(toep, (1, 2, 0, 3)).reshape(4, 128, 31 * 128)

    def state_in(lbr, lbi):
        return jnp.concatenate([jnp.swapaxes(lbr, -1, -2), jnp.swapaxes(lbi, -1, -2)], axis=-1)

    rev16 = np.arange(T - 1, -1, -1)
    ws_c = jnp.stack([state_in(lb_r[rev16, 0], lb_i[rev16, 0]),
                      state_in(lb_r[:T, 1], lb_i[:T, 1])], axis=0)

    def state_out(prd, pid):
        cr = c_re[None] * prd[:, :, None, :] - c_im[None] * pid[:, :, None, :]
        ci = c_re[None] * pid[:, :, None, :] + c_im[None] * prd[:, :, None, :]
        return jnp.concatenate([cr, -ci], axis=-1)

    down16 = np.arange(T, 0, -1)
    wo_c = jnp.stack([state_out(pr[1:T + 1, 0], pi[1:T + 1, 0]),
                      state_out(pr[down16, 1], pi[down16, 1])], axis=0)
    la = jnp.concatenate([pr[T], pr[T]], axis=-1)
    lb = jnp.concatenate([-pi[T], pi[T]], axis=-1)
    lam16 = jnp.stack([la[0], lb[0], la[1], lb[1]], axis=0)
    lam16 = jnp.transpose(lam16.reshape(4, 4, 1024), (1, 0, 2))
    return dict(toep=toep.astype(bf16), ws_c=ws_c, wo_c=wo_c, lam16=lam16)


def _s5_expand_in_body(c_ref, o_ref):
    o_ref[...] = jnp.zeros(o_ref.shape, o_ref.dtype)
    for j in range(S5_CHUNK):
        for g in range(8):
            o_ref[0, 0, 0, j, g * 16:(g + 1) * 16, g * 128:(g + 1) * 128] = c_ref[0, 0, j, g].astype(bf16)


def _s5_expand_out_body(c_ref, e_ref, o_ref):
    for i in range(S5_CHUNK):
        for g in range(8):
            tile = _dot_tn(c_ref[0, 0, i, g].astype(bf16), e_ref[g])
            o_ref[0, 0, 0, g * 128:(g + 1) * 128, i * 128:(i + 1) * 128] = tile.astype(bf16)


def s5_expand_weights(ws_c, wo_c):
    c_spec = pl.BlockSpec((1, 1, S5_CHUNK, 8, S5_CH, 128), lambda l, d, lb: (l, d, 0, lb, 0, 0))
    ws = pl.pallas_call(
        _s5_expand_in_body,
        grid=(DEPTH, 2, 4),
        in_specs=[c_spec],
        out_specs=pl.BlockSpec((1, 1, 1, S5_CHUNK, 128, 1024), lambda l, d, lb: (l, d, lb, 0, 0, 0)),
        out_shape=jax.ShapeDtypeStruct((DEPTH, 2, 4, S5_CHUNK, 128, 1024), bf16),
        compiler_params=_cparams(("parallel", "parallel", "parallel")),
        name="s5_expand_in",
    )(ws_c)
    place = np.zeros((8, S5_CH, 128), np.float32)
    for g in range(8):
        place[g, np.arange(S5_CH), g * S5_CH + np.arange(S5_CH)] = 1.0
    wo = pl.pallas_call(
        _s5_expand_out_body,
        grid=(DEPTH, 2, 4),
        in_specs=[c_spec, pl.BlockSpec((8, S5_CH, 128), lambda l, d, lb: (0, 0, 0))],
        out_specs=pl.BlockSpec((1, 1, 1, 1024, 2048), lambda l, d, lb: (l, d, lb, 0, 0)),
        out_shape=jax.ShapeDtypeStruct((DEPTH, 2, 4, 1024, 2048), bf16),
        compiler_params=_cparams(("parallel", "parallel", "parallel")),
        name="s5_expand_out",
    )(wo_c, jnp.asarray(place, bf16))
    return ws, wo


def _merge_body(gp_ref, b0_ref, b1_ref, b2_ref, b3_ref, gw_ref, gb_ref, wb_ref, wo_ref, x_ref, g1_ref, o_ref,
                acc_s):
    b = pl.program_id(1)
    for bb, br_ref in enumerate((b0_ref, b1_ref, b2_ref, b3_ref)):
        @pl.when(b == bb)
        def _(br_ref=br_ref, bb=bb):
            br = br_ref[...]
            if bb == 3:
                g = jax.nn.gelu(br)
                br = g * jax.nn.sigmoid(_dot(g.astype(bf16), gw_ref[0]) + gb_ref[0])
            up = _dot(br.astype(bf16), wb_ref[0, bb])
            t = (0.5 * jnp.tanh(0.5 * gp_ref[...].astype(f32)) + 0.5) * up
            if bb == 0:
                acc_s[...] = t
            else:
                acc_s[...] += t

    @pl.when(b == 3)
    def _():
        mix = _dot(acc_s[...].astype(bf16), wo_ref[0])
        o_ref[...] = x_ref[...] + g1_ref[0, 0] * mix


def merge_branches(PB, branches, glu_w, glu_b, w_branch, w_out, x, mod, layer, tm=512):
    br_spec = pl.BlockSpec((tm, MIX_W), lambda i, b: (i, 0))
    return pl.pallas_call(
        _merge_body,
        grid=(N_TOK // tm, 4),
        in_specs=[
            pl.BlockSpec((tm, D), lambda i, b: (i, B_GATE // D + b)),
            br_spec, br_spec, br_spec, br_spec,
            pl.BlockSpec((1, MIX_W, MIX_W), lambda i, b: (layer, 0, 0)),
            pl.BlockSpec((1, 1, MIX_W), lambda i, b: (layer, 0, 0)),
            pl.BlockSpec((1, 4, MIX_W, D), lambda i, b: (layer, 0, 0, 0), pipeline_mode=pl.Buffered(1)),
            pl.BlockSpec((1, D, D), lambda i, b: (layer, 0, 0), pipeline_mode=pl.Buffered(1)),
            pl.BlockSpec((tm, D), lambda i, b: (i, 0)),
            _mod_spec(layer, 2, tm),
        ],
        out_specs=pl.BlockSpec((tm, D), lambda i, b: (i, 0)),
        out_shape=jax.ShapeDtypeStruct((N_TOK, D), f32),
        scratch_shapes=[pltpu.VMEM((tm, D), f32)],
        compiler_params=_cparams(("parallel", "arbitrary")),
        name="merge_branches",
    )(PB, *branches, glu_w, glu_b, w_branch, w_out, x, mod)


def _ffn_body(*refs, n_e, steps_per_e, final_norm):
    it = iter(refs)
    x_ref, sh_ref, sc_ref, gt_ref, g_ref, w1_ref, w3_ref, w2_ref = [next(it) for _ in range(8)]
    if n_e > 1:
        rwh_ref, rwl_ref, rb_ref = [next(it) for _ in range(3)]
    fg_ref = next(it) if final_norm else None
    o_ref = next(it)
    h_s = next(it)
    acc_s = o_ref
    gate_s = next(it) if n_e > 1 else None
    j = pl.program_id(1)
    nj = pl.num_programs(1)

    @pl.when(j == 0)
    def _():
        h = _modnorm(x_ref[...], g_ref[0], sc_ref[0, 0], sh_ref[0, 0])
        h_s[...] = h.astype(bf16)
        if n_e > 1:
            logits = _dot_hi(h, rwh_ref[0], rwl_ref[0]) + rb_ref[0]
            lane = lax.broadcasted_iota(jnp.int32, logits.shape, 1)
            neg = jnp.float32(-jnp.inf)
            lv = jnp.where(lane < n_e, logits, neg)
            m1 = jnp.max(lv, axis=1, keepdims=True)
            i1 = jnp.min(jnp.where(lv == m1, lane, 128), axis=1, keepdims=True)
            lv2 = jnp.where(lane == i1, neg, lv)
            m2 = jnp.max(lv2, axis=1, keepdims=True)
            i2 = jnp.min(jnp.where(lv2 == m2, lane, 128), axis=1, keepdims=True)
            e2 = jnp.exp(m2 - m1)
            p1 = 1.0 / (1.0 + e2)
            gate_s[...] = jnp.where(lane == i1, p1, 0.0) + jnp.where(lane == i2, e2 * p1, 0.0)

    hb = h_s[...]
    hid = _silu(_dot(hb, w1_ref[0, 0])) * _dot(hb, w3_ref[0, 0])
    if n_e > 1:
        e = j // steps_per_e
        lane = lax.broadcasted_iota(jnp.int32, gate_s.shape, 1)
        hid = hid * jnp.sum(jnp.where(lane == e, gate_s[...], 0.0), axis=1, keepdims=True)
    hidb = hid.astype(bf16)
    cb = 512

    @pl.when(j == 0)
    def _():
        for c0 in range(0, D, cb):
            acc_s[:, c0:c0 + cb] = _dot(hidb, w2_ref[0, 0, :, c0:c0 + cb])

    @pl.when(j > 0)
    def _():
        for c0 in range(0, D, cb):
            acc_s[:, c0:c0 + cb] += _dot(hidb, w2_ref[0, 0, :, c0:c0 + cb])

    @pl.when(j == nj - 1)
    def _():
        y = x_ref[...] + gt_ref[0, 0] * acc_s[...]
        if final_norm:
            ms = jnp.mean(y * y, axis=-1, keepdims=True)
            y = y * lax.rsqrt(ms + EPS) * fg_ref[...]
        o_ref[...] = y


def ffn_layer(x, mod, g, w1, w3, w2, layer, widx, router=None, final_g=None, rows=None, tm=1024, tf=512):
    rows = (0, N_TOK) if rows is None else rows
    n_e, f = w1.shape[1], w1.shape[3]
    spe = f // tf
    final = final_g is not None
    tile0, n_tiles = rows[0] // tm, rows[1] // tm
    in_specs = [
        pl.BlockSpec((tm, D), lambda i, j: (tile0 + i, 0), pipeline_mode=pl.Buffered(1)),
        _mod_spec(layer, 3, tm, tile0),
        _mod_spec(layer, 4, tm, tile0),
        _mod_spec(layer, 5, tm, tile0),
        pl.BlockSpec((1, 1, D), lambda i, j: (layer, 0, 0)),
        pl.BlockSpec((1, 1, D, tf), lambda i, j: (widx, j // spe, 0, j % spe)),
        pl.BlockSpec((1, 1, D, tf), lambda i, j: (widx, j // spe, 0, j % spe)),
        pl.BlockSpec((1, 1, tf, D), lambda i, j: (widx, j // spe, j % spe, 0)),
    ]
    args = [x, mod, mod, mod, g, w1, w3, w2]
    scratch = [pltpu.VMEM((tm, D), bf16)]
    if n_e > 1:
        in_specs += [pl.BlockSpec((1, D, 128), lambda i, j: (widx, 0, 0)),
                     pl.BlockSpec((1, D, 128), lambda i, j: (widx, 0, 0)),
                     pl.BlockSpec((1, 1, 128), lambda i, j: (widx, 0, 0))]
        args += list(router)
        scratch.append(pltpu.VMEM((tm, 128), f32))
    if final:
        in_specs.append(pl.BlockSpec((1, D), lambda i, j: (0, 0)))
        args.append(final_g.reshape(1, D))
    return pl.pallas_call(
        functools.partial(_ffn_body, n_e=n_e, steps_per_e=spe, final_norm=final),
        grid=(n_tiles, n_e * spe),
        in_specs=in_specs,
        out_specs=pl.BlockSpec((tm, D), lambda i, j: (i, 0)),
        out_shape=jax.ShapeDtypeStruct((rows[1], D), f32),
        scratch_shapes=scratch,
        compiler_params=_cparams(("parallel", "arbitrary")),
        name="ffn_moe" if n_e > 1 else "ffn_dense",
    )(*args)


def _split_hi_lo(w):
    hi = w.astype(bf16)
    lo = (w - hi.astype(f32)).astype(bf16)
    return hi, lo


_W_IN_SEGMENTS = ((2080, 3104), (5680, 6192), (6192, 14384), (0, 1536), (1568, 2080), (3104, 5664))
_W_IN_SMALL = ((1536, 1568), (5664, 5680))
_W_IN_COLS = 14384


def _reorder_body(wt_ref, o_ref, os_ref):
    c = 0
    for a, b in _W_IN_SEGMENTS:
        for r in range(a, b, 128):
            o_ref[0, :, c:c + 128] = wt_ref[0, r:r + 128, :].T.astype(bf16)
            c += 128
    rows = [wt_ref[0, a:b, :] for a, b in _W_IN_SMALL]
    rows.append(jnp.zeros((128 - sum(r.shape[0] for r in rows), 128), f32))
    os_ref[0] = jnp.concatenate(rows, axis=0).T.astype(bf16)


def _reorder_w_in(w):
    wt = jnp.swapaxes(w, 1, 2)
    return pl.pallas_call(
        _reorder_body,
        grid=(DEPTH, D // 128),
        in_specs=[pl.BlockSpec((1, _W_IN_COLS, 128), lambda l, r: (l, 0, r))],
        out_specs=[pl.BlockSpec((1, 128, N_PF + N_PB), lambda l, r: (l, r, 0)),
                   pl.BlockSpec((1, 128, 128), lambda l, r: (l, r, 0))],
        out_shape=[jax.ShapeDtypeStruct((DEPTH, D, N_PF + N_PB), bf16),
                   jax.ShapeDtypeStruct((DEPTH, D, 128), bf16)],
        compiler_params=_cparams(("parallel", "parallel")),
        name="reorder_w_in",
    )(wt)


def _gla_gate_weights(wa2, ba):
    t = wa2.reshape(DEPTH, 2, GLA_RANK, 2, 128)
    w = jnp.einsum("ldrhc,de->lhdrec", t, jnp.eye(2, dtype=f32))
    w = w.reshape(DEPTH, 2, 2 * GLA_RANK, 256)
    w = jnp.pad(w, ((0, 0), (0, 0), (SM_LR, 128 - SM_LR - 2 * GLA_RANK), (0, 0)))
    b = jnp.transpose(ba.reshape(DEPTH, 2, 2, 128), (0, 2, 1, 3)).reshape(DEPTH, 2, 1, 256)
    hi, lo = _split_hi_lo(w)
    return hi, lo, b


def _hgrn_bounds(logits):
    p = jax.nn.softmax(logits.astype(f32), axis=1)
    lower = jnp.maximum(jnp.cumsum(p, axis=1) - p[:, :1], 0.0)
    lb = jnp.transpose(lower, (1, 0, 2)).reshape(DEPTH, 2, HGRN_HEADS, HGRN_DIM)
    rows = jnp.stack([lb, 1.0 - lb], axis=3)
    rows = jnp.transpose(rows, (0, 2, 1, 3, 4)).reshape(DEPTH, HGRN_HEADS, 4, HGRN_DIM)
    return jnp.pad(rows, ((0, 0), (0, 0), (0, 4), (0, 0)))


def _ssd_select_mats():
    e64 = np.zeros((2, 2, 128, 256), np.float32)
    e128 = np.zeros((2, 2, 128, 512), np.float32)
    for g in range(2):
        for d in range(2):
            for h in range(4):
                lane = SM_DT + d * 8 + g * 4 + h
                e64[g, d, lane, h * 64:(h + 1) * 64] = 1.0
                e128[g, d, lane, h * 128:(h + 1) * 128] = 1.0
    return jnp.asarray(e64, bf16), jnp.asarray(e128, bf16)


def _ssd_params(conv_w, conv_b, dt_bias, a_log, d_skip, norm_g):
    cwb = jnp.concatenate([conv_w, conv_b[:, None]], axis=1)
    cw = jnp.concatenate([cwb[..., 0:512].reshape(DEPTH, 4, 2, 256),
                          cwb[..., 512:768].reshape(DEPTH, 4, 2, 128),
                          cwb[..., 768:1024].reshape(DEPTH, 4, 2, 128)], axis=-1)
    cw = jnp.transpose(cw, (0, 2, 1, 3))
    lanes = ((0, 0), (SM_DT, 128 - SM_DT - 16))
    dtb = jnp.pad(dt_bias.reshape(DEPTH, 16), lanes).reshape(DEPTH, 1, 128)
    alog = jnp.pad(a_log.reshape(DEPTH, 16), lanes).reshape(DEPTH, 1, 128)
    dsk = jnp.repeat(d_skip, SSM_HEADDIM, axis=1).reshape(DEPTH, 2, 1, 256)
    return cw, dtb, alog, dsk, norm_g.reshape(DEPTH, 2, 1, 256)


def kernel(x_prompt, x_sample, c, c_ctx, state_gla, state_hgrn, state_ssm, state_s5_re, state_s5_im, norm1_g, norm2_g, ada_w, ada_b, w_in, gla_wa2, gla_ba, gla_norm_g, hgrn_lb_logits, hgrn_norm_g, ssm_conv_w, ssm_conv_b, ssm_a_log, ssm_dt_bias, ssm_d, ssm_norm_g, s5_a_re, s5_a_im, s5_log_dt, s5_b_re, s5_b_im, s5_c_re, s5_c_im, s5_d, s5_glu_w, s5_glu_b, w_branch, w_out, ffn_w1, ffn_w3, ffn_w2, router_w, router_b, moe_w1, moe_w3, moe_w2, final_norm_g):
    w_all, w_small = _reorder_w_in(w_in)
    wah, wal, gba = _gla_gate_weights(gla_wa2, gla_ba)
    gng = gla_norm_g.reshape(DEPTH, 1, GLA_DV)
    lbp = _hgrn_bounds(hgrn_lb_logits)
    hng = hgrn_norm_g.reshape(DEPTH, 1, HGRN_DIM)
    e64, e128 = _ssd_select_mats()
    cw, dtb, alog, dsk, sng = _ssd_params(ssm_conv_w, ssm_conv_b, ssm_dt_bias, ssm_a_log, ssm_d, ssm_norm_g)
    w5 = jax.vmap(_s5_layer_weights)(s5_a_re, s5_a_im, s5_log_dt, s5_b_re, s5_b_im, s5_c_re, s5_c_im)
    w5["ws"], w5["wo"] = s5_expand_weights(w5["ws_c"], w5["wo_c"])
    d5 = s5_d.reshape(DEPTH, 4, 1, 128)
    glu_w = s5_glu_w.astype(bf16)
    glu_b = s5_glu_b.reshape(DEPTH, 1, MIX_W)
    wbr = w_branch.astype(bf16)
    wout = w_out.astype(bf16)
    n1 = norm1_g.reshape(DEPTH, 1, D)
    n2 = norm2_g.reshape(DEPTH, 1, D)
    fw1, fw3, fw2 = ffn_w1.astype(bf16)[:, None], ffn_w3.astype(bf16)[:, None], ffn_w2.astype(bf16)[:, None]
    mw1, mw3, mw2 = moe_w1.astype(bf16), moe_w3.astype(bf16), moe_w2.astype(bf16)
    rw = jnp.pad(router_w, ((0, 0), (0, 0), (0, 128 - N_EXPERTS)))
    router = _split_hi_lo(rw) + (jnp.pad(router_b, ((0, 0), (0, 128 - N_EXPERTS))).reshape(-1, 1, 128),)
    s5_state = jnp.concatenate([state_s5_re, state_s5_im], axis=-1).reshape(
        DEC_BATCH, DEPTH, 2, 2 * S5_GROUPS * S5_STATE)

    x = assemble_tokens(x_prompt.reshape(N_CTX, D), x_sample.reshape(N_LAT, D), _pos_table())
    cond8 = jnp.concatenate([c_ctx[None], c, jnp.zeros((MOD_ROWS - 1 - DEC_BATCH, D), f32)], axis=0)
    mod = ada_modulation(cond8, ada_w, ada_b).reshape(DEPTH, MOD_ROWS * 6, 1, D)

    lat0 = N_CTX // DEC_SEQ
    ctx = dict(B=BATCH, L=SEQ, row0=0)
    lat = dict(B=DEC_BATCH, L=DEC_SEQ, row0=lat0)
    st_gla = st_hg = st_ssm = st_s5 = None
    for l in range(DEPTH):
        PF, PB, SM = in_projection(x, mod, n1, w_all, w_small, l)

        o_gla, st_gla = gla_mixer(SM, PB, wah, wal, gba, gng, layer=l, st_prev=st_gla, nseq=2, **ctx)
        (o_gla,) = gla_mixer(SM, PB, wah, wal, gba, gng, layer=l, state=state_gla, prev=o_gla, **lat)

        o_hg, st_hg = hgrn_mixer(PF, PB, lbp, hng, layer=l, st_prev=st_hg, nseq=2, **ctx)
        (o_hg,) = hgrn_mixer(PF, PB, lbp, hng, layer=l, state=state_hgrn, prev=o_hg, **lat)

        o_ssm, st_ssm = ssd_mixer(SM, PB, cw, dtb, alog, e64, e128, dsk, sng, layer=l, st_prev=st_ssm, **ctx)
        (o_ssm,) = ssd_mixer(SM, PB, cw, dtb, alog, e64, e128, dsk, sng, layer=l, state=state_ssm,
                             prev=o_ssm, **lat)

        y5, st_s5 = s5_mixer(PF, w5, d5, B=BATCH, L=SEQ, s_b=min(8, BATCH), row0=0, layer=l, st_prev=st_s5)
        (y5,) = s5_mixer(PF, w5, d5, B=DEC_BATCH, L=DEC_SEQ, s_b=1, row0=lat0, layer=l,
                         state=s5_state, prev=y5)

        x = merge_branches(PB, (o_gla, o_hg, o_ssm, y5), glu_w, glu_b, wbr, wout, x, mod, l)

        if l % 2 == 0:
            ffn = functools.partial(ffn_layer, x, mod, n2, fw1, fw3, fw2, l, l // 2)
        else:
            ffn = functools.partial(ffn_layer, x, mod, n2, mw1, mw3, mw2, l, l // 2, router=router)
        if l < DEPTH - 1:
            x = ffn()
        else:
            y_ctx = ffn(final_g=final_norm_g, rows=(0, N_CTX))
            y_lat = ffn(final_g=final_norm_g, rows=(N_CTX, N_LAT))

    st_s5 =st_s5.reshape(BATCH, DEPTH, 2, S5_GROUPS, 2 * S5_STATE)
    return (y_ctx.reshape(BATCH, SEQ, D), y_lat.reshape(DEC_BATCH, DEC_SEQ, D),
            st_gla, st_hg, st_ssm, st_s5[..., :S5_STATE], st_s5[..., S5_STATE:])
```

```python
import functools

import numpy as np
import jax
import jax.numpy as jnp
from jax import lax
from jax.experimental import pallas as pl
from jax.experimental.pallas import tpu as pltpu

f32 = jnp.float32
bf16 = jnp.bfloat16

D = 2048
BATCH = 32
SEQ = 256
DEPTH = 4
DEC_BATCH = 4
DEC_SEQ = 4096
GRID_W = 64
MIX_W = 512
GLA_HEADS = 4
GLA_DK = 64
GLA_DV = 128
GLA_RANK = 16
GLA_NORMALIZER = 16.0
HGRN_HEADS = 4
HGRN_DIM = 128
SSM_HEADS = 8
SSM_HEADDIM = 64
SSM_GROUPS = 2
SSM_STATE = 128
S5_CH = 16
S5_STATE = 64
S5_GROUPS = 32
D_FF = 5632
N_EXPERTS = 8
D_FF_EXPERT = 1024
EPS = 1e-6

N_CTX = BATCH * SEQ
N_LAT = DEC_BATCH * DEC_SEQ
N_TOK = N_CTX + N_LAT
MOD_ROWS = 8

F_HG_F, F_S5_U = 0, 1024
N_PF = 1536
B_GATE, B_GLA_Q, B_GLA_K, B_GLA_V, B_GLA_R = 0, 8192, 8448, 8704, 9216
B_HG_Q, B_HG_I, B_HG_G, B_SSM_Z, B_SSM_XBC = 9728, 10240, 10752, 11264, 11776
N_PB = 12800
SM_LR, SM_DT = 0, 32

GLS_CHUNK = 128
GLS_SUB = 32
SSD_CHUNK = 128
S5_CHUNK = 16

VMEM_LIMIT = 58 * 1024 * 1024


def _cparams(sem):
    return pltpu.CompilerParams(dimension_semantics=sem, vmem_limit_bytes=VMEM_LIMIT)


def _dot(a, b):
    return jnp.dot(a, b, preferred_element_type=f32)


def _dot_nt(a, b):
    return lax.dot_general(a, b, (((1,), (1,)), ((), ())), preferred_element_type=f32)


def _dot_tn(a, b):
    return lax.dot_general(a, b, (((0,), (0,)), ((), ())), preferred_element_type=f32)


def _split2(x):
    hi = x.astype(bf16)
    return hi, (x - hi.astype(f32)).astype(bf16)


def _dot_sel_lhs(t, x):
    x1, x2 = _split2(x)
    return _dot(t, x1) + _dot(t, x2)


def _dot_sel_rhs(x, e):
    x1, x2 = _split2(x)
    return _dot(x1, e) + _dot(x2, e)


def _dot_sel_rhs3(x, e):
    x1, x2 = _split2(x)
    x3 = (x - x1.astype(f32) - x2.astype(f32)).astype(bf16)
    return _dot(x1, e) + _dot(x2, e) + _dot(x3, e)


def _dot_hi(x, w_hi, w_lo):
    x1, x2 = _split2(x)
    return _dot(x1, w_hi) + _dot(x2, w_hi) + _dot(x1, w_lo)


def _log_sigmoid(x):
    return jnp.minimum(x, 0.0) - jnp.log1p(jnp.exp(-jnp.abs(x)))


def _softplus(x):
    return jnp.maximum(x, 0.0) + jnp.log1p(jnp.exp(-jnp.abs(x)))


def _silu(x):
    return x * jax.nn.sigmoid(x)


def _modnorm(x, g, scale, shift):
    ms = jnp.mean(x * x, axis=-1, keepdims=True)
    y = x * lax.rsqrt(ms + EPS) * g
    return y * (1.0 + scale) + shift


def _tri(n, upper):
    ii = lax.broadcasted_iota(jnp.int32, (n, n), 0)
    jj = lax.broadcasted_iota(jnp.int32, (n, n), 1)
    return (ii <= jj) if upper else (ii >= jj)


def _mod_row(i, tm):
    start = i * tm
    return jnp.where(start < N_CTX, 0, 1 + (start - N_CTX) // DEC_SEQ)


def _mod_spec(layer, k, tm, tile0=0):
    return pl.BlockSpec((1, 1, 1, D), lambda i, j: (layer, _mod_row(tile0 + i, tm) * 6 + k, 0, 0))


def _assemble_body(xp_ref, xs_ref, pos_ref, o_ref, *, n_ctx_tiles):
    i = pl.program_id(0)

    @pl.when(i < n_ctx_tiles)
    def _():
        o_ref[...] = xp_ref[...]

    @pl.when(i >= n_ctx_tiles)
    def _():
        o_ref[...] = xs_ref[...] + pos_ref[...]


def assemble_tokens(xp2, xs2, pos, tm=512):
    nct = N_CTX // tm
    npos = DEC_SEQ // tm
    return pl.pallas_call(
        functools.partial(_assemble_body, n_ctx_tiles=nct),
        grid=(N_TOK // tm,),
        in_specs=[
            pl.BlockSpec((tm, D), lambda i: (jnp.minimum(i, nct - 1), 0)),
            pl.BlockSpec((tm, D), lambda i: (jnp.maximum(i - nct, 0), 0)),
            pl.BlockSpec((tm, D), lambda i: (jnp.maximum(i - nct, 0) % npos, 0)),
        ],
        out_specs=pl.BlockSpec((tm, D), lambda i: (i, 0)),
        out_shape=jax.ShapeDtypeStruct((N_TOK, D), f32),
        compiler_params=_cparams(("parallel",)),
        name="assemble_tokens",
    )(xp2, xs2, pos)


def _pos_table():
    rows = DEC_SEQ // GRID_W
    row = jnp.repeat(jnp.arange(rows, dtype=f32), GRID_W)
    col = jnp.tile(jnp.arange(GRID_W, dtype=f32), rows)

    def sincos(p, d):
        half = d // 2
        omega = 1.0 / (10000.0 ** (jnp.arange(half, dtype=f32) / half))
        ang = p[:, None] * omega[None, :]
        return jnp.concatenate([jnp.sin(ang), jnp.cos(ang)], axis=-1)

    return jnp.concatenate([sincos(row, D // 2), sincos(col, D // 2)], axis=-1)


def _ada_body(c_ref, w_ref, b_ref, o_ref):
    s = _silu(c_ref[...])
    s1, s2 = _split2(s)
    w = w_ref[0].astype(bf16)
    o_ref[0] = _dot(s1, w) + _dot(s2, w) + b_ref[0]


def ada_modulation(cond8, ada_w, ada_b, tn=1024):
    return pl.pallas_call(
        _ada_body,
        grid=(DEPTH, 6 * D // tn),
        in_specs=[
            pl.BlockSpec((MOD_ROWS, D), lambda l, j: (0, 0)),
            pl.BlockSpec((1, D, tn), lambda l, j: (l, 0, j)),
            pl.BlockSpec((1, 1, tn), lambda l, j: (l, 0, j)),
        ],
        out_specs=pl.BlockSpec((1, MOD_ROWS, tn), lambda l, j: (l, 0, j)),
        out_shape=jax.ShapeDtypeStruct((DEPTH, MOD_ROWS, 6 * D), f32),
        compiler_params=_cparams(("parallel", "parallel")),
        name="ada_modulation",
    )(cond8, ada_w, ada_b.reshape(DEPTH, 1, 6 * D))


def _inproj_body(x_ref, sh_ref, sc_ref, g_ref, w_ref, ws_ref, pf_ref, pb_ref, sm_ref, h_s, *, nf):
    j = pl.program_id(1)

    @pl.when(j == 0)
    def _():
        h = _modnorm(x_ref[...], g_ref[0], sc_ref[0, 0], sh_ref[0, 0]).astype(bf16)
        h_s[...] = h
        sm_ref[...] = _dot(h, ws_ref[0])

    y = _dot(h_s[...], w_ref[0])

    @pl.when(j < nf)
    def _():
        pf_ref[...] = y

    @pl.when(j >= nf)
    def _():
        pb_ref[...] = y.astype(bf16)


def in_projection(x, mod, g, w_all, w_small, layer, tm=1024, tn=512):
    nf = N_PF // tn
    nb = N_PB // tn
    return pl.pallas_call(
        functools.partial(_inproj_body, nf=nf),
        grid=(N_TOK // tm, nf + nb),
        in_specs=[
            pl.BlockSpec((tm, D), lambda i, j: (i, 0)),
            _mod_spec(layer, 0, tm),
            _mod_spec(layer, 1, tm),
            pl.BlockSpec((1, 1, D), lambda i, j: (layer, 0, 0)),
            pl.BlockSpec((1, D, tn), lambda i, j: (layer, 0, j)),
            pl.BlockSpec((1, D, 128), lambda i, j: (layer, 0, 0)),
        ],
        out_specs=[
            pl.BlockSpec((tm, tn), lambda i, j: (i, jnp.minimum(j, nf - 1))),
            pl.BlockSpec((tm, tn), lambda i, j: (i, jnp.maximum(j - nf, 0))),
            pl.BlockSpec((tm, 128), lambda i, j: (i, 0)),
        ],
        out_shape=[
            jax.ShapeDtypeStruct((N_TOK, N_PF), f32),
            jax.ShapeDtypeStruct((N_TOK, N_PB), bf16),
            jax.ShapeDtypeStruct((N_TOK, 128), f32),
        ],
        scratch_shapes=[pltpu.VMEM((tm, D), bf16)],
        compiler_params=_cparams(("parallel", "arbitrary")),
        name="in_projection",
    )(x, mod, mod, g, w_all, w_small)


def _gls_dir(q, k, g, d):
    c, sub = GLS_CHUNK, GLS_SUB
    nb = c // sub
    tri = jnp.where(_tri(c, upper=(d == 1)), 1.0, 0.0).astype(bf16)
    b = _dot_sel_lhs(tri, g)
    zero = jnp.zeros((1, 128), f32)
    if d == 0:
        mid = [b[sub * i + sub // 2 - 1:sub * i + sub // 2] for i in range(nb)]
        end = [b[sub * i + sub - 1:sub * i + sub] for i in range(nb)]
        start = [zero] + end[:-1]
        total = end[-1]
    else:
        mid = [b[sub * i + sub // 2:sub * i + sub // 2 + 1] for i in range(nb)]
        end = [b[sub * i:sub * i + 1] for i in range(nb)]
        start = end[1:] + [zero]
        total = end[0]

    def rows(vals):
        return jnp.concatenate([jnp.broadcast_to(v, (sub, 128)) for v in vals], axis=0)

    r_mid = rows(mid)
    qd = q * jnp.exp(b - r_mid)
    kd = k * jnp.exp(r_mid - b)
    qo = qd * rows([jnp.exp(m - s) for m, s in zip(mid, start)])
    ke = kd * rows([jnp.exp(e - m) for m, e in zip(mid, end)])
    q_in = qo * rows([jnp.exp(s) for s in start])
    k_out = ke * rows([jnp.exp(total - e) for e in end])
    kr = []
    for i in range(nb):
        prior = range(i) if d == 0 else range(i + 1, nb)
        if len(prior) == 0:
            kr.append(None)
            continue
        blocks = []
        for jb in range(nb):
            if jb in prior:
                blocks.append(ke[jb * sub:(jb + 1) * sub] * jnp.exp(start[i] - end[jb]))
            else:
                blocks.append(jnp.zeros((sub, 128), f32))
        kr.append(jnp.concatenate(blocks, axis=0).astype(bf16))
    return qd, kd.astype(bf16), qo, kr, q_in, k_out, jnp.exp(total)


def _gls_head(parts, v, st, mask, kmask):
    qd, kd, qo, kr, q_in, k_out, dec = parts
    sub = GLS_SUB
    if kmask is not None:
        qd, qo, q_in, k_out = qd * kmask, qo * kmask, q_in * kmask, k_out * kmask
    att = jnp.where(mask, _dot_nt(qd.astype(bf16), kd), 0.0)
    qob = qo.astype(bf16)
    offs = []
    for i, kri in enumerate(kr):
        if kri is None:
            offs.append(jnp.zeros((sub, GLS_CHUNK), f32))
        else:
            offs.append(_dot_nt(qob[i * sub:(i + 1) * sub], kri))
    att = att + jnp.concatenate(offs, axis=0)
    vb = v.astype(bf16)
    o = _dot(att.astype(bf16), vb) + _dot_nt(q_in.astype(bf16), st.astype(bf16))
    st_new = st * dec + _dot_tn(vb, k_out.astype(bf16))
    return o, st_new


def _gls_masks():
    c, sub = GLS_CHUNK, GLS_SUB
    ii = lax.broadcasted_iota(jnp.int32, (c, c), 0)
    jj = lax.broadcasted_iota(jnp.int32, (c, c), 1)
    same = (ii // sub) == (jj // sub)
    return [same & (ii >= jj), same & (ii <= jj)]


def _head_post(o, gate_in, ng):
    ms = jnp.mean(o * o, axis=-1, keepdims=True)
    return o * lax.rsqrt(ms + EPS) * ng * _silu(gate_in.astype(f32))


def _scan_loops(n, nseq, chunk_fn, post_fn, o_ref, init):
    C = GLS_CHUNK

    def store(r0, outs, final):
        for h, o in enumerate(outs):
            sl = (pl.ds(r0, C), slice(h * 128, (h + 1) * 128))
            if final:
                o_ref[sl] = post_fn(r0, h, o_ref[sl] + o)
            else:
                o_ref[sl] = o

    def half(final):
        def body(t, carry):
            new = []
            for s, (sf, sb) in enumerate(carry):
                r0, of, sf = chunk_fn(s * n + t, 0, sf)
                store(r0, of, final)
                r1, ob, sb = chunk_fn(s * n + n - 1 - t, 1, sb)
                store(r1, ob, final)
                new.append((sf, sb))
            return tuple(new)
        return body

    unroll = 4 if (n // 2) % 4 == 0 else 1
    carry = lax.fori_loop(0, n // 2, half(False), tuple(init), unroll=unroll)
    return lax.fori_loop(n // 2, n, half(True), carry, unroll=unroll)


def _gla_body(*refs, L, nseq, has_init, chain_state):
    it = iter(refs)
    q_ref, k_ref, v_ref, r_ref, sm_ref, wah_ref, wal_ref, ba_ref, ng_ref = [next(it) for _ in range(9)]
    s0_ref = next(it) if has_init else None
    if has_init or chain_state:
        next(it)
    o_ref = next(it)
    st_ref = None if has_init else next(it)
    C = GLS_CHUNK
    n = L // C
    lane = lax.broadcasted_iota(jnp.int32, (1, 128), 1)
    kmasks = [jnp.where(lane < 64, 1.0, 0.0), jnp.where(lane >= 64, 1.0, 0.0)]
    masks = _gls_masks()
    wah, wal, ba, ng = wah_ref[0, 0], wal_ref[0, 0], ba_ref[0, 0], ng_ref[0]

    def chunk(cidx, d, sts):
        r0 = pl.multiple_of(cidx * C, C)
        q = q_ref[pl.ds(r0, C), :].astype(f32) * (GLA_DK ** -0.5)
        k = k_ref[pl.ds(r0, C), :].astype(f32)
        sm = sm_ref[pl.ds(r0, C), :]
        sl = slice(d * 128, (d + 1) * 128)
        g = _log_sigmoid(_dot_hi(sm, wah[:, sl], wal[:, sl]) + ba[:, sl]) * (1.0 / GLA_NORMALIZER)
        parts = _gls_dir(q, k, g, d)
        outs, new = [], []
        for h in range(2):
            v = v_ref[pl.ds(r0, C), h * 128:(h + 1) * 128]
            o, s = _gls_head(parts, v, sts[h], masks[d], kmasks[h])
            outs.append(o)
            new.append(s)
        return r0, outs, new

    def post(r0, h, tot):
        return _head_post(tot, r_ref[pl.ds(r0, C), h * 128:(h + 1) * 128], ng)

    def init(d, h):
        if not has_init:
            return jnp.zeros((128, 128), f32)
        s = s0_ref[0, 0, d, h]
        z = jnp.zeros((64, 128), f32)
        full = jnp.concatenate([s, z], axis=0) if h == 0 else jnp.concatenate([z, s], axis=0)
        return full.T

    states = _scan_loops(n, nseq, chunk, post, o_ref,
                         [([init(0, 0), init(0, 1)], [init(1, 0), init(1, 1)]) for _ in range(nseq)])
    if st_ref is not None:
        for s, dirs in enumerate(states):
            for d, sts in enumerate(dirs):
                for h in range(2):
                    st_ref[s, 0, d, h] = sts[h].T[h * 64:(h + 1) * 64, :]


def gla_mixer(SM, PB, wa_hi, wa_lo, ba, ng, *, B, L, row0, layer, nseq=1, state=None, prev=None, st_prev=None):
    has_init = state is not None
    chain_state = st_prev is not None
    R = nseq * L
    in_specs = [
        pl.BlockSpec((R, 128), lambda b, hp: (row0 + b, B_GLA_Q // 128 + hp)),
        pl.BlockSpec((R, 128), lambda b, hp: (row0 + b, B_GLA_K // 128 + hp)),
        pl.BlockSpec((R, 256), lambda b, hp: (row0 + b, B_GLA_V // 256 + hp)),
        pl.BlockSpec((R, 256), lambda b, hp: (row0 + b, B_GLA_R // 256 + hp)),
        pl.BlockSpec((R, 128), lambda b, hp: (row0 + b, 0)),
        pl.BlockSpec((1, 1, 128, 256), lambda b, hp: (layer, hp, 0, 0)),
        pl.BlockSpec((1, 1, 128, 256), lambda b, hp: (layer, hp, 0, 0)),
        pl.BlockSpec((1, 1, 1, 256), lambda b, hp: (layer, hp, 0, 0)),
        pl.BlockSpec((1, 1, 128), lambda b, hp: (layer, 0, 0)),
    ]
    args = [PB, PB, PB, PB, SM, wa_hi, wa_lo, ba, ng]
    aliases = {}
    out_specs = [pl.BlockSpec((R, 256), lambda b, hp: (row0 + b, hp))]
    out_shape = [jax.ShapeDtypeStruct((N_TOK, MIX_W), f32)]
    if has_init:
        assert nseq == 1
        in_specs.append(pl.BlockSpec((1, 1, 2, 2, 64, 128), lambda b, hp: (b, layer, 0, hp, 0, 0)))
        args.append(state)
        in_specs.append(pl.BlockSpec(memory_space=pl.ANY))
        args.append(prev)
        aliases = {len(args) - 1: 0}
    else:
        if chain_state:
            in_specs.append(pl.BlockSpec(memory_space=pl.ANY))
            args.append(st_prev)
            aliases = {len(args) - 1: 1}
        out_specs.append(pl.BlockSpec((nseq, 1, 2, 2, 64, 128), lambda b, hp: (b, layer, 0, hp, 0, 0)))
        out_shape.append(jax.ShapeDtypeStruct((B, DEPTH, 2, GLA_HEADS, GLA_DK, GLA_DV), f32))
    return pl.pallas_call(
        functools.partial(_gla_body, L=L, nseq=nseq, has_init=has_init, chain_state=chain_state),
        grid=(B // nseq, 2),
        in_specs=in_specs,
        out_specs=out_specs,
        out_shape=out_shape,
        input_output_aliases=aliases,
        compiler_params=_cparams(("parallel", "parallel")),
        name="gla_mixer",
    )(*args)


def _hgrn_body(*refs, L, nseq, has_init, chain_state):
    it = iter(refs)
    q_ref, f0_ref, f1_ref, i_ref, g_ref, lb_ref, ng_ref = [next(it) for _ in range(7)]
    s0_ref = next(it) if has_init else None
    if has_init or chain_state:
        next(it)
    o_ref = next(it)
    st_ref = None if has_init else next(it)
    C = GLS_CHUNK
    n = L // C
    masks = _gls_masks()
    ng = ng_ref[0]
    f_refs = (f0_ref, f1_ref)

    def chunk(cidx, d, sts):
        r0 = pl.multiple_of(cidx * C, C)
        outs, new = [], []
        for h in range(2):
            sl = slice(h * 128, (h + 1) * 128)
            lbp = lb_ref[0, h]
            q = q_ref[pl.ds(r0, C), sl].astype(f32)
            fp = f_refs[d][pl.ds(r0, C), sl]
            lb, one_m_lb = lbp[2 * d:2 * d + 1], lbp[2 * d + 1:2 * d + 2]
            e = jnp.exp(-jnp.abs(fp))
            inv = 1.0 / (1.0 + e)
            pos = fp >= 0.0
            sig = jnp.where(pos, inv, e * inv)
            k = one_m_lb * jnp.where(pos, e * inv, inv)
            g = jnp.log(lb + one_m_lb * sig)
            parts = _gls_dir(q, k, g, d)
            o, s = _gls_head(parts, i_ref[pl.ds(r0, C), sl], sts[h], masks[d], None)
            outs.append(o)
            new.append(s)
        return r0, outs, new

    def post(r0, h, tot):
        return _head_post(tot, g_ref[pl.ds(r0, C), h * 128:(h + 1) * 128], ng)

    def init(d, h):
        if not has_init:
            return jnp.zeros((128, 128), f32)
        return s0_ref[0, 0, d, h].T

    states = _scan_loops(n, nseq, chunk, post, o_ref,
                         [([init(0, 0), init(0, 1)], [init(1, 0), init(1, 1)]) for _ in range(nseq)])
    if st_ref is not None:
        for s, dirs in enumerate(states):
            for d, sts in enumerate(dirs):
                for h in range(2):
                    st_ref[s, 0, d, h] = sts[h].T


def hgrn_mixer(PF, PB, lbp, ng, *, B, L, row0, layer, nseq=1, state=None, prev=None, st_prev=None):
    has_init = state is not None
    chain_state = st_prev is not None
    R = nseq * L
    in_specs = [
        pl.BlockSpec((R, 256), lambda b, hp: (row0 + b, B_HG_Q // 256 + hp)),
        pl.BlockSpec((R, 256), lambda b, hp: (row0 + b, F_HG_F // 256 + hp)),
        pl.BlockSpec((R, 256), lambda b, hp: (row0 + b, F_HG_F // 256 + 2 + hp)),
        pl.BlockSpec((R, 256), lambda b, hp: (row0 + b, B_HG_I // 256 + hp)),
        pl.BlockSpec((R, 256), lambda b, hp: (row0 + b, B_HG_G // 256 + hp)),
        pl.BlockSpec((1, 2, 8, 128), lambda b, hp: (layer, hp, 0, 0)),
        pl.BlockSpec((1, 1, 128), lambda b, hp: (layer, 0, 0)),
    ]
    args = [PB, PF, PF, PB, PB, lbp, ng]
    aliases = {}
    out_specs = [pl.BlockSpec((R, 256), lambda b, hp: (row0 + b, hp))]
    out_shape = [jax.ShapeDtypeStruct((N_TOK, MIX_W), f32)]
    if has_init:
        assert nseq == 1
        in_specs.append(pl.BlockSpec((1, 1, 2, 2, 128, 128), lambda b, hp: (b, layer, 0, hp, 0, 0)))
        args.append(state)
        in_specs.append(pl.BlockSpec(memory_space=pl.ANY))
        args.append(prev)
        aliases = {len(args) - 1: 0}
    else:
        if chain_state:
            in_specs.append(pl.BlockSpec(memory_space=pl.ANY))
            args.append(st_prev)
            aliases = {len(args) - 1: 1}
        out_specs.append(pl.BlockSpec((nseq, 1, 2, 2, 128, 128), lambda b, hp: (b, layer, 0, hp, 0, 0)))
        out_shape.append(jax.ShapeDtypeStruct((B, DEPTH, 2, HGRN_HEADS, HGRN_DIM, HGRN_DIM), f32))
    return pl.pallas_call(
        functools.partial(_hgrn_body, L=L, nseq=nseq, has_init=has_init, chain_state=chain_state),
        grid=(B // nseq, 2),
        in_specs=in_specs,
        out_specs=out_specs,
        out_shape=out_shape,
        input_output_aliases=aliases,
        compiler_params=_cparams(("parallel", "parallel")),
        name="hgrn_mixer",
    )(*args)


def _ssd_body(*refs, L, has_init, chain_state):
    it = iter(refs)
    (xs_ref, bm_ref, cm_ref, z_ref, sm_ref, cw_ref, dtb_ref, alog_ref, e64_ref, e128_ref,
     dsk_ref, ng_ref) = [next(it) for _ in range(12)]
    s0_ref = next(it) if has_init else None
    if has_init or chain_state:
        next(it)
    o_ref = next(it)
    st_ref = None if has_init else next(it)
    xc_s = next(it)
    C = SSD_CHUNK
    n = L // C
    cw = cw_ref[0, 0]
    rowi = lax.broadcasted_iota(jnp.int32, (C, 1), 0)

    def conv_block(rb, _):
        r0 = pl.multiple_of(rb * C, C)
        rp = pl.multiple_of(jnp.maximum(r0 - 16, 0), 16)
        rn = pl.multiple_of(jnp.minimum(r0 + C, L - 16), 16)
        has_p = jnp.where(rb > 0, 1.0, 0.0)
        has_n = jnp.where(rb < n - 1, 1.0, 0.0)
        for ref, c0, w in ((xs_ref, 0, 256), (bm_ref, 256, 128), (cm_ref, 384, 128)):
            x = ref[pl.ds(r0, C), :].astype(f32)
            prev = ref[pl.ds(rp, 16), :].astype(f32)[15:16] * has_p
            nxt = ref[pl.ds(rn, 16), :].astype(f32)[0:1] * has_n
            x_dn = jnp.where(rowi == 0, prev, pltpu.roll(x, 1, axis=0))
            x_up = jnp.where(rowi == C - 1, nxt, pltpu.roll(x, C - 1, axis=0))
            y = (cw[0:1, c0:c0 + w] * x_dn + cw[1:2, c0:c0 + w] * x + cw[2:3, c0:c0 + w] * x_up
                 + cw[3:4, c0:c0 + w])
            xc_s[pl.ds(r0, C), c0:c0 + w] = _silu(y)
        return 0

    lax.fori_loop(0, n, conv_block, 0)

    masks = [_tri(C, upper=False), _tri(C, upper=True)]
    tris = [jnp.where(m, 1.0, 0.0).astype(bf16) for m in masks]
    lane256 = lax.broadcasted_iota(jnp.int32, (1, 256), 1)
    hmask = [jnp.where((lane256 >= h * 64) & (lane256 < (h + 1) * 64), 1.0, 0.0) for h in range(4)]
    dtb = dtb_ref[0]
    a_row = -jnp.exp(alog_ref[0])
    neg_inf = jnp.float32(-jnp.inf)

    def chunk(cidx, d, st):
        r0 = pl.multiple_of(cidx * C, C)
        xs = xc_s[pl.ds(r0, C), 0:256]
        bm = xc_s[pl.ds(r0, C), 256:384].astype(bf16)
        cm = xc_s[pl.ds(r0, C), 384:512].astype(bf16)
        dt_all = _softplus(sm_ref[pl.ds(r0, C), :] + dtb)
        a_all = dt_all * a_row
        dt64 = _dot_sel_rhs(dt_all, e64_ref[0, d])
        acs_all = _dot_sel_lhs(tris[d], a_all)
        acs64 = _dot_sel_rhs3(acs_all, e64_ref[0, d])
        acs128 = _dot_sel_rhs3(acs_all, e128_ref[0, d])
        end = acs64[C - 1:C] if d == 0 else acs64[0:1]
        xdt = xs * dt64
        scores = _dot_nt(cm, bm)
        y = jnp.exp(acs64) * _dot(cm, st.astype(bf16))
        for h in range(4):
            colb = acs128[:, h * 128:(h + 1) * 128]
            diff = jnp.where(masks[d], colb - colb.T, neg_inf)
            p = (scores * jnp.exp(diff)).astype(bf16)
            y = y + _dot(p, (xdt * hmask[h]).astype(bf16))
        xd = (xdt * jnp.exp(end - acs64)).astype(bf16)
        st_new = st * jnp.exp(end) + _dot_tn(bm, xd)
        return r0, y, st_new

    dsk = dsk_ref[0, 0]
    ng = ng_ref[0, 0]

    def post(r0, y):
        xs = xc_s[pl.ds(r0, C), 0:256]
        t = (y + dsk * xs) * _silu(z_ref[pl.ds(r0, C), :].astype(f32))
        ms = jnp.mean(t * t, axis=-1, keepdims=True)
        return t * lax.rsqrt(ms + EPS) * ng

    def first_half(t, carry):
        sf, sb = carry
        r0, yf, sf = chunk(t, 0, sf)
        o_ref[pl.ds(r0, C), :] = yf
        r1, yb, sb = chunk(n - 1 - t, 1, sb)
        o_ref[pl.ds(r1, C), :] = yb
        return sf, sb

    def second_half(t, carry):
        sf, sb = carry
        r0, yf, sf = chunk(t, 0, sf)
        o_ref[pl.ds(r0, C), :] = post(r0, o_ref[pl.ds(r0, C), :] + yf)
        r1, yb, sb = chunk(n - 1 - t, 1, sb)
        o_ref[pl.ds(r1, C), :] = post(r1, o_ref[pl.ds(r1, C), :] + yb)
        return sf, sb

    def init(d):
        if not has_init:
            return jnp.zeros((128, 256), f32)
        return s0_ref[0, 0, d].reshape(256, 128).T

    carry = (init(0), init(1))
    unroll = 4 if (n // 2) % 4 == 0 else 1
    carry = lax.fori_loop(0, n // 2, first_half, carry, unroll=unroll)
    sf, sb = lax.fori_loop(n // 2, n, second_half, carry, unroll=unroll)
    if st_ref is not None:
        st_ref[0, 0, 0] = sf.T.reshape(4, 64, 128)
        st_ref[0, 0, 1] = sb.T.reshape(4, 64, 128)


def ssd_mixer(SM, PB, cw, dtb, alog, e64, e128, dsk, ng, *, B, L, row0, layer, state=None, prev=None,
              st_prev=None):
    has_init = state is not None
    chain_state = st_prev is not None
    xb = B_SSM_XBC
    in_specs = [
        pl.BlockSpec((L, 256), lambda b, g: (row0 + b, xb // 256 + g)),
        pl.BlockSpec((L, 128), lambda b, g: (row0 + b, (xb + 512) // 128 + g)),
        pl.BlockSpec((L, 128), lambda b, g: (row0 + b, (xb + 768) // 128 + g)),
        pl.BlockSpec((L, 256), lambda b, g: (row0 + b, B_SSM_Z // 256 + g)),
        pl.BlockSpec((L, 128), lambda b, g: (row0 + b, 0)),
        pl.BlockSpec((1, 1, 4, 512), lambda b, g: (layer, g, 0, 0)),
        pl.BlockSpec((1, 1, 128), lambda b, g: (layer, 0, 0)),
        pl.BlockSpec((1, 1, 128), lambda b, g: (layer, 0, 0)),
        pl.BlockSpec((1, 2, 128, 256), lambda b, g: (g, 0, 0, 0)),
        pl.BlockSpec((1, 2, 128, 512), lambda b, g: (g, 0, 0, 0)),
        pl.BlockSpec((1, 1, 1, 256), lambda b, g: (layer, g, 0, 0)),
        pl.BlockSpec((1, 1, 1, 256), lambda b, g: (layer, g, 0, 0)),
    ]
    args = [PB, PB, PB, PB, SM, cw, dtb, alog, e64, e128, dsk, ng]
    aliases = {}
    out_specs = [pl.BlockSpec((L, 256), lambda b, g: (row0 + b, g))]
    out_shape = [jax.ShapeDtypeStruct((N_TOK, MIX_W), f32)]
    if has_init:
        in_specs.append(pl.BlockSpec((1, 1, 2, 4, 64, 128), lambda b, g: (b, layer, 0, g, 0, 0)))
        args.append(state)
        in_specs.append(pl.BlockSpec(memory_space=pl.ANY))
        args.append(prev)
        aliases = {len(args) - 1: 0}
    else:
        if chain_state:
            in_specs.append(pl.BlockSpec(memory_space=pl.ANY))
            args.append(st_prev)
            aliases = {len(args) - 1: 1}
        out_specs.append(pl.BlockSpec((1, 1, 2, 4, 64, 128), lambda b, g: (b, layer, 0, g, 0, 0)))
        out_shape.append(jax.ShapeDtypeStruct((B, DEPTH, 2, SSM_HEADS, SSM_HEADDIM, SSM_STATE), f32))
    return pl.pallas_call(
        functools.partial(_ssd_body, L=L, has_init=has_init, chain_state=chain_state),
        grid=(B, SSM_GROUPS),
        in_specs=in_specs,
        out_specs=out_specs,
        out_shape=out_shape,
        scratch_shapes=[pltpu.VMEM((L, 512), f32)],
        input_output_aliases=aliases,
        compiler_params=_cparams(("parallel", "parallel")),
        name="ssd_mixer",
    )(*args)


def _s5_body(*refs, n, n_c, s_b, has_init, chain_state):
    it = iter(refs)
    u_ref, wt_ref, wsf_ref, wsb_ref, wof_ref, wob_ref, lam_ref, dsk_ref = [next(it) for _ in range(8)]
    if has_init:
        h0_ref = next(it)
        next(it)
    elif chain_state:
        next(it)
    o_ref = next(it)
    if not has_init:
        hs_ref = next(it)
    zf_s, zb_s, hf_s, hb_s, zfs_s, zbs_s, acc_s, za_s, zc_s = [next(it) for _ in range(9)]
    T = S5_CHUNK

    for j in range(T):
        ub = u_ref[pl.ds(j, n, stride=T), :].astype(bf16)
        a = _dot(ub, wt_ref[0, 0, :, (T - 1 - j) * 128:(2 * T - 1 - j) * 128])
        f = _dot(ub, wsf_ref[0, 0, 0, j])
        b = _dot(ub, wsb_ref[0, 0, 0, j])
        if j == 0:
            acc_s[...] = a
            za_s[...] = f
            zc_s[...] = b
        else:
            acc_s[...] += a
            za_s[...] += f
            zc_s[...] += b
    for kk in range(8):
        zf_s[kk] = za_s[:, kk * 128:(kk + 1) * 128]
        zb_s[kk] = zc_s[:, kk * 128:(kk + 1) * 128]

    lam = lam_ref[0, 0]

    for kk in range(8):
        zfs_s[kk] = pltpu.roll(zf_s[kk], 64, axis=1)
        zbs_s[kk] = pltpu.roll(zb_s[kk], 64, axis=1)

    def rows(c):
        return pl.ds(c, s_b, stride=n_c) if s_b > 1 else pl.ds(c, 1)

    def scan_step(c, h, hs, z_s, zs_s, h_s, la, lb):
        new, news = [], []
        for kk in range(8):
            h_s[kk, rows(c), :] = h[kk]
            sl = slice(kk * 128, (kk + 1) * 128)
            new.append(la[:, sl] * h[kk] + lb[:, sl] * hs[kk] + z_s[kk, rows(c), :])
            news.append(la[:, sl] * hs[kk] - lb[:, sl] * h[kk] + zs_s[kk, rows(c), :])
        return tuple(new), tuple(news)

    def step(t, carry):
        hf, hfs, hb, hbs = carry
        hf, hfs = scan_step(t, hf, hfs, zf_s, zfs_s, hf_s, lam[0:1], lam[1:2])
        hb, hbs = scan_step(n_c - 1 - t, hb, hbs, zb_s, zbs_s, hb_s, lam[2:3], lam[3:4])
        return hf, hfs, hb, hbs

    if has_init:
        h0f = tuple(h0_ref[:, 0, 0, kk * 128:(kk + 1) * 128] for kk in range(8))
        h0b = tuple(h0_ref[:, 0, 1, kk * 128:(kk + 1) * 128] for kk in range(8))
    else:
        h0f = h0b = tuple(jnp.zeros((s_b, 128), f32) for _ in range(8))
    swap = lambda hh: tuple(pltpu.roll(v, 64, axis=1) for v in hh)
    hf, _, hb, _ = lax.fori_loop(0, n_c, step, (h0f, swap(h0f), h0b, swap(h0b)))
    if not has_init:
        for kk in range(8):
            hs_ref[:, 0, 0, kk * 128:(kk + 1) * 128] = hf[kk]
            hs_ref[:, 0, 1, kk * 128:(kk + 1) * 128] = hb[kk]

    hin_f = jnp.concatenate([hf_s[kk] for kk in range(8)], axis=1).astype(bf16)
    hin_b = jnp.concatenate([hb_s[kk] for kk in range(8)], axis=1).astype(bf16)
    acc_s[...] += _dot(hin_f, wof_ref[0, 0, 0])
    acc_s[...] += _dot(hin_b, wob_ref[0, 0, 0])
    dsk = dsk_ref[0, 0]
    for i in range(T):
        o_ref[pl.ds(i, n, stride=T), :] = (acc_s[:, i * 128:(i + 1) * 128]
                                           + dsk * u_ref[pl.ds(i, n, stride=T), :])


def s5_mixer(PF, w, dsk, *, B, L, s_b, row0, layer, state=None, prev=None, st_prev=None):
    has_init = state is not None
    chain_state = st_prev is not None
    n_c = L // S5_CHUNK
    n = s_b * n_c
    rows = s_b * L
    one = pl.Buffered(1)
    in_specs = [
        pl.BlockSpec((rows, 128), lambda lb, sb: (row0 + sb, F_S5_U // 128 + lb)),
        pl.BlockSpec((1, 1, 128, 31 * 128), lambda lb, sb: (layer, lb, 0, 0), pipeline_mode=one),
        pl.BlockSpec((1, 1, 1, 16, 128, 1024), lambda lb, sb: (layer, 0, lb, 0, 0, 0), pipeline_mode=one),
        pl.BlockSpec((1, 1, 1, 16, 128, 1024), lambda lb, sb: (layer, 1, lb, 0, 0, 0), pipeline_mode=one),
        pl.BlockSpec((1, 1, 1, 1024, 2048), lambda lb, sb: (layer, 0, lb, 0, 0), pipeline_mode=one),
        pl.BlockSpec((1, 1, 1, 1024, 2048), lambda lb, sb: (layer, 1, lb, 0, 0), pipeline_mode=one),
        pl.BlockSpec((1, 1, 4, 1024), lambda lb, sb: (layer, lb, 0, 0)),
        pl.BlockSpec((1, 1, 1, 128), lambda lb, sb: (layer, lb, 0, 0)),
    ]
    args = [PF, w["toep"], w["ws"], w["ws"], w["wo"], w["wo"], w["lam16"], dsk]
    aliases = {}
    out_specs = [pl.BlockSpec((rows, 128), lambda lb, sb: (row0 + sb, lb))]
    out_shape = [jax.ShapeDtypeStruct((N_TOK, MIX_W), f32)]
    if has_init:
        in_specs.append(pl.BlockSpec((s_b, 1, 2, 1024), lambda lb, sb: (sb, layer, 0, lb)))
        args.append(state)
        in_specs.append(pl.BlockSpec(memory_space=pl.ANY))
        args.append(prev)
        aliases = {len(args) - 1: 0}
    else:
        if chain_state:
            in_specs.append(pl.BlockSpec(memory_space=pl.ANY))
            args.append(st_prev)
            aliases = {len(args) - 1: 1}
        out_specs.append(pl.BlockSpec((s_b, 1, 2, 1024), lambda lb, sb: (sb, layer, 0, lb)))
        out_shape.append(jax.ShapeDtypeStruct((B, DEPTH, 2, 2 * S5_GROUPS * S5_STATE), f32))
    return pl.pallas_call(
        functools.partial(_s5_body, n=n, n_c=n_c, s_b=s_b, has_init=has_init, chain_state=chain_state),
        grid=(4, B // s_b),
        in_specs=in_specs,
        out_specs=out_specs,
        out_shape=out_shape,
        scratch_shapes=([pltpu.VMEM((8, n, 128), f32)] * 6
                        + [pltpu.VMEM((n, 2048), f32), pltpu.VMEM((n, 1024), f32), pltpu.VMEM((n, 1024), f32)]),
        input_output_aliases=aliases,
        compiler_params=_cparams(("parallel", "parallel")),
        name="s5_mixer",
    )(*args)


def _s5_layer_weights(a_re, a_im, log_dt, b_re, b_im, c_re, c_im):
    hp = lax.Precision.HIGHEST
    T = S5_CHUNK
    dt = jnp.exp(log_dt)[..., None]
    lmag, ang = a_re * dt, a_im * dt
    taus = jnp.arange(T + 1, dtype=f32)[:, None, None, None]
    mag = jnp.exp(lmag[None] * taus)
    pr = mag * jnp.cos(ang[None] * taus)
    pi = mag * jnp.sin(ang[None] * taus)
    lam_r, lam_i = pr[1], pi[1]
    den = a_re * a_re + a_im * a_im
    zr = ((lam_r - 1.0) * a_re + lam_i * a_im) / den
    zi = (lam_i * a_re - (lam_r - 1.0) * a_im) / den
    bb_r = zr[..., None] * b_re[None] - zi[..., None] * b_im[None]
    bb_i = zr[..., None] * b_im[None] + zi[..., None] * b_re[None]
    lb_r = pr[..., None] * bb_r[None] - pi[..., None] * bb_i[None]
    lb_i = pr[..., None] * bb_i[None] + pi[..., None] * bb_r[None]
    pr, pi, lb_r, lb_i = lax.optimization_barrier((pr, pi, lb_r, lb_i))
    kk = jnp.sum(c_re[None, None, :, :, :, None] * lb_r[:, :, :, None, :, :]
                 - c_im[None, None, :, :, :, None] * lb_i[:, :, :, None, :, :], axis=4)
    eye8 = jnp.eye(8, dtype=f32)

    def blockdiag(m):
        lead = m.shape[:-3]
        a, b = m.shape[-2], m.shape[-1]
        m = m.reshape(lead + (4, 8, a, b))
        out = m[..., :, :, :, None, :] * eye8[:, None, :, None]
        return out.reshape(lead + (4, 8 * a, 8 * b))

    kt = jnp.swapaxes(kk, -1, -2)
    down = np.arange(T - 1, 0, -1)
    deltas = jnp.concatenate([kt[down, 1], kt[0:1, 0] + kt[0:1, 1], kt[1:T, 0]], axis=0)
    toep = blockdiag(deltas)
    toep = jnp.transpose(toep, (1, 2, 0, 3)).reshape(4, 128, 31 * 128)

    def state_in(lbr, lbi):
        return jnp.concatenate([jnp.swapaxes(lbr, -1, -2), jnp.swapaxes(lbi, -1, -2)], axis=-1)

    rev16 = np.arange(T - 1, -1, -1)
    ws_c = jnp.stack([state_in(lb_r[rev16, 0], lb_i[rev16, 0]),
                      state_in(lb_r[:T, 1], lb_i[:T, 1])], axis=0)

    def state_out(prd, pid):
        cr = c_re[None] * prd[:, :, None, :] - c_im[None] * pid[:, :, None, :]
        ci = c_re[None] * pid[:, :, None, :] + c_im[None] * prd[:, :, None, :]
        return jnp.concatenate([cr, -ci], axis=-1)

    down16 = np.arange(T, 0, -1)
    wo_c = jnp.stack([state_out(pr[1:T + 1, 0], pi[1:T + 1, 0]),
                      state_out(pr[down16, 1], pi[down16, 1])], axis=0)
    la = jnp.concatenate([pr[T], pr[T]], axis=-1)
    lb = jnp.concatenate([-pi[T], pi[T]], axis=-1)
    lam16 = jnp.stack([la[0], lb[0], la[1], lb[1]], axis=0)
    lam16 = jnp.transpose(lam16.reshape(4, 4, 1024), (1, 0, 2))
    return dict(toep=toep.astype(bf16), ws_c=ws_c, wo_c=wo_c, lam16=lam16)


def _s5_expand_in_body(c_ref, o_ref):
    o_ref[...] = jnp.zeros(o_ref.shape, o_ref.dtype)
    for j in range(S5_CHUNK):
        for g in range(8):
            o_ref[0, 0, 0, j, g * 16:(g + 1) * 16, g * 128:(g + 1) * 128] = c_ref[0, 0, j, g].astype(bf16)


def _s5_expand_out_body(c_ref, e_ref, o_ref):
    for i in range(S5_CHUNK):
        for g in range(8):
            tile = _dot_tn(c_ref[0, 0, i, g].astype(bf16), e_ref[g])
            o_ref[0, 0, 0, g * 128:(g + 1) * 128, i * 128:(i + 1) * 128] = tile.astype(bf16)


def s5_expand_weights(ws_c, wo_c):
    c_spec = pl.BlockSpec((1, 1, S5_CHUNK, 8, S5_CH, 128), lambda l, d, lb: (l, d, 0, lb, 0, 0))
    ws = pl.pallas_call(
        _s5_expand_in_body,
        grid=(DEPTH, 2, 4),
        in_specs=[c_spec],
        out_specs=pl.BlockSpec((1, 1, 1, S5_CHUNK, 128, 1024), lambda l, d, lb: (l, d, lb, 0, 0, 0)),
        out_shape=jax.ShapeDtypeStruct((DEPTH, 2, 4, S5_CHUNK, 128, 1024), bf16),
        compiler_params=_cparams(("parallel", "parallel", "parallel")),
        name="s5_expand_in",
    )(ws_c)
    place = np.zeros((8, S5_CH, 128), np.float32)
    for g in range(8):
        place[g, np.arange(S5_CH), g * S5_CH + np.arange(S5_CH)] = 1.0
    wo = pl.pallas_call(
        _s5_expand_out_body,
        grid=(DEPTH, 2, 4),
        in_specs=[c_spec, pl.BlockSpec((8, S5_CH, 128), lambda l, d, lb: (0, 0, 0))],
        out_specs=pl.BlockSpec((1, 1, 1, 1024, 2048), lambda l, d, lb: (l, d, lb, 0, 0)),
        out_shape=jax.ShapeDtypeStruct((DEPTH, 2, 4, 1024, 2048), bf16),
        compiler_params=_cparams(("parallel", "parallel", "parallel")),
        name="s5_expand_out",
    )(wo_c, jnp.asarray(place, bf16))
    return ws, wo


def _merge_body(gp_ref, b0_ref, b1_ref, b2_ref, b3_ref, gw_ref, gb_ref, wb_ref, wo_ref, x_ref, g1_ref, o_ref,
                acc_s):
    b = pl.program_id(1)
    for bb, br_ref in enumerate((b0_ref, b1_ref, b2_ref, b3_ref)):
        @pl.when(b == bb)
        def _(br_ref=br_ref, bb=bb):
            br = br_ref[...]
            if bb == 3:
                g = jax.nn.gelu(br)
                br = g * jax.nn.sigmoid(_dot(g.astype(bf16), gw_ref[0]) + gb_ref[0])
            up = _dot(br.astype(bf16), wb_ref[0, bb])
            t = (0.5 * jnp.tanh(0.5 * gp_ref[...].astype(f32)) + 0.5) * up
            if bb == 0:
                acc_s[...] = t
            else:
                acc_s[...] += t

    @pl.when(b == 3)
    def _():
        mix = _dot(acc_s[...].astype(bf16), wo_ref[0])
        o_ref[...] = x_ref[...] + g1_ref[0, 0] * mix


def merge_branches(PB, branches, glu_w, glu_b, w_branch, w_out, x, mod, layer, tm=512):
    br_spec = pl.BlockSpec((tm, MIX_W), lambda i, b: (i, 0))
    return pl.pallas_call(
        _merge_body,
        grid=(N_TOK // tm, 4),
        in_specs=[
            pl.BlockSpec((tm, D), lambda i, b: (i, B_GATE // D + b)),
            br_spec, br_spec, br_spec, br_spec,
            pl.BlockSpec((1, MIX_W, MIX_W), lambda i, b: (layer, 0, 0)),
            pl.BlockSpec((1, 1, MIX_W), lambda i, b: (layer, 0, 0)),
            pl.BlockSpec((1, 4, MIX_W, D), lambda i, b: (layer, 0, 0, 0), pipeline_mode=pl.Buffered(1)),
            pl.BlockSpec((1, D, D), lambda i, b: (layer, 0, 0), pipeline_mode=pl.Buffered(1)),
            pl.BlockSpec((tm, D), lambda i, b: (i, 0)),
            _mod_spec(layer, 2, tm),
        ],
        out_specs=pl.BlockSpec((tm, D), lambda i, b: (i, 0)),
        out_shape=jax.ShapeDtypeStruct((N_TOK, D), f32),
        scratch_shapes=[pltpu.VMEM((tm, D), f32)],
        compiler_params=_cparams(("parallel", "arbitrary")),
        name="merge_branches",
    )(PB, *branches, glu_w, glu_b, w_branch, w_out, x, mod)


def _ffn_body(*refs, n_e, steps_per_e, final_norm):
    it = iter(refs)
    x_ref, sh_ref, sc_ref, gt_ref, g_ref, w1_ref, w3_ref, w2_ref = [next(it) for _ in range(8)]
    if n_e > 1:
        rwh_ref, rwl_ref, rb_ref = [next(it) for _ in range(3)]
    fg_ref = next(it) if final_norm else None
    o_ref = next(it)
    h_s = next(it)
    acc_s = o_ref
    gate_s = next(it) if n_e > 1 else None
    j = pl.program_id(1)
    nj = pl.num_programs(1)

    @pl.when(j == 0)
    def _():
        h = _modnorm(x_ref[...], g_ref[0], sc_ref[0, 0], sh_ref[0, 0])
        h_s[...] = h.astype(bf16)
        if n_e > 1:
            logits = _dot_hi(h, rwh_ref[0], rwl_ref[0]) + rb_ref[0]
            lane = lax.broadcasted_iota(jnp.int32, logits.shape, 1)
            neg = jnp.float32(-jnp.inf)
            lv = jnp.where(lane < n_e, logits, neg)
            m1 = jnp.max(lv, axis=1, keepdims=True)
            i1 = jnp.min(jnp.where(lv == m1, lane, 128), axis=1, keepdims=True)
            lv2 = jnp.where(lane == i1, neg, lv)
            m2 = jnp.max(lv2, axis=1, keepdims=True)
            i2 = jnp.min(jnp.where(lv2 == m2, lane, 128), axis=1, keepdims=True)
            e2 = jnp.exp(m2 - m1)
            p1 = 1.0 / (1.0 + e2)
            gate_s[...] = jnp.where(lane == i1, p1, 0.0) + jnp.where(lane == i2, e2 * p1, 0.0)

    hb = h_s[...]
    hid = _silu(_dot(hb, w1_ref[0, 0])) * _dot(hb, w3_ref[0, 0])
    if n_e > 1:
        e = j // steps_per_e
        lane = lax.broadcasted_iota(jnp.int32, gate_s.shape, 1)
        hid = hid * jnp.sum(jnp.where(lane == e, gate_s[...], 0.0), axis=1, keepdims=True)
    hidb = hid.astype(bf16)
    cb = 512

    @pl.when(j == 0)
    def _():
        for c0 in range(0, D, cb):
            acc_s[:, c0:c0 + cb] = _dot(hidb, w2_ref[0, 0, :, c0:c0 + cb])

    @pl.when(j > 0)
    def _():
        for c0 in range(0, D, cb):
            acc_s[:, c0:c0 + cb] += _dot(hidb, w2_ref[0, 0, :, c0:c0 + cb])

    @pl.when(j == nj - 1)
    def _():
        y = x_ref[...] + gt_ref[0, 0] * acc_s[...]
        if final_norm:
            ms = jnp.mean(y * y, axis=-1, keepdims=True)
            y = y * lax.rsqrt(ms + EPS) * fg_ref[...]
        o_ref[...] = y


def ffn_layer(x, mod, g, w1, w3, w2, layer, widx, router=None, final_g=None, rows=None, tm=1024, tf=512):
    rows = (0, N_TOK) if rows is None else rows
    n_e, f = w1.shape[1], w1.shape[3]
    spe = f // tf
    final = final_g is not None
    tile0, n_tiles = rows[0] // tm, rows[1] // tm
    in_specs = [
        pl.BlockSpec((tm, D), lambda i, j: (tile0 + i, 0), pipeline_mode=pl.Buffered(1)),
        _mod_spec(layer, 3, tm, tile0),
        _mod_spec(layer, 4, tm, tile0),
        _mod_spec(layer, 5, tm, tile0),
        pl.BlockSpec((1, 1, D), lambda i, j: (layer, 0, 0)),
        pl.BlockSpec((1, 1, D, tf), lambda i, j: (widx, j // spe, 0, j % spe)),
        pl.BlockSpec((1, 1, D, tf), lambda i, j: (widx, j // spe, 0, j % spe)),
        pl.BlockSpec((1, 1, tf, D), lambda i, j: (widx, j // spe, j % spe, 0)),
    ]
    args = [x, mod, mod, mod, g, w1, w3, w2]
    scratch = [pltpu.VMEM((tm, D), bf16)]
    if n_e > 1:
        in_specs += [pl.BlockSpec((1, D, 128), lambda i, j: (widx, 0, 0)),
                     pl.BlockSpec((1, D, 128), lambda i, j: (widx, 0, 0)),
                     pl.BlockSpec((1, 1, 128), lambda i, j: (widx, 0, 0))]
        args += list(router)
        scratch.append(pltpu.VMEM((tm, 128), f32))
    if final:
        in_specs.append(pl.BlockSpec((1, D), lambda i, j: (0, 0)))
        args.append(final_g.reshape(1, D))
    return pl.pallas_call(
        functools.partial(_ffn_body, n_e=n_e, steps_per_e=spe, final_norm=final),
        grid=(n_tiles, n_e * spe),
        in_specs=in_specs,
        out_specs=pl.BlockSpec((tm, D), lambda i, j: (i, 0)),
        out_shape=jax.ShapeDtypeStruct((rows[1], D), f32),
        scratch_shapes=scratch,
        compiler_params=_cparams(("parallel", "arbitrary")),
        name="ffn_moe" if n_e > 1 else "ffn_dense",
    )(*args)


def _split_hi_lo(w):
    hi = w.astype(bf16)
    lo = (w - hi.astype(f32)).astype(bf16)
    return hi, lo


_W_IN_SEGMENTS = ((2080, 3104), (5680, 6192), (6192, 14384), (0, 1536), (1568, 2080), (3104, 5664))
_W_IN_SMALL = ((1536, 1568), (5664, 5680))
_W_IN_COLS = 14384


def _reorder_body(wt_ref, o_ref, os_ref):
    c = 0
    for a, b in _W_IN_SEGMENTS:
        for r in range(a, b, 128):
            o_ref[0, :, c:c + 128] = wt_ref[0, r:r + 128, :].T.astype(bf16)
            c += 128
    rows = [wt_ref[0, a:b, :] for a, b in _W_IN_SMALL]
    rows.append(jnp.zeros((128 - sum(r.shape[0] for r in rows), 128), f32))
    os_ref[0] = jnp.concatenate(rows, axis=0).T.astype(bf16)


def _reorder_w_in(w):
    wt = jnp.swapaxes(w, 1, 2)
    return pl.pallas_call(
        _reorder_body,
        grid=(DEPTH, D // 128),
        in_specs=[pl.BlockSpec((1, _W_IN_COLS, 128), lambda l, r: (l, 0, r))],
        out_specs=[pl.BlockSpec((1, 128, N_PF + N_PB), lambda l, r: (l, r, 0)),
                   pl.BlockSpec((1, 128, 128), lambda l, r: (l, r, 0))],
        out_shape=[jax.ShapeDtypeStruct((DEPTH, D, N_PF + N_PB), bf16),
                   jax.ShapeDtypeStruct((DEPTH, D, 128), bf16)],
        compiler_params=_cparams(("parallel", "parallel")),
        name="reorder_w_in",
    )(wt)


def _gla_gate_weights(wa2, ba):
    t = wa2.reshape(DEPTH, 2, GLA_RANK, 2, 128)
    w = jnp.einsum("ldrhc,de->lhdrec", t, jnp.eye(2, dtype=f32))
    w = w.reshape(DEPTH, 2, 2 * GLA_RANK, 256)
    w = jnp.pad(w, ((0, 0), (0, 0), (SM_LR, 128 - SM_LR - 2 * GLA_RANK), (0, 0)))
    b = jnp.transpose(ba.reshape(DEPTH, 2, 2, 128), (0, 2, 1, 3)).reshape(DEPTH, 2, 1, 256)
    hi, lo = _split_hi_lo(w)
    return hi, lo, b


def _hgrn_bounds(logits):
    p = jax.nn.softmax(logits.astype(f32), axis=1)
    lower = jnp.maximum(jnp.cumsum(p, axis=1) - p[:, :1], 0.0)
    lb = jnp.transpose(lower, (1, 0, 2)).reshape(DEPTH, 2, HGRN_HEADS, HGRN_DIM)
    rows = jnp.stack([lb, 1.0 - lb], axis=3)
    rows = jnp.transpose(rows, (0, 2, 1, 3, 4)).reshape(DEPTH, HGRN_HEADS, 4, HGRN_DIM)
    return jnp.pad(rows, ((0, 0), (0, 0), (0, 4), (0, 0)))


def _ssd_select_mats():
    e64 = np.zeros((2, 2, 128, 256), np.float32)
    e128 = np.zeros((2, 2, 128, 512), np.float32)
    for g in range(2):
        for d in range(2):
            for h in range(4):
                lane = SM_DT + d * 8 + g * 4 + h
                e64[g, d, lane, h * 64:(h + 1) * 64] = 1.0
                e128[g, d, lane, h * 128:(h + 1) * 128] = 1.0
    return jnp.asarray(e64, bf16), jnp.asarray(e128, bf16)


def _ssd_params(conv_w, conv_b, dt_bias, a_log, d_skip, norm_g):
    cwb = jnp.concatenate([conv_w, conv_b[:, None]], axis=1)
    cw = jnp.concatenate([cwb[..., 0:512].reshape(DEPTH, 4, 2, 256),
                          cwb[..., 512:768].reshape(DEPTH, 4, 2, 128),
                          cwb[..., 768:1024].reshape(DEPTH, 4, 2, 128)], axis=-1)
    cw = jnp.transpose(cw, (0, 2, 1, 3))
    lanes = ((0, 0), (SM_DT, 128 - SM_DT - 16))
    dtb = jnp.pad(dt_bias.reshape(DEPTH, 16), lanes).reshape(DEPTH, 1, 128)
    alog = jnp.pad(a_log.reshape(DEPTH, 16), lanes).reshape(DEPTH, 1, 128)
    dsk = jnp.repeat(d_skip, SSM_HEADDIM, axis=1).reshape(DEPTH, 2, 1, 256)
    return cw, dtb, alog, dsk, norm_g.reshape(DEPTH, 2, 1, 256)


def kernel(x_prompt, x_sample, c, c_ctx, state_gla, state_hgrn, state_ssm, state_s5_re, state_s5_im, norm1_g, norm2_g, ada_w, ada_b, w_in, gla_wa2, gla_ba, gla_norm_g, hgrn_lb_logits, hgrn_norm_g, ssm_conv_w, ssm_conv_b, ssm_a_log, ssm_dt_bias, ssm_d, ssm_norm_g, s5_a_re, s5_a_im, s5_log_dt, s5_b_re, s5_b_im, s5_c_re, s5_c_im, s5_d, s5_glu_w, s5_glu_b, w_branch, w_out, ffn_w1, ffn_w3, ffn_w2, router_w, router_b, moe_w1, moe_w3, moe_w2, final_norm_g):
    w_all, w_small = _reorder_w_in(w_in)
    wah, wal, gba = _gla_gate_weights(gla_wa2, gla_ba)
    gng = gla_norm_g.reshape(DEPTH, 1, GLA_DV)
    lbp = _hgrn_bounds(hgrn_lb_logits)
    hng = hgrn_norm_g.reshape(DEPTH, 1, HGRN_DIM)
    e64, e128 = _ssd_select_mats()
    cw, dtb, alog, dsk, sng = _ssd_params(ssm_conv_w, ssm_conv_b, ssm_dt_bias, ssm_a_log, ssm_d, ssm_norm_g)
    w5 = jax.vmap(_s5_layer_weights)(s5_a_re, s5_a_im, s5_log_dt, s5_b_re, s5_b_im, s5_c_re, s5_c_im)
    w5["ws"], w5["wo"] = s5_expand_weights(w5["ws_c"], w5["wo_c"])
    d5 = s5_d.reshape(DEPTH, 4, 1, 128)
    glu_w = s5_glu_w.astype(bf16)
    glu_b = s5_glu_b.reshape(DEPTH, 1, MIX_W)
    wbr = w_branch.astype(bf16)
    wout = w_out.astype(bf16)
    n1 = norm1_g.reshape(DEPTH, 1, D)
    n2 = norm2_g.reshape(DEPTH, 1, D)
    fw1, fw3, fw2 = ffn_w1.astype(bf16)[:, None], ffn_w3.astype(bf16)[:, None], ffn_w2.astype(bf16)[:, None]
    mw1, mw3, mw2 = moe_w1.astype(bf16), moe_w3.astype(bf16), moe_w2.astype(bf16)
    rw = jnp.pad(router_w, ((0, 0), (0, 0), (0, 128 - N_EXPERTS)))
    router = _split_hi_lo(rw) + (jnp.pad(router_b, ((0, 0), (0, 128 - N_EXPERTS))).reshape(-1, 1, 128),)
    s5_state = jnp.concatenate([state_s5_re, state_s5_im], axis=-1).reshape(
        DEC_BATCH, DEPTH, 2, 2 * S5_GROUPS * S5_STATE)

    x = assemble_tokens(x_prompt.reshape(N_CTX, D), x_sample.reshape(N_LAT, D), _pos_table())
    cond8 = jnp.concatenate([c_ctx[None], c, jnp.zeros((MOD_ROWS - 1 - DEC_BATCH, D), f32)], axis=0)
    mod = ada_modulation(cond8, ada_w, ada_b).reshape(DEPTH, MOD_ROWS * 6, 1, D)

    lat0 = N_CTX // DEC_SEQ
    ctx = dict(B=BATCH, L=SEQ, row0=0)
    lat = dict(B=DEC_BATCH, L=DEC_SEQ, row0=lat0)
    st_gla = st_hg = st_ssm = st_s5 = None
    for l in range(DEPTH):
        PF, PB, SM = in_projection(x, mod, n1, w_all, w_small, l)

        o_gla, st_gla = gla_mixer(SM, PB, wah, wal, gba, gng, layer=l, st_prev=st_gla, nseq=2, **ctx)
        (o_gla,) = gla_mixer(SM, PB, wah, wal, gba, gng, layer=l, state=state_gla, prev=o_gla, **lat)

        o_hg, st_hg = hgrn_mixer(PF, PB, lbp, hng, layer=l, st_prev=st_hg, nseq=2, **ctx)
        (o_hg,) = hgrn_mixer(PF, PB, lbp, hng, layer=l, state=state_hgrn, prev=o_hg, **lat)

        o_ssm, st_ssm = ssd_mixer(SM, PB, cw, dtb, alog, e64, e128, dsk, sng, layer=l, st_prev=st_ssm, **ctx)
        (o_ssm,) = ssd_mixer(SM, PB, cw, dtb, alog, e64, e128, dsk, sng, layer=l, state=state_ssm,
                             prev=o_ssm, **lat)

        y5, st_s5 = s5_mixer(PF, w5, d5, B=BATCH, L=SEQ, s_b=min(8, BATCH), row0=0, layer=l, st_prev=st_s5)
        (y5,) = s5_mixer(PF, w5, d5, B=DEC_BATCH, L=DEC_SEQ, s_b=1, row0=lat0, layer=l,
                         state=s5_state, prev=y5)

        x = merge_branches(PB, (o_gla, o_hg, o_ssm, y5), glu_w, glu_b, wbr, wout, x, mod, l)

        if l % 2 == 0:
            ffn = functools.partial(ffn_layer, x, mod, n2, fw1, fw3, fw2, l, l // 2)
        else:
            ffn = functools.partial(ffn_layer, x, mod, n2, mw1, mw3, mw2, l, l // 2, router=router)
        if l < DEPTH - 1:
            x = ffn()
        else:
            y_ctx = ffn(final_g=final_norm_g, rows=(0, N_CTX))
            y_lat = ffn(final_g=final_norm_g, rows=(N_CTX, N_LAT))

    st_s5 =st_s5.reshape(BATCH, DEPTH, 2, S5_GROUPS, 2 * S5_STATE)
    return (y_ctx.reshape(BATCH, SEQ, D), y_lat.reshape(DEC_BATCH, DEC_SEQ, D),
            st_gla, st_hg, st_ssm, st_s5[..., :S5_STATE], st_s5[..., S5_STATE:])
```

```python
import functools

import numpy as np
import jax
import jax.numpy as jnp
from jax import lax
from jax.experimental import pallas as pl
from jax.experimental.pallas import tpu as pltpu

f32 = jnp.float32
bf16 = jnp.bfloat16

D = 2048
BATCH = 32
SEQ = 256
DEPTH = 4
DEC_BATCH = 4
DEC_SEQ = 4096
GRID_W = 64
MIX_W = 512
GLA_HEADS = 4
GLA_DK = 64
GLA_DV = 128
GLA_RANK = 16
GLA_NORMALIZER = 16.0
HGRN_HEADS = 4
HGRN_DIM = 128
SSM_HEADS = 8
SSM_HEADDIM = 64
SSM_GROUPS = 2
SSM_STATE = 128
S5_CH = 16
S5_STATE = 64
S5_GROUPS = 32
D_FF = 5632
N_EXPERTS = 8
D_FF_EXPERT = 1024
EPS = 1e-6

N_CTX = BATCH * SEQ
N_LAT = DEC_BATCH * DEC_SEQ
N_TOK = N_CTX + N_LAT
MOD_ROWS = 8

F_HG_F, F_S5_U = 0, 1024
N_PF = 1536
B_GATE, B_GLA_Q, B_GLA_K, B_GLA_V, B_GLA_R = 0, 8192, 8448, 8704, 9216
B_HG_Q, B_HG_I, B_HG_G, B_SSM_Z, B_SSM_XBC = 9728, 10240, 10752, 11264, 11776
N_PB = 12800
SM_LR, SM_DT = 0, 32

GLS_CHUNK = 128
GLS_SUB = 32
SSD_CHUNK = 128
S5_CHUNK = 16

VMEM_LIMIT = 58 * 1024 * 1024


def _cparams(sem):
    return pltpu.CompilerParams(dimension_semantics=sem, vmem_limit_bytes=VMEM_LIMIT)


def _dot(a, b):
    return jnp.dot(a, b, preferred_element_type=f32)


def _dot_nt(a, b):
    return lax.dot_general(a, b, (((1,), (1,)), ((), ())), preferred_element_type=f32)


def _dot_tn(a, b):
    return lax.dot_general(a, b, (((0,), (0,)), ((), ())), preferred_element_type=f32)


def _split2(x):
    hi = x.astype(bf16)
    return hi, (x - hi.astype(f32)).astype(bf16)


def _dot_sel_lhs(t, x):
    x1, x2 = _split2(x)
    return _dot(t, x1) + _dot(t, x2)


def _dot_sel_rhs(x, e):
    x1, x2 = _split2(x)
    return _dot(x1, e) + _dot(x2, e)


def _dot_sel_rhs3(x, e):
    x1, x2 = _split2(x)
    x3 = (x - x1.astype(f32) - x2.astype(f32)).astype(bf16)
    return _dot(x1, e) + _dot(x2, e) + _dot(x3, e)


def _dot_hi(x, w_hi, w_lo):
    x1, x2 = _split2(x)
    return _dot(x1, w_hi) + _dot(x2, w_hi) + _dot(x1, w_lo)


def _log_sigmoid(x):
    return jnp.minimum(x, 0.0) - jnp.log1p(jnp.exp(-jnp.abs(x)))


def _softplus(x):
    return jnp.maximum(x, 0.0) + jnp.log1p(jnp.exp(-jnp.abs(x)))


def _silu(x):
    return x * jax.nn.sigmoid(x)


def _modnorm(x, g, scale, shift):
    ms = jnp.mean(x * x, axis=-1, keepdims=True)
    y = x * lax.rsqrt(ms + EPS) * g
    return y * (1.0 + scale) + shift


def _tri(n, upper):
    ii = lax.broadcasted_iota(jnp.int32, (n, n), 0)
    jj = lax.broadcasted_iota(jnp.int32, (n, n), 1)
    return (ii <= jj) if upper else (ii >= jj)


def _mod_row(i, tm):
    start = i * tm
    return jnp.where(start < N_CTX, 0, 1 + (start - N_CTX) // DEC_SEQ)


def _mod_spec(layer, k, tm, tile0=0):
    return pl.BlockSpec((1, 1, 1, D), lambda i, j: (layer, _mod_row(tile0 + i, tm) * 6 + k, 0, 0))


def _assemble_body(xp_ref, xs_ref, pos_ref, o_ref, *, n_ctx_tiles):
    i = pl.program_id(0)

    @pl.when(i < n_ctx_tiles)
    def _():
        o_ref[...] = xp_ref[...]

    @pl.when(i >= n_ctx_tiles)
    def _():
        o_ref[...] = xs_ref[...] + pos_ref[...]


def assemble_tokens(xp2, xs2, pos, tm=512):
    nct = N_CTX // tm
    npos = DEC_SEQ // tm
    return pl.pallas_call(
        functools.partial(_assemble_body, n_ctx_tiles=nct),
        grid=(N_TOK // tm,),
        in_specs=[
            pl.BlockSpec((tm, D), lambda i: (jnp.minimum(i, nct - 1), 0)),
            pl.BlockSpec((tm, D), lambda i: (jnp.maximum(i - nct, 0), 0)),
            pl.BlockSpec((tm, D), lambda i: (jnp.maximum(i - nct, 0) % npos, 0)),
        ],
        out_specs=pl.BlockSpec((tm, D), lambda i: (i, 0)),
        out_shape=jax.ShapeDtypeStruct((N_TOK, D), f32),
        compiler_params=_cparams(("parallel",)),
        name="assemble_tokens",
    )(xp2, xs2, pos)


def _pos_table():
    rows = DEC_SEQ // GRID_W
    row = jnp.repeat(jnp.arange(rows, dtype=f32), GRID_W)
    col = jnp.tile(jnp.arange(GRID_W, dtype=f32), rows)

    def sincos(p, d):
        half = d // 2
        omega = 1.0 / (10000.0 ** (jnp.arange(half, dtype=f32) / half))
        ang = p[:, None] * omega[None, :]
        return jnp.concatenate([jnp.sin(ang), jnp.cos(ang)], axis=-1)

    return jnp.concatenate([sincos(row, D // 2), sincos(col, D // 2)], axis=-1)


def _ada_body(c_ref, w_ref, b_ref, o_ref):
    s = _silu(c_ref[...])
    s1, s2 = _split2(s)
    w = w_ref[0].astype(bf16)
    o_ref[0] = _dot(s1, w) + _dot(s2, w) + b_ref[0]


def ada_modulation(cond8, ada_w, ada_b, tn=1024):
    return pl.pallas_call(
        _ada_body,
        grid=(DEPTH, 6 * D // tn),
        in_specs=[
            pl.BlockSpec((MOD_ROWS, D), lambda l, j: (0, 0)),
            pl.BlockSpec((1, D, tn), lambda l, j: (l, 0, j)),
            pl.BlockSpec((1, 1, tn), lambda l, j: (l, 0, j)),
        ],
        out_specs=pl.BlockSpec((1, MOD_ROWS, tn), lambda l, j: (l, 0, j)),
        out_shape=jax.ShapeDtypeStruct((DEPTH, MOD_ROWS, 6 * D), f32),
        compiler_params=_cparams(("parallel", "parallel")),
        name="ada_modulation",
    )(cond8, ada_w, ada_b.reshape(DEPTH, 1, 6 * D))


def _inproj_body(x_ref, sh_ref, sc_ref, g_ref, w_ref, ws_ref, pf_ref, pb_ref, sm_ref, h_s, *, nf):
    j = pl.program_id(1)

    @pl.when(j == 0)
    def _():
        h = _modnorm(x_ref[...], g_ref[0], sc_ref[0, 0], sh_ref[0, 0]).astype(bf16)
        h_s[...] = h
        sm_ref[...] = _dot(h, ws_ref[0])

    y = _dot(h_s[...], w_ref[0])

    @pl.when(j < nf)
    def _():
        pf_ref[...] = y

    @pl.when(j >= nf)
    def _():
        pb_ref[...] = y.astype(bf16)


def in_projection(x, mod, g, w_all, w_small, layer, tm=1024, tn=512):
    nf = N_PF // tn
    nb = N_PB // tn
    return pl.pallas_call(
        functools.partial(_inproj_body, nf=nf),
        grid=(N_TOK // tm, nf + nb),
        in_specs=[
            pl.BlockSpec((tm, D), lambda i, j: (i, 0)),
            _mod_spec(layer, 0, tm),
            _mod_spec(layer, 1, tm),
            pl.BlockSpec((1, 1, D), lambda i, j: (layer, 0, 0)),
            pl.BlockSpec((1, D, tn), lambda i, j: (layer, 0, j)),
            pl.BlockSpec((1, D, 128), lambda i, j: (layer, 0, 0)),
        ],
        out_specs=[
            pl.BlockSpec((tm, tn), lambda i, j: (i, jnp.minimum(j, nf - 1))),
            pl.BlockSpec((tm, tn), lambda i, j: (i, jnp.maximum(j - nf, 0))),
            pl.BlockSpec((tm, 128), lambda i, j: (i, 0)),
        ],
        out_shape=[
            jax.ShapeDtypeStruct((N_TOK, N_PF), f32),
            jax.ShapeDtypeStruct((N_TOK, N_PB), bf16),
            jax.ShapeDtypeStruct((N_TOK, 128), f32),
        ],
        scratch_shapes=[pltpu.VMEM((tm, D), bf16)],
        compiler_params=_cparams(("parallel", "arbitrary")),
        name="in_projection",
    )(x, mod, mod, g, w_all, w_small)


def _gls_dir(q, k, g, d):
    c, sub = GLS_CHUNK, GLS_SUB
    nb = c // sub
    tri = jnp.where(_tri(c, upper=(d == 1)), 1.0, 0.0).astype(bf16)
    b = _dot_sel_lhs(tri, g)
    zero = jnp.zeros((1, 128), f32)
    if d == 0:
        mid = [b[sub * i + sub // 2 - 1:sub * i + sub // 2] for i in range(nb)]
        end = [b[sub * i + sub - 1:sub * i + sub] for i in range(nb)]
        start = [zero] + end[:-1]
        total = end[-1]
    else:
        mid = [b[sub * i + sub // 2:sub * i + sub // 2 + 1] for i in range(nb)]
        end = [b[sub * i:sub * i + 1] for i in range(nb)]
        start = end[1:] + [zero]
        total = end[0]

    def rows(vals):
        return jnp.concatenate([jnp.broadcast_to(v, (sub, 128)) for v in vals], axis=0)

    r_mid = rows(mid)
    qd = q * jnp.exp(b - r_mid)
    kd = k * jnp.exp(r_mid - b)
    qo = qd * rows([jnp.exp(m - s) for m, s in zip(mid, start)])
    ke = kd * rows([jnp.exp(e - m) for m, e in zip(mid, end)])
    q_in = qo * rows([jnp.exp(s) for s in start])
    k_out = ke * rows([jnp.exp(total - e) for e in end])
    kr = []
    for i in range(nb):
        prior = range(i) if d == 0 else range(i + 1, nb)
        if len(prior) == 0:
            kr.append(None)
            continue
        blocks = []
        for jb in range(nb):
            if jb in prior:
                blocks.append(ke[jb * sub:(jb + 1) * sub] * jnp.exp(start[i] - end[jb]))
            else:
                blocks.append(jnp.zeros((sub, 128), f32))
        kr.append(jnp.concatenate(blocks, axis=0).astype(bf16))
    return qd, kd.astype(bf16), qo, kr, q_in, k_out, jnp.exp(total)


def _gls_head(parts, v, st, mask, kmask):
    qd, kd, qo, kr, q_in, k_out, dec = parts
    sub = GLS_SUB
    if kmask is not None:
        qd, qo, q_in, k_out = qd * kmask, qo * kmask, q_in * kmask, k_out * kmask
    att = jnp.where(mask, _dot_nt(qd.astype(bf16), kd), 0.0)
    qob = qo.astype(bf16)
    offs = []
    for i, kri in enumerate(kr):
        if kri is None:
            offs.append(jnp.zeros((sub, GLS_CHUNK), f32))
        else:
            offs.append(_dot_nt(qob[i * sub:(i + 1) * sub], kri))
    att = att + jnp.concatenate(offs, axis=0)
    vb = v.astype(bf16)
    o = _dot(att.astype(bf16), vb) + _dot_nt(q_in.astype(bf16), st.astype(bf16))
    st_new = st * dec + _dot_tn(vb, k_out.astype(bf16))
    return o, st_new


def _gls_masks():
    c, sub = GLS_CHUNK, GLS_SUB
    ii = lax.broadcasted_iota(jnp.int32, (c, c), 0)
    jj = lax.broadcasted_iota(jnp.int32, (c, c), 1)
    same = (ii // sub) == (jj // sub)
    return [same & (ii >= jj), same & (ii <= jj)]


def _head_post(o, gate_in, ng):
    ms = jnp.mean(o * o, axis=-1, keepdims=True)
    return o * lax.rsqrt(ms + EPS) * ng * _silu(gate_in.astype(f32))


def _scan_loops(n, nseq, chunk_fn, post_fn, o_ref, init):
    C = GLS_CHUNK

    def store(r0, outs, final):
        for h, o in enumerate(outs):
            sl = (pl.ds(r0, C), slice(h * 128, (h + 1) * 128))
            if final:
                o_ref[sl] = post_fn(r0, h, o_ref[sl] + o)
            else:
                o_ref[sl] = o

    def half(final):
        def body(t, carry):
            new = []
            for s, (sf, sb) in enumerate(carry):
                r0, of, sf = chunk_fn(s * n + t, 0, sf)
                store(r0, of, final)
                r1, ob, sb = chunk_fn(s * n + n - 1 - t, 1, sb)
                store(r1, ob, final)
                new.append((sf, sb))
            return tuple(new)
        return body

    unroll = 4 if (n // 2) % 4 == 0 else 1
    carry = lax.fori_loop(0, n // 2, half(False), tuple(init), unroll=unroll)
    return lax.fori_loop(n // 2, n, half(True), carry, unroll=unroll)


def _gla_body(*refs, L, nseq, has_init, chain_state):
    it = iter(refs)
    q_ref, k_ref, v_ref, r_ref, sm_ref, wah_ref, wal_ref, ba_ref, ng_ref = [next(it) for _ in range(9)]
    s0_ref = next(it) if has_init else None
    if has_init or chain_state:
        next(it)
    o_ref = next(it)
    st_ref = None if has_init else next(it)
    C = GLS_CHUNK
    n = L // C
    lane = lax.broadcasted_iota(jnp.int32, (1, 128), 1)
    kmasks = [jnp.where(lane < 64, 1.0, 0.0), jnp.where(lane >= 64, 1.0, 0.0)]
    masks = _gls_masks()
    wah, wal, ba, ng = wah_ref[0, 0], wal_ref[0, 0], ba_ref[0, 0], ng_ref[0]

    def chunk(cidx, d, sts):
        r0 = pl.multiple_of(cidx * C, C)
        q = q_ref[pl.ds(r0, C), :].astype(f32) * (GLA_DK ** -0.5)
        k = k_ref[pl.ds(r0, C), :].astype(f32)
        sm = sm_ref[pl.ds(r0, C), :]
        sl = slice(d * 128, (d + 1) * 128)
        g = _log_sigmoid(_dot_hi(sm, wah[:, sl], wal[:, sl]) + ba[:, sl]) * (1.0 / GLA_NORMALIZER)
        parts = _gls_dir(q, k, g, d)
        outs, new = [], []
        for h in range(2):
            v = v_ref[pl.ds(r0, C), h * 128:(h + 1) * 128]
            o, s = _gls_head(parts, v, sts[h], masks[d], kmasks[h])
            outs.append(o)
            new.append(s)
        return r0, outs, new

    def post(r0, h, tot):
        return _head_post(tot, r_ref[pl.ds(r0, C), h * 128:(h + 1) * 128], ng)

    def init(d, h):
        if not has_init:
            return jnp.zeros((128, 128), f32)
        s = s0_ref[0, 0, d, h]
        z = jnp.zeros((64, 128), f32)
        full = jnp.concatenate([s, z], axis=0) if h == 0 else jnp.concatenate([z, s], axis=0)
        return full.T

    states = _scan_loops(n, nseq, chunk, post, o_ref,
                         [([init(0, 0), init(0, 1)], [init(1, 0), init(1, 1)]) for _ in range(nseq)])
    if st_ref is not None:
        for s, dirs in enumerate(states):
            for d, sts in enumerate(dirs):
                for h in range(2):
                    st_ref[s, 0, d, h] = sts[h].T[h * 64:(h + 1) * 64, :]


def gla_mixer(SM, PB, wa_hi, wa_lo, ba, ng, *, B, L, row0, layer, nseq=1, state=None, prev=None, st_prev=None):
    has_init = state is not None
    chain_state = st_prev is not None
    R = nseq * L
    in_specs = [
        pl.BlockSpec((R, 128), lambda b, hp: (row0 + b, B_GLA_Q // 128 + hp)),
        pl.BlockSpec((R, 128), lambda b, hp: (row0 + b, B_GLA_K // 128 + hp)),
        pl.BlockSpec((R, 256), lambda b, hp: (row0 + b, B_GLA_V // 256 + hp)),
        pl.BlockSpec((R, 256), lambda b, hp: (row0 + b, B_GLA_R // 256 + hp)),
        pl.BlockSpec((R, 128), lambda b, hp: (row0 + b, 0)),
        pl.BlockSpec((1, 1, 128, 256), lambda b, hp: (layer, hp, 0, 0)),
        pl.BlockSpec((1, 1, 128, 256), lambda b, hp: (layer, hp, 0, 0)),
        pl.BlockSpec((1, 1, 1, 256), lambda b, hp: (layer, hp, 0, 0)),
        pl.BlockSpec((1, 1, 128), lambda b, hp: (layer, 0, 0)),
    ]
    args = [PB, PB, PB, PB, SM, wa_hi, wa_lo, ba, ng]
    aliases = {}
    out_specs = [pl.BlockSpec((R, 256), lambda b, hp: (row0 + b, hp))]
    out_shape = [jax.ShapeDtypeStruct((N_TOK, MIX_W), f32)]
    if has_init:
        assert nseq == 1
        in_specs.append(pl.BlockSpec((1, 1, 2, 2, 64, 128), lambda b, hp: (b, layer, 0, hp, 0, 0)))
        args.append(state)
        in_specs.append(pl.BlockSpec(memory_space=pl.ANY))
        args.append(prev)
        aliases = {len(args) - 1: 0}
    else:
        if chain_state:
            in_specs.append(pl.BlockSpec(memory_space=pl.ANY))
            args.append(st_prev)
            aliases = {len(args) - 1: 1}
        out_specs.append(pl.BlockSpec((nseq, 1, 2, 2, 64, 128), lambda b, hp: (b, layer, 0, hp, 0, 0)))
        out_shape.append(jax.ShapeDtypeStruct((B, DEPTH, 2, GLA_HEADS, GLA_DK, GLA_DV), f32))
    return pl.pallas_call(
        functools.partial(_gla_body, L=L, nseq=nseq, has_init=has_init, chain_state=chain_state),
        grid=(B // nseq, 2),
        in_specs=in_specs,
        out_specs=out_specs,
        out_shape=out_shape,
        input_output_aliases=aliases,
        compiler_params=_cparams(("parallel", "parallel")),
        name="gla_mixer",
    )(*args)


def _hgrn_body(*refs, L, nseq, has_init, chain_state):
    it = iter(refs)
    q_ref, f0_ref, f1_ref, i_ref, g_ref, lb_ref, ng_ref = [next(it) for _ in range(7)]
    s0_ref = next(it) if has_init else None
    if has_init or chain_state:
        next(it)
    o_ref = next(it)
    st_ref = None if has_init else next(it)
    C = GLS_CHUNK
    n = L // C
    masks = _gls_masks()
    ng = ng_ref[0]
    f_refs = (f0_ref, f1_ref)

    def chunk(cidx, d, sts):
        r0 = pl.multiple_of(cidx * C, C)
        outs, new = [], []
        for h in range(2):
            sl = slice(h * 128, (h + 1) * 128)
            lbp = lb_ref[0, h]
            q = q_ref[pl.ds(r0, C), sl].astype(f32)
            fp = f_refs[d][pl.ds(r0, C), sl]
            lb, one_m_lb = lbp[2 * d:2 * d + 1], lbp[2 * d + 1:2 * d + 2]
            e = jnp.exp(-jnp.abs(fp))
            inv = 1.0 / (1.0 + e)
            pos = fp >= 0.0
            sig = jnp.where(pos, inv, e * inv)
            k = one_m_lb * jnp.where(pos, e * inv, inv)
            g = jnp.log(lb + one_m_lb * sig)
            parts = _gls_dir(q, k, g, d)
            o, s = _gls_head(parts, i_ref[pl.ds(r0, C), sl], sts[h], masks[d], None)
            outs.append(o)
            new.append(s)
        return r0, outs, new

    def post(r0, h, tot):
        return _head_post(tot, g_ref[pl.ds(r0, C), h * 128:(h + 1) * 128], ng)

    def init(d, h):
        if not has_init:
            return jnp.zeros((128, 128), f32)
        return s0_ref[0, 0, d, h].T

    states = _scan_loops(n, nseq, chunk, post, o_ref,
                         [([init(0, 0), init(0, 1)], [init(1, 0), init(1, 1)]) for _ in range(nseq)])
    if st_ref is not None:
        for s, dirs in enumerate(states):
            for d, sts in enumerate(dirs):
                for h in range(2):
                    st_ref[s, 0, d, h] = sts[h].T


def hgrn_mixer(PF, PB, lbp, ng, *, B, L, row0, layer, nseq=1, state=None, prev=None, st_prev=None):
    has_init = state is not None
    chain_state = st_prev is not None
    R = nseq * L
    in_specs = [
        pl.BlockSpec((R, 256), lambda b, hp: (row0 + b, B_HG_Q // 256 + hp)),
        pl.BlockSpec((R, 256), lambda b, hp: (row0 + b, F_HG_F // 256 + hp)),
        pl.BlockSpec((R, 256), lambda b, hp: (row0 + b, F_HG_F // 256 + 2 + hp)),
        pl.BlockSpec((R, 256), lambda b, hp: (row0 + b, B_HG_I // 256 + hp)),
        pl.BlockSpec((R, 256), lambda b, hp: (row0 + b, B_HG_G // 256 + hp)),
        pl.BlockSpec((1, 2, 8, 128), lambda b, hp: (layer, hp, 0, 0)),
        pl.BlockSpec((1, 1, 128), lambda b, hp: (layer, 0, 0)),
    ]
    args = [PB, PF, PF, PB, PB, lbp, ng]
    aliases = {}
    out_specs = [pl.BlockSpec((R, 256), lambda b, hp: (row0 + b, hp))]
    out_shape = [jax.ShapeDtypeStruct((N_TOK, MIX_W), f32)]
    if has_init:
        assert nseq == 1
        in_specs.append(pl.BlockSpec((1, 1, 2, 2, 128, 128), lambda b, hp: (b, layer, 0, hp, 0, 0)))
        args.append(state)
        in_specs.append(pl.BlockSpec(memory_space=pl.ANY))
        args.append(prev)
        aliases = {len(args) - 1: 0}
    else:
        if chain_state:
            in_specs.append(pl.BlockSpec(memory_space=pl.ANY))
            args.append(st_prev)
            aliases = {len(args) - 1: 1}
        out_specs.append(pl.BlockSpec((nseq, 1, 2, 2, 128, 128), lambda b, hp: (b, layer, 0, hp, 0, 0)))
        out_shape.append(jax.ShapeDtypeStruct((B, DEPTH, 2, HGRN_HEADS, HGRN_DIM, HGRN_DIM), f32))
    return pl.pallas_call(
        functools.partial(_hgrn_body, L=L, nseq=nseq, has_init=has_init, chain_state=chain_state),
        grid=(B // nseq, 2),
        in_specs=in_specs,
        out_specs=out_specs,
        out_shape=out_shape,
        input_output_aliases=aliases,
        compiler_params=_cparams(("parallel", "parallel")),
        name="hgrn_mixer",
    )(*args)


def _ssd_body(*refs, L, has_init, chain_state):
    it = iter(refs)
    (xs_ref, bm_ref, cm_ref, z_ref, sm_ref, cw_ref, dtb_ref, alog_ref, e64_ref, e128_ref,
     dsk_ref, ng_ref) = [next(it) for _ in range(12)]
    s0_ref = next(it) if has_init else None
    if has_init or chain_state:
        next(it)
    o_ref = next(it)
    st_ref = None if has_init else next(it)
    xc_s = next(it)
    C = SSD_CHUNK
    n = L // C
    cw = cw_ref[0, 0]
    rowi = lax.broadcasted_iota(jnp.int32, (C, 1), 0)

    def conv_block(rb, _):
        r0 = pl.multiple_of(rb * C, C)
        rp = pl.multiple_of(jnp.maximum(r0 - 16, 0), 16)
        rn = pl.multiple_of(jnp.minimum(r0 + C, L - 16), 16)
        has_p = jnp.where(rb > 0, 1.0, 0.0)
        has_n = jnp.where(rb < n - 1, 1.0, 0.0)
        for ref, c0, w in ((xs_ref, 0, 256), (bm_ref, 256, 128), (cm_ref, 384, 128)):
            x = ref[pl.ds(r0, C), :].astype(f32)
            prev = ref[pl.ds(rp, 16), :].astype(f32)[15:16] * has_p
            nxt = ref[pl.ds(rn, 16), :].astype(f32)[0:1] * has_n
            x_dn = jnp.where(rowi == 0, prev, pltpu.roll(x, 1, axis=0))
            x_up = jnp.where(rowi == C - 1, nxt, pltpu.roll(x, C - 1, axis=0))
            y = (cw[0:1, c0:c0 + w] * x_dn + cw[1:2, c0:c0 + w] * x + cw[2:3, c0:c0 + w] * x_up
                 + cw[3:4, c0:c0 + w])
            xc_s[pl.ds(r0, C), c0:c0 + w] = _silu(y)
        return 0

    lax.fori_loop(0, n, conv_block, 0)

    masks = [_tri(C, upper=False), _tri(C, upper=True)]
    tris = [jnp.where(m, 1.0, 0.0).astype(bf16) for m in masks]
    lane256 = lax.broadcasted_iota(jnp.int32, (1, 256), 1)
    hmask = [jnp.where((lane256 >= h * 64) & (lane256 < (h + 1) * 64), 1.0, 0.0) for h in range(4)]
    dtb = dtb_ref[0]
    a_row = -jnp.exp(alog_ref[0])
    neg_inf = jnp.float32(-jnp.inf)

    def chunk(cidx, d, st):
        r0 = pl.multiple_of(cidx * C, C)
        xs = xc_s[pl.ds(r0, C), 0:256]
        bm = xc_s[pl.ds(r0, C), 256:384].astype(bf16)
        cm = xc_s[pl.ds(r0, C), 384:512].astype(bf16)
        dt_all = _softplus(sm_ref[pl.ds(r0, C), :] + dtb)
        a_all = dt_all * a_row
        dt64 = _dot_sel_rhs(dt_all, e64_ref[0, d])
        acs_all = _dot_sel_lhs(tris[d], a_all)
        acs64 = _dot_sel_rhs3(acs_all, e64_ref[0, d])
        acs128 = _dot_sel_rhs3(acs_all, e128_ref[0, d])
        end = acs64[C - 1:C] if d == 0 else acs64[0:1]
        xdt = xs * dt64
        scores = _dot_nt(cm, bm)
        y = jnp.exp(acs64) * _dot(cm, st.astype(bf16))
        for h in range(4):
            colb = acs128[:, h * 128:(h + 1) * 128]
            diff = jnp.where(masks[d], colb - colb.T, neg_inf)
            p = (scores * jnp.exp(diff)).astype(bf16)
            y = y + _dot(p, (xdt * hmask[h]).astype(bf16))
        xd = (xdt * jnp.exp(end - acs64)).astype(bf16)
        st_new = st * jnp.exp(end) + _dot_tn(bm, xd)
        return r0, y, st_new

    dsk = dsk_ref[0, 0]
    ng = ng_ref[0, 0]

    def post(r0, y):
        xs = xc_s[pl.ds(r0, C), 0:256]
        t = (y + dsk * xs) * _silu(z_ref[pl.ds(r0, C), :].astype(f32))
        ms = jnp.mean(t * t, axis=-1, keepdims=True)
        return t * lax.rsqrt(ms + EPS) * ng

    def first_half(t, carry):
        sf, sb = carry
        r0, yf, sf = chunk(t, 0, sf)
        o_ref[pl.ds(r0, C), :] = yf
        r1, yb, sb = chunk(n - 1 - t, 1, sb)
        o_ref[pl.ds(r1, C), :] = yb
        return sf, sb

    def second_half(t, carry):
        sf, sb = carry
        r0, yf, sf = chunk(t, 0, sf)
        o_ref[pl.ds(r0, C), :] = post(r0, o_ref[pl.ds(r0, C), :] + yf)
        r1, yb, sb = chunk(n - 1 - t, 1, sb)
        o_ref[pl.ds(r1, C), :] = post(r1, o_ref[pl.ds(r1, C), :] + yb)
        return sf, sb

    def init(d):
        if not has_init:
            return jnp.zeros((128, 256), f32)
        return s0_ref[0, 0, d].reshape(256, 128).T

    carry = (init(0), init(1))
    unroll = 4 if (n // 2) % 4 == 0 else 1
    carry = lax.fori_loop(0, n // 2, first_half, carry, unroll=unroll)
    sf, sb = lax.fori_loop(n // 2, n, second_half, carry, unroll=unroll)
    if st_ref is not None:
        st_ref[0, 0, 0] = sf.T.reshape(4, 64, 128)
        st_ref[0, 0, 1] = sb.T.reshape(4, 64, 128)


def ssd_mixer(SM, PB, cw, dtb, alog, e64, e128, dsk, ng, *, B, L, row0, layer, state=None, prev=None,
              st_prev=None):
    has_init = state is not None
    chain_state = st_prev is not None
    xb = B_SSM_XBC
    in_specs = [
        pl.BlockSpec((L, 256), lambda b, g: (row0 + b, xb // 256 + g)),
        pl.BlockSpec((L, 128), lambda b, g: (row0 + b, (xb + 512) // 128 + g)),
        pl.BlockSpec((L, 128), lambda b, g: (row0 + b, (xb + 768) // 128 + g)),
        pl.BlockSpec((L, 256), lambda b, g: (row0 + b, B_SSM_Z // 256 + g)),
        pl.BlockSpec((L, 128), lambda b, g: (row0 + b, 0)),
        pl.BlockSpec((1, 1, 4, 512), lambda b, g: (layer, g, 0, 0)),
        pl.BlockSpec((1, 1, 128), lambda b, g: (layer, 0, 0)),
        pl.BlockSpec((1, 1, 128), lambda b, g: (layer, 0, 0)),
        pl.BlockSpec((1, 2, 128, 256), lambda b, g: (g, 0, 0, 0)),
        pl.BlockSpec((1, 2, 128, 512), lambda b, g: (g, 0, 0, 0)),
        pl.BlockSpec((1, 1, 1, 256), lambda b, g: (layer, g, 0, 0)),
        pl.BlockSpec((1, 1, 1, 256), lambda b, g: (layer, g, 0, 0)),
    ]
    args = [PB, PB, PB, PB, SM, cw, dtb, alog, e64, e128, dsk, ng]
    aliases = {}
    out_specs = [pl.BlockSpec((L, 256), lambda b, g: (row0 + b, g))]
    out_shape = [jax.ShapeDtypeStruct((N_TOK, MIX_W), f32)]
    if has_init:
        in_specs.append(pl.BlockSpec((1, 1, 2, 4, 64, 128), lambda b, g: (b, layer, 0, g, 0, 0)))
        args.append(state)
        in_specs.append(pl.BlockSpec(memory_space=pl.ANY))
        args.append(prev)
        aliases = {len(args) - 1: 0}
    else:
        if chain_state:
            in_specs.append(pl.BlockSpec(memory_space=pl.ANY))
            args.append(st_prev)
            aliases = {len(args) - 1: 1}
        out_specs.append(pl.BlockSpec((1, 1, 2, 4, 64, 128), lambda b, g: (b, layer, 0, g, 0, 0)))
        out_shape.append(jax.ShapeDtypeStruct((B, DEPTH, 2, SSM_HEADS, SSM_HEADDIM, SSM_STATE), f32))
    return pl.pallas_call(
        functools.partial(_ssd_body, L=L, has_init=has_init, chain_state=chain_state),
        grid=(B, SSM_GROUPS),
        in_specs=in_specs,
        out_specs=out_specs,
        out_shape=out_shape,
        scratch_shapes=[pltpu.VMEM((L, 512), f32)],
        input_output_aliases=aliases,
        compiler_params=_cparams(("parallel", "parallel")),
        name="ssd_mixer",
    )(*args)


def _s5_body(*refs, n, n_c, s_b, has_init, chain_state):
    it = iter(refs)
    u_ref, wt_ref, wsf_ref, wsb_ref, wof_ref, wob_ref, lam_ref, dsk_ref = [next(it) for _ in range(8)]
    if has_init:
        h0_ref = next(it)
        next(it)
    elif chain_state:
        next(it)
    o_ref = next(it)
    if not has_init:
        hs_ref = next(it)
    zf_s, zb_s, hf_s, hb_s, zfs_s, zbs_s, acc_s, za_s, zc_s = [next(it) for _ in range(9)]
    T = S5_CHUNK

    for j in range(T):
        ub = u_ref[pl.ds(j, n, stride=T), :].astype(bf16)
        a = _dot(ub, wt_ref[0, 0, :, (T - 1 - j) * 128:(2 * T - 1 - j) * 128])
        f = _dot(ub, wsf_ref[0, 0, 0, j])
        b = _dot(ub, wsb_ref[0, 0, 0, j])
        if j == 0:
            acc_s[...] = a
            za_s[...] = f
            zc_s[...] = b
        else:
            acc_s[...] += a
            za_s[...] += f
            zc_s[...] += b
    for kk in range(8):
        zf_s[kk] = za_s[:, kk * 128:(kk + 1) * 128]
        zb_s[kk] = zc_s[:, kk * 128:(kk + 1) * 128]

    lam = lam_ref[0, 0]

    for kk in range(8):
        zfs_s[kk] = pltpu.roll(zf_s[kk], 64, axis=1)
        zbs_s[kk] = pltpu.roll(zb_s[kk], 64, axis=1)

    def rows(c):
        return pl.ds(c, s_b, stride=n_c) if s_b > 1 else pl.ds(c, 1)

    def scan_step(c, h, hs, z_s, zs_s, h_s, la, lb):
        new, news = [], []
        for kk in range(8):
            h_s[kk, rows(c), :] = h[kk]
            sl = slice(kk * 128, (kk + 1) * 128)
            new.append(la[:, sl] * h[kk] + lb[:, sl] * hs[kk] + z_s[kk, rows(c), :])
            news.append(la[:, sl] * hs[kk] - lb[:, sl] * h[kk] + zs_s[kk, rows(c), :])
        return tuple(new), tuple(news)

    def step(t, carry):
        hf, hfs, hb, hbs = carry
        hf, hfs = scan_step(t, hf, hfs, zf_s, zfs_s, hf_s, lam[0:1], lam[1:2])
        hb, hbs = scan_step(n_c - 1 - t, hb, hbs, zb_s, zbs_s, hb_s, lam[2:3], lam[3:4])
        return hf, hfs, hb, hbs

    if has_init:
        h0f = tuple(h0_ref[:, 0, 0, kk * 128:(kk + 1) * 128] for kk in range(8))
        h0b = tuple(h0_ref[:, 0, 1, kk * 128:(kk + 1) * 128] for kk in range(8))
    else:
        h0f = h0b = tuple(jnp.zeros((s_b, 128), f32) for _ in range(8))
    swap = lambda hh: tuple(pltpu.roll(v, 64, axis=1) for v in hh)
    hf, _, hb, _ = lax.fori_loop(0, n_c, step, (h0f, swap(h0f), h0b, swap(h0b)))
    if not has_init:
        for kk in range(8):
            hs_ref[:, 0, 0, kk * 128:(kk + 1) * 128] = hf[kk]
            hs_ref[:, 0, 1, kk * 128:(kk + 1) * 128] = hb[kk]

    hin_f = jnp.concatenate([hf_s[kk] for kk in range(8)], axis=1).astype(bf16)
    hin_b = jnp.concatenate([hb_s[kk] for kk in range(8)], axis=1).astype(bf16)
    acc_s[...] += _dot(hin_f, wof_ref[0, 0, 0])
    acc_s[...] += _dot(hin_b, wob_ref[0, 0, 0])
    dsk = dsk_ref[0, 0]
    for i in range(T):
        o_ref[pl.ds(i, n, stride=T), :] = (acc_s[:, i * 128:(i + 1) * 128]
                                           + dsk * u_ref[pl.ds(i, n, stride=T), :])


def s5_mixer(PF, w, dsk, *, B, L, s_b, row0, layer, state=None, prev=None, st_prev=None):
    has_init = state is not None
    chain_state = st_prev is not None
    n_c = L // S5_CHUNK
    n = s_b * n_c
    rows = s_b * L
    one = pl.Buffered(1)
    in_specs = [
        pl.BlockSpec((rows, 128), lambda lb, sb: (row0 + sb, F_S5_U // 128 + lb)),
        pl.BlockSpec((1, 1, 128, 31 * 128), lambda lb, sb: (layer, lb, 0, 0), pipeline_mode=one),
        pl.BlockSpec((1, 1, 1, 16, 128, 1024), lambda lb, sb: (layer, 0, lb, 0, 0, 0), pipeline_mode=one),
        pl.BlockSpec((1, 1, 1, 16, 128, 1024), lambda lb, sb: (layer, 1, lb, 0, 0, 0), pipeline_mode=one),
        pl.BlockSpec((1, 1, 1, 1024, 2048), lambda lb, sb: (layer, 0, lb, 0, 0), pipeline_mode=one),
        pl.BlockSpec((1, 1, 1, 1024, 2048), lambda lb, sb: (layer, 1, lb, 0, 0), pipeline_mode=one),
        pl.BlockSpec((1, 1, 4, 1024), lambda lb, sb: (layer, lb, 0, 0)),
        pl.BlockSpec((1, 1, 1, 128), lambda lb, sb: (layer, lb, 0, 0)),
    ]
    args = [PF, w["toep"], w["ws"], w["ws"], w["wo"], w["wo"], w["lam16"], dsk]
    aliases = {}
    out_specs = [pl.BlockSpec((rows, 128), lambda lb, sb: (row0 + sb, lb))]
    out_shape = [jax.ShapeDtypeStruct((N_TOK, MIX_W), f32)]
    if has_init:
        in_specs.append(pl.BlockSpec((s_b, 1, 2, 1024), lambda lb, sb: (sb, layer, 0, lb)))
        args.append(state)
        in_specs.append(pl.BlockSpec(memory_space=pl.ANY))
        args.append(prev)
        aliases = {len(args) - 1: 0}
    else:
        if chain_state:
            in_specs.append(pl.BlockSpec(memory_space=pl.ANY))
            args.append(st_prev)
            aliases = {len(args) - 1: 1}
        out_specs.append(pl.BlockSpec((s_b, 1, 2, 1024), lambda lb, sb: (sb, layer, 0, lb)))
        out_shape.append(jax.ShapeDtypeStruct((B, DEPTH, 2, 2 * S5_GROUPS * S5_STATE), f32))
    return pl.pallas_call(
        functools.partial(_s5_body, n=n, n_c=n_c, s_b=s_b, has_init=has_init, chain_state=chain_state),
        grid=(4, B // s_b),
        in_specs=in_specs,
        out_specs=out_specs,
        out_shape=out_shape,
        scratch_shapes=([pltpu.VMEM((8, n, 128), f32)] * 6
                        + [pltpu.VMEM((n, 2048), f32), pltpu.VMEM((n, 1024), f32), pltpu.VMEM((n, 1024), f32)]),
        input_output_aliases=aliases,
        compiler_params=_cparams(("parallel", "parallel")),
        name="s5_mixer",
    )(*args)


def _s5_layer_weights(a_re, a_im, log_dt, b_re, b_im, c_re, c_im):
    hp = lax.Precision.HIGHEST
    T = S5_CHUNK
    dt = jnp.exp(log_dt)[..., None]
    lmag, ang = a_re * dt, a_im * dt
    taus = jnp.arange(T + 1, dtype=f32)[:, None, None, None]
    mag = jnp.exp(lmag[None] * taus)
    pr = mag * jnp.cos(ang[None] * taus)
    pi = mag * jnp.sin(ang[None] * taus)
    lam_r, lam_i = pr[1], pi[1]
    den = a_re * a_re + a_im * a_im
    zr = ((lam_r - 1.0) * a_re + lam_i * a_im) / den
    zi = (lam_i * a_re - (lam_r - 1.0) * a_im) / den
    bb_r = zr[..., None] * b_re[None] - zi[..., None] * b_im[None]
    bb_i = zr[..., None] * b_im[None] + zi[..., None] * b_re[None]
    lb_r = pr[..., None] * bb_r[None] - pi[..., None] * bb_i[None]
    lb_i = pr[..., None] * bb_i[None] + pi[..., None] * bb_r[None]
    pr, pi, lb_r, lb_i = lax.optimization_barrier((pr, pi, lb_r, lb_i))
    kk = jnp.sum(c_re[None, None, :, :, :, None] * lb_r[:, :, :, None, :, :]
                 - c_im[None, None, :, :, :, None] * lb_i[:, :, :, None, :, :], axis=4)
    eye8 = jnp.eye(8, dtype=f32)

    def blockdiag(m):
        lead = m.shape[:-3]
        a, b = m.shape[-2], m.shape[-1]
        m = m.reshape(lead + (4, 8, a, b))
        out = m[..., :, :, :, None, :] * eye8[:, None, :, None]
        return out.reshape(lead + (4, 8 * a, 8 * b))

    kt = jnp.swapaxes(kk, -1, -2)
    down = np.arange(T - 1, 0, -1)
    deltas = jnp.concatenate([kt[down, 1], kt[0:1, 0] + kt[0:1, 1], kt[1:T, 0]], axis=0)
    toep = blockdiag(deltas)
    toep = jnp.transpose(toep, (1, 2, 0, 3)).reshape(4, 128, 31 * 128)

    def state_in(lbr, lbi):
        return jnp.concatenate([jnp.swapaxes(lbr, -1, -2), jnp.swapaxes(lbi, -1, -2)], axis=-1)

    rev16 = np.arange(T - 1, -1, -1)
    ws_c = jnp.stack([state_in(lb_r[rev16, 0], lb_i[rev16, 0]),
                      state_in(lb_r[:T, 1], lb_i[:T, 1])], axis=0)

    def state_out(prd, pid):
        cr = c_re[None] * prd[:, :, None, :] - c_im[None] * pid[:, :, None, :]
        ci = c_re[None] * pid[:, :, None, :] + c_im[None] * prd[:, :, None, :]
        return jnp.concatenate([cr, -ci], axis=-1)

    down16 = np.arange(T, 0, -1)
    wo_c = jnp.stack([state_out(pr[1:T + 1, 0], pi[1:T + 1, 0]),
                      state_out(pr[down16, 1], pi[down16, 1])], axis=0)
    la = jnp.concatenate([pr[T], pr[T]], axis=-1)
    lb = jnp.concatenate([-pi[T], pi[T]], axis=-1)
    lam16 = jnp.stack([la[0], lb[0], la[1], lb[1]], axis=0)
    lam16 = jnp.transpose(lam16.reshape(4, 4, 1024), (1, 0, 2))
    return dict(toep=toep.astype(bf16), ws_c=ws_c, wo_c=wo_c, lam16=lam16)


def _s5_expand_in_body(c_ref, o_ref):
    o_ref[...] = jnp.zeros(o_ref.shape, o_ref.dtype)
    for j in range(S5_CHUNK):
        for g in range(8):
            o_ref[0, 0, 0, j, g * 16:(g + 1) * 16, g * 128:(g + 1) * 128] = c_ref[0, 0, j, g].astype(bf16)


def _s5_expand_out_body(c_ref, e_ref, o_ref):
    for i in range(S5_CHUNK):
        for g in range(8):
            tile = _dot_tn(c_ref[0, 0, i, g].astype(bf16), e_ref[g])
            o_ref[0, 0, 0, g * 128:(g + 1) * 128, i * 128:(i + 1) * 128] = tile.astype(bf16)


def s5_expand_weights(ws_c, wo_c):
    c_spec = pl.BlockSpec((1, 1, S5_CHUNK, 8, S5_CH, 128), lambda l, d, lb: (l, d, 0, lb, 0, 0))
    ws = pl.pallas_call(
        _s5_expand_in_body,
        grid=(DEPTH, 2, 4),
        in_specs=[c_spec],
        out_specs=pl.BlockSpec((1, 1, 1, S5_CHUNK, 128, 1024), lambda l, d, lb: (l, d, lb, 0, 0, 0)),
        out_shape=jax.ShapeDtypeStruct((DEPTH, 2, 4, S5_CHUNK, 128, 1024), bf16),
        compiler_params=_cparams(("parallel", "parallel", "parallel")),
        name="s5_expand_in",
    )(ws_c)
    place = np.zeros((8, S5_CH, 128), np.float32)
    for g in range(8):
        place[g, np.arange(S5_CH), g * S5_CH + np.arange(S5_CH)] = 1.0
    wo = pl.pallas_call(
        _s5_expand_out_body,
        grid=(DEPTH, 2, 4),
        in_specs=[c_spec, pl.BlockSpec((8, S5_CH, 128), lambda l, d, lb: (0, 0, 0))],
        out_specs=pl.BlockSpec((1, 1, 1, 1024, 2048), lambda l, d, lb: (l, d, lb, 0, 0)),
        out_shape=jax.ShapeDtypeStruct((DEPTH, 2, 4, 1024, 2048), bf16),
        compiler_params=_cparams(("parallel", "parallel", "parallel")),
        name="s5_expand_out",
    )(wo_c, jnp.asarray(place, bf16))
    return ws, wo


def _merge_body(gp_ref, b0_ref, b1_ref, b2_ref, b3_ref, gw_ref, gb_ref, wb_ref, wo_ref, x_ref, g1_ref, o_ref,
                acc_s):
    b = pl.program_id(1)
    for bb, br_ref in enumerate((b0_ref, b1_ref, b2_ref, b3_ref)):
        @pl.when(b == bb)
        def _(br_ref=br_ref, bb=bb):
            br = br_ref[...]
            if bb == 3:
                g = jax.nn.gelu(br)
                br = g * jax.nn.sigmoid(_dot(g.astype(bf16), gw_ref[0]) + gb_ref[0])
            up = _dot(br.astype(bf16), wb_ref[0, bb])
            t = (0.5 * jnp.tanh(0.5 * gp_ref[...].astype(f32)) + 0.5) * up
            if bb == 0:
                acc_s[...] = t
            else:
                acc_s[...] += t

    @pl.when(b == 3)
    def _():
        mix = _dot(acc_s[...].astype(bf16), wo_ref[0])
        o_ref[...] = x_ref[...] + g1_ref[0, 0] * mix


def merge_branches(PB, branches, glu_w, glu_b, w_branch, w_out, x, mod, layer, tm=512):
    br_spec = pl.BlockSpec((tm, MIX_W), lambda i, b: (i, 0))
    return pl.pallas_call(
        _merge_body,
        grid=(N_TOK // tm, 4),
        in_specs=[
            pl.BlockSpec((tm, D), lambda i, b: (i, B_GATE // D + b)),
            br_spec, br_spec, br_spec, br_spec,
            pl.BlockSpec((1, MIX_W, MIX_W), lambda i, b: (layer, 0, 0)),
            pl.BlockSpec((1, 1, MIX_W), lambda i, b: (layer, 0, 0)),
            pl.BlockSpec((1, 4, MIX_W, D), lambda i, b: (layer, 0, 0, 0), pipeline_mode=pl.Buffered(1)),
            pl.BlockSpec((1, D, D), lambda i, b: (layer, 0, 0), pipeline_mode=pl.Buffered(1)),
            pl.BlockSpec((tm, D), lambda i, b: (i, 0)),
            _mod_spec(layer, 2, tm),
        ],
        out_specs=pl.BlockSpec((tm, D), lambda i, b: (i, 0)),
        out_shape=jax.ShapeDtypeStruct((N_TOK, D), f32),
        scratch_shapes=[pltpu.VMEM((tm, D), f32)],
        compiler_params=_cparams(("parallel", "arbitrary")),
        name="merge_branches",
    )(PB, *branches, glu_w, glu_b, w_branch, w_out, x, mod)


def _ffn_body(*refs, n_e, steps_per_e, final_norm):
    it = iter(refs)
    x_ref, sh_ref, sc_ref, gt_ref, g_ref, w1_ref, w3_ref, w2_ref = [next(it) for _ in range(8)]
    if n_e > 1:
        rwh_ref, rwl_ref, rb_ref = [next(it) for _ in range(3)]
    fg_ref = next(it) if final_norm else None
    o_ref = next(it)
    h_s = next(it)
    acc_s = o_ref
    gate_s = next(it) if n_e > 1 else None
    j = pl.program_id(1)
    nj = pl.num_programs(1)

    @pl.when(j == 0)
    def _():
        h = _modnorm(x_ref[...], g_ref[0], sc_ref[0, 0], sh_ref[0, 0])
        h_s[...] = h.astype(bf16)
        if n_e > 1:
            logits = _dot_hi(h, rwh_ref[0], rwl_ref[0]) + rb_ref[0]
            lane = lax.broadcasted_iota(jnp.int32, logits.shape, 1)
            neg = jnp.float32(-jnp.inf)
            lv = jnp.where(lane < n_e, logits, neg)
            m1 = jnp.max(lv, axis=1, keepdims=True)
            i1 = jnp.min(jnp.where(lv == m1, lane, 128), axis=1, keepdims=True)
            lv2 = jnp.where(lane == i1, neg, lv)
            m2 = jnp.max(lv2, axis=1, keepdims=True)
            i2 = jnp.min(jnp.where(lv2 == m2, lane, 128), axis=1, keepdims=True)
            e2 = jnp.exp(m2 - m1)
            p1 = 1.0 / (1.0 + e2)
            gate_s[...] = jnp.where(lane == i1, p1, 0.0) + jnp.where(lane == i2, e2 * p1, 0.0)

    hb = h_s[...]
    hid = _silu(_dot(hb, w1_ref[0, 0])) * _dot(hb, w3_ref[0, 0])
    if n_e > 1:
        e = j // steps_per_e
        lane = lax.broadcasted_iota(jnp.int32, gate_s.shape, 1)
        hid = hid * jnp.sum(jnp.where(lane == e, gate_s[...], 0.0), axis=1, keepdims=True)
    hidb = hid.astype(bf16)
    cb = 512

    @pl.when(j == 0)
    def _():
        for c0 in range(0, D, cb):
            acc_s[:, c0:c0 + cb] = _dot(hidb, w2_ref[0, 0, :, c0:c0 + cb])

    @pl.when(j > 0)
    def _():
        for c0 in range(0, D, cb):
            acc_s[:, c0:c0 + cb] += _dot(hidb, w2_ref[0, 0, :, c0:c0 + cb])

    @pl.when(j == nj - 1)
    def _():
        y = x_ref[...] + gt_ref[0, 0] * acc_s[...]
        if final_norm:
            ms = jnp.mean(y * y, axis=-1, keepdims=True)
            y = y * lax.rsqrt(ms + EPS) * fg_ref[...]
        o_ref[...] = y


def ffn_layer(x, mod, g, w1, w3, w2, layer, widx, router=None, final_g=None, rows=None, tm=1024, tf=512):
    rows = (0, N_TOK) if rows is None else rows
    n_e, f = w1.shape[1], w1.shape[3]
    spe = f // tf
    final = final_g is not None
    tile0, n_tiles = rows[0] // tm, rows[1] // tm
    in_specs = [
        pl.BlockSpec((tm, D), lambda i, j: (tile0 + i, 0), pipeline_mode=pl.Buffered(1)),
        _mod_spec(layer, 3, tm, tile0),
        _mod_spec(layer, 4, tm, tile0),
        _mod_spec(layer, 5, tm, tile0),
        pl.BlockSpec((1, 1, D), lambda i, j: (layer, 0, 0)),
        pl.BlockSpec((1, 1, D, tf), lambda i, j: (widx, j // spe, 0, j % spe)),
        pl.BlockSpec((1, 1, D, tf), lambda i, j: (widx, j // spe, 0, j % spe)),
        pl.BlockSpec((1, 1, tf, D), lambda i, j: (widx, j // spe, j % spe, 0)),
    ]
    args = [x, mod, mod, mod, g, w1, w3, w2]
    scratch = [pltpu.VMEM((tm, D), bf16)]
    if n_e > 1:
        in_specs += [pl.BlockSpec((1, D, 128), lambda i, j: (widx, 0, 0)),
                     pl.BlockSpec((1, D, 128), lambda i, j: (widx, 0, 0)),
                     pl.BlockSpec((1, 1, 128), lambda i, j: (widx, 0, 0))]
        args += list(router)
        scratch.append(pltpu.VMEM((tm, 128), f32))
    if final:
        in_specs.append(pl.BlockSpec((1, D), lambda i, j: (0, 0)))
        args.append(final_g.reshape(1, D))
    return pl.pallas_call(
        functools.partial(_ffn_body, n_e=n_e, steps_per_e=spe, final_norm=final),
        grid=(n_tiles, n_e * spe),
        in_specs=in_specs,
        out_specs=pl.BlockSpec((tm, D), lambda i, j: (i, 0)),
        out_shape=jax.ShapeDtypeStruct((rows[1], D), f32),
        scratch_shapes=scratch,
        compiler_params=_cparams(("parallel", "arbitrary")),
        name="ffn_moe" if n_e > 1 else "ffn_dense",
    )(*args)


def _split_hi_lo(w):
    hi = w.astype(bf16)
    lo = (w - hi.astype(f32)).astype(bf16)
    return hi, lo


_W_IN_SEGMENTS = ((2080, 3104), (5680, 6192), (6192, 14384), (0, 1536), (1568, 2080), (3104, 5664))
_W_IN_SMALL = ((1536, 1568), (5664, 5680))
_W_IN_COLS = 14384


def _reorder_body(wt_ref, o_ref, os_ref):
    c = 0
    for a, b in _W_IN_SEGMENTS:
        for r in range(a, b, 128):
            o_ref[0, :, c:c + 128] = wt_ref[0, r:r + 128, :].T.astype(bf16)
            c += 128
    rows = [wt_ref[0, a:b, :] for a, b in _W_IN_SMALL]
    rows.append(jnp.zeros((128 - sum(r.shape[0] for r in rows), 128), f32))
    os_ref[0] = jnp.concatenate(rows, axis=0).T.astype(bf16)


def _reorder_w_in(w):
    wt = jnp.swapaxes(w, 1, 2)
    return pl.pallas_call(
        _reorder_body,
        grid=(DEPTH, D // 128),
        in_specs=[pl.BlockSpec((1, _W_IN_COLS, 128), lambda l, r: (l, 0, r))],
        out_specs=[pl.BlockSpec((1, 128, N_PF + N_PB), lambda l, r: (l, r, 0)),
                   pl.BlockSpec((1, 128, 128), lambda l, r: (l, r, 0))],
        out_shape=[jax.ShapeDtypeStruct((DEPTH, D, N_PF + N_PB), bf16),
                   jax.ShapeDtypeStruct((DEPTH, D, 128), bf16)],
        compiler_params=_cparams(("parallel", "parallel")),
        name="reorder_w_in",
    )(wt)


def _gla_gate_weights(wa2, ba):
    t = wa2.reshape(DEPTH, 2, GLA_RANK, 2, 128)
    w = jnp.einsum("ldrhc,de->lhdrec", t, jnp.eye(2, dtype=f32))
    w = w.reshape(DEPTH, 2, 2 * GLA_RANK, 256)
    w = jnp.pad(w, ((0, 0), (0, 0), (SM_LR, 128 - SM_LR - 2 * GLA_RANK), (0, 0)))
    b = jnp.transpose(ba.reshape(DEPTH, 2, 2, 128), (0, 2, 1, 3)).reshape(DEPTH, 2, 1, 256)
    hi, lo = _split_hi_lo(w)
    return hi, lo, b


def _hgrn_bounds(logits):
    p = jax.nn.softmax(logits.astype(f32), axis=1)
    lower = jnp.maximum(jnp.cumsum(p, axis=1) - p[:, :1], 0.0)
    lb = jnp.transpose(lower, (1, 0, 2)).reshape(DEPTH, 2, HGRN_HEADS, HGRN_DIM)
    rows = jnp.stack([lb, 1.0 - lb], axis=3)
    rows = jnp.transpose(rows, (0, 2, 1, 3, 4)).reshape(DEPTH, HGRN_HEADS, 4, HGRN_DIM)
    return jnp.pad(rows, ((0, 0), (0, 0), (0, 4), (0, 0)))


def _ssd_select_mats():
    e64 = np.zeros((2, 2, 128, 256), np.float32)
    e128 = np.zeros((2, 2, 128, 512), np.float32)
    for g in range(2):
        for d in range(2):
            for h in range(4):
                lane = SM_DT + d * 8 + g * 4 + h
                e64[g, d, lane, h * 64:(h + 1) * 64] = 1.0
                e128[g, d, lane, h * 128:(h + 1) * 128] = 1.0
    return jnp.asarray(e64, bf16), jnp.asarray(e128, bf16)


def _ssd_params(conv_w, conv_b, dt_bias, a_log, d_skip, norm_g):
    cwb = jnp.concatenate([conv_w, conv_b[:, None]], axis=1)
    cw = jnp.concatenate([cwb[..., 0:512].reshape(DEPTH, 4, 2, 256),
                          cwb[..., 512:768].reshape(DEPTH, 4, 2, 128),
                          cwb[..., 768:1024].reshape(DEPTH, 4, 2, 128)], axis=-1)
    cw = jnp.transpose(cw, (0, 2, 1, 3))
    lanes = ((0, 0), (SM_DT, 128 - SM_DT - 16))
    dtb = jnp.pad(dt_bias.reshape(DEPTH, 16), lanes).reshape(DEPTH, 1, 128)
    alog = jnp.pad(a_log.reshape(DEPTH, 16), lanes).reshape(DEPTH, 1, 128)
    dsk = jnp.repeat(d_skip, SSM_HEADDIM, axis=1).reshape(DEPTH, 2, 1, 256)
    return cw, dtb, alog, dsk, norm_g.reshape(DEPTH, 2, 1, 256)


def kernel(x_prompt, x_sample, c, c_ctx, state_gla, state_hgrn, state_ssm, state_s5_re, state_s5_im, norm1_g, norm2_g, ada_w, ada_b, w_in, gla_wa2, gla_ba, gla_norm_g, hgrn_lb_logits, hgrn_norm_g, ssm_conv_w, ssm_conv_b, ssm_a_log, ssm_dt_bias, ssm_d, ssm_norm_g, s5_a_re, s5_a_im, s5_log_dt, s5_b_re, s5_b_im, s5_c_re, s5_c_im, s5_d, s5_glu_w, s5_glu_b, w_branch, w_out, ffn_w1, ffn_w3, ffn_w2, router_w, router_b, moe_w1, moe_w3, moe_w2, final_norm_g):
    w_all, w_small = _reorder_w_in(w_in)
    wah, wal, gba = _gla_gate_weights(gla_wa2, gla_ba)
    gng = gla_norm_g.reshape(DEPTH, 1, GLA_DV)
    lbp = _hgrn_bounds(hgrn_lb_logits)
    hng = hgrn_norm_g.reshape(DEPTH, 1, HGRN_DIM)
    e64, e128 = _ssd_select_mats()
    cw, dtb, alog, dsk, sng = _ssd_params(ssm_conv_w, ssm_conv_b, ssm_dt_bias, ssm_a_log, ssm_d, ssm_norm_g)
    w5 = jax.vmap(_s5_layer_weights)(s5_a_re, s5_a_im, s5_log_dt, s5_b_re, s5_b_im, s5_c_re, s5_c_im)
    w5["ws"], w5["wo"] = s5_expand_weights(w5["ws_c"], w5["wo_c"])
    d5 = s5_d.reshape(DEPTH, 4, 1, 128)
    glu_w = s5_glu_w.astype(bf16)
    glu_b = s5_glu_b.reshape(DEPTH, 1, MIX_W)
    wbr = w_branch.astype(bf16)
    wout = w_out.astype(bf16)
    n1 = norm1_g.reshape(DEPTH, 1, D)
    n2 = norm2_g.reshape(DEPTH, 1, D)
    fw1, fw3, fw2 = ffn_w1.astype(bf16)[:, None], ffn_w3.astype(bf16)[:, None], ffn_w2.astype(bf16)[:, None]
    mw1, mw3, mw2 = moe_w1.astype(bf16), moe_w3.astype(bf16), moe_w2.astype(bf16)
    rw = jnp.pad(router_w, ((0, 0), (0, 0), (0, 128 - N_EXPERTS)))
    router = _split_hi_lo(rw) + (jnp.pad(router_b, ((0, 0), (0, 128 - N_EXPERTS))).reshape(-1, 1, 128),)
    s5_state = jnp.concatenate([state_s5_re, state_s5_im], axis=-1).reshape(
        DEC_BATCH, DEPTH, 2, 2 * S5_GROUPS * S5_STATE)

    x = assemble_tokens(x_prompt.reshape(N_CTX, D), x_sample.reshape(N_LAT, D), _pos_table())
    cond8 = jnp.concatenate([c_ctx[None], c, jnp.zeros((MOD_ROWS - 1 - DEC_BATCH, D), f32)], axis=0)
    mod = ada_modulation(cond8, ada_w, ada_b).reshape(DEPTH, MOD_ROWS * 6, 1, D)

    lat0 = N_CTX // DEC_SEQ
    ctx = dict(B=BATCH, L=SEQ, row0=0)
    lat = dict(B=DEC_BATCH, L=DEC_SEQ, row0=lat0)
    st_gla = st_hg = st_ssm = st_s5 = None
    for l in range(DEPTH):
        PF, PB, SM = in_projection(x, mod, n1, w_all, w_small, l)

        o_gla, st_gla = gla_mixer(SM, PB, wah, wal, gba, gng, layer=l, st_prev=st_gla, nseq=2, **ctx)
        (o_gla,) = gla_mixer(SM, PB, wah, wal, gba, gng, layer=l, state=state_gla, prev=o_gla, **lat)

        o_hg, st_hg = hgrn_mixer(PF, PB, lbp, hng, layer=l, st_prev=st_hg, nseq=2, **ctx)
        (o_hg,) = hgrn_mixer(PF, PB, lbp, hng, layer=l, state=state_hgrn, prev=o_hg, **lat)

        o_ssm, st_ssm = ssd_mixer(SM, PB, cw, dtb, alog, e64, e128, dsk, sng, layer=l, st_prev=st_ssm, **ctx)
        (o_ssm,) = ssd_mixer(SM, PB, cw, dtb, alog, e64, e128, dsk, sng, layer=l, state=state_ssm,
                             prev=o_ssm, **lat)

        y5, st_s5 = s5_mixer(PF, w5, d5, B=BATCH, L=SEQ, s_b=min(8, BATCH), row0=0, layer=l, st_prev=st_s5)
        sb_lat = 2 if (DEC_BATCH % 2 == 0 and N_CTX % (2 * DEC_SEQ) == 0) else 1
        (y5,) = s5_mixer(PF, w5, d5, B=DEC_BATCH, L=DEC_SEQ, s_b=sb_lat, row0=N_CTX // (sb_lat * DEC_SEQ),
                         layer=l, state=s5_state, prev=y5)

        x = merge_branches(PB, (o_gla, o_hg, o_ssm, y5), glu_w, glu_b, wbr, wout, x, mod, l)

        if l % 2 == 0:
            ffn = functools.partial(ffn_layer, x, mod, n2, fw1, fw3, fw2, l, l // 2)
        else:
            ffn = functools.partial(ffn_layer, x, mod, n2, mw1, mw3, mw2, l, l // 2, router=router)
        if l < DEPTH - 1:
            x = ffn()
        else:
            y_ctx = ffn(final_g=final_norm_g, rows=(0, N_CTX))
            y_lat = ffn(final_g=final_norm_g, rows=(N_CTX, N_LAT))

    st_s5 =st_s5.reshape(BATCH, DEPTH, 2, S5_GROUPS, 2 * S5_STATE)
    return (y_ctx.reshape(BATCH, SEQ, D), y_lat.reshape(DEC_BATCH, DEC_SEQ, D),
            st_gla, st_hg, st_ssm, st_s5[..., :S5_STATE], st_s5[..., S5_STATE:])
```

```python
import functools

import numpy as np
import jax
import jax.numpy as jnp
from jax import lax
from jax.experimental import pallas as pl
from jax.experimental.pallas import tpu as pltpu

f32 = jnp.float32
bf16 = jnp.bfloat16

D = 2048
BATCH = 32
SEQ = 256
DEPTH = 4
DEC_BATCH = 4
DEC_SEQ = 4096
GRID_W = 64
MIX_W = 512
GLA_HEADS = 4
GLA_DK = 64
GLA_DV = 128
GLA_RANK = 16
GLA_NORMALIZER = 16.0
HGRN_HEADS = 4
HGRN_DIM = 128
SSM_HEADS = 8
SSM_HEADDIM = 64
SSM_GROUPS = 2
SSM_STATE = 128
S5_CH = 16
S5_STATE = 64
S5_GROUPS = 32
D_FF = 5632
N_EXPERTS = 8
D_FF_EXPERT = 1024
EPS = 1e-6

N_CTX = BATCH * SEQ
N_LAT = DEC_BATCH * DEC_SEQ
N_TOK = N_CTX + N_LAT
MOD_ROWS = 8

F_HG_F, F_S5_U = 0, 1024
N_PF = 1536
B_GATE, B_GLA_Q, B_GLA_K, B_GLA_V, B_GLA_R = 0, 8192, 8448, 8704, 9216
B_HG_Q, B_HG_I, B_HG_G, B_SSM_Z, B_SSM_XBC = 9728, 10240, 10752, 11264, 11776
N_PB = 12800
SM_LR, SM_DT = 0, 32

GLS_CHUNK = 128
GLS_SUB = 32
SSD_CHUNK = 128
S5_CHUNK = 16

VMEM_LIMIT = 58 * 1024 * 1024


def _cparams(sem):
    return pltpu.CompilerParams(dimension_semantics=sem, vmem_limit_bytes=VMEM_LIMIT)


def _dot(a, b):
    return jnp.dot(a, b, preferred_element_type=f32)


def _dot_nt(a, b):
    return lax.dot_general(a, b, (((1,), (1,)), ((), ())), preferred_element_type=f32)


def _dot_tn(a, b):
    return lax.dot_general(a, b, (((0,), (0,)), ((), ())), preferred_element_type=f32)


def _split2(x):
    hi = x.astype(bf16)
    return hi, (x - hi.astype(f32)).astype(bf16)


def _dot_sel_lhs(t, x):
    x1, x2 = _split2(x)
    return _dot(t, x1) + _dot(t, x2)


def _dot_sel_rhs(x, e):
    x1, x2 = _split2(x)
    return _dot(x1, e) + _dot(x2, e)


def _dot_sel_rhs3(x, e):
    x1, x2 = _split2(x)
    x3 = (x - x1.astype(f32) - x2.astype(f32)).astype(bf16)
    return _dot(x1, e) + _dot(x2, e) + _dot(x3, e)


def _dot_hi(x, w_hi, w_lo):
    x1, x2 = _split2(x)
    return _dot(x1, w_hi) + _dot(x2, w_hi) + _dot(x1, w_lo)


def _log_sigmoid(x):
    return jnp.minimum(x, 0.0) - jnp.log1p(jnp.exp(-jnp.abs(x)))


def _softplus(x):
    return jnp.maximum(x, 0.0) + jnp.log1p(jnp.exp(-jnp.abs(x)))


def _silu(x):
    return x * jax.nn.sigmoid(x)


def _modnorm(x, g, scale, shift):
    ms = jnp.mean(x * x, axis=-1, keepdims=True)
    y = x * lax.rsqrt(ms + EPS) * g
    return y * (1.0 + scale) + shift


def _tri(n, upper):
    ii = lax.broadcasted_iota(jnp.int32, (n, n), 0)
    jj = lax.broadcasted_iota(jnp.int32, (n, n), 1)
    return (ii <= jj) if upper else (ii >= jj)


def _mod_row(i, tm):
    start = i * tm
    return jnp.where(start < N_CTX, 0, 1 + (start - N_CTX) // DEC_SEQ)


def _mod_spec(layer, k, tm, tile0=0):
    return pl.BlockSpec((1, 1, 1, D), lambda i, j: (layer, _mod_row(tile0 + i, tm) * 6 + k, 0, 0))


def _assemble_body(xp_ref, xs_ref, pos_ref, o_ref, *, n_ctx_tiles):
    i = pl.program_id(0)

    @pl.when(i < n_ctx_tiles)
    def _():
        o_ref[...] = xp_ref[...]

    @pl.when(i >= n_ctx_tiles)
    def _():
        o_ref[...] = xs_ref[...] + pos_ref[...]


def assemble_tokens(xp2, xs2, pos, tm=512):
    nct = N_CTX // tm
    npos = DEC_SEQ // tm
    return pl.pallas_call(
        functools.partial(_assemble_body, n_ctx_tiles=nct),
        grid=(N_TOK // tm,),
        in_specs=[
            pl.BlockSpec((tm, D), lambda i: (jnp.minimum(i, nct - 1), 0)),
            pl.BlockSpec((tm, D), lambda i: (jnp.maximum(i - nct, 0), 0)),
            pl.BlockSpec((tm, D), lambda i: (jnp.maximum(i - nct, 0) % npos, 0)),
        ],
        out_specs=pl.BlockSpec((tm, D), lambda i: (i, 0)),
        out_shape=jax.ShapeDtypeStruct((N_TOK, D), f32),
        compiler_params=_cparams(("parallel",)),
        name="assemble_tokens",
    )(xp2, xs2, pos)


def _pos_table():
    rows = DEC_SEQ // GRID_W
    row = jnp.repeat(jnp.arange(rows, dtype=f32), GRID_W)
    col = jnp.tile(jnp.arange(GRID_W, dtype=f32), rows)

    def sincos(p, d):
        half = d // 2
        omega = 1.0 / (10000.0 ** (jnp.arange(half, dtype=f32) / half))
        ang = p[:, None] * omega[None, :]
        return jnp.concatenate([jnp.sin(ang), jnp.cos(ang)], axis=-1)

    return jnp.concatenate([sincos(row, D // 2), sincos(col, D // 2)], axis=-1)


def _ada_body(c_ref, w_ref, b_ref, o_ref):
    s = _silu(c_ref[...])
    s1, s2 = _split2(s)
    w = w_ref[0].astype(bf16)
    o_ref[0] = _dot(s1, w) + _dot(s2, w) + b_ref[0]


def ada_modulation(cond8, ada_w, ada_b, tn=1024):
    return pl.pallas_call(
        _ada_body,
        grid=(DEPTH, 6 * D // tn),
        in_specs=[
            pl.BlockSpec((MOD_ROWS, D), lambda l, j: (0, 0)),
            pl.BlockSpec((1, D, tn), lambda l, j: (l, 0, j)),
            pl.BlockSpec((1, 1, tn), lambda l, j: (l, 0, j)),
        ],
        out_specs=pl.BlockSpec((1, MOD_ROWS, tn), lambda l, j: (l, 0, j)),
        out_shape=jax.ShapeDtypeStruct((DEPTH, MOD_ROWS, 6 * D), f32),
        compiler_params=_cparams(("parallel", "parallel")),
        name="ada_modulation",
    )(cond8, ada_w, ada_b.reshape(DEPTH, 1, 6 * D))


def _inproj_body(x_ref, sh_ref, sc_ref, g_ref, w_ref, ws_ref, pf_ref, pb_ref, sm_ref, h_s, *, nf):
    j = pl.program_id(1)

    @pl.when(j == 0)
    def _():
        h = _modnorm(x_ref[...], g_ref[0], sc_ref[0, 0], sh_ref[0, 0]).astype(bf16)
        h_s[...] = h
        sm_ref[...] = _dot(h, ws_ref[0])

    y = _dot(h_s[...], w_ref[0])

    @pl.when(j < nf)
    def _():
        pf_ref[...] = y

    @pl.when(j >= nf)
    def _():
        pb_ref[...] = y.astype(bf16)


def in_projection(x, mod, g, w_all, w_small, layer, tm=1024, tn=512):
    nf = N_PF // tn
    nb = N_PB // tn
    return pl.pallas_call(
        functools.partial(_inproj_body, nf=nf),
        grid=(N_TOK // tm, nf + nb),
        in_specs=[
            pl.BlockSpec((tm, D), lambda i, j: (i, 0)),
            _mod_spec(layer, 0, tm),
            _mod_spec(layer, 1, tm),
            pl.BlockSpec((1, 1, D), lambda i, j: (layer, 0, 0)),
            pl.BlockSpec((1, D, tn), lambda i, j: (layer, 0, j)),
            pl.BlockSpec((1, D, 128), lambda i, j: (layer, 0, 0)),
        ],
        out_specs=[
            pl.BlockSpec((tm, tn), lambda i, j: (i, jnp.minimum(j, nf - 1))),
            pl.BlockSpec((tm, tn), lambda i, j: (i, jnp.maximum(j - nf, 0))),
            pl.BlockSpec((tm, 128), lambda i, j: (i, 0)),
        ],
        out_shape=[
            jax.ShapeDtypeStruct((N_TOK, N_PF), f32),
            jax.ShapeDtypeStruct((N_TOK, N_PB), bf16),
            jax.ShapeDtypeStruct((N_TOK, 128), f32),
        ],
        scratch_shapes=[pltpu.VMEM((tm, D), bf16)],
        compiler_params=_cparams(("parallel", "arbitrary")),
        name="in_projection",
    )(x, mod, mod, g, w_all, w_small)


def _gls_dir(q, k, g, d):
    c, sub = GLS_CHUNK, GLS_SUB
    nb = c // sub
    tri = jnp.where(_tri(c, upper=(d == 1)), 1.0, 0.0).astype(bf16)
    b = _dot_sel_lhs(tri, g)
    zero = jnp.zeros((1, 128), f32)
    if d == 0:
        mid = [b[sub * i + sub // 2 - 1:sub * i + sub // 2] for i in range(nb)]
        end = [b[sub * i + sub - 1:sub * i + sub] for i in range(nb)]
        start = [zero] + end[:-1]
        total = end[-1]
    else:
        mid = [b[sub * i + sub // 2:sub * i + sub // 2 + 1] for i in range(nb)]
        end = [b[sub * i:sub * i + 1] for i in range(nb)]
        start = end[1:] + [zero]
        total = end[0]

    def rows(vals):
        return jnp.concatenate([jnp.broadcast_to(v, (sub, 128)) for v in vals], axis=0)

    r_mid = rows(mid)
    qd = q * jnp.exp(b - r_mid)
    kd = k * jnp.exp(r_mid - b)
    qo = qd * rows([jnp.exp(m - s) for m, s in zip(mid, start)])
    ke = kd * rows([jnp.exp(e - m) for m, e in zip(mid, end)])
    q_in = qo * rows([jnp.exp(s) for s in start])
    k_out = ke * rows([jnp.exp(total - e) for e in end])
    kr = []
    for i in range(nb):
        prior = range(i) if d == 0 else range(i + 1, nb)
        if len(prior) == 0:
            kr.append(None)
            continue
        blocks = []
        for jb in range(nb):
            if jb in prior:
                blocks.append(ke[jb * sub:(jb + 1) * sub] * jnp.exp(start[i] - end[jb]))
            else:
                blocks.append(jnp.zeros((sub, 128), f32))
        kr.append(jnp.concatenate(blocks, axis=0).astype(bf16))
    return qd, kd.astype(bf16), qo, kr, q_in, k_out, jnp.exp(total)


def _gls_head(parts, v, st, mask, kmask):
    qd, kd, qo, kr, q_in, k_out, dec = parts
    sub = GLS_SUB
    if kmask is not None:
        qd, qo, q_in, k_out = qd * kmask, qo * kmask, q_in * kmask, k_out * kmask
    att = jnp.where(mask, _dot_nt(qd.astype(bf16), kd), 0.0)
    qob = qo.astype(bf16)
    offs = []
    for i, kri in enumerate(kr):
        if kri is None:
            offs.append(jnp.zeros((sub, GLS_CHUNK), f32))
        else:
            offs.append(_dot_nt(qob[i * sub:(i + 1) * sub], kri))
    att = att + jnp.concatenate(offs, axis=0)
    vb = v.astype(bf16)
    o = _dot(att.astype(bf16), vb) + _dot_nt(q_in.astype(bf16), st.astype(bf16))
    st_new = st * dec + _dot_tn(vb, k_out.astype(bf16))
    return o, st_new


def _gls_masks():
    c, sub = GLS_CHUNK, GLS_SUB
    ii = lax.broadcasted_iota(jnp.int32, (c, c), 0)
    jj = lax.broadcasted_iota(jnp.int32, (c, c), 1)
    same = (ii // sub) == (jj // sub)
    return [same & (ii >= jj), same & (ii <= jj)]


def _head_post(o, gate_in, ng):
    ms = jnp.mean(o * o, axis=-1, keepdims=True)
    return o * lax.rsqrt(ms + EPS) * ng * _silu(gate_in.astype(f32))


def _scan_loops(n, nseq, chunk_fn, post_fn, o_ref, init):
    C = GLS_CHUNK

    def store(r0, outs, final):
        for h, o in enumerate(outs):
            sl = (pl.ds(r0, C), slice(h * 128, (h + 1) * 128))
            if final:
                o_ref[sl] = post_fn(r0, h, o_ref[sl] + o)
            else:
                o_ref[sl] = o

    def half(final):
        def body(t, carry):
            new = []
            for s, (sf, sb) in enumerate(carry):
                r0, of, sf = chunk_fn(s * n + t, 0, sf)
                store(r0, of, final)
                r1, ob, sb = chunk_fn(s * n + n - 1 - t, 1, sb)
                store(r1, ob, final)
                new.append((sf, sb))
            return tuple(new)
        return body

    unroll = 4 if (n // 2) % 4 == 0 else 1
    carry = lax.fori_loop(0, n // 2, half(False), tuple(init), unroll=unroll)
    return lax.fori_loop(n // 2, n, half(True), carry, unroll=unroll)


def _gla_body(*refs, L, nseq, has_init, chain_state):
    it = iter(refs)
    q_ref, k_ref, v_ref, r_ref, sm_ref, wah_ref, wal_ref, ba_ref, ng_ref = [next(it) for _ in range(9)]
    s0_ref = next(it) if has_init else None
    if has_init or chain_state:
        next(it)
    o_ref = next(it)
    st_ref = None if has_init else next(it)
    C = GLS_CHUNK
    n = L // C
    lane = lax.broadcasted_iota(jnp.int32, (1, 128), 1)
    kmasks = [jnp.where(lane < 64, 1.0, 0.0), jnp.where(lane >= 64, 1.0, 0.0)]
    masks = _gls_masks()
    wah, wal, ba, ng = wah_ref[0, 0], wal_ref[0, 0], ba_ref[0, 0], ng_ref[0]

    def chunk(cidx, d, sts):
        r0 = pl.multiple_of(cidx * C, C)
        q = q_ref[pl.ds(r0, C), :].astype(f32) * (GLA_DK ** -0.5)
        k = k_ref[pl.ds(r0, C), :].astype(f32)
        sm = sm_ref[pl.ds(r0, C), :]
        sl = slice(d * 128, (d + 1) * 128)
        g = _log_sigmoid(_dot_hi(sm, wah[:, sl], wal[:, sl]) + ba[:, sl]) * (1.0 / GLA_NORMALIZER)
        parts = _gls_dir(q, k, g, d)
        outs, new = [], []
        for h in range(2):
            v = v_ref[pl.ds(r0, C), h * 128:(h + 1) * 128]
            o, s = _gls_head(parts, v, sts[h], masks[d], kmasks[h])
            outs.append(o)
            new.append(s)
        return r0, outs, new

    def post(r0, h, tot):
        return _head_post(tot, r_ref[pl.ds(r0, C), h * 128:(h + 1) * 128], ng)

    def init(d, h):
        if not has_init:
            return jnp.zeros((128, 128), f32)
        s = s0_ref[0, 0, d, h]
        z = jnp.zeros((64, 128), f32)
        full = jnp.concatenate([s, z], axis=0) if h == 0 else jnp.concatenate([z, s], axis=0)
        return full.T

    states = _scan_loops(n, nseq, chunk, post, o_ref,
                         [([init(0, 0), init(0, 1)], [init(1, 0), init(1, 1)]) for _ in range(nseq)])
    if st_ref is not None:
        for s, dirs in enumerate(states):
            for d, sts in enumerate(dirs):
                for h in range(2):
                    st_ref[s, 0, d, h] = sts[h].T[h * 64:(h + 1) * 64, :]


def gla_mixer(SM, PB, wa_hi, wa_lo, ba, ng, *, B, L, row0, layer, nseq=1, state=None, prev=None, st_prev=None):
    has_init = state is not None
    chain_state = st_prev is not None
    R = nseq * L
    in_specs = [
        pl.BlockSpec((R, 128), lambda b, hp: (row0 + b, B_GLA_Q // 128 + hp)),
        pl.BlockSpec((R, 128), lambda b, hp: (row0 + b, B_GLA_K // 128 + hp)),
        pl.BlockSpec((R, 256), lambda b, hp: (row0 + b, B_GLA_V // 256 + hp)),
        pl.BlockSpec((R, 256), lambda b, hp: (row0 + b, B_GLA_R // 256 + hp)),
        pl.BlockSpec((R, 128), lambda b, hp: (row0 + b, 0)),
        pl.BlockSpec((1, 1, 128, 256), lambda b, hp: (layer, hp, 0, 0)),
        pl.BlockSpec((1, 1, 128, 256), lambda b, hp: (layer, hp, 0, 0)),
        pl.BlockSpec((1, 1, 1, 256), lambda b, hp: (layer, hp, 0, 0)),
        pl.BlockSpec((1, 1, 128), lambda b, hp: (layer, 0, 0)),
    ]
    args = [PB, PB, PB, PB, SM, wa_hi, wa_lo, ba, ng]
    aliases = {}
    out_specs = [pl.BlockSpec((R, 256), lambda b, hp: (row0 + b, hp))]
    out_shape = [jax.ShapeDtypeStruct((N_TOK, MIX_W), f32)]
    if has_init:
        assert nseq == 1
        in_specs.append(pl.BlockSpec((1, 1, 2, 2, 64, 128), lambda b, hp: (b, layer, 0, hp, 0, 0)))
        args.append(state)
        in_specs.append(pl.BlockSpec(memory_space=pl.ANY))
        args.append(prev)
        aliases = {len(args) - 1: 0}
    else:
        if chain_state:
            in_specs.append(pl.BlockSpec(memory_space=pl.ANY))
            args.append(st_prev)
            aliases = {len(args) - 1: 1}
        out_specs.append(pl.BlockSpec((nseq, 1, 2, 2, 64, 128), lambda b, hp: (b, layer, 0, hp, 0, 0)))
        out_shape.append(jax.ShapeDtypeStruct((B, DEPTH, 2, GLA_HEADS, GLA_DK, GLA_DV), f32))
    return pl.pallas_call(
        functools.partial(_gla_body, L=L, nseq=nseq, has_init=has_init, chain_state=chain_state),
        grid=(B // nseq, 2),
        in_specs=in_specs,
        out_specs=out_specs,
        out_shape=out_shape,
        input_output_aliases=aliases,
        compiler_params=_cparams(("parallel", "parallel")),
        name="gla_mixer",
    )(*args)


def _hgrn_body(*refs, L, nseq, has_init, chain_state):
    it = iter(refs)
    q_ref, f0_ref, f1_ref, i_ref, g_ref, lb_ref, ng_ref = [next(it) for _ in range(7)]
    s0_ref = next(it) if has_init else None
    if has_init or chain_state:
        next(it)
    o_ref = next(it)
    st_ref = None if has_init else next(it)
    C = GLS_CHUNK
    n = L // C
    masks = _gls_masks()
    ng = ng_ref[0]
    f_refs = (f0_ref, f1_ref)

    def chunk(cidx, d, sts):
        r0 = pl.multiple_of(cidx * C, C)
        outs, new = [], []
        for h in range(2):
            sl = slice(h * 128, (h + 1) * 128)
            lbp = lb_ref[0, h]
            q = q_ref[pl.ds(r0, C), sl].astype(f32)
            fp = f_refs[d][pl.ds(r0, C), sl]
            lb, one_m_lb = lbp[2 * d:2 * d + 1], lbp[2 * d + 1:2 * d + 2]
            e = jnp.exp(-jnp.abs(fp))
            inv = 1.0 / (1.0 + e)
            pos = fp >= 0.0
            sig = jnp.where(pos, inv, e * inv)
            k = one_m_lb * jnp.where(pos, e * inv, inv)
            g = jnp.log(lb + one_m_lb * sig)
            parts = _gls_dir(q, k, g, d)
            o, s = _gls_head(parts, i_ref[pl.ds(r0, C), sl], sts[h], masks[d], None)
            outs.append(o)
            new.append(s)
        return r0, outs, new

    def post(r0, h, tot):
        return _head_post(tot, g_ref[pl.ds(r0, C), h * 128:(h + 1) * 128], ng)

    def init(d, h):
        if not has_init:
            return jnp.zeros((128, 128), f32)
        return s0_ref[0, 0, d, h].T

    states = _scan_loops(n, nseq, chunk, post, o_ref,
                         [([init(0, 0), init(0, 1)], [init(1, 0), init(1, 1)]) for _ in range(nseq)])
    if st_ref is not None:
        for s, dirs in enumerate(states):
            for d, sts in enumerate(dirs):
                for h in range(2):
                    st_ref[s, 0, d, h] = sts[h].T


def hgrn_mixer(PF, PB, lbp, ng, *, B, L, row0, layer, nseq=1, state=None, prev=None, st_prev=None):
    has_init = state is not None
    chain_state = st_prev is not None
    R = nseq * L
    in_specs = [
        pl.BlockSpec((R, 256), lambda b, hp: (row0 + b, B_HG_Q // 256 + hp)),
        pl.BlockSpec((R, 256), lambda b, hp: (row0 + b, F_HG_F // 256 + hp)),
        pl.BlockSpec((R, 256), lambda b, hp: (row0 + b, F_HG_F // 256 + 2 + hp)),
        pl.BlockSpec((R, 256), lambda b, hp: (row0 + b, B_HG_I // 256 + hp)),
        pl.BlockSpec((R, 256), lambda b, hp: (row0 + b, B_HG_G // 256 + hp)),
        pl.BlockSpec((1, 2, 8, 128), lambda b, hp: (layer, hp, 0, 0)),
        pl.BlockSpec((1, 1, 128), lambda b, hp: (layer, 0, 0)),
    ]
    args = [PB, PF, PF, PB, PB, lbp, ng]
    aliases = {}
    out_specs = [pl.BlockSpec((R, 256), lambda b, hp: (row0 + b, hp))]
    out_shape = [jax.ShapeDtypeStruct((N_TOK, MIX_W), f32)]
    if has_init:
        assert nseq == 1
        in_specs.append(pl.BlockSpec((1, 1, 2, 2, 128, 128), lambda b, hp: (b, layer, 0, hp, 0, 0)))
        args.append(state)
        in_specs.append(pl.BlockSpec(memory_space=pl.ANY))
        args.append(prev)
        aliases = {len(args) - 1: 0}
    else:
        if chain_state:
            in_specs.append(pl.BlockSpec(memory_space=pl.ANY))
            args.append(st_prev)
            aliases = {len(args) - 1: 1}
        out_specs.append(pl.BlockSpec((nseq, 1, 2, 2, 128, 128), lambda b, hp: (b, layer, 0, hp, 0, 0)))
        out_shape.append(jax.ShapeDtypeStruct((B, DEPTH, 2, HGRN_HEADS, HGRN_DIM, HGRN_DIM), f32))
    return pl.pallas_call(
        functools.partial(_hgrn_body, L=L, nseq=nseq, has_init=has_init, chain_state=chain_state),
        grid=(B // nseq, 2),
        in_specs=in_specs,
        out_specs=out_specs,
        out_shape=out_shape,
        input_output_aliases=aliases,
        compiler_params=_cparams(("parallel", "parallel")),
        name="hgrn_mixer",
    )(*args)


def _ssd_body(*refs, L, has_init, chain_state):
    it = iter(refs)
    (xs_ref, bm_ref, cm_ref, z_ref, sm_ref, cw_ref, dtb_ref, alog_ref, e64_ref, e128_ref,
     dsk_ref, ng_ref) = [next(it) for _ in range(12)]
    s0_ref = next(it) if has_init else None
    if has_init or chain_state:
        next(it)
    o_ref = next(it)
    st_ref = None if has_init else next(it)
    xc_s = next(it)
    C = SSD_CHUNK
    n = L // C
    cw = cw_ref[0, 0]
    rowi = lax.broadcasted_iota(jnp.int32, (C, 1), 0)

    def conv_block(rb, _):
        r0 = pl.multiple_of(rb * C, C)
        rp = pl.multiple_of(jnp.maximum(r0 - 16, 0), 16)
        rn = pl.multiple_of(jnp.minimum(r0 + C, L - 16), 16)
        has_p = jnp.where(rb > 0, 1.0, 0.0)
        has_n = jnp.where(rb < n - 1, 1.0, 0.0)
        for ref, c0, w in ((xs_ref, 0, 256), (bm_ref, 256, 128), (cm_ref, 384, 128)):
            x = ref[pl.ds(r0, C), :].astype(f32)
            prev = ref[pl.ds(rp, 16), :].astype(f32)[15:16] * has_p
            nxt = ref[pl.ds(rn, 16), :].astype(f32)[0:1] * has_n
            x_dn = jnp.where(rowi == 0, prev, pltpu.roll(x, 1, axis=0))
            x_up = jnp.where(rowi == C - 1, nxt, pltpu.roll(x, C - 1, axis=0))
            y = (cw[0:1, c0:c0 + w] * x_dn + cw[1:2, c0:c0 + w] * x + cw[2:3, c0:c0 + w] * x_up
                 + cw[3:4, c0:c0 + w])
            xc_s[pl.ds(r0, C), c0:c0 + w] = _silu(y)
        return 0

    lax.fori_loop(0, n, conv_block, 0)

    masks = [_tri(C, upper=False), _tri(C, upper=True)]
    tris = [jnp.where(m, 1.0, 0.0).astype(bf16) for m in masks]
    lane256 = lax.broadcasted_iota(jnp.int32, (1, 256), 1)
    hmask = [jnp.where((lane256 >= h * 64) & (lane256 < (h + 1) * 64), 1.0, 0.0) for h in range(4)]
    dtb = dtb_ref[0]
    a_row = -jnp.exp(alog_ref[0])
    neg_inf = jnp.float32(-jnp.inf)

    def chunk(cidx, d, st):
        r0 = pl.multiple_of(cidx * C, C)
        xs = xc_s[pl.ds(r0, C), 0:256]
        bm = xc_s[pl.ds(r0, C), 256:384].astype(bf16)
        cm = xc_s[pl.ds(r0, C), 384:512].astype(bf16)
        dt_all = _softplus(sm_ref[pl.ds(r0, C), :] + dtb)
        a_all = dt_all * a_row
        dt64 = _dot_sel_rhs(dt_all, e64_ref[0, d])
        acs_all = _dot_sel_lhs(tris[d], a_all)
        acs64 = _dot_sel_rhs3(acs_all, e64_ref[0, d])
        acs128 = _dot_sel_rhs3(acs_all, e128_ref[0, d])
        end = acs64[C - 1:C] if d == 0 else acs64[0:1]
        xdt = xs * dt64
        scores = _dot_nt(cm, bm)
        y = jnp.exp(acs64) * _dot(cm, st.astype(bf16))
        ps, xh = [], []
        for h in range(4):
            colb = acs128[:, h * 128:(h + 1) * 128]
            diff = jnp.where(masks[d], colb - colb.T, neg_inf)
            ps.append((scores * jnp.exp(diff)).astype(bf16))
            xh.append((xdt * hmask[h]).astype(bf16))
        y = y + _dot(jnp.concatenate(ps, axis=1), jnp.concatenate(xh, axis=0))
        xd = (xdt * jnp.exp(end - acs64)).astype(bf16)
        st_new = st * jnp.exp(end) + _dot_tn(bm, xd)
        return r0, y, st_new

    dsk = dsk_ref[0, 0]
    ng = ng_ref[0, 0]

    def post(r0, y):
        xs = xc_s[pl.ds(r0, C), 0:256]
        t = (y + dsk * xs) * _silu(z_ref[pl.ds(r0, C), :].astype(f32))
        ms = jnp.mean(t * t, axis=-1, keepdims=True)
        return t * lax.rsqrt(ms + EPS) * ng

    def first_half(t, carry):
        sf, sb = carry
        r0, yf, sf = chunk(t, 0, sf)
        o_ref[pl.ds(r0, C), :] = yf
        r1, yb, sb = chunk(n - 1 - t, 1, sb)
        o_ref[pl.ds(r1, C), :] = yb
        return sf, sb

    def second_half(t, carry):
        sf, sb = carry
        r0, yf, sf = chunk(t, 0, sf)
        o_ref[pl.ds(r0, C), :] = post(r0, o_ref[pl.ds(r0, C), :] + yf)
        r1, yb, sb = chunk(n - 1 - t, 1, sb)
        o_ref[pl.ds(r1, C), :] = post(r1, o_ref[pl.ds(r1, C), :] + yb)
        return sf, sb

    def init(d):
        if not has_init:
            return jnp.zeros((128, 256), f32)
        return s0_ref[0, 0, d].reshape(256, 128).T

    carry = (init(0), init(1))
    unroll = 4 if (n // 2) % 4 == 0 else 1
    carry = lax.fori_loop(0, n // 2, first_half, carry, unroll=unroll)
    sf, sb = lax.fori_loop(n // 2, n, second_half, carry, unroll=unroll)
    if st_ref is not None:
        st_ref[0, 0, 0] = sf.T.reshape(4, 64, 128)
        st_ref[0, 0, 1] = sb.T.reshape(4, 64, 128)


def ssd_mixer(SM, PB, cw, dtb, alog, e64, e128, dsk, ng, *, B, L, row0, layer, state=None, prev=None,
              st_prev=None):
    has_init = state is not None
    chain_state = st_prev is not None
    xb = B_SSM_XBC
    in_specs = [
        pl.BlockSpec((L, 256), lambda b, g: (row0 + b, xb // 256 + g)),
        pl.BlockSpec((L, 128), lambda b, g: (row0 + b, (xb + 512) // 128 + g)),
        pl.BlockSpec((L, 128), lambda b, g: (row0 + b, (xb + 768) // 128 + g)),
        pl.BlockSpec((L, 256), lambda b, g: (row0 + b, B_SSM_Z // 256 + g)),
        pl.BlockSpec((L, 128), lambda b, g: (row0 + b, 0)),
        pl.BlockSpec((1, 1, 4, 512), lambda b, g: (layer, g, 0, 0)),
        pl.BlockSpec((1, 1, 128), lambda b, g: (layer, 0, 0)),
        pl.BlockSpec((1, 1, 128), lambda b, g: (layer, 0, 0)),
        pl.BlockSpec((1, 2, 128, 256), lambda b, g: (g, 0, 0, 0)),
        pl.BlockSpec((1, 2, 128, 512), lambda b, g: (g, 0, 0, 0)),
        pl.BlockSpec((1, 1, 1, 256), lambda b, g: (layer, g, 0, 0)),
        pl.BlockSpec((1, 1, 1, 256), lambda b, g: (layer, g, 0, 0)),
    ]
    args = [PB, PB, PB, PB, SM, cw, dtb, alog, e64, e128, dsk, ng]
    aliases = {}
    out_specs = [pl.BlockSpec((L, 256), lambda b, g: (row0 + b, g))]
    out_shape = [jax.ShapeDtypeStruct((N_TOK, MIX_W), f32)]
    if has_init:
        in_specs.append(pl.BlockSpec((1, 1, 2, 4, 64, 128), lambda b, g: (b, layer, 0, g, 0, 0)))
        args.append(state)
        in_specs.append(pl.BlockSpec(memory_space=pl.ANY))
        args.append(prev)
        aliases = {len(args) - 1: 0}
    else:
        if chain_state:
            in_specs.append(pl.BlockSpec(memory_space=pl.ANY))
            args.append(st_prev)
            aliases = {len(args) - 1: 1}
        out_specs.append(pl.BlockSpec((1, 1, 2, 4, 64, 128), lambda b, g: (b, layer, 0, g, 0, 0)))
        out_shape.append(jax.ShapeDtypeStruct((B, DEPTH, 2, SSM_HEADS, SSM_HEADDIM, SSM_STATE), f32))
    return pl.pallas_call(
        functools.partial(_ssd_body, L=L, has_init=has_init, chain_state=chain_state),
        grid=(B, SSM_GROUPS),
        in_specs=in_specs,
        out_specs=out_specs,
        out_shape=out_shape,
        scratch_shapes=[pltpu.VMEM((L, 512), f32)],
        input_output_aliases=aliases,
        compiler_params=_cparams(("parallel", "parallel")),
        name="ssd_mixer",
    )(*args)


def _s5_body(*refs, n, n_c, s_b, has_init, chain_state):
    it = iter(refs)
    u_ref, wt_ref, wsf_ref, wsb_ref, wof_ref, wob_ref, lam_ref, dsk_ref = [next(it) for _ in range(8)]
    if has_init:
        h0_ref = next(it)
        next(it)
    elif chain_state:
        next(it)
    o_ref = next(it)
    if not has_init:
        hs_ref = next(it)
    zf_s, zb_s, hf_s, hb_s, zfs_s, zbs_s, acc_s, za_s, zc_s = [next(it) for _ in range(9)]
    T = S5_CHUNK

    for j in range(T):
        ub = u_ref[pl.ds(j, n, stride=T), :].astype(bf16)
        a = _dot(ub, wt_ref[0, 0, :, (T - 1 - j) * 128:(2 * T - 1 - j) * 128])
        f = _dot(ub, wsf_ref[0, 0, 0, j])
        b = _dot(ub, wsb_ref[0, 0, 0, j])
        if j == 0:
            acc_s[...] = a
            za_s[...] = f
            zc_s[...] = b
        else:
            acc_s[...] += a
            za_s[...] += f
            zc_s[...] += b
    for kk in range(8):
        zf_s[kk] = za_s[:, kk * 128:(kk + 1) * 128]
        zb_s[kk] = zc_s[:, kk * 128:(kk + 1) * 128]

    lam = lam_ref[0, 0]

    for kk in range(8):
        zfs_s[kk] = pltpu.roll(zf_s[kk], 64, axis=1)
        zbs_s[kk] = pltpu.roll(zb_s[kk], 64, axis=1)

    def rows(c):
        return pl.ds(c, s_b, stride=n_c) if s_b > 1 else pl.ds(c, 1)

    def scan_step(c, h, hs, z_s, zs_s, h_s, la, lb):
        new, news = [], []
        for kk in range(8):
            h_s[kk, rows(c), :] = h[kk]
            sl = slice(kk * 128, (kk + 1) * 128)
            new.append(la[:, sl] * h[kk] + lb[:, sl] * hs[kk] + z_s[kk, rows(c), :])
            news.append(la[:, sl] * hs[kk] - lb[:, sl] * h[kk] + zs_s[kk, rows(c), :])
        return tuple(new), tuple(news)

    def step(t, carry):
        hf, hfs, hb, hbs = carry
        hf, hfs = scan_step(t, hf, hfs, zf_s, zfs_s, hf_s, lam[0:1], lam[1:2])
        hb, hbs = scan_step(n_c - 1 - t, hb, hbs, zb_s, zbs_s, hb_s, lam[2:3], lam[3:4])
        return hf, hfs, hb, hbs

    if has_init:
        h0f = tuple(h0_ref[:, 0, 0, kk * 128:(kk + 1) * 128] for kk in range(8))
        h0b = tuple(h0_ref[:, 0, 1, kk * 128:(kk + 1) * 128] for kk in range(8))
    else:
        h0f = h0b = tuple(jnp.zeros((s_b, 128), f32) for _ in range(8))
    swap = lambda hh: tuple(pltpu.roll(v, 64, axis=1) for v in hh)
    hf, _, hb, _ = lax.fori_loop(0, n_c, step, (h0f, swap(h0f), h0b, swap(h0b)))
    if not has_init:
        for kk in range(8):
            hs_ref[:, 0, 0, kk * 128:(kk + 1) * 128] = hf[kk]
            hs_ref[:, 0, 1, kk * 128:(kk + 1) * 128] = hb[kk]

    hin_f = jnp.concatenate([hf_s[kk] for kk in range(8)], axis=1).astype(bf16)
    hin_b = jnp.concatenate([hb_s[kk] for kk in range(8)], axis=1).astype(bf16)
    acc_s[...] += _dot(hin_f, wof_ref[0, 0, 0])
    acc_s[...] += _dot(hin_b, wob_ref[0, 0, 0])
    dsk = dsk_ref[0, 0]
    for i in range(T):
        o_ref[pl.ds(i, n, stride=T), :] = (acc_s[:, i * 128:(i + 1) * 128]
                                           + dsk * u_ref[pl.ds(i, n, stride=T), :])


def s5_mixer(PF, w, dsk, *, B, L, s_b, row0, layer, state=None, prev=None, st_prev=None):
    has_init = state is not None
    chain_state = st_prev is not None
    n_c = L // S5_CHUNK
    n = s_b * n_c
    rows = s_b * L
    one = pl.Buffered(1)
    in_specs = [
        pl.BlockSpec((rows, 128), lambda lb, sb: (row0 + sb, F_S5_U // 128 + lb)),
        pl.BlockSpec((1, 1, 128, 31 * 128), lambda lb, sb: (layer, lb, 0, 0), pipeline_mode=one),
        pl.BlockSpec((1, 1, 1, 16, 128, 1024), lambda lb, sb: (layer, 0, lb, 0, 0, 0), pipeline_mode=one),
        pl.BlockSpec((1, 1, 1, 16, 128, 1024), lambda lb, sb: (layer, 1, lb, 0, 0, 0), pipeline_mode=one),
        pl.BlockSpec((1, 1, 1, 1024, 2048), lambda lb, sb: (layer, 0, lb, 0, 0), pipeline_mode=one),
        pl.BlockSpec((1, 1, 1, 1024, 2048), lambda lb, sb: (layer, 1, lb, 0, 0), pipeline_mode=one),
        pl.BlockSpec((1, 1, 4, 1024), lambda lb, sb: (layer, lb, 0, 0)),
        pl.BlockSpec((1, 1, 1, 128), lambda lb, sb: (layer, lb, 0, 0)),
    ]
    args = [PF, w["toep"], w["ws"], w["ws"], w["wo"], w["wo"], w["lam16"], dsk]
    aliases = {}
    out_specs = [pl.BlockSpec((rows, 128), lambda lb, sb: (row0 + sb, lb))]
    out_shape = [jax.ShapeDtypeStruct((N_TOK, MIX_W), f32)]
    if has_init:
        in_specs.append(pl.BlockSpec((s_b, 1, 2, 1024), lambda lb, sb: (sb, layer, 0, lb)))
        args.append(state)
        in_specs.append(pl.BlockSpec(memory_space=pl.ANY))
        args.append(prev)
        aliases = {len(args) - 1: 0}
    else:
        if chain_state:
            in_specs.append(pl.BlockSpec(memory_space=pl.ANY))
            args.append(st_prev)
            aliases = {len(args) - 1: 1}
        out_specs.append(pl.BlockSpec((s_b, 1, 2, 1024), lambda lb, sb: (sb, layer, 0, lb)))
        out_shape.append(jax.ShapeDtypeStruct((B, DEPTH, 2, 2 * S5_GROUPS * S5_STATE), f32))
    return pl.pallas_call(
        functools.partial(_s5_body, n=n, n_c=n_c, s_b=s_b, has_init=has_init, chain_state=chain_state),
        grid=(4, B // s_b),
        in_specs=in_specs,
        out_specs=out_specs,
        out_shape=out_shape,
        scratch_shapes=([pltpu.VMEM((8, n, 128), f32)] * 6
                        + [pltpu.VMEM((n, 2048), f32), pltpu.VMEM((n, 1024), f32), pltpu.VMEM((n, 1024), f32)]),
        input_output_aliases=aliases,
        compiler_params=_cparams(("parallel", "parallel")),
        name="s5_mixer",
    )(*args)


def _s5_layer_weights(a_re, a_im, log_dt, b_re, b_im, c_re, c_im):
    hp = lax.Precision.HIGHEST
    T = S5_CHUNK
    dt = jnp.exp(log_dt)[..., None]
    lmag, ang = a_re * dt, a_im * dt
    taus = jnp.arange(T + 1, dtype=f32)[:, None, None, None]
    mag = jnp.exp(lmag[None] * taus)
    pr = mag * jnp.cos(ang[None] * taus)
    pi = mag * jnp.sin(ang[None] * taus)
    lam_r, lam_i = pr[1], pi[1]
    den = a_re * a_re + a_im * a_im
    zr = ((lam_r - 1.0) * a_re + lam_i * a_im) / den
    zi = (lam_i * a_re - (lam_r - 1.0) * a_im) / den
    bb_r = zr[..., None] * b_re[None] - zi[..., None] * b_im[None]
    bb_i = zr[..., None] * b_im[None] + zi[..., None] * b_re[None]
    lb_r = pr[..., None] * bb_r[None] - pi[..., None] * bb_i[None]
    lb_i = pr[..., None] * bb_i[None] + pi[..., None] * bb_r[None]
    pr, pi, lb_r, lb_i = lax.optimization_barrier((pr, pi, lb_r, lb_i))
    kk = jnp.sum(c_re[None, None, :, :, :, None] * lb_r[:, :, :, None, :, :]
                 - c_im[None, None, :, :, :, None] * lb_i[:, :, :, None, :, :], axis=4)
    eye8 = jnp.eye(8, dtype=f32)

    def blockdiag(m):
        lead = m.shape[:-3]
        a, b = m.shape[-2], m.shape[-1]
        m = m.reshape(lead + (4, 8, a, b))
        out = m[..., :, :, :, None, :] * eye8[:, None, :, None]
        return out.reshape(lead + (4, 8 * a, 8 * b))

    kt = jnp.swapaxes(kk, -1, -2)
    down = np.arange(T - 1, 0, -1)
    deltas = jnp.concatenate([kt[down, 1], kt[0:1, 0] + kt[0:1, 1], kt[1:T, 0]], axis=0)
    toep = blockdiag(deltas)
    toep = jnp.transpose(toep, (1, 2, 0, 3)).reshape(4, 128, 31 * 128)

    def state_in(lbr, lbi):
        return jnp.concatenate([jnp.swapaxes(lbr, -1, -2), jnp.swapaxes(lbi, -1, -2)], axis=-1)

    rev16 = np.arange(T - 1, -1, -1)
    ws_c = jnp.stack([state_in(lb_r[rev16, 0], lb_i[rev16, 0]),
                      state_in(lb_r[:T, 1], lb_i[:T, 1])], axis=0)

    def state_out(prd, pid):
        cr = c_re[None] * prd[:, :, None, :] - c_im[None] * pid[:, :, None, :]
        ci = c_re[None] * pid[:, :, None, :] + c_im[None] * prd[:, :, None, :]
        return jnp.concatenate([cr, -ci], axis=-1)

    down16 = np.arange(T, 0, -1)
    wo_c = jnp.stack([state_out(pr[1:T + 1, 0], pi[1:T + 1, 0]),
                      state_out(pr[down16, 1], pi[down16, 1])], axis=0)
    la = jnp.concatenate([pr[T], pr[T]], axis=-1)
    lb = jnp.concatenate([-pi[T], pi[T]], axis=-1)
    lam16 = jnp.stack([la[0], lb[0], la[1], lb[1]], axis=0)
    lam16 = jnp.transpose(lam16.reshape(4, 4, 1024), (1, 0, 2))
    return dict(toep=toep.astype(bf16), ws_c=ws_c, wo_c=wo_c, lam16=lam16)


def _s5_expand_in_body(c_ref, o_ref):
    o_ref[...] = jnp.zeros(o_ref.shape, o_ref.dtype)
    for j in range(S5_CHUNK):
        for g in range(8):
            o_ref[0, 0, 0, j, g * 16:(g + 1) * 16, g * 128:(g + 1) * 128] = c_ref[0, 0, j, g].astype(bf16)


def _s5_expand_out_body(c_ref, e_ref, o_ref):
    for i in range(S5_CHUNK):
        for g in range(8):
            tile = _dot_tn(c_ref[0, 0, i, g].astype(bf16), e_ref[g])
            o_ref[0, 0, 0, g * 128:(g + 1) * 128, i * 128:(i + 1) * 128] = tile.astype(bf16)


def s5_expand_weights(ws_c, wo_c):
    c_spec = pl.BlockSpec((1, 1, S5_CHUNK, 8, S5_CH, 128), lambda l, d, lb: (l, d, 0, lb, 0, 0))
    ws = pl.pallas_call(
        _s5_expand_in_body,
        grid=(DEPTH, 2, 4),
        in_specs=[c_spec],
        out_specs=pl.BlockSpec((1, 1, 1, S5_CHUNK, 128, 1024), lambda l, d, lb: (l, d, lb, 0, 0, 0)),
        out_shape=jax.ShapeDtypeStruct((DEPTH, 2, 4, S5_CHUNK, 128, 1024), bf16),
        compiler_params=_cparams(("parallel", "parallel", "parallel")),
        name="s5_expand_in",
    )(ws_c)
    place = np.zeros((8, S5_CH, 128), np.float32)
    for g in range(8):
        place[g, np.arange(S5_CH), g * S5_CH + np.arange(S5_CH)] = 1.0
    wo = pl.pallas_call(
        _s5_expand_out_body,
        grid=(DEPTH, 2, 4),
        in_specs=[c_spec, pl.BlockSpec((8, S5_CH, 128), lambda l, d, lb: (0, 0, 0))],
        out_specs=pl.BlockSpec((1, 1, 1, 1024, 2048), lambda l, d, lb: (l, d, lb, 0, 0)),
        out_shape=jax.ShapeDtypeStruct((DEPTH, 2, 4, 1024, 2048), bf16),
        compiler_params=_cparams(("parallel", "parallel", "parallel")),
        name="s5_expand_out",
    )(wo_c, jnp.asarray(place, bf16))
    return ws, wo


def _merge_body(gp_ref, b0_ref, b1_ref, b2_ref, b3_ref, gw_ref, gb_ref, wb_ref, wo_ref, x_ref, g1_ref, o_ref,
                acc_s):
    b = pl.program_id(1)
    for bb, br_ref in enumerate((b0_ref, b1_ref, b2_ref, b3_ref)):
        @pl.when(b == bb)
        def _(br_ref=br_ref, bb=bb):
            br = br_ref[...]
            if bb == 3:
                g = jax.nn.gelu(br)
                br = g * jax.nn.sigmoid(_dot(g.astype(bf16), gw_ref[0]) + gb_ref[0])
            up = _dot(br.astype(bf16), wb_ref[0, bb])
            t = (0.5 * jnp.tanh(0.5 * gp_ref[...].astype(f32)) + 0.5) * up
            if bb == 0:
                acc_s[...] = t
            else:
                acc_s[...] += t

    @pl.when(b == 3)
    def _():
        mix = _dot(acc_s[...].astype(bf16), wo_ref[0])
        o_ref[...] = x_ref[...] + g1_ref[0, 0] * mix


def merge_branches(PB, branches, glu_w, glu_b, w_branch, w_out, x, mod, layer, tm=512):
    br_spec = pl.BlockSpec((tm, MIX_W), lambda i, b: (i, 0))
    return pl.pallas_call(
        _merge_body,
        grid=(N_TOK // tm, 4),
        in_specs=[
            pl.BlockSpec((tm, D), lambda i, b: (i, B_GATE // D + b)),
            br_spec, br_spec, br_spec, br_spec,
            pl.BlockSpec((1, MIX_W, MIX_W), lambda i, b: (layer, 0, 0)),
            pl.BlockSpec((1, 1, MIX_W), lambda i, b: (layer, 0, 0)),
            pl.BlockSpec((1, 4, MIX_W, D), lambda i, b: (layer, 0, 0, 0), pipeline_mode=pl.Buffered(1)),
            pl.BlockSpec((1, D, D), lambda i, b: (layer, 0, 0), pipeline_mode=pl.Buffered(1)),
            pl.BlockSpec((tm, D), lambda i, b: (i, 0)),
            _mod_spec(layer, 2, tm),
        ],
        out_specs=pl.BlockSpec((tm, D), lambda i, b: (i, 0)),
        out_shape=jax.ShapeDtypeStruct((N_TOK, D), f32),
        scratch_shapes=[pltpu.VMEM((tm, D), f32)],
        compiler_params=_cparams(("parallel", "arbitrary")),
        name="merge_branches",
    )(PB, *branches, glu_w, glu_b, w_branch, w_out, x, mod)


def _ffn_body(*refs, n_e, steps_per_e, final_norm):
    it = iter(refs)
    x_ref, sh_ref, sc_ref, gt_ref, g_ref, w1_ref, w3_ref, w2_ref = [next(it) for _ in range(8)]
    if n_e > 1:
        rwh_ref, rwl_ref, rb_ref = [next(it) for _ in range(3)]
    fg_ref = next(it) if final_norm else None
    o_ref = next(it)
    h_s = next(it)
    acc_s = o_ref
    gate_s = next(it) if n_e > 1 else None
    j = pl.program_id(1)
    nj = pl.num_programs(1)

    @pl.when(j == 0)
    def _():
        h = _modnorm(x_ref[...], g_ref[0], sc_ref[0, 0], sh_ref[0, 0])
        h_s[...] = h.astype(bf16)
        if n_e > 1:
            logits = _dot_hi(h, rwh_ref[0], rwl_ref[0]) + rb_ref[0]
            lane = lax.broadcasted_iota(jnp.int32, logits.shape, 1)
            neg = jnp.float32(-jnp.inf)
            lv = jnp.where(lane < n_e, logits, neg)
            m1 = jnp.max(lv, axis=1, keepdims=True)
            i1 = jnp.min(jnp.where(lv == m1, lane, 128), axis=1, keepdims=True)
            lv2 = jnp.where(lane == i1, neg, lv)
            m2 = jnp.max(lv2, axis=1, keepdims=True)
            i2 = jnp.min(jnp.where(lv2 == m2, lane, 128), axis=1, keepdims=True)
            e2 = jnp.exp(m2 - m1)
            p1 = 1.0 / (1.0 + e2)
            gate_s[...] = jnp.where(lane == i1, p1, 0.0) + jnp.where(lane == i2, e2 * p1, 0.0)

    hb = h_s[...]
    hid = _silu(_dot(hb, w1_ref[0, 0])) * _dot(hb, w3_ref[0, 0])
    if n_e > 1:
        e = j // steps_per_e
        lane = lax.broadcasted_iota(jnp.int32, gate_s.shape, 1)
        hid = hid * jnp.sum(jnp.where(lane == e, gate_s[...], 0.0), axis=1, keepdims=True)
    hidb = hid.astype(bf16)
    cb = 512

    @pl.when(j == 0)
    def _():
        for c0 in range(0, D, cb):
            acc_s[:, c0:c0 + cb] = _dot(hidb, w2_ref[0, 0, :, c0:c0 + cb])

    @pl.when(j > 0)
    def _():
        for c0 in range(0, D, cb):
            acc_s[:, c0:c0 + cb] += _dot(hidb, w2_ref[0, 0, :, c0:c0 + cb])

    @pl.when(j == nj - 1)
    def _():
        y = x_ref[...] + gt_ref[0, 0] * acc_s[...]
        if final_norm:
            ms = jnp.mean(y * y, axis=-1, keepdims=True)
            y = y * lax.rsqrt(ms + EPS) * fg_ref[...]
        o_ref[...] = y


def ffn_layer(x, mod, g, w1, w3, w2, layer, widx, router=None, final_g=None, rows=None, tm=1024, tf=512):
    rows = (0, N_TOK) if rows is None else rows
    n_e, f = w1.shape[1], w1.shape[3]
    spe = f // tf
    final = final_g is not None
    tile0, n_tiles = rows[0] // tm, rows[1] // tm
    in_specs = [
        pl.BlockSpec((tm, D), lambda i, j: (tile0 + i, 0), pipeline_mode=pl.Buffered(1)),
        _mod_spec(layer, 3, tm, tile0),
        _mod_spec(layer, 4, tm, tile0),
        _mod_spec(layer, 5, tm, tile0),
        pl.BlockSpec((1, 1, D), lambda i, j: (layer, 0, 0)),
        pl.BlockSpec((1, 1, D, tf), lambda i, j: (widx, j // spe, 0, j % spe)),
        pl.BlockSpec((1, 1, D, tf), lambda i, j: (widx, j // spe, 0, j % spe)),
        pl.BlockSpec((1, 1, tf, D), lambda i, j: (widx, j // spe, j % spe, 0)),
    ]
    args = [x, mod, mod, mod, g, w1, w3, w2]
    scratch = [pltpu.VMEM((tm, D), bf16)]
    if n_e > 1:
        in_specs += [pl.BlockSpec((1, D, 128), lambda i, j: (widx, 0, 0)),
                     pl.BlockSpec((1, D, 128), lambda i, j: (widx, 0, 0)),
                     pl.BlockSpec((1, 1, 128), lambda i, j: (widx, 0, 0))]
        args += list(router)
        scratch.append(pltpu.VMEM((tm, 128), f32))
    if final:
        in_specs.append(pl.BlockSpec((1, D), lambda i, j: (0, 0)))
        args.append(final_g.reshape(1, D))
    return pl.pallas_call(
        functools.partial(_ffn_body, n_e=n_e, steps_per_e=spe, final_norm=final),
        grid=(n_tiles, n_e * spe),
        in_specs=in_specs,
        out_specs=pl.BlockSpec((tm, D), lambda i, j: (i, 0)),
        out_shape=jax.ShapeDtypeStruct((rows[1], D), f32),
        scratch_shapes=scratch,
        compiler_params=_cparams(("parallel", "arbitrary")),
        name="ffn_moe" if n_e > 1 else "ffn_dense",
    )(*args)


def _split_hi_lo(w):
    hi = w.astype(bf16)
    lo = (w - hi.astype(f32)).astype(bf16)
    return hi, lo


_W_IN_SEGMENTS = ((2080, 3104), (5680, 6192), (6192, 14384), (0, 1536), (1568, 2080), (3104, 5664))
_W_IN_SMALL = ((1536, 1568), (5664, 5680))
_W_IN_COLS = 14384


def _reorder_body(wt_ref, o_ref, os_ref):
    c = 0
    for a, b in _W_IN_SEGMENTS:
        for r in range(a, b, 128):
            o_ref[0, :, c:c + 128] = wt_ref[0, r:r + 128, :].T.astype(bf16)
            c += 128
    rows = [wt_ref[0, a:b, :] for a, b in _W_IN_SMALL]
    rows.append(jnp.zeros((128 - sum(r.shape[0] for r in rows), 128), f32))
    os_ref[0] = jnp.concatenate(rows, axis=0).T.astype(bf16)


def _reorder_w_in(w):
    wt = jnp.swapaxes(w, 1, 2)
    return pl.pallas_call(
        _reorder_body,
        grid=(DEPTH, D // 128),
        in_specs=[pl.BlockSpec((1, _W_IN_COLS, 128), lambda l, r: (l, 0, r))],
        out_specs=[pl.BlockSpec((1, 128, N_PF + N_PB), lambda l, r: (l, r, 0)),
                   pl.BlockSpec((1, 128, 128), lambda l, r: (l, r, 0))],
        out_shape=[jax.ShapeDtypeStruct((DEPTH, D, N_PF + N_PB), bf16),
                   jax.ShapeDtypeStruct((DEPTH, D, 128), bf16)],
        compiler_params=_cparams(("parallel", "parallel")),
        name="reorder_w_in",
    )(wt)


def _gla_gate_weights(wa2, ba):
    t = wa2.reshape(DEPTH, 2, GLA_RANK, 2, 128)
    w = jnp.einsum("ldrhc,de->lhdrec", t, jnp.eye(2, dtype=f32))
    w = w.reshape(DEPTH, 2, 2 * GLA_RANK, 256)
    w = jnp.pad(w, ((0, 0), (0, 0), (SM_LR, 128 - SM_LR - 2 * GLA_RANK), (0, 0)))
    b = jnp.transpose(ba.reshape(DEPTH, 2, 2, 128), (0, 2, 1, 3)).reshape(DEPTH, 2, 1, 256)
    hi, lo = _split_hi_lo(w)
    return hi, lo, b


def _hgrn_bounds(logits):
    p = jax.nn.softmax(logits.astype(f32), axis=1)
    lower = jnp.maximum(jnp.cumsum(p, axis=1) - p[:, :1], 0.0)
    lb = jnp.transpose(lower, (1, 0, 2)).reshape(DEPTH, 2, HGRN_HEADS, HGRN_DIM)
    rows = jnp.stack([lb, 1.0 - lb], axis=3)
    rows = jnp.transpose(rows, (0, 2, 1, 3, 4)).reshape(DEPTH, HGRN_HEADS, 4, HGRN_DIM)
    return jnp.pad(rows, ((0, 0), (0, 0), (0, 4), (0, 0)))


def _ssd_select_mats():
    e64 = np.zeros((2, 2, 128, 256), np.float32)
    e128 = np.zeros((2, 2, 128, 512), np.float32)
    for g in range(2):
        for d in range(2):
            for h in range(4):
                lane = SM_DT + d * 8 + g * 4 + h
                e64[g, d, lane, h * 64:(h + 1) * 64] = 1.0
                e128[g, d, lane, h * 128:(h + 1) * 128] = 1.0
    return jnp.asarray(e64, bf16), jnp.asarray(e128, bf16)


def _ssd_params(conv_w, conv_b, dt_bias, a_log, d_skip, norm_g):
    cwb = jnp.concatenate([conv_w, conv_b[:, None]], axis=1)
    cw = jnp.concatenate([cwb[..., 0:512].reshape(DEPTH, 4, 2, 256),
                          cwb[..., 512:768].reshape(DEPTH, 4, 2, 128),
                          cwb[..., 768:1024].reshape(DEPTH, 4, 2, 128)], axis=-1)
    cw = jnp.transpose(cw, (0, 2, 1, 3))
    lanes = ((0, 0), (SM_DT, 128 - SM_DT - 16))
    dtb = jnp.pad(dt_bias.reshape(DEPTH, 16), lanes).reshape(DEPTH, 1, 128)
    alog = jnp.pad(a_log.reshape(DEPTH, 16), lanes).reshape(DEPTH, 1, 128)
    dsk = jnp.repeat(d_skip, SSM_HEADDIM, axis=1).reshape(DEPTH, 2, 1, 256)
    return cw, dtb, alog, dsk, norm_g.reshape(DEPTH, 2, 1, 256)


def kernel(x_prompt, x_sample, c, c_ctx, state_gla, state_hgrn, state_ssm, state_s5_re, state_s5_im, norm1_g, norm2_g, ada_w, ada_b, w_in, gla_wa2, gla_ba, gla_norm_g, hgrn_lb_logits, hgrn_norm_g, ssm_conv_w, ssm_conv_b, ssm_a_log, ssm_dt_bias, ssm_d, ssm_norm_g, s5_a_re, s5_a_im, s5_log_dt, s5_b_re, s5_b_im, s5_c_re, s5_c_im, s5_d, s5_glu_w, s5_glu_b, w_branch, w_out, ffn_w1, ffn_w3, ffn_w2, router_w, router_b, moe_w1, moe_w3, moe_w2, final_norm_g):
    w_all, w_small = _reorder_w_in(w_in)
    wah, wal, gba = _gla_gate_weights(gla_wa2, gla_ba)
    gng = gla_norm_g.reshape(DEPTH, 1, GLA_DV)
    lbp = _hgrn_bounds(hgrn_lb_logits)
    hng = hgrn_norm_g.reshape(DEPTH, 1, HGRN_DIM)
    e64, e128 = _ssd_select_mats()
    cw, dtb, alog, dsk, sng = _ssd_params(ssm_conv_w, ssm_conv_b, ssm_dt_bias, ssm_a_log, ssm_d, ssm_norm_g)
    w5 = jax.vmap(_s5_layer_weights)(s5_a_re, s5_a_im, s5_log_dt, s5_b_re, s5_b_im, s5_c_re, s5_c_im)
    w5["ws"], w5["wo"] = s5_expand_weights(w5["ws_c"], w5["wo_c"])
    d5 = s5_d.reshape(DEPTH, 4, 1, 128)
    glu_w = s5_glu_w.astype(bf16)
    glu_b = s5_glu_b.reshape(DEPTH, 1, MIX_W)
    wbr = w_branch.astype(bf16)
    wout = w_out.astype(bf16)
    n1 = norm1_g.reshape(DEPTH, 1, D)
    n2 = norm2_g.reshape(DEPTH, 1, D)
    fw1, fw3, fw2 = ffn_w1.astype(bf16)[:, None], ffn_w3.astype(bf16)[:, None], ffn_w2.astype(bf16)[:, None]
    mw1, mw3, mw2 = moe_w1.astype(bf16), moe_w3.astype(bf16), moe_w2.astype(bf16)
    rw = jnp.pad(router_w, ((0, 0), (0, 0), (0, 128 - N_EXPERTS)))
    router = _split_hi_lo(rw) + (jnp.pad(router_b, ((0, 0), (0, 128 - N_EXPERTS))).reshape(-1, 1, 128),)
    s5_state = jnp.concatenate([state_s5_re, state_s5_im], axis=-1).reshape(
        DEC_BATCH, DEPTH, 2, 2 * S5_GROUPS * S5_STATE)

    x = assemble_tokens(x_prompt.reshape(N_CTX, D), x_sample.reshape(N_LAT, D), _pos_table())
    cond8 = jnp.concatenate([c_ctx[None], c, jnp.zeros((MOD_ROWS - 1 - DEC_BATCH, D), f32)], axis=0)
    mod = ada_modulation(cond8, ada_w, ada_b).reshape(DEPTH, MOD_ROWS * 6, 1, D)

    lat0 = N_CTX // DEC_SEQ
    ctx = dict(B=BATCH, L=SEQ, row0=0)
    lat = dict(B=DEC_BATCH, L=DEC_SEQ, row0=lat0)
    st_gla = st_hg = st_ssm = st_s5 = None
    for l in range(DEPTH):
        PF, PB, SM = in_projection(x, mod, n1, w_all, w_small, l)

        o_gla, st_gla = gla_mixer(SM, PB, wah, wal, gba, gng, layer=l, st_prev=st_gla, nseq=2, **ctx)
        (o_gla,) = gla_mixer(SM, PB, wah, wal, gba, gng, layer=l, state=state_gla, prev=o_gla, **lat)

        o_hg, st_hg = hgrn_mixer(PF, PB, lbp, hng, layer=l, st_prev=st_hg, nseq=2, **ctx)
        (o_hg,) = hgrn_mixer(PF, PB, lbp, hng, layer=l, state=state_hgrn, prev=o_hg, **lat)

        o_ssm, st_ssm = ssd_mixer(SM, PB, cw, dtb, alog, e64, e128, dsk, sng, layer=l, st_prev=st_ssm, **ctx)
        (o_ssm,) = ssd_mixer(SM, PB, cw, dtb, alog, e64, e128, dsk, sng, layer=l, state=state_ssm,
                             prev=o_ssm, **lat)

        y5, st_s5 = s5_mixer(PF, w5, d5, B=BATCH, L=SEQ, s_b=min(8, BATCH), row0=0, layer=l, st_prev=st_s5)
        sb_lat = 2 if (DEC_BATCH % 2 == 0 and N_CTX % (2 * DEC_SEQ) == 0) else 1
        (y5,) = s5_mixer(PF, w5, d5, B=DEC_BATCH, L=DEC_SEQ, s_b=sb_lat, row0=N_CTX // (sb_lat * DEC_SEQ),
                         layer=l, state=s5_state, prev=y5)

        x = merge_branches(PB, (o_gla, o_hg, o_ssm, y5), glu_w, glu_b, wbr, wout, x, mod, l)

        if l % 2 == 0:
            ffn = functools.partial(ffn_layer, x, mod, n2, fw1, fw3, fw2, l, l // 2)
        else:
            ffn = functools.partial(ffn_layer, x, mod, n2, mw1, mw3, mw2, l, l // 2, router=router)
        if l < DEPTH - 1:
            x = ffn()
        else:
            y_ctx = ffn(final_g=final_norm_g, rows=(0, N_CTX))
            y_lat = ffn(final_g=final_norm_g, rows=(N_CTX, N_LAT))

    st_s5 =st_s5.reshape(BATCH, DEPTH, 2, S5_GROUPS, 2 * S5_STATE)
    return (y_ctx.reshape(BATCH, SEQ, D), y_lat.reshape(DEC_BATCH, DEC_SEQ, D),
            st_gla, st_hg, st_ssm, st_s5[..., :S5_STATE], st_s5[..., S5_STATE:])
```
